```python
import math
import jax, jax.numpy as jnp
from jax import lax
import numpy as np

D_MODEL = 1024
BATCH = 2
SEQ = 8192
DEPTH = 2

CHUNK = 64
Q_BLOCK = 128
MEM_LEN = 256
GMLP_BLOCK = 128
A_GROUPS = 4
A_CH = 128
A_WIDTH = A_GROUPS * A_CH
B_HEADS = 4
B_HEAD_DIM = 128
B_WIDTH = B_HEADS * B_HEAD_DIM
C_HEADS = 4
C_QK_DIM = 64
C_V_DIM = 2 * C_QK_DIM
C_WIDTH = C_HEADS * C_V_DIM
D_GROUPS = 8
D_WIDTH = 512
CONV_W = 3
EVEN_IN = 2 * A_WIDTH + 3 * B_WIDTH + B_HEADS
ODD_IN = 2 * (2 * C_HEADS * C_QK_DIM) + C_WIDTH + 3 * D_WIDTH
EVEN_MIX = A_WIDTH + B_WIDTH
ODD_MIX = C_WIDTH + D_WIDTH
M_HEADS = 4
M_HEAD_DIM = D_MODEL // M_HEADS
D_FF = 2816
N_EXPERTS = 8
TOP_K = 2
D_FF_EXPERT = 3584
REL_BUCKETS = 32
REL_MAX_DIST = 128
EPS = 1e-6
N_EVEN = (DEPTH + 1) // 2
N_ODD = DEPTH // 2

kernel_name = 'hybrid_streaming_encoder_gmlp_fox_diffattn_shortconv_moe'


def rmsnorm(x, g):
    xf = x.astype(jnp.float32)
    y = xf * lax.rsqrt(jnp.mean(xf * xf, axis=-1, keepdims=True) + EPS)
    return (y * g.astype(jnp.float32)).astype(x.dtype)


def split_cols(z, widths):
    offs = np.cumsum([0] + list(widths))
    return [z[..., int(a):int(b)] for a, b in zip(offs[:-1], offs[1:])]


def to_blocks(t):
    b, h, s, d = t.shape
    return t.reshape(b, h, s // Q_BLOCK, Q_BLOCK, d).transpose(2, 0, 1, 3, 4)


def from_blocks(t):
    nb, b, h, q, d = t.shape
    return t.transpose(1, 2, 0, 3, 4).reshape(b, h, nb * q, d)


def swiglu(h, wg, wu, wd):
    return (jax.nn.silu(h @ wg) * (h @ wu)) @ wd


def gmlp_spatial_gate(u, v, v_gain, w_s, b_s):
    bsz, s, _ = u.shape
    nblk = s // GMLP_BLOCK
    u = jax.nn.gelu(u).reshape(bsz, nblk, GMLP_BLOCK, A_GROUPS, A_CH)
    v = jax.nn.gelu(v).reshape(bsz, nblk, GMLP_BLOCK, A_GROUPS, A_CH)
    v = rmsnorm(v, v_gain.reshape(A_GROUPS, A_CH))
    pos = jnp.arange(GMLP_BLOCK)
    mask = (pos[None, :] // CHUNK) <= (pos[:, None] // CHUNK)
    w = jnp.where(mask[None], w_s, 0)
    mixed = jnp.einsum('gij,bnjgc->bnigc', w, v) + b_s.T[None, None, :, :, None]
    return (u * mixed).reshape(bsz, s, A_WIDTH)


def forgetting_attention(q, k, v, f_logit):
    b, h, s, _ = q.shape
    nb = s // Q_BLOCK
    c = jnp.cumsum(jax.nn.log_sigmoid(f_logit.astype(jnp.float32)), axis=-1)
    cb = c.reshape(b, h, nb, Q_BLOCK).transpose(2, 0, 1, 3)
    kpos = jnp.arange(s)
    scale = B_HEAD_DIM ** -0.5

    def block(args):
        qi, ci, bi = args
        qpos = bi * Q_BLOCK + jnp.arange(Q_BLOCK)
        lg = jnp.einsum('bhqd,bhkd->bhqk', qi, k).astype(jnp.float32) * scale
        lg = lg + ci[..., :, None] - c[:, :, None, :]
        lg = jnp.where(kpos[None, :] <= qpos[:, None], lg, -jnp.inf)
        p = jax.nn.softmax(lg, axis=-1).astype(v.dtype)
        return jnp.einsum('bhqk,bhkd->bhqd', p, v)

    return from_blocks(lax.map(block, (to_blocks(q), cb, jnp.arange(nb))))


def rel_bucket(rel):
    n_half = REL_BUCKETS // 2
    max_exact = n_half // 2
    ret = jnp.where(rel > 0, n_half, 0)
    n = jnp.abs(rel)
    nf = jnp.maximum(n, 1).astype(jnp.float32)
    large = max_exact + (jnp.log(nf / max_exact) / math.log(REL_MAX_DIST / max_exact)
                         * (n_half - max_exact)).astype(jnp.int32)
    large = jnp.minimum(large, n_half - 1)
    return ret + jnp.where(n < max_exact, n, large)


def differential_attention(q1, q2, k1, k2, v, lam, rel_table):
    s = q1.shape[2]
    nb = s // Q_BLOCK
    kpos = jnp.arange(s)
    scale = C_QK_DIM ** -0.5

    def block(args):
        q1i, q2i, bi = args
        qpos = bi * Q_BLOCK + jnp.arange(Q_BLOCK)
        bias = rel_table[rel_bucket(kpos[None, :] - qpos[:, None])]
        bias = bias.transpose(2, 0, 1).astype(jnp.float32)[None]
        mask = (kpos[None, :] // CHUNK) <= (qpos[:, None] // CHUNK)

        def probs(qi, ki):
            lg = jnp.einsum('bhqd,bhkd->bhqk', qi, ki).astype(jnp.float32) * scale + bias
            return jax.nn.softmax(jnp.where(mask, lg, -jnp.inf), axis=-1)

        p = probs(q1i, k1) - lam * probs(q2i, k2)
        return jnp.einsum('bhqk,bhkd->bhqd', p.astype(v.dtype), v)

    return from_blocks(lax.map(block, (to_blocks(q1), to_blocks(q2), jnp.arange(nb))))


def short_conv_mixer(h, gb, gc, conv_w):
    y = gc * h
    y = lax.conv_general_dilated(y, conv_w[:, None, :].astype(y.dtype), window_strides=(1,),
                                 padding=[(CONV_W - 1, 0)], dimension_numbers=('NWC', 'WIO', 'NWC'),
                                 feature_group_count=D_WIDTH)
    return gb * y


def even_mixer(xn, w_in, b_f, v_gain, w_s, b_s, w_out):
    b, s, _ = xn.shape
    u, v, q, k, vv, f = split_cols(xn @ w_in, [A_WIDTH, A_WIDTH, B_WIDTH, B_WIDTH, B_WIDTH, B_HEADS])
    ya = gmlp_spatial_gate(u, v, v_gain, w_s, b_s)
    heads = lambda t: t.reshape(b, s, B_HEADS, B_HEAD_DIM).transpose(0, 2, 1, 3)
    yb = forgetting_attention(heads(q), heads(k), heads(vv), (f + b_f).transpose(0, 2, 1))
    yb = yb.transpose(0, 2, 1, 3).reshape(b, s, B_WIDTH)
    return jnp.concatenate([ya, yb], axis=-1) @ w_out


def odd_mixer(xn, w_in, lq1, lk1, lq2, lk2, subln_gain, conv_w, rel_table, w_out, lam_init):
    b, s, _ = xn.shape
    qk_w = 2 * C_HEADS * C_QK_DIM
    q, k, v, h, gb, gc = split_cols(xn @ w_in, [qk_w, qk_w, C_WIDTH, D_WIDTH, D_WIDTH, D_WIDTH])
    q = q.reshape(b, s, 2, C_HEADS, C_QK_DIM).transpose(2, 0, 3, 1, 4)
    k = k.reshape(b, s, 2, C_HEADS, C_QK_DIM).transpose(2, 0, 3, 1, 4)
    v = v.reshape(b, s, C_HEADS, C_V_DIM).transpose(0, 2, 1, 3)
    lam = (jnp.exp(jnp.sum(lq1.astype(jnp.float32) * lk1.astype(jnp.float32)))
           - jnp.exp(jnp.sum(lq2.astype(jnp.float32) * lk2.astype(jnp.float32))) + lam_init)
    o = differential_attention(q[0], q[1], k[0], k[1], v, lam, rel_table)
    o = rmsnorm(o, subln_gain.reshape(C_HEADS, 1, C_V_DIM)) * (1.0 - lam_init)
    yc = o.transpose(0, 2, 1, 3).reshape(b, s, C_WIDTH)
    yd = short_conv_mixer(h, gb, gc, conv_w)
    return jnp.concatenate([yc, yd], axis=-1) @ w_out


def memory_cross_attention(xn, memn, w_q, w_kv, w_o):
    b, s, _ = xn.shape
    m = memn.shape[1]
    q = (xn @ w_q).reshape(b, s, M_HEADS, M_HEAD_DIM)
    kv = (memn @ w_kv).reshape(b, m, 2, M_HEADS, M_HEAD_DIM)
    k, v = kv[:, :, 0], kv[:, :, 1]
    lg = jnp.einsum('bshd,bmhd->bhsm', q, k).astype(jnp.float32) * M_HEAD_DIM ** -0.5
    p = jax.nn.softmax(lg, axis=-1).astype(v.dtype)
    o = jnp.einsum('bhsm,bmhd->bshd', p, v).reshape(b, s, D_MODEL)
    return o @ w_o


def moe_swiglu(h, w_router, wg, wu, wd):
    b, s, d = h.shape
    t = h.reshape(b * s, d)
    logits = (t @ w_router).astype(jnp.float32)
    top_v, top_i = lax.top_k(logits, TOP_K)
    gates = jax.nn.softmax(top_v, axis=-1)
    combine = jnp.sum(jax.nn.one_hot(top_i, N_EXPERTS, dtype=jnp.float32) * gates[..., None], axis=1)
    out = jnp.zeros_like(t)
    for e in range(N_EXPERTS):
        out = out + combine[:, e:e + 1].astype(t.dtype) * swiglu(t, wg[e], wu[e], wd[e])
    return out.reshape(b, s, d)


def setup_inputs(seed: int = 0) -> dict:
    key = jax.random.key(seed)
    ks = iter(jax.random.split(key, 48))
    f32 = jnp.float32

    def nrm(shape, fan_in):
        return jax.random.normal(next(ks), shape, f32) * fan_in ** -0.5

    def gain(shape):
        return 1.0 + 0.05 * jax.random.normal(next(ks), shape, f32)

    return {
        'x': jax.random.normal(next(ks), (BATCH, SEQ, D_MODEL), f32),
        'mem': jax.random.normal(next(ks), (BATCH, MEM_LEN, D_MODEL), f32),
        'norm_mix': gain((DEPTH, D_MODEL)),
        'norm_mem_q': gain((DEPTH, D_MODEL)),
        'norm_mem_kv': gain((DEPTH, D_MODEL)),
        'norm_ffn': gain((DEPTH, D_MODEL)),
        'norm_final': gain((D_MODEL,)),
        'even_w_in': nrm((N_EVEN, D_MODEL, EVEN_IN), D_MODEL),
        'fox_b_f': jax.random.uniform(next(ks), (N_EVEN, B_HEADS), f32, 1.0, 4.0),
        'gmlp_v_gain': gain((N_EVEN, A_WIDTH)),
        'gmlp_w_s': nrm((N_EVEN, A_GROUPS, GMLP_BLOCK, GMLP_BLOCK), GMLP_BLOCK),
        'gmlp_b_s': 1.0 + 0.1 * jax.random.normal(next(ks), (N_EVEN, A_GROUPS, GMLP_BLOCK), f32),
        'even_w_out': nrm((N_EVEN, EVEN_MIX, D_MODEL), EVEN_MIX),
        'odd_w_in': nrm((N_ODD, D_MODEL, ODD_IN), D_MODEL),
        'diff_lambda_q1': 0.1 * jax.random.normal(next(ks), (N_ODD, C_QK_DIM), f32),
        'diff_lambda_k1': 0.1 * jax.random.normal(next(ks), (N_ODD, C_QK_DIM), f32),
        'diff_lambda_q2': 0.1 * jax.random.normal(next(ks), (N_ODD, C_QK_DIM), f32),
        'diff_lambda_k2': 0.1 * jax.random.normal(next(ks), (N_ODD, C_QK_DIM), f32),
        'diff_subln_gain': gain((N_ODD, C_WIDTH)),
        'conv_w': nrm((N_ODD, CONV_W, D_WIDTH), CONV_W),
        'odd_w_out': nrm((N_ODD, ODD_MIX, D_MODEL), ODD_MIX),
        'rel_bias': 0.2 * jax.random.normal(next(ks), (REL_BUCKETS, C_HEADS), f32),
        'mem_w_q': nrm((DEPTH, D_MODEL, D_MODEL), D_MODEL),
        'mem_w_kv': nrm((DEPTH, D_MODEL, 2 * D_MODEL), D_MODEL),
        'mem_w_o': nrm((DEPTH, D_MODEL, D_MODEL), D_MODEL),
        'ffn_w_gate': nrm((N_EVEN, D_MODEL, D_FF), D_MODEL),
        'ffn_w_up': nrm((N_EVEN, D_MODEL, D_FF), D_MODEL),
        'ffn_w_down': nrm((N_EVEN, D_FF, D_MODEL), D_FF),
        'router_w': nrm((N_ODD, D_MODEL, N_EXPERTS), D_MODEL),
        'moe_w_gate': nrm((N_ODD, N_EXPERTS, D_MODEL, D_FF_EXPERT), D_MODEL),
        'moe_w_up': nrm((N_ODD, N_EXPERTS, D_MODEL, D_FF_EXPERT), D_MODEL),
        'moe_w_down': nrm((N_ODD, N_EXPERTS, D_FF_EXPERT, D_MODEL), D_FF_EXPERT),
    }


def reference(x, mem, norm_mix, norm_mem_q, norm_mem_kv, norm_ffn, norm_final,
              even_w_in, fox_b_f, gmlp_v_gain, gmlp_w_s, gmlp_b_s, even_w_out,
              odd_w_in, diff_lambda_q1, diff_lambda_k1, diff_lambda_q2, diff_lambda_k2,
              diff_subln_gain, conv_w, odd_w_out, rel_bias,
              mem_w_q, mem_w_kv, mem_w_o,
              ffn_w_gate, ffn_w_up, ffn_w_down,
              router_w, moe_w_gate, moe_w_up, moe_w_down):
    for l in range(DEPTH):
        i = l // 2
        xn = rmsnorm(x, norm_mix[l])
        if l % 2 == 0:
            x = x + even_mixer(xn, even_w_in[i], fox_b_f[i], gmlp_v_gain[i], gmlp_w_s[i],
                               gmlp_b_s[i], even_w_out[i])
        else:
            lam_init = 0.8 - 0.6 * math.exp(-0.3 * l)
            x = x + odd_mixer(xn, odd_w_in[i], diff_lambda_q1[i], diff_lambda_k1[i],
                              diff_lambda_q2[i], diff_lambda_k2[i], diff_subln_gain[i],
                              conv_w[i], rel_bias, odd_w_out[i], lam_init)
        x = x + memory_cross_attention(rmsnorm(x, norm_mem_q[l]), rmsnorm(mem, norm_mem_kv[l]),
                                       mem_w_q[l], mem_w_kv[l], mem_w_o[l])
        h = rmsnorm(x, norm_ffn[l])
        if l % 2 == 0:
            x = x + swiglu(h, ffn_w_gate[i], ffn_w_up[i], ffn_w_down[i])
        else:
            x = x + moe_swiglu(h, router_w[i], moe_w_gate[i], moe_w_up[i], moe_w_down[i])
    return rmsnorm(x, norm_final)
```

```python
import functools
import math

import numpy as np
import jax
import jax.numpy as jnp
from jax import lax
from jax.experimental import pallas as pl
from jax.experimental.pallas import tpu as pltpu

F32 = jnp.float32
BF16 = jnp.bfloat16
I32 = jnp.int32

EPS = 1e-6
NEG = -1e30
LANES = 128
VMEM_LIMIT = 56 * 1024 * 1024

CHUNK = 64
A_GROUPS = 4
N_HEADS = 4
N_EXPERTS = 8
REL_BUCKETS = 32
REL_MAX_DIST = 128

TOK_TILE = 512
ATT_BLOCK = 512
MOE_TILE = 512
MOE_FF_CHUNK = 512
FFN_CHUNK = 512
DISPATCH_CHUNK = 256
COMBINE_TILE = 256


def _dot(a, b):
    return jnp.dot(a, b, preferred_element_type=F32)


def _dot_nt(a, b):
    return lax.dot_general(a, b, (((1,), (1,)), ((), ())), preferred_element_type=F32)


def _rmsnorm(x, g):
    return x * lax.rsqrt(jnp.mean(x * x, axis=-1, keepdims=True) + EPS) * g


def _params(*sem):
    return pltpu.CompilerParams(dimension_semantics=sem, vmem_limit_bytes=VMEM_LIMIT)


def _const_spec(shape):
    return pl.BlockSpec(shape, lambda *_: (0,) * len(shape))


def _split3(v):
    a1 = v.astype(BF16)
    r1 = v - a1.astype(F32)
    a2 = r1.astype(BF16)
    a3 = (r1 - a2.astype(F32)).astype(BF16)
    return a1, a2, a3


def _even_in_kernel(x_ref, g_ref, w_ref, bf_ref, vg_ref, ws_ref, bs_ref,
                    ya_ref, qkv_ref, c_ref, carry_ref, *, tiles_per_batch):
    i = pl.program_id(0)
    tm = x_ref.shape[0]
    aw = ya_ref.shape[1]
    qw = qkv_ref.shape[1]
    xn = _rmsnorm(x_ref[...], g_ref[...]).astype(BF16)

    qkv_ref[...] = _dot(xn, w_ref[:, 2 * aw:2 * aw + qw]).astype(BF16)

    fl = _dot(xn, w_ref[:, 2 * aw + qw:]) + bf_ref[...]
    ls = jnp.minimum(fl, 0.0) - jnp.log1p(jnp.exp(-jnp.abs(fl)))
    row = lax.broadcasted_iota(I32, (tm, tm), 0)
    col = lax.broadcasted_iota(I32, (tm, tm), 1)
    tri = jnp.where(col <= row, 1.0, 0.0).astype(BF16)
    a1, a2, a3 = _split3(ls)
    csum = _dot(tri, a1) + _dot(tri, a2) + _dot(tri, a3)

    @pl.when(i % tiles_per_batch == 0)
    def _():
        carry_ref[...] = jnp.zeros_like(carry_ref)

    csum = csum + carry_ref[0:1, :]
    c_ref[...] = csum
    carry_ref[...] = jnp.broadcast_to(csum[tm - 1:tm, :], carry_ref.shape)

    gu = jax.nn.gelu(_dot(xn, w_ref[:, 0:aw]), approximate=True)
    gv = jax.nn.gelu(_dot(xn, w_ref[:, aw:2 * aw]), approximate=True)
    blk = ws_ref.shape[1]
    ch = aw // A_GROUPS
    r = lax.broadcasted_iota(I32, (blk, blk), 0)
    c = lax.broadcasted_iota(I32, (blk, blk), 1)
    causal = (c // CHUNK) <= (r // CHUNK)
    for g in range(A_GROUPS):
        vgrp = gv[:, g * ch:(g + 1) * ch]
        vn = _rmsnorm(vgrp, vg_ref[:, g * ch:(g + 1) * ch]).astype(BF16)
        wmix = jnp.where(causal, ws_ref[g], 0.0).astype(BF16)
        for n in range(tm // blk):
            mixed = _dot(wmix, vn[n * blk:(n + 1) * blk, :]) + bs_ref[g]
            ya_ref[n * blk:(n + 1) * blk, g * ch:(g + 1) * ch] = (
                gu[n * blk:(n + 1) * blk, g * ch:(g + 1) * ch] * mixed).astype(BF16)


def _even_in(x, g, w, bf, vgain, ws, bs, *, seq):
    t, d = x.shape
    tm = min(TOK_TILE, seq)
    aw = vgain.shape[1]
    qw = w.shape[1] - 2 * aw - LANES
    blk = ws.shape[1]
    return pl.pallas_call(
        functools.partial(_even_in_kernel, tiles_per_batch=seq // tm),
        grid=(t // tm,),
        in_specs=[
            pl.BlockSpec((tm, d), lambda i: (i, 0)),
            _const_spec((1, d)),
            _const_spec(w.shape),
            _const_spec((1, LANES)),
            _const_spec((1, aw)),
            _const_spec(ws.shape),
            _const_spec(bs.shape),
        ],
        out_specs=[
            pl.BlockSpec((tm, aw), lambda i: (i, 0)),
            pl.BlockSpec((tm, qw), lambda i: (i, 0)),
            pl.BlockSpec((tm, LANES), lambda i: (i, 0)),
        ],
        out_shape=[
            jax.ShapeDtypeStruct((t, aw), BF16),
            jax.ShapeDtypeStruct((t, qw), BF16),
            jax.ShapeDtypeStruct((t, LANES), F32),
        ],
        scratch_shapes=[pltpu.VMEM((8, LANES), F32)],
        compiler_params=_params("arbitrary"),
        name="even_in",
    )(x, g, w, bf, vgain, ws, bs)


def _tri_pairs(nblk):
    qi = np.array([i for i in range(nblk) for _ in range(i + 1)], np.int32)
    kj = np.array([j for i in range(nblk) for j in range(i + 1)], np.int32)
    return jnp.asarray(qi), jnp.asarray(kj)


def _online_softmax_step(s, v, m_ref, l_ref, acc_ref):
    m_prev = m_ref[...]
    m_new = jnp.maximum(m_prev, jnp.max(s, axis=-1, keepdims=True))
    alpha = jnp.exp(m_prev - m_new)
    p = jnp.exp(s - m_new)
    l_ref[...] = alpha * l_ref[...] + jnp.sum(p, axis=-1, keepdims=True)
    acc_ref[...] = alpha * acc_ref[...] + _dot(p.astype(BF16), v)
    m_ref[...] = m_new


def _fox_kernel(qi_ref, kj_ref, q_ref, k_ref, v_ref, ccol_ref, crow_ref, o_ref,
                m_ref, l_ref, acc_ref, cq_ref):
    h = pl.program_id(1)
    step = pl.program_id(2)
    i = qi_ref[step]
    j = kj_ref[step]
    tq, tk = q_ref.shape[0], k_ref.shape[0]

    @pl.when(j == 0)
    def _():
        m_ref[...] = jnp.full_like(m_ref, NEG)
        l_ref[...] = jnp.zeros_like(l_ref)
        acc_ref[...] = jnp.zeros_like(acc_ref)
        lane = lax.broadcasted_iota(I32, ccol_ref.shape, 1)
        cq_ref[...] = jnp.sum(jnp.where(lane == h, ccol_ref[...], 0.0), axis=-1, keepdims=True)

    s = _dot_nt(q_ref[...], k_ref[...]) + cq_ref[...] - crow_ref[0]

    @pl.when(j < i)
    def _():
        _online_softmax_step(s, v_ref[...], m_ref, l_ref, acc_ref)

    @pl.when(j == i)
    def _():
        row = lax.broadcasted_iota(I32, (tq, tk), 0)
        col = lax.broadcasted_iota(I32, (tq, tk), 1)
        _online_softmax_step(jnp.where(col <= row, s, NEG), v_ref[...], m_ref, l_ref, acc_ref)
        o_ref[...] = (acc_ref[...] / l_ref[...]).astype(o_ref.dtype)


def _fox_attention(qkv, ccol, crow, *, batch, seq):
    t = qkv.shape[0]
    hd = LANES
    blk = min(ATT_BLOCK, seq)
    nblk = seq // blk
    qi, kj = _tri_pairs(nblk)
    grid_spec = pltpu.PrefetchScalarGridSpec(
        num_scalar_prefetch=2,
        grid=(batch, N_HEADS, qi.shape[0]),
        in_specs=[
            pl.BlockSpec((blk, hd), lambda b, h, s, qi, kj: (b * nblk + qi[s], h)),
            pl.BlockSpec((blk, hd), lambda b, h, s, qi, kj: (b * nblk + kj[s], N_HEADS + h)),
            pl.BlockSpec((blk, hd), lambda b, h, s, qi, kj: (b * nblk + kj[s], 2 * N_HEADS + h)),
            pl.BlockSpec((blk, LANES), lambda b, h, s, qi, kj: (b * nblk + qi[s], 0)),
            pl.BlockSpec((1, 1, blk), lambda b, h, s, qi, kj: (b * N_HEADS + h, 0, kj[s])),
        ],
        out_specs=pl.BlockSpec((blk, hd), lambda b, h, s, qi, kj: (b * nblk + qi[s], h)),
        scratch_shapes=[
            pltpu.VMEM((blk, 1), F32),
            pltpu.VMEM((blk, 1), F32),
            pltpu.VMEM((blk, hd), F32),
            pltpu.VMEM((blk, 1), F32),
        ],
    )
    return pl.pallas_call(
        _fox_kernel,
        grid_spec=grid_spec,
        out_shape=jax.ShapeDtypeStruct((t, N_HEADS * hd), BF16),
        compiler_params=_params("arbitrary", "arbitrary", "arbitrary"),
        name="fox_attention",
    )(qi, kj, qkv, qkv, qkv, ccol, crow)


def _diff_kernel(qi_ref, kj_ref, lam_ref, q_ref, k_ref, v_ref, bias_ref, gain_ref, o_ref,
                 qs_ref, m_ref, l_ref, acc_ref, *, out_scale):
    step = pl.program_id(2)
    i = qi_ref[step]
    j = kj_ref[step]
    tq = q_ref.shape[0]
    half = q_ref.shape[1] // 2

    @pl.when(j == 0)
    def _():
        m_ref[...] = jnp.full_like(m_ref, NEG)
        l_ref[...] = jnp.zeros_like(l_ref)
        acc_ref[...] = jnp.zeros_like(acc_ref)
        q = q_ref[...]
        lane = lax.broadcasted_iota(I32, q.shape, 1)
        zero = jnp.zeros_like(q)
        qs_ref[0:tq, :] = jnp.where(lane < half, q, zero)
        qs_ref[tq:2 * tq, :] = jnp.where(lane < half, zero, q)

    s = _dot_nt(qs_ref[...], k_ref[...])

    @pl.when(j < i - 1)
    def _():
        _online_softmax_step(s, v_ref[...], m_ref, l_ref, acc_ref)

    @pl.when(j == i - 1)
    def _():
        b = bias_ref[0, 1]
        _online_softmax_step(s + jnp.concatenate([b, b], axis=0), v_ref[...], m_ref, l_ref, acc_ref)

    @pl.when(j == i)
    def _():
        b = bias_ref[0, 0]
        _online_softmax_step(s + jnp.concatenate([b, b], axis=0), v_ref[...], m_ref, l_ref, acc_ref)
        o = acc_ref[...] / l_ref[...]
        o = o[0:tq, :] - lam_ref[0, 0] * o[tq:2 * tq, :]
        o_ref[...] = (_rmsnorm(o, gain_ref[...]) * out_scale).astype(o_ref.dtype)


def _diff_attention(qkv, lam, bias, gain, *, batch, seq, out_scale):
    t = qkv.shape[0]
    hd = LANES
    blk = bias.shape[2]
    nblk = seq // blk
    qi, kj = _tri_pairs(nblk)
    grid_spec = pltpu.PrefetchScalarGridSpec(
        num_scalar_prefetch=2,
        grid=(batch, N_HEADS, qi.shape[0]),
        in_specs=[
            pl.BlockSpec(memory_space=pltpu.SMEM),
            pl.BlockSpec((blk, hd), lambda b, h, s, qi, kj: (b * nblk + qi[s], h)),
            pl.BlockSpec((blk, hd), lambda b, h, s, qi, kj: (b * nblk + kj[s], N_HEADS + h)),
            pl.BlockSpec((blk, hd), lambda b, h, s, qi, kj: (b * nblk + kj[s], 2 * N_HEADS + h)),
            pl.BlockSpec((1, 2, blk, blk), lambda b, h, s, qi, kj: (h, 0, 0, 0)),
            pl.BlockSpec((1, hd), lambda b, h, s, qi, kj: (0, h)),
        ],
        out_specs=pl.BlockSpec((blk, hd), lambda b, h, s, qi, kj: (b * nblk + qi[s], h)),
        scratch_shapes=[
            pltpu.VMEM((2 * blk, hd), BF16),
            pltpu.VMEM((2 * blk, 1), F32),
            pltpu.VMEM((2 * blk, 1), F32),
            pltpu.VMEM((2 * blk, hd), F32),
        ],
    )
    return pl.pallas_call(
        functools.partial(_diff_kernel, out_scale=out_scale),
        grid_spec=grid_spec,
        out_shape=jax.ShapeDtypeStruct((t, N_HEADS * hd), BF16),
        compiler_params=_params("arbitrary", "arbitrary", "arbitrary"),
        name="diff_attention",
    )(qi, kj, lam, qkv, qkv, qkv, bias, gain)


def _rel_bucket(rel):
    n_half = REL_BUCKETS // 2
    max_exact = n_half // 2
    ret = jnp.where(rel > 0, n_half, 0)
    n = jnp.abs(rel)
    nf = jnp.maximum(n, 1).astype(F32)
    large = max_exact + (jnp.log(nf / max_exact) / math.log(REL_MAX_DIST / max_exact)
                         * (n_half - max_exact)).astype(I32)
    large = jnp.minimum(large, n_half - 1)
    return ret + jnp.where(n < max_exact, n, large)


def _diff_bias_blocks(rel_table, blk):
    assert blk >= REL_MAX_DIST
    r = jnp.arange(blk)[:, None]
    c = jnp.arange(blk)[None, :]
    far = rel_table[REL_BUCKETS // 2 - 1]
    diag = rel_table[_rel_bucket(c - r)] - far
    prev = rel_table[_rel_bucket(c - r - blk)] - far
    diag = jnp.where(((c // CHUNK) <= (r // CHUNK))[..., None], diag, NEG)
    return jnp.stack([diag, prev], axis=0).transpose(3, 0, 1, 2).astype(F32)


def _odd_in_kernel(x_ref, g_ref, w_ref, cw_ref, qkv_ref, yd_ref, zbuf_ref, *, tiles_per_batch):
    i = pl.program_id(0)
    tm = x_ref.shape[0]
    qw = qkv_ref.shape[1]
    dw = yd_ref.shape[1]
    xn = _rmsnorm(x_ref[...], g_ref[...]).astype(BF16)
    qkv_ref[...] = _dot(xn, w_ref[:, 0:qw]).astype(BF16)
    hh = _dot(xn, w_ref[:, qw:qw + dw])
    gb = _dot(xn, w_ref[:, qw + dw:qw + 2 * dw])
    gc = _dot(xn, w_ref[:, qw + 2 * dw:qw + 3 * dw])
    z = gc * hh

    @pl.when(i % tiles_per_batch == 0)
    def _():
        zbuf_ref[0:8, :] = jnp.zeros((8, dw), F32)

    @pl.when(i % tiles_per_batch != 0)
    def _():
        zbuf_ref[0:8, :] = zbuf_ref[tm:tm + 8, :]

    zbuf_ref[8:tm + 8, :] = z
    y = (cw_ref[0:1, :] * zbuf_ref[6:tm + 6, :] + cw_ref[1:2, :] * zbuf_ref[7:tm + 7, :]
         + cw_ref[2:3, :] * z)
    yd_ref[...] = (gb * y).astype(BF16)


def _odd_in(x, g, w, cw, *, seq, qw):
    t, d = x.shape
    tm = min(TOK_TILE, seq)
    dw = cw.shape[1]
    return pl.pallas_call(
        functools.partial(_odd_in_kernel, tiles_per_batch=seq // tm),
        grid=(t // tm,),
        in_specs=[
            pl.BlockSpec((tm, d), lambda i: (i, 0)),
            _const_spec((1, d)),
            _const_spec(w.shape),
            _const_spec(cw.shape),
        ],
        out_specs=[
            pl.BlockSpec((tm, qw), lambda i: (i, 0)),
            pl.BlockSpec((tm, dw), lambda i: (i, 0)),
        ],
        out_shape=[
            jax.ShapeDtypeStruct((t, qw), BF16),
            jax.ShapeDtypeStruct((t, dw), BF16),
        ],
        scratch_shapes=[pltpu.VMEM((tm + 8, dw), F32)],
        compiler_params=_params("arbitrary"),
        name="odd_in",
    )(x, g, w, cw)


def _mem_kv_kernel(mem_ref, g_ref, w_ref, k_ref, v_ref):
    d = mem_ref.shape[2]
    mn = _rmsnorm(mem_ref[0], g_ref[...]).astype(BF16)
    k_ref[0] = _dot(mn, w_ref[:, 0:d]).astype(BF16)
    v_ref[0] = _dot(mn, w_ref[:, d:2 * d]).astype(BF16)


def _mem_kv(mem, g, w):
    b, m, d = mem.shape
    return pl.pallas_call(
        _mem_kv_kernel,
        grid=(b,),
        in_specs=[
            pl.BlockSpec((1, m, d), lambda i: (i, 0, 0)),
            _const_spec((1, d)),
            _const_spec(w.shape),
        ],
        out_specs=[pl.BlockSpec((1, m, d), lambda i: (i, 0, 0))] * 2,
        out_shape=[jax.ShapeDtypeStruct((b, m, d), BF16)] * 2,
        compiler_params=_params("arbitrary"),
        name="mem_kv",
    )(mem, g, w)


def _post_mixer_kernel(x_ref, ya_ref, yb_ref, wout_ref, g_ref, wq_ref, k_ref, v_ref, wo_ref, o_ref):
    wa = ya_ref.shape[1]
    d = x_ref.shape[1]
    hd = d // N_HEADS
    x1 = x_ref[...] + _dot(ya_ref[...], wout_ref[0:wa, :]) + _dot(yb_ref[...], wout_ref[wa:, :])
    xn = _rmsnorm(x1, g_ref[...]).astype(BF16)
    q = _dot(xn, wq_ref[...]).astype(BF16)
    heads = []
    for h in range(N_HEADS):
        lg = _dot_nt(q[:, h * hd:(h + 1) * hd], k_ref[0, :, h * hd:(h + 1) * hd])
        p = jnp.exp(lg - jnp.max(lg, axis=-1, keepdims=True))
        p = p / jnp.sum(p, axis=-1, keepdims=True)
        heads.append(_dot(p.astype(BF16), v_ref[0, :, h * hd:(h + 1) * hd]).astype(BF16))
    o_ref[...] = x1 + _dot(jnp.concatenate(heads, axis=-1), wo_ref[...])


def _post_mixer(x, ya, yb, wout, g, wq, kmem, vmem, wo, *, seq):
    t, d = x.shape
    tm = min(TOK_TILE, seq)
    nt = seq // tm
    m = kmem.shape[1]
    wa = ya.shape[1]
    return pl.pallas_call(
        _post_mixer_kernel,
        grid=(t // tm,),
        in_specs=[
            pl.BlockSpec((tm, d), lambda i: (i, 0)),
            pl.BlockSpec((tm, wa), lambda i: (i, 0)),
            pl.BlockSpec((tm, yb.shape[1]), lambda i: (i, 0)),
            _const_spec(wout.shape),
            _const_spec((1, d)),
            _const_spec(wq.shape),
            pl.BlockSpec((1, m, d), lambda i: (i // nt, 0, 0)),
            pl.BlockSpec((1, m, d), lambda i: (i // nt, 0, 0)),
            _const_spec(wo.shape),
        ],
        out_specs=pl.BlockSpec((tm, d), lambda i: (i, 0)),
        out_shape=jax.ShapeDtypeStruct((t, d), F32),
        compiler_params=_params("arbitrary"),
        name="post_mixer",
    )(x, ya, yb, wout, g, wq, kmem, vmem, wo)


def _ffn_kernel(x_ref, g_ref, wg_ref, wu_ref, wd_ref, o_ref):
    x = x_ref[...]
    h = _rmsnorm(x, g_ref[...]).astype(BF16)
    ff = wg_ref.shape[1]
    acc = x
    for c0 in range(0, ff, FFN_CHUNK):
        c1 = min(c0 + FFN_CHUNK, ff)
        a = _dot(h, wg_ref[:, c0:c1])
        u = _dot(h, wu_ref[:, c0:c1])
        acc = acc + _dot((jax.nn.silu(a) * u).astype(BF16), wd_ref[c0:c1, :])
    o_ref[...] = acc


def _ffn(x, g, wg, wu, wd, *, seq):
    t, d = x.shape
    tm = min(TOK_TILE, seq)
    return pl.pallas_call(
        _ffn_kernel,
        grid=(t // tm,),
        in_specs=[
            pl.BlockSpec((tm, d), lambda i: (i, 0)),
            _const_spec((1, d)),
            _const_spec(wg.shape),
            _const_spec(wu.shape),
            _const_spec(wd.shape),
        ],
        out_specs=pl.BlockSpec((tm, d), lambda i: (i, 0)),
        out_shape=jax.ShapeDtypeStruct((t, d), F32),
        compiler_params=_params("arbitrary"),
        name="ffn",
    )(x, g, wg, wu, wd)


META_I1, META_I2, META_R1, META_R2, META_G1, META_G2 = range(6)


def _router_kernel(x_ref, g_ref, wr_ref, h_ref, meta_ref, cnt_ref, carry_ref):
    i = pl.program_id(0)
    tm = x_ref.shape[0]
    h = _rmsnorm(x_ref[...], g_ref[...])
    h_ref[...] = h
    logits = jnp.dot(h, wr_ref[...], preferred_element_type=F32, precision=lax.Precision.HIGHEST)
    lane = lax.broadcasted_iota(I32, logits.shape, 1)
    logits = jnp.where(lane < N_EXPERTS, logits, NEG)
    m1 = jnp.max(logits, axis=-1, keepdims=True)
    i1 = jnp.min(jnp.where(logits == m1, lane, LANES), axis=-1, keepdims=True)
    rest = jnp.where(lane == i1, NEG, logits)
    m2 = jnp.max(rest, axis=-1, keepdims=True)
    i2 = jnp.min(jnp.where(rest == m2, lane, LANES), axis=-1, keepdims=True)
    e = jnp.exp(m2 - m1)
    g1 = 1.0 / (1.0 + e)
    g2 = e / (1.0 + e)

    sel = jnp.where((lane == i1) | (lane == i2), 1.0, 0.0)
    row = lax.broadcasted_iota(I32, (tm, tm), 0)
    col = lax.broadcasted_iota(I32, (tm, tm), 1)
    strict = jnp.where(col < row, 1.0, 0.0).astype(BF16)

    @pl.when(i == 0)
    def _():
        carry_ref[...] = jnp.zeros_like(carry_ref)

    rank = _dot(strict, sel.astype(BF16)) + carry_ref[0:1, :]
    total = rank[tm - 1:tm, :] + sel[tm - 1:tm, :]
    carry_ref[...] = jnp.broadcast_to(total, carry_ref.shape)
    cnt_ref[...] = jnp.broadcast_to(total, cnt_ref.shape)
    r1 = jnp.sum(jnp.where(lane == i1, rank, 0.0), axis=-1, keepdims=True)
    r2 = jnp.sum(jnp.where(lane == i2, rank, 0.0), axis=-1, keepdims=True)
    meta = jnp.zeros_like(logits)
    for slot, val in ((META_I1, i1.astype(F32)), (META_I2, i2.astype(F32)), (META_R1, r1),
                      (META_R2, r2), (META_G1, g1), (META_G2, g2)):
        meta = jnp.where(lane == slot, val, meta)
    meta_ref[...] = meta


def _router(x, g, wr, *, seq):
    t, d = x.shape
    tm = min(TOK_TILE, seq)
    return pl.pallas_call(
        _router_kernel,
        grid=(t // tm,),
        in_specs=[
            pl.BlockSpec((tm, d), lambda i: (i, 0)),
            _const_spec((1, d)),
            _const_spec(wr.shape),
        ],
        out_specs=[
            pl.BlockSpec((tm, d), lambda i: (i, 0)),
            pl.BlockSpec((tm, LANES), lambda i: (i, 0)),
            _const_spec((8, LANES)),
        ],
        out_shape=[
            jax.ShapeDtypeStruct((t, d), F32),
            jax.ShapeDtypeStruct((t, LANES), F32),
            jax.ShapeDtypeStruct((8, LANES), F32),
        ],
        scratch_shapes=[pltpu.VMEM((8, LANES), F32)],
        compiler_params=_params("arbitrary"),
        name="router",
    )(x, g, wr)


def _row_copy(src, src_row, dst, dst_row, sem):
    return pltpu.make_async_copy(src.at[pl.ds(src_row, 1)], dst.at[pl.ds(dst_row, 1)], sem)


def _dispatch_kernel(pos_ref, pad_ref, h_ref, xs_ref, zrow_ref, sem_ref, zsem_ref):
    c = pl.program_id(0)
    n_chunks = pl.num_programs(0)

    @pl.when(c == 0)
    def _():
        zrow_ref[...] = jnp.zeros_like(zrow_ref)
        for e in range(N_EXPERTS):
            start = pad_ref[2 * e]
            count = pad_ref[2 * e + 1] - start

            def zissue(r, carry, start=start):
                _row_copy(zrow_ref, 0, xs_ref, start + r, zsem_ref).start()
                return carry

            def zwait(r, carry):
                _row_copy(zrow_ref, 0, xs_ref, 0, zsem_ref).wait()
                return carry

            lax.fori_loop(0, count, zissue, 0)
            lax.fori_loop(0, count, zwait, 0)

        tail = pad_ref[2 * N_EXPERTS - 1]
        groups = (xs_ref.shape[0] - tail) // 8

        def tissue(r, carry):
            start = pl.multiple_of(tail + 8 * r, 8)
            pltpu.make_async_copy(zrow_ref, xs_ref.at[pl.ds(start, 8)], zsem_ref).start()
            return carry

        def twait(r, carry):
            pltpu.make_async_copy(zrow_ref, xs_ref.at[pl.ds(0, 8)], zsem_ref).wait()
            return carry

        lax.fori_loop(0, groups, tissue, 0)
        lax.fori_loop(0, groups, twait, 0)

    def issue(r, carry):
        tok = c * DISPATCH_CHUNK + r
        _row_copy(h_ref, tok, xs_ref, pos_ref[2 * tok], sem_ref.at[c % 2]).start()
        _row_copy(h_ref, tok, xs_ref, pos_ref[2 * tok + 1], sem_ref.at[c % 2]).start()
        return carry

    def drain(slot):
        def body(r, carry):
            _row_copy(h_ref, 0, xs_ref, 0, sem_ref.at[slot]).wait()
            _row_copy(h_ref, 0, xs_ref, 0, sem_ref.at[slot]).wait()
            return carry
        lax.fori_loop(0, DISPATCH_CHUNK, body, 0)

    lax.fori_loop(0, DISPATCH_CHUNK, issue, 0)

    @pl.when(c > 0)
    def _():
        drain((c + 1) % 2)

    @pl.when(c == n_chunks - 1)
    def _():
        drain(c % 2)


def _dispatch(pos, pad, h, n_rows):
    t, d = h.shape
    grid_spec = pltpu.PrefetchScalarGridSpec(
        num_scalar_prefetch=2,
        grid=(t // DISPATCH_CHUNK,),
        in_specs=[pl.BlockSpec(memory_space=pl.ANY)],
        out_specs=pl.BlockSpec(memory_space=pl.ANY),
        scratch_shapes=[
            pltpu.VMEM((8, d), F32),
            pltpu.SemaphoreType.DMA((2,)),
            pltpu.SemaphoreType.DMA(()),
        ],
    )
    return pl.pallas_call(
        _dispatch_kernel,
        grid_spec=grid_spec,
        out_shape=jax.ShapeDtypeStruct((n_rows, d), F32),
        compiler_params=_params("arbitrary"),
        name="moe_dispatch",
    )(pos, pad, h)


def _experts_kernel(te_ref, nused_ref, xs_ref, wg_ref, wu_ref, wd_ref, y_ref, xb_ref, acc_ref):
    p = pl.program_id(0)
    f = pl.program_id(1)

    @pl.when((p >= nused_ref[0]) & (f == 0))
    def _():
        y_ref[...] = jnp.zeros_like(y_ref)

    @pl.when(p < nused_ref[0])
    def _():
        @pl.when(f == 0)
        def _():
            xb_ref[...] = xs_ref[...].astype(BF16)
            acc_ref[...] = jnp.zeros_like(acc_ref)

        xb = xb_ref[...]
        a = _dot(xb, wg_ref[0])
        u = _dot(xb, wu_ref[0])
        acc_ref[...] += _dot((jax.nn.silu(a) * u).astype(BF16), wd_ref[0])

        @pl.when(f == pl.num_programs(1) - 1)
        def _():
            y_ref[...] = acc_ref[...]


def _experts(tile_expert, n_used, xs, wg, wu, wd):
    n_rows, d = xs.shape
    ff = wg.shape[2]
    tm = MOE_TILE
    tf = MOE_FF_CHUNK
    n_tiles = n_rows // tm

    def x_map(p, f, te, nu):
        return (jnp.minimum(p, nu[0] - 1), 0)

    grid_spec = pltpu.PrefetchScalarGridSpec(
        num_scalar_prefetch=2,
        grid=(n_tiles, ff // tf),
        in_specs=[
            pl.BlockSpec((tm, d), x_map),
            pl.BlockSpec((1, d, tf), lambda p, f, te, nu: (te[p], 0, f)),
            pl.BlockSpec((1, d, tf), lambda p, f, te, nu: (te[p], 0, f)),
            pl.BlockSpec((1, tf, d), lambda p, f, te, nu: (te[p], f, 0)),
        ],
        out_specs=pl.BlockSpec((tm, d), lambda p, f, te, nu: (p, 0)),
        scratch_shapes=[pltpu.VMEM((tm, d), BF16), pltpu.VMEM((tm, d), F32)],
    )
    return pl.pallas_call(
        _experts_kernel,
        grid_spec=grid_spec,
        out_shape=jax.ShapeDtypeStruct((n_rows, d), F32),
        compiler_params=_params("arbitrary", "arbitrary"),
        name="moe_experts",
    )(tile_expert, n_used, xs, wg, wu, wd)


def _combine_kernel(pos_ref, y_ref, x_ref, meta_ref, g_ref, o_ref, buf_ref, sem_ref):
    i = pl.program_id(0)
    n = pl.num_programs(0)
    tc = x_ref.shape[0]

    def fetch(tile, slot):
        def body(r, carry):
            tok = tile * tc + r
            for k in range(2):
                pltpu.make_async_copy(y_ref.at[pl.ds(pos_ref[2 * tok + k], 1)],
                                      buf_ref.at[slot, k, pl.ds(r, 1)], sem_ref.at[slot]).start()
            return carry
        lax.fori_loop(0, tc, body, 0)

    @pl.when(i == 0)
    def _():
        fetch(0, 0)

    @pl.when(i + 1 < n)
    def _():
        fetch(i + 1, (i + 1) % 2)

    slot = i % 2

    def drain(r, carry):
        for k in range(2):
            pltpu.make_async_copy(y_ref.at[pl.ds(0, 1)], buf_ref.at[slot, k, pl.ds(0, 1)],
                                  sem_ref.at[slot]).wait()
        return carry

    lax.fori_loop(0, tc, drain, 0)
    meta = meta_ref[...]
    g1 = meta[:, META_G1:META_G1 + 1]
    g2 = meta[:, META_G2:META_G2 + 1]
    x = x_ref[...] + g1 * buf_ref[slot, 0] + g2 * buf_ref[slot, 1]
    o_ref[...] = _rmsnorm(x, g_ref[...])


def _combine(pos, y, x, meta, g):
    t, d = x.shape
    tc = COMBINE_TILE
    grid_spec = pltpu.PrefetchScalarGridSpec(
        num_scalar_prefetch=1,
        grid=(t // tc,),
        in_specs=[
            pl.BlockSpec(memory_space=pl.ANY),
            pl.BlockSpec((tc, d), lambda i, pos: (i, 0)),
            pl.BlockSpec((tc, LANES), lambda i, pos: (i, 0)),
            pl.BlockSpec((1, d), lambda i, pos: (0, 0)),
        ],
        out_specs=pl.BlockSpec((tc, d), lambda i, pos: (i, 0)),
        scratch_shapes=[pltpu.VMEM((2, 2, tc, d), F32), pltpu.SemaphoreType.DMA((2,))],
    )
    return pl.pallas_call(
        _combine_kernel,
        grid_spec=grid_spec,
        out_shape=jax.ShapeDtypeStruct((t, d), F32),
        compiler_params=_params("arbitrary"),
        name="moe_combine",
    )(pos, y, x, meta, g)


def _moe_plan(meta, counts, n_tiles):
    cnt = counts[0, :N_EXPERTS].astype(I32)
    tiles = (cnt + MOE_TILE - 1) // MOE_TILE
    tile_end = jnp.cumsum(tiles)
    row_off = (tile_end - tiles) * MOE_TILE
    n_used = tile_end[-1:]
    experts = jnp.arange(N_EXPERTS, dtype=I32)

    def rows(idx_col, rank_col):
        idx = meta[:, idx_col].astype(I32)
        off = jnp.sum(jnp.where(idx[:, None] == experts[None, :], row_off[None, :], 0), axis=1)
        return off + meta[:, rank_col].astype(I32)

    pos = jnp.stack([rows(META_I1, META_R1), rows(META_I2, META_R2)], axis=1).reshape(-1)
    tile_ids = jnp.arange(n_tiles, dtype=I32)
    te = jnp.sum((tile_end[None, :] <= tile_ids[:, None]).astype(I32), axis=1)
    last = jnp.sum((tile_end <= n_used - 1).astype(I32))
    te = jnp.minimum(te, last).astype(I32)
    pad = jnp.stack([row_off + cnt, row_off + tiles * MOE_TILE], axis=1).reshape(-1).astype(I32)
    return pos.astype(I32), pad, te, n_used.astype(I32)


def _moe_and_final_norm(x, g_ffn, wr, wg, wu, wd, g_final, *, seq):
    t, d = x.shape
    n_tiles = 2 * t // MOE_TILE + N_EXPERTS
    h, meta, counts = _router(x, g_ffn, wr, seq=seq)
    pos, pad, te, n_used = _moe_plan(meta, counts, n_tiles)
    xs = _dispatch(pos, pad, h, n_tiles * MOE_TILE)
    y = _experts(te, n_used, xs, wg, wu, wd)
    return _combine(pos, y, x, meta, g_final)


def _row(v):
    return v.reshape(1, -1).astype(F32)


def kernel(x, mem, norm_mix, norm_mem_q, norm_mem_kv, norm_ffn, norm_final, even_w_in, fox_b_f, gmlp_v_gain, gmlp_w_s, gmlp_b_s, even_w_out, odd_w_in, diff_lambda_q1, diff_lambda_k1, diff_lambda_q2, diff_lambda_k2, diff_subln_gain, conv_w, odd_w_out, rel_bias, mem_w_q, mem_w_kv, mem_w_o, ffn_w_gate, ffn_w_up, ffn_w_down, router_w, moe_w_gate, moe_w_up, moe_w_down):
    batch, seq, d = x.shape
    t = batch * seq
    depth = norm_mix.shape[0]
    assert depth == 2 and seq % ATT_BLOCK == 0 and seq % TOK_TILE == 0
    xf = x.reshape(t, d)
    hd = d // N_HEADS
    aw = gmlp_v_gain.shape[1]
    n_blk = gmlp_w_s.shape[2]

    w_in = even_w_in[0]
    q0 = 2 * aw
    bw = N_HEADS * LANES
    scale = jnp.ones((w_in.shape[1],), F32).at[q0:q0 + bw].set(float(LANES) ** -0.5)
    w_in = jnp.pad(w_in * scale, ((0, 0), (0, LANES - N_HEADS))).astype(BF16)
    bf = jnp.pad(fox_b_f[0], (0, LANES - N_HEADS)).reshape(1, LANES)
    bs = jnp.broadcast_to(gmlp_b_s[0][:, :, None], (A_GROUPS, n_blk, aw // A_GROUPS)).astype(F32)
    ya, qkv, ccol = _even_in(xf, _row(norm_mix[0]), w_in, bf, _row(gmlp_v_gain[0]),
                             gmlp_w_s[0], bs, seq=seq)
    crow = ccol[:, :N_HEADS].reshape(batch, seq, N_HEADS).transpose(0, 2, 1)
    crow = crow.reshape(batch * N_HEADS, 1, seq)
    yb = _fox_attention(qkv, ccol, crow, batch=batch, seq=seq)

    def mem_attention(layer, x_in, ya_, yb_, w_out):
        km, vm = _mem_kv(mem, _row(norm_mem_kv[layer]), mem_w_kv[layer].astype(BF16))
        wq = (mem_w_q[layer] * float(hd) ** -0.5).astype(BF16)
        return _post_mixer(x_in, ya_, yb_, w_out.astype(BF16), _row(norm_mem_q[layer]), wq, km, vm,
                           mem_w_o[layer].astype(BF16), seq=seq)

    xf = mem_attention(0, xf, ya, yb, even_w_out[0])
    xf = _ffn(xf, _row(norm_ffn[0]), ffn_w_gate[0].astype(BF16), ffn_w_up[0].astype(BF16),
              ffn_w_down[0].astype(BF16), seq=seq)

    w_in = odd_w_in[0]
    qk_dim = LANES // 2
    qk_w = 2 * N_HEADS * qk_dim
    perm = np.arange(qk_w).reshape(2, N_HEADS, qk_dim).transpose(1, 0, 2).reshape(-1)
    cols = np.concatenate([perm, qk_w + perm, np.arange(2 * qk_w, w_in.shape[1])])
    scale = jnp.ones((w_in.shape[1],), F32).at[:qk_w].set(float(qk_dim) ** -0.5)
    w_in = (w_in * scale)[:, cols].astype(BF16)
    qkv, yd = _odd_in(xf, _row(norm_mix[1]), w_in, conv_w[0].astype(F32), seq=seq,
                      qw=2 * qk_w + N_HEADS * LANES)
    lam_init = 0.8 - 0.6 * math.exp(-0.3 * 1)
    lam = (jnp.exp(jnp.sum(diff_lambda_q1[0] * diff_lambda_k1[0]))
           - jnp.exp(jnp.sum(diff_lambda_q2[0] * diff_lambda_k2[0])) + lam_init)
    bias = _diff_bias_blocks(rel_bias, min(ATT_BLOCK, seq))
    yc = _diff_attention(qkv, lam.reshape(1, 1).astype(F32), bias, _row(diff_subln_gain[0]),
                         batch=batch, seq=seq, out_scale=1.0 - lam_init)
    xf = mem_attention(1, xf, yc, yd, odd_w_out[0])
    wr = jnp.pad(router_w[0], ((0, 0), (0, LANES - N_EXPERTS))).astype(F32)
    out = _moe_and_final_norm(xf, _row(norm_ffn[1]), wr, moe_w_gate[0].astype(BF16),
                              moe_w_up[0].astype(BF16), moe_w_down[0].astype(BF16),
                              _row(norm_final), seq=seq)
    return out.reshape(batch, seq, d)
```

```python
import functools
import math

import numpy as np
import jax
import jax.numpy as jnp
from jax import lax
from jax.experimental import pallas as pl
from jax.experimental.pallas import tpu as pltpu

F32 = jnp.float32
BF16 = jnp.bfloat16
I32 = jnp.int32

EPS = 1e-6
NEG = -1e30
LANES = 128
VMEM_LIMIT = 56 * 1024 * 1024

CHUNK = 64
A_GROUPS = 4
N_HEADS = 4
N_EXPERTS = 8
REL_BUCKETS = 32
REL_MAX_DIST = 128

TOK_TILE = 512
ATT_BLOCK = 512
MOE_TILE = 512
MOE_FF_CHUNK = 512
FFN_CHUNK = 512
DISPATCH_CHUNK = 256
COMBINE_TILE = 256


def _dot(a, b):
    return jnp.dot(a, b, preferred_element_type=F32)


def _dot_nt(a, b):
    return lax.dot_general(a, b, (((1,), (1,)), ((), ())), preferred_element_type=F32)


def _rmsnorm(x, g):
    return x * lax.rsqrt(jnp.mean(x * x, axis=-1, keepdims=True) + EPS) * g


def _params(*sem):
    return pltpu.CompilerParams(dimension_semantics=sem, vmem_limit_bytes=VMEM_LIMIT)


def _const_spec(shape):
    return pl.BlockSpec(shape, lambda *_: (0,) * len(shape))


def _split3(v):
    a1 = v.astype(BF16)
    r1 = v - a1.astype(F32)
    a2 = r1.astype(BF16)
    a3 = (r1 - a2.astype(F32)).astype(BF16)
    return a1, a2, a3


def _even_in_kernel(x_ref, g_ref, w_ref, bf_ref, vg_ref, ws_ref, bs_ref,
                    ya_ref, qkv_ref, c_ref, carry_ref, *, tiles_per_batch):
    i = pl.program_id(0)
    tm = x_ref.shape[0]
    aw = ya_ref.shape[1]
    qw = qkv_ref.shape[1]
    xn = _rmsnorm(x_ref[...], g_ref[...]).astype(BF16)

    qkv_ref[...] = _dot(xn, w_ref[:, 2 * aw:2 * aw + qw]).astype(BF16)

    fl = _dot(xn, w_ref[:, 2 * aw + qw:]) + bf_ref[...]
    ls = jnp.minimum(fl, 0.0) - jnp.log1p(jnp.exp(-jnp.abs(fl)))
    row = lax.broadcasted_iota(I32, (tm, tm), 0)
    col = lax.broadcasted_iota(I32, (tm, tm), 1)
    tri = jnp.where(col <= row, 1.0, 0.0).astype(BF16)
    a1, a2, a3 = _split3(ls)
    csum = _dot(tri, a1) + _dot(tri, a2) + _dot(tri, a3)

    @pl.when(i % tiles_per_batch == 0)
    def _():
        carry_ref[...] = jnp.zeros_like(carry_ref)

    csum = csum + carry_ref[0:1, :]
    c_ref[...] = csum
    carry_ref[...] = jnp.broadcast_to(csum[tm - 1:tm, :], carry_ref.shape)

    gu = jax.nn.gelu(_dot(xn, w_ref[:, 0:aw]), approximate=True)
    gv = jax.nn.gelu(_dot(xn, w_ref[:, aw:2 * aw]), approximate=True)
    blk = ws_ref.shape[1]
    ch = aw // A_GROUPS
    r = lax.broadcasted_iota(I32, (blk, blk), 0)
    c = lax.broadcasted_iota(I32, (blk, blk), 1)
    causal = (c // CHUNK) <= (r // CHUNK)
    for g in range(A_GROUPS):
        vgrp = gv[:, g * ch:(g + 1) * ch]
        vn = _rmsnorm(vgrp, vg_ref[:, g * ch:(g + 1) * ch]).astype(BF16)
        wmix = jnp.where(causal, ws_ref[g], 0.0).astype(BF16)
        for n in range(tm // blk):
            mixed = _dot(wmix, vn[n * blk:(n + 1) * blk, :]) + bs_ref[g]
            ya_ref[n * blk:(n + 1) * blk, g * ch:(g + 1) * ch] = (
                gu[n * blk:(n + 1) * blk, g * ch:(g + 1) * ch] * mixed).astype(BF16)


def _even_in(x, g, w, bf, vgain, ws, bs, *, seq):
    t, d = x.shape
    tm = min(TOK_TILE, seq)
    aw = vgain.shape[1]
    qw = w.shape[1] - 2 * aw - LANES
    blk = ws.shape[1]
    return pl.pallas_call(
        functools.partial(_even_in_kernel, tiles_per_batch=seq // tm),
        grid=(t // tm,),
        in_specs=[
            pl.BlockSpec((tm, d), lambda i: (i, 0)),
            _const_spec((1, d)),
            _const_spec(w.shape),
            _const_spec((1, LANES)),
            _const_spec((1, aw)),
            _const_spec(ws.shape),
            _const_spec(bs.shape),
        ],
        out_specs=[
            pl.BlockSpec((tm, aw), lambda i: (i, 0)),
            pl.BlockSpec((tm, qw), lambda i: (i, 0)),
            pl.BlockSpec((tm, LANES), lambda i: (i, 0)),
        ],
        out_shape=[
            jax.ShapeDtypeStruct((t, aw), BF16),
            jax.ShapeDtypeStruct((t, qw), BF16),
            jax.ShapeDtypeStruct((t, LANES), F32),
        ],
        scratch_shapes=[pltpu.VMEM((8, LANES), F32)],
        compiler_params=_params("arbitrary"),
        name="even_in",
    )(x, g, w, bf, vgain, ws, bs)


def _tri_pairs(nblk):
    qi = np.array([i for i in range(nblk) for _ in range(i + 1)], np.int32)
    kj = np.array([j for i in range(nblk) for j in range(i + 1)], np.int32)
    return jnp.asarray(qi), jnp.asarray(kj)


def _online_softmax_step(s, v, m_ref, l_ref, acc_ref):
    m_prev = m_ref[...]
    m_new = jnp.maximum(m_prev, jnp.max(s, axis=-1, keepdims=True))
    alpha = jnp.exp(m_prev - m_new)
    p = jnp.exp(s - m_new)
    l_ref[...] = alpha * l_ref[...] + jnp.sum(p, axis=-1, keepdims=True)
    acc_ref[...] = alpha * acc_ref[...] + _dot(p.astype(BF16), v)
    m_ref[...] = m_new


def _fox_kernel(qi_ref, kj_ref, q_ref, k_ref, v_ref, ccol_ref, crow_ref, o_ref,
                m_ref, l_ref, acc_ref, cq_ref):
    h = pl.program_id(1)
    step = pl.program_id(2)
    i = qi_ref[step]
    j = kj_ref[step]
    tq, tk = q_ref.shape[0], k_ref.shape[0]

    @pl.when(j == 0)
    def _():
        m_ref[...] = jnp.full_like(m_ref, NEG)
        l_ref[...] = jnp.zeros_like(l_ref)
        acc_ref[...] = jnp.zeros_like(acc_ref)
        lane = lax.broadcasted_iota(I32, ccol_ref.shape, 1)
        cq_ref[...] = jnp.sum(jnp.where(lane == h, ccol_ref[...], 0.0), axis=-1, keepdims=True)

    s = _dot_nt(q_ref[...], k_ref[...]) + cq_ref[...] - crow_ref[0]

    @pl.when(j < i)
    def _():
        _online_softmax_step(s, v_ref[...], m_ref, l_ref, acc_ref)

    @pl.when(j == i)
    def _():
        row = lax.broadcasted_iota(I32, (tq, tk), 0)
        col = lax.broadcasted_iota(I32, (tq, tk), 1)
        _online_softmax_step(jnp.where(col <= row, s, NEG), v_ref[...], m_ref, l_ref, acc_ref)
        o_ref[...] = (acc_ref[...] / l_ref[...]).astype(o_ref.dtype)


def _fox_attention(qkv, ccol, crow, *, batch, seq):
    t = qkv.shape[0]
    hd = LANES
    blk = min(ATT_BLOCK, seq)
    nblk = seq // blk
    qi, kj = _tri_pairs(nblk)
    grid_spec = pltpu.PrefetchScalarGridSpec(
        num_scalar_prefetch=2,
        grid=(batch, N_HEADS, qi.shape[0]),
        in_specs=[
            pl.BlockSpec((blk, hd), lambda b, h, s, qi, kj: (b * nblk + qi[s], h)),
            pl.BlockSpec((blk, hd), lambda b, h, s, qi, kj: (b * nblk + kj[s], N_HEADS + h)),
            pl.BlockSpec((blk, hd), lambda b, h, s, qi, kj: (b * nblk + kj[s], 2 * N_HEADS + h)),
            pl.BlockSpec((blk, LANES), lambda b, h, s, qi, kj: (b * nblk + qi[s], 0)),
            pl.BlockSpec((1, 1, blk), lambda b, h, s, qi, kj: (b * N_HEADS + h, 0, kj[s])),
        ],
        out_specs=pl.BlockSpec((blk, hd), lambda b, h, s, qi, kj: (b * nblk + qi[s], h)),
        scratch_shapes=[
            pltpu.VMEM((blk, 1), F32),
            pltpu.VMEM((blk, 1), F32),
            pltpu.VMEM((blk, hd), F32),
            pltpu.VMEM((blk, 1), F32),
        ],
    )
    return pl.pallas_call(
        _fox_kernel,
        grid_spec=grid_spec,
        out_shape=jax.ShapeDtypeStruct((t, N_HEADS * hd), BF16),
        compiler_params=_params("arbitrary", "arbitrary", "arbitrary"),
        name="fox_attention",
    )(qi, kj, qkv, qkv, qkv, ccol, crow)


def _diff_kernel(qi_ref, kj_ref, lam_ref, q_ref, k_ref, v_ref, bias_ref, gain_ref, o_ref,
                 qs_ref, m_ref, l_ref, acc_ref, *, out_scale):
    step = pl.program_id(2)
    i = qi_ref[step]
    j = kj_ref[step]
    tq = q_ref.shape[0]
    half = q_ref.shape[1] // 2

    @pl.when(j == 0)
    def _():
        m_ref[...] = jnp.full_like(m_ref, NEG)
        l_ref[...] = jnp.zeros_like(l_ref)
        acc_ref[...] = jnp.zeros_like(acc_ref)
        q = q_ref[...]
        lane = lax.broadcasted_iota(I32, q.shape, 1)
        zero = jnp.zeros_like(q)
        qs_ref[0:tq, :] = jnp.where(lane < half, q, zero)
        qs_ref[tq:2 * tq, :] = jnp.where(lane < half, zero, q)

    s = _dot_nt(qs_ref[...], k_ref[...])

    @pl.when(j < i - 1)
    def _():
        _online_softmax_step(s, v_ref[...], m_ref, l_ref, acc_ref)

    @pl.when(j == i - 1)
    def _():
        b = bias_ref[0, 1]
        _online_softmax_step(s + jnp.concatenate([b, b], axis=0), v_ref[...], m_ref, l_ref, acc_ref)

    @pl.when(j == i)
    def _():
        b = bias_ref[0, 0]
        _online_softmax_step(s + jnp.concatenate([b, b], axis=0), v_ref[...], m_ref, l_ref, acc_ref)
        o = acc_ref[...] / l_ref[...]
        o = o[0:tq, :] - lam_ref[0, 0] * o[tq:2 * tq, :]
        o_ref[...] = (_rmsnorm(o, gain_ref[...]) * out_scale).astype(o_ref.dtype)


def _diff_attention(qkv, lam, bias, gain, *, batch, seq, out_scale):
    t = qkv.shape[0]
    hd = LANES
    blk = bias.shape[2]
    nblk = seq // blk
    qi, kj = _tri_pairs(nblk)
    grid_spec = pltpu.PrefetchScalarGridSpec(
        num_scalar_prefetch=2,
        grid=(batch, N_HEADS, qi.shape[0]),
        in_specs=[
            pl.BlockSpec(memory_space=pltpu.SMEM),
            pl.BlockSpec((blk, hd), lambda b, h, s, qi, kj: (b * nblk + qi[s], h)),
            pl.BlockSpec((blk, hd), lambda b, h, s, qi, kj: (b * nblk + kj[s], N_HEADS + h)),
            pl.BlockSpec((blk, hd), lambda b, h, s, qi, kj: (b * nblk + kj[s], 2 * N_HEADS + h)),
            pl.BlockSpec((1, 2, blk, blk), lambda b, h, s, qi, kj: (h, 0, 0, 0)),
            pl.BlockSpec((1, hd), lambda b, h, s, qi, kj: (0, h)),
        ],
        out_specs=pl.BlockSpec((blk, hd), lambda b, h, s, qi, kj: (b * nblk + qi[s], h)),
        scratch_shapes=[
            pltpu.VMEM((2 * blk, hd), BF16),
            pltpu.VMEM((2 * blk, 1), F32),
            pltpu.VMEM((2 * blk, 1), F32),
            pltpu.VMEM((2 * blk, hd), F32),
        ],
    )
    return pl.pallas_call(
        functools.partial(_diff_kernel, out_scale=out_scale),
        grid_spec=grid_spec,
        out_shape=jax.ShapeDtypeStruct((t, N_HEADS * hd), BF16),
        compiler_params=_params("arbitrary", "arbitrary", "arbitrary"),
        name="diff_attention",
    )(qi, kj, lam, qkv, qkv, qkv, bias, gain)


def _rel_bucket(rel):
    n_half = REL_BUCKETS // 2
    max_exact = n_half // 2
    ret = jnp.where(rel > 0, n_half, 0)
    n = jnp.abs(rel)
    nf = jnp.maximum(n, 1).astype(F32)
    large = max_exact + (jnp.log(nf / max_exact) / math.log(REL_MAX_DIST / max_exact)
                         * (n_half - max_exact)).astype(I32)
    large = jnp.minimum(large, n_half - 1)
    return ret + jnp.where(n < max_exact, n, large)


def _rel_bias_kernel(table_ref, idx_ref, o_ref):
    h = pl.program_id(0)
    idx = idx_ref[0]
    far = table_ref[REL_BUCKETS // 2 - 1, h]
    out = jnp.full(idx.shape, NEG, F32)
    for b in range(REL_BUCKETS):
        out = jnp.where(idx == b, table_ref[b, h] - far, out)
    o_ref[0, 0] = out


def _diff_bias_blocks(rel_table, blk):
    assert blk >= REL_MAX_DIST
    r = jnp.arange(blk)[:, None]
    c = jnp.arange(blk)[None, :]
    diag = jnp.where((c // CHUNK) <= (r // CHUNK), _rel_bucket(c - r), -1)
    prev = _rel_bucket(c - r - blk)
    idx = jnp.stack([diag, prev], axis=0).astype(I32)
    n_heads = rel_table.shape[1]
    return pl.pallas_call(
        _rel_bias_kernel,
        grid=(n_heads, 2),
        in_specs=[
            pl.BlockSpec(memory_space=pltpu.SMEM),
            pl.BlockSpec((1, blk, blk), lambda h, s: (s, 0, 0)),
        ],
        out_specs=pl.BlockSpec((1, 1, blk, blk), lambda h, s: (h, s, 0, 0)),
        out_shape=jax.ShapeDtypeStruct((n_heads, 2, blk, blk), F32),
        compiler_params=_params("arbitrary", "arbitrary"),
        name="rel_bias",
    )(rel_table.astype(F32), idx)


def _odd_in_kernel(x_ref, g_ref, w_ref, cw_ref, qkv_ref, yd_ref, zbuf_ref, *, tiles_per_batch):
    i = pl.program_id(0)
    tm = x_ref.shape[0]
    qw = qkv_ref.shape[1]
    dw = yd_ref.shape[1]
    xn = _rmsnorm(x_ref[...], g_ref[...]).astype(BF16)
    qkv_ref[...] = _dot(xn, w_ref[:, 0:qw]).astype(BF16)
    hh = _dot(xn, w_ref[:, qw:qw + dw])
    gb = _dot(xn, w_ref[:, qw + dw:qw + 2 * dw])
    gc = _dot(xn, w_ref[:, qw + 2 * dw:qw + 3 * dw])
    z = gc * hh

    @pl.when(i % tiles_per_batch == 0)
    def _():
        zbuf_ref[0:8, :] = jnp.zeros((8, dw), F32)

    @pl.when(i % tiles_per_batch != 0)
    def _():
        zbuf_ref[0:8, :] = zbuf_ref[tm:tm + 8, :]

    zbuf_ref[8:tm + 8, :] = z
    y = (cw_ref[0:1, :] * zbuf_ref[6:tm + 6, :] + cw_ref[1:2, :] * zbuf_ref[7:tm + 7, :]
         + cw_ref[2:3, :] * z)
    yd_ref[...] = (gb * y).astype(BF16)


def _odd_in(x, g, w, cw, *, seq, qw):
    t, d = x.shape
    tm = min(TOK_TILE, seq)
    dw = cw.shape[1]
    return pl.pallas_call(
        functools.partial(_odd_in_kernel, tiles_per_batch=seq // tm),
        grid=(t // tm,),
        in_specs=[
            pl.BlockSpec((tm, d), lambda i: (i, 0)),
            _const_spec((1, d)),
            _const_spec(w.shape),
            _const_spec(cw.shape),
        ],
        out_specs=[
            pl.BlockSpec((tm, qw), lambda i: (i, 0)),
            pl.BlockSpec((tm, dw), lambda i: (i, 0)),
        ],
        out_shape=[
            jax.ShapeDtypeStruct((t, qw), BF16),
            jax.ShapeDtypeStruct((t, dw), BF16),
        ],
        scratch_shapes=[pltpu.VMEM((tm + 8, dw), F32)],
        compiler_params=_params("arbitrary"),
        name="odd_in",
    )(x, g, w, cw)


def _mem_kv_kernel(mem_ref, g_ref, w_ref, k_ref, v_ref):
    d = mem_ref.shape[2]
    mn = _rmsnorm(mem_ref[0], g_ref[...]).astype(BF16)
    k_ref[0] = _dot(mn, w_ref[:, 0:d]).astype(BF16)
    v_ref[0] = _dot(mn, w_ref[:, d:2 * d]).astype(BF16)


def _mem_kv(mem, g, w):
    b, m, d = mem.shape
    return pl.pallas_call(
        _mem_kv_kernel,
        grid=(b,),
        in_specs=[
            pl.BlockSpec((1, m, d), lambda i: (i, 0, 0)),
            _const_spec((1, d)),
            _const_spec(w.shape),
        ],
        out_specs=[pl.BlockSpec((1, m, d), lambda i: (i, 0, 0))] * 2,
        out_shape=[jax.ShapeDtypeStruct((b, m, d), BF16)] * 2,
        compiler_params=_params("arbitrary"),
        name="mem_kv",
    )(mem, g, w)


def _post_mixer_kernel(x_ref, ya_ref, yb_ref, wout_ref, g_ref, wq_ref, k_ref, v_ref, wo_ref, o_ref):
    wa = ya_ref.shape[1]
    d = x_ref.shape[1]
    hd = d // N_HEADS
    x1 = x_ref[...] + _dot(ya_ref[...], wout_ref[0:wa, :]) + _dot(yb_ref[...], wout_ref[wa:, :])
    xn = _rmsnorm(x1, g_ref[...]).astype(BF16)
    q = _dot(xn, wq_ref[...]).astype(BF16)
    heads = []
    for h in range(N_HEADS):
        lg = _dot_nt(q[:, h * hd:(h + 1) * hd], k_ref[0, :, h * hd:(h + 1) * hd])
        p = jnp.exp(lg - jnp.max(lg, axis=-1, keepdims=True))
        p = p / jnp.sum(p, axis=-1, keepdims=True)
        heads.append(_dot(p.astype(BF16), v_ref[0, :, h * hd:(h + 1) * hd]).astype(BF16))
    o_ref[...] = x1 + _dot(jnp.concatenate(heads, axis=-1), wo_ref[...])


def _post_mixer(x, ya, yb, wout, g, wq, kmem, vmem, wo, *, seq):
    t, d = x.shape
    tm = min(TOK_TILE, seq)
    nt = seq // tm
    m = kmem.shape[1]
    wa = ya.shape[1]
    return pl.pallas_call(
        _post_mixer_kernel,
        grid=(t // tm,),
        in_specs=[
            pl.BlockSpec((tm, d), lambda i: (i, 0)),
            pl.BlockSpec((tm, wa), lambda i: (i, 0)),
            pl.BlockSpec((tm, yb.shape[1]), lambda i: (i, 0)),
            _const_spec(wout.shape),
            _const_spec((1, d)),
            _const_spec(wq.shape),
            pl.BlockSpec((1, m, d), lambda i: (i // nt, 0, 0)),
            pl.BlockSpec((1, m, d), lambda i: (i // nt, 0, 0)),
            _const_spec(wo.shape),
        ],
        out_specs=pl.BlockSpec((tm, d), lambda i: (i, 0)),
        out_shape=jax.ShapeDtypeStruct((t, d), F32),
        compiler_params=_params("arbitrary"),
        name="post_mixer",
    )(x, ya, yb, wout, g, wq, kmem, vmem, wo)


def _ffn_kernel(x_ref, g_ref, wg_ref, wu_ref, wd_ref, o_ref):
    x = x_ref[...]
    h = _rmsnorm(x, g_ref[...]).astype(BF16)
    ff = wg_ref.shape[1]
    acc = x
    for c0 in range(0, ff, FFN_CHUNK):
        c1 = min(c0 + FFN_CHUNK, ff)
        a = _dot(h, wg_ref[:, c0:c1])
        u = _dot(h, wu_ref[:, c0:c1])
        acc = acc + _dot((jax.nn.silu(a) * u).astype(BF16), wd_ref[c0:c1, :])
    o_ref[...] = acc


def _ffn(x, g, wg, wu, wd, *, seq):
    t, d = x.shape
    tm = min(TOK_TILE, seq)
    return pl.pallas_call(
        _ffn_kernel,
        grid=(t // tm,),
        in_specs=[
            pl.BlockSpec((tm, d), lambda i: (i, 0)),
            _const_spec((1, d)),
            _const_spec(wg.shape),
            _const_spec(wu.shape),
            _const_spec(wd.shape),
        ],
        out_specs=pl.BlockSpec((tm, d), lambda i: (i, 0)),
        out_shape=jax.ShapeDtypeStruct((t, d), F32),
        compiler_params=_params("arbitrary"),
        name="ffn",
    )(x, g, wg, wu, wd)


META_I1, META_I2, META_R1, META_R2, META_G1, META_G2 = range(6)


def _router_kernel(x_ref, g_ref, wr_ref, meta_ref, cnt_ref, carry_ref):
    i = pl.program_id(0)
    tm = x_ref.shape[0]
    h = _rmsnorm(x_ref[...], g_ref[...])
    logits = jnp.dot(h, wr_ref[...], preferred_element_type=F32, precision=lax.Precision.HIGHEST)
    lane = lax.broadcasted_iota(I32, logits.shape, 1)
    logits = jnp.where(lane < N_EXPERTS, logits, NEG)
    m1 = jnp.max(logits, axis=-1, keepdims=True)
    i1 = jnp.min(jnp.where(logits == m1, lane, LANES), axis=-1, keepdims=True)
    rest = jnp.where(lane == i1, NEG, logits)
    m2 = jnp.max(rest, axis=-1, keepdims=True)
    i2 = jnp.min(jnp.where(rest == m2, lane, LANES), axis=-1, keepdims=True)
    e = jnp.exp(m2 - m1)
    g1 = 1.0 / (1.0 + e)
    g2 = e / (1.0 + e)

    sel = jnp.where((lane == i1) | (lane == i2), 1.0, 0.0)
    row = lax.broadcasted_iota(I32, (tm, tm), 0)
    col = lax.broadcasted_iota(I32, (tm, tm), 1)
    strict = jnp.where(col < row, 1.0, 0.0).astype(BF16)

    @pl.when(i == 0)
    def _():
        carry_ref[...] = jnp.zeros_like(carry_ref)

    rank = _dot(strict, sel.astype(BF16)) + carry_ref[0:1, :]
    total = rank[tm - 1:tm, :] + sel[tm - 1:tm, :]
    carry_ref[...] = jnp.broadcast_to(total, carry_ref.shape)
    cnt_ref[...] = jnp.broadcast_to(total, cnt_ref.shape)
    r1 = jnp.sum(jnp.where(lane == i1, rank, 0.0), axis=-1, keepdims=True)
    r2 = jnp.sum(jnp.where(lane == i2, rank, 0.0), axis=-1, keepdims=True)
    meta = jnp.zeros_like(logits)
    for slot, val in ((META_I1, i1.astype(F32)), (META_I2, i2.astype(F32)), (META_R1, r1),
                      (META_R2, r2), (META_G1, g1), (META_G2, g2)):
        meta = jnp.where(lane == slot, val, meta)
    meta_ref[...] = meta


def _router(x, g, wr, *, seq):
    t, d = x.shape
    tm = min(TOK_TILE, seq)
    return pl.pallas_call(
        _router_kernel,
        grid=(t // tm,),
        in_specs=[
            pl.BlockSpec((tm, d), lambda i: (i, 0)),
            _const_spec((1, d)),
            _const_spec(wr.shape),
        ],
        out_specs=[
            pl.BlockSpec((tm, LANES), lambda i: (i, 0)),
            _const_spec((8, LANES)),
        ],
        out_shape=[
            jax.ShapeDtypeStruct((t, LANES), F32),
            jax.ShapeDtypeStruct((8, LANES), F32),
        ],
        scratch_shapes=[pltpu.VMEM((8, LANES), F32)],
        compiler_params=_params("arbitrary"),
        name="router",
    )(x, g, wr)


def _row_copy(src, src_row, dst, dst_row, sem):
    return pltpu.make_async_copy(src.at[pl.ds(src_row, 1)], dst.at[pl.ds(dst_row, 1)], sem)


def _dispatch_kernel(pos_ref, pad_ref, x_ref, g_ref, xs_ref, h_ref, zrow_ref, sem_ref, zsem_ref):
    c = pl.program_id(0)
    n_chunks = pl.num_programs(0)
    tc = x_ref.shape[0]
    slot = c % 2

    @pl.when(c == 0)
    def _():
        zrow_ref[...] = jnp.zeros_like(zrow_ref)
        for e in range(N_EXPERTS):
            start = pad_ref[2 * e]
            count = pad_ref[2 * e + 1] - start

            def zissue(r, carry, start=start):
                _row_copy(zrow_ref, 0, xs_ref, start + r, zsem_ref).start()
                return carry

            def zwait(r, carry):
                _row_copy(zrow_ref, 0, xs_ref, 0, zsem_ref).wait()
                return carry

            lax.fori_loop(0, count, zissue, 0)
            lax.fori_loop(0, count, zwait, 0)

        tail = pad_ref[2 * N_EXPERTS - 1]
        groups = (xs_ref.shape[0] - tail) // 8

        def tissue(r, carry):
            start = pl.multiple_of(tail + 8 * r, 8)
            pltpu.make_async_copy(zrow_ref, xs_ref.at[pl.ds(start, 8)], zsem_ref).start()
            return carry

        def twait(r, carry):
            pltpu.make_async_copy(zrow_ref, xs_ref.at[pl.ds(0, 8)], zsem_ref).wait()
            return carry

        lax.fori_loop(0, groups, tissue, 0)
        lax.fori_loop(0, groups, twait, 0)

    h_ref[slot] = _rmsnorm(x_ref[...], g_ref[...])

    def issue(r, carry):
        tok = c * tc + r
        _row_copy(h_ref.at[slot], r, xs_ref, pos_ref[2 * tok], sem_ref.at[slot]).start()
        _row_copy(h_ref.at[slot], r, xs_ref, pos_ref[2 * tok + 1], sem_ref.at[slot]).start()
        return carry

    def drain(s):
        def body(r, carry):
            _row_copy(h_ref.at[s], 0, xs_ref, 0, sem_ref.at[s]).wait()
            _row_copy(h_ref.at[s], 0, xs_ref, 0, sem_ref.at[s]).wait()
            return carry
        lax.fori_loop(0, tc, body, 0)

    lax.fori_loop(0, tc, issue, 0)

    @pl.when(c > 0)
    def _():
        drain(1 - slot)

    @pl.when(c == n_chunks - 1)
    def _():
        drain(slot)


def _dispatch(pos, pad, x, g, n_rows):
    t, d = x.shape
    tc = DISPATCH_CHUNK
    grid_spec = pltpu.PrefetchScalarGridSpec(
        num_scalar_prefetch=2,
        grid=(t // tc,),
        in_specs=[
            pl.BlockSpec((tc, d), lambda c, pos, pad: (c, 0)),
            pl.BlockSpec((1, d), lambda c, pos, pad: (0, 0)),
        ],
        out_specs=pl.BlockSpec(memory_space=pl.ANY),
        scratch_shapes=[
            pltpu.VMEM((2, tc, d), F32),
            pltpu.VMEM((8, d), F32),
            pltpu.SemaphoreType.DMA((2,)),
            pltpu.SemaphoreType.DMA(()),
        ],
    )
    return pl.pallas_call(
        _dispatch_kernel,
        grid_spec=grid_spec,
        out_shape=jax.ShapeDtypeStruct((n_rows, d), F32),
        compiler_params=_params("arbitrary"),
        name="moe_dispatch",
    )(pos, pad, x, g)


def _experts_kernel(te_ref, nused_ref, xs_ref, wg_ref, wu_ref, wd_ref, y_ref, xb_ref, acc_ref):
    p = pl.program_id(0)
    f = pl.program_id(1)

    @pl.when((p >= nused_ref[0]) & (f == 0))
    def _():
        y_ref[...] = jnp.zeros_like(y_ref)

    @pl.when(p < nused_ref[0])
    def _():
        @pl.when(f == 0)
        def _():
            xb_ref[...] = xs_ref[...].astype(BF16)
            acc_ref[...] = jnp.zeros_like(acc_ref)

        xb = xb_ref[...]
        a = _dot(xb, wg_ref[0])
        u = _dot(xb, wu_ref[0])
        acc_ref[...] += _dot((jax.nn.silu(a) * u).astype(BF16), wd_ref[0])

        @pl.when(f == pl.num_programs(1) - 1)
        def _():
            y_ref[...] = acc_ref[...]


def _experts(tile_expert, n_used, xs, wg, wu, wd):
    n_rows, d = xs.shape
    ff = wg.shape[2]
    tm = MOE_TILE
    tf = MOE_FF_CHUNK
    n_tiles = n_rows // tm

    def x_map(p, f, te, nu):
        return (jnp.minimum(p, nu[0] - 1), 0)

    grid_spec = pltpu.PrefetchScalarGridSpec(
        num_scalar_prefetch=2,
        grid=(n_tiles, ff // tf),
        in_specs=[
            pl.BlockSpec((tm, d), x_map),
            pl.BlockSpec((1, d, tf), lambda p, f, te, nu: (te[p], 0, f)),
            pl.BlockSpec((1, d, tf), lambda p, f, te, nu: (te[p], 0, f)),
            pl.BlockSpec((1, tf, d), lambda p, f, te, nu: (te[p], f, 0)),
        ],
        out_specs=pl.BlockSpec((tm, d), lambda p, f, te, nu: (p, 0)),
        scratch_shapes=[pltpu.VMEM((tm, d), BF16), pltpu.VMEM((tm, d), F32)],
    )
    return pl.pallas_call(
        _experts_kernel,
        grid_spec=grid_spec,
        out_shape=jax.ShapeDtypeStruct((n_rows, d), F32),
        compiler_params=_params("arbitrary", "arbitrary"),
        name="moe_experts",
    )(tile_expert, n_used, xs, wg, wu, wd)


def _combine_kernel(pos_ref, y_ref, x_ref, meta_ref, g_ref, o_ref, buf_ref, sem_ref):
    i = pl.program_id(0)
    n = pl.num_programs(0)
    tc = x_ref.shape[0]

    def fetch(tile, slot):
        def body(r, carry):
            tok = tile * tc + r
            for k in range(2):
                pltpu.make_async_copy(y_ref.at[pl.ds(pos_ref[2 * tok + k], 1)],
                                      buf_ref.at[slot, k, pl.ds(r, 1)], sem_ref.at[slot]).start()
            return carry
        lax.fori_loop(0, tc, body, 0)

    @pl.when(i == 0)
    def _():
        fetch(0, 0)

    @pl.when(i + 1 < n)
    def _():
        fetch(i + 1, (i + 1) % 2)

    slot = i % 2

    def drain(r, carry):
        for k in range(2):
            pltpu.make_async_copy(y_ref.at[pl.ds(0, 1)], buf_ref.at[slot, k, pl.ds(0, 1)],
                                  sem_ref.at[slot]).wait()
        return carry

    lax.fori_loop(0, tc, drain, 0)
    meta = meta_ref[...]
    g1 = meta[:, META_G1:META_G1 + 1]
    g2 = meta[:, META_G2:META_G2 + 1]
    x = x_ref[...] + g1 * buf_ref[slot, 0] + g2 * buf_ref[slot, 1]
    o_ref[...] = _rmsnorm(x, g_ref[...])


def _combine(pos, y, x, meta, g):
    t, d = x.shape
    tc = COMBINE_TILE
    grid_spec = pltpu.PrefetchScalarGridSpec(
        num_scalar_prefetch=1,
        grid=(t // tc,),
        in_specs=[
            pl.BlockSpec(memory_space=pl.ANY),
            pl.BlockSpec((tc, d), lambda i, pos: (i, 0)),
            pl.BlockSpec((tc, LANES), lambda i, pos: (i, 0)),
            pl.BlockSpec((1, d), lambda i, pos: (0, 0)),
        ],
        out_specs=pl.BlockSpec((tc, d), lambda i, pos: (i, 0)),
        scratch_shapes=[pltpu.VMEM((2, 2, tc, d), F32), pltpu.SemaphoreType.DMA((2,))],
    )
    return pl.pallas_call(
        _combine_kernel,
        grid_spec=grid_spec,
        out_shape=jax.ShapeDtypeStruct((t, d), F32),
        compiler_params=_params("arbitrary"),
        name="moe_combine",
    )(pos, y, x, meta, g)


def _moe_plan(meta, counts, n_tiles):
    cnt = counts[0, :N_EXPERTS].astype(I32)
    tiles = (cnt + MOE_TILE - 1) // MOE_TILE
    tile_end = jnp.cumsum(tiles)
    row_off = (tile_end - tiles) * MOE_TILE
    n_used = tile_end[-1:]
    experts = jnp.arange(N_EXPERTS, dtype=I32)

    def rows(idx_col, rank_col):
        idx = meta[:, idx_col].astype(I32)
        off = jnp.sum(jnp.where(idx[:, None] == experts[None, :], row_off[None, :], 0), axis=1)
        return off + meta[:, rank_col].astype(I32)

    pos = jnp.stack([rows(META_I1, META_R1), rows(META_I2, META_R2)], axis=1).reshape(-1)
    tile_ids = jnp.arange(n_tiles, dtype=I32)
    te = jnp.sum((tile_end[None, :] <= tile_ids[:, None]).astype(I32), axis=1)
    last = jnp.sum((tile_end <= n_used - 1).astype(I32))
    te = jnp.minimum(te, last).astype(I32)
    pad = jnp.stack([row_off + cnt, row_off + tiles * MOE_TILE], axis=1).reshape(-1).astype(I32)
    return pos.astype(I32), pad, te, n_used.astype(I32)


def _moe_and_final_norm(x, g_ffn, wr, wg, wu, wd, g_final, *, seq):
    t, d = x.shape
    n_tiles = 2 * t // MOE_TILE + N_EXPERTS
    meta, counts = _router(x, g_ffn, wr, seq=seq)
    pos, pad, te, n_used = _moe_plan(meta, counts, n_tiles)
    xs = _dispatch(pos, pad, x, g_ffn, n_tiles * MOE_TILE)
    y = _experts(te, n_used, xs, wg, wu, wd)
    return _combine(pos, y, x, meta, g_final)


def _row(v):
    return v.reshape(1, -1).astype(F32)


def kernel(x, mem, norm_mix, norm_mem_q, norm_mem_kv, norm_ffn, norm_final, even_w_in, fox_b_f, gmlp_v_gain, gmlp_w_s, gmlp_b_s, even_w_out, odd_w_in, diff_lambda_q1, diff_lambda_k1, diff_lambda_q2, diff_lambda_k2, diff_subln_gain, conv_w, odd_w_out, rel_bias, mem_w_q, mem_w_kv, mem_w_o, ffn_w_gate, ffn_w_up, ffn_w_down, router_w, moe_w_gate, moe_w_up, moe_w_down):
    batch, seq, d = x.shape
    t = batch * seq
    depth = norm_mix.shape[0]
    assert depth == 2 and seq % ATT_BLOCK == 0 and seq % TOK_TILE == 0
    xf = x.reshape(t, d)
    hd = d // N_HEADS
    aw = gmlp_v_gain.shape[1]
    n_blk = gmlp_w_s.shape[2]

    w_in = even_w_in[0]
    q0 = 2 * aw
    bw = N_HEADS * LANES
    scale = jnp.ones((w_in.shape[1],), F32).at[q0:q0 + bw].set(float(LANES) ** -0.5)
    w_in = jnp.pad(w_in * scale, ((0, 0), (0, LANES - N_HEADS))).astype(BF16)
    bf = jnp.pad(fox_b_f[0], (0, LANES - N_HEADS)).reshape(1, LANES)
    bs = jnp.broadcast_to(gmlp_b_s[0][:, :, None], (A_GROUPS, n_blk, aw // A_GROUPS)).astype(F32)
    ya, qkv, ccol = _even_in(xf, _row(norm_mix[0]), w_in, bf, _row(gmlp_v_gain[0]),
                             gmlp_w_s[0], bs, seq=seq)
    crow = ccol[:, :N_HEADS].reshape(batch, seq, N_HEADS).transpose(0, 2, 1)
    crow = crow.reshape(batch * N_HEADS, 1, seq)
    yb = _fox_attention(qkv, ccol, crow, batch=batch, seq=seq)

    def mem_attention(layer, x_in, ya_, yb_, w_out):
        km, vm = _mem_kv(mem, _row(norm_mem_kv[layer]), mem_w_kv[layer].astype(BF16))
        wq = (mem_w_q[layer] * float(hd) ** -0.5).astype(BF16)
        return _post_mixer(x_in, ya_, yb_, w_out.astype(BF16), _row(norm_mem_q[layer]), wq, km, vm,
                           mem_w_o[layer].astype(BF16), seq=seq)

    xf = mem_attention(0, xf, ya, yb, even_w_out[0])
    xf = _ffn(xf, _row(norm_ffn[0]), ffn_w_gate[0].astype(BF16), ffn_w_up[0].astype(BF16),
              ffn_w_down[0].astype(BF16), seq=seq)

    w_in = odd_w_in[0]
    qk_dim = LANES // 2
    qk_w = 2 * N_HEADS * qk_dim
    perm = np.arange(qk_w).reshape(2, N_HEADS, qk_dim).transpose(1, 0, 2).reshape(-1)
    cols = np.concatenate([perm, qk_w + perm, np.arange(2 * qk_w, w_in.shape[1])])
    scale = jnp.ones((w_in.shape[1],), F32).at[:qk_w].set(float(qk_dim) ** -0.5)
    w_in = (w_in * scale)[:, cols].astype(BF16)
    qkv, yd = _odd_in(xf, _row(norm_mix[1]), w_in, conv_w[0].astype(F32), seq=seq,
                      qw=2 * qk_w + N_HEADS * LANES)
    lam_init = 0.8 - 0.6 * math.exp(-0.3 * 1)
    lam = (jnp.exp(jnp.sum(diff_lambda_q1[0] * diff_lambda_k1[0]))
           - jnp.exp(jnp.sum(diff_lambda_q2[0] * diff_lambda_k2[0])) + lam_init)
    bias = _diff_bias_blocks(rel_bias, min(ATT_BLOCK, seq))
    yc = _diff_attention(qkv, lam.reshape(1, 1).astype(F32), bias, _row(diff_subln_gain[0]),
                         batch=batch, seq=seq, out_scale=1.0 - lam_init)
    xf = mem_attention(1, xf, yc, yd, odd_w_out[0])
    wr = jnp.pad(router_w[0], ((0, 0), (0, LANES - N_EXPERTS))).astype(F32)
    out = _moe_and_final_norm(xf, _row(norm_ffn[1]), wr, moe_w_gate[0].astype(BF16),
                              moe_w_up[0].astype(BF16), moe_w_down[0].astype(BF16),
                              _row(norm_final), seq=seq)
    return out.reshape(batch, seq, d)
```

```python
import functools
import math

import numpy as np
import jax
import jax.numpy as jnp
from jax import lax
from jax.experimental import pallas as pl
from jax.experimental.pallas import tpu as pltpu

F32 = jnp.float32
BF16 = jnp.bfloat16
I32 = jnp.int32

EPS = 1e-6
NEG = -1e30
LANES = 128
VMEM_LIMIT = 56 * 1024 * 1024

CHUNK = 64
A_GROUPS = 4
N_HEADS = 4
N_EXPERTS = 8
REL_BUCKETS = 32
REL_MAX_DIST = 128

TOK_TILE = 512
ATT_BLOCK = 512
ATT_STRIP = 32
MOE_TILE = 512
MOE_FF_CHUNK = 512
FFN_CHUNK = 512
DISPATCH_CHUNK = 256
COMBINE_TILE = 256


def _dot(a, b):
    return jnp.dot(a, b, preferred_element_type=F32)


def _dot_nt(a, b):
    return lax.dot_general(a, b, (((1,), (1,)), ((), ())), preferred_element_type=F32)


def _rmsnorm(x, g):
    return x * lax.rsqrt(jnp.mean(x * x, axis=-1, keepdims=True) + EPS) * g


def _params(*sem):
    return pltpu.CompilerParams(dimension_semantics=sem, vmem_limit_bytes=VMEM_LIMIT)


def _const_spec(shape):
    return pl.BlockSpec(shape, lambda *_: (0,) * len(shape))


def _split3(v):
    a1 = v.astype(BF16)
    r1 = v - a1.astype(F32)
    a2 = r1.astype(BF16)
    a3 = (r1 - a2.astype(F32)).astype(BF16)
    return a1, a2, a3


def _even_in_kernel(x_ref, g_ref, w_ref, bf_ref, vg_ref, ws_ref, bs_ref,
                    ya_ref, qkv_ref, aug_ref, carry_ref, *, tiles_per_batch):
    i = pl.program_id(0)
    tm = x_ref.shape[0]
    aw = ya_ref.shape[1]
    qw = qkv_ref.shape[1]
    xn = _rmsnorm(x_ref[...], g_ref[...]).astype(BF16)

    qkv_ref[...] = _dot(xn, w_ref[:, 2 * aw:2 * aw + qw]).astype(BF16)

    fl = _dot(xn, w_ref[:, 2 * aw + qw:]) + bf_ref[...]
    ls = jnp.minimum(fl, 0.0) - jnp.log1p(jnp.exp(-jnp.abs(fl)))
    row = lax.broadcasted_iota(I32, (tm, tm), 0)
    col = lax.broadcasted_iota(I32, (tm, tm), 1)
    tri = jnp.where(col <= row, 1.0, 0.0).astype(BF16)
    a1, a2, a3 = _split3(ls)
    csum = _dot(tri, a1) + _dot(tri, a2) + _dot(tri, a3)

    @pl.when(i % tiles_per_batch == 0)
    def _():
        carry_ref[...] = jnp.zeros_like(carry_ref)

    csum = csum + carry_ref[0:1, :]
    carry_ref[...] = jnp.broadcast_to(csum[tm - 1:tm, :], carry_ref.shape)

    lane = lax.broadcasted_iota(I32, (tm, LANES), 1)
    for h in range(N_HEADS):
        c1, c2, c3 = (piece.astype(F32)
                      for piece in _split3(jnp.broadcast_to(csum[:, h:h + 1], (tm, LANES))))
        aq = jnp.where(lane == 0, c1, jnp.where(lane == 1, c2, jnp.where(lane == 2, c3,
                       jnp.where(lane < 6, 1.0, 0.0))))
        ak = jnp.where(lane == 3, -c1, jnp.where(lane == 4, -c2, jnp.where(lane == 5, -c3,
                       jnp.where(lane < 3, 1.0, 0.0))))
        aug_ref[:, h * LANES:(h + 1) * LANES] = aq.astype(BF16)
        aug_ref[:, (N_HEADS + h) * LANES:(N_HEADS + h + 1) * LANES] = ak.astype(BF16)

    gu = jax.nn.gelu(_dot(xn, w_ref[:, 0:aw]), approximate=True)
    gv = jax.nn.gelu(_dot(xn, w_ref[:, aw:2 * aw]), approximate=True)
    blk = ws_ref.shape[1]
    ch = aw // A_GROUPS
    r = lax.broadcasted_iota(I32, (blk, blk), 0)
    c = lax.broadcasted_iota(I32, (blk, blk), 1)
    causal = (c // CHUNK) <= (r // CHUNK)
    for g in range(A_GROUPS):
        vgrp = gv[:, g * ch:(g + 1) * ch]
        vn = _rmsnorm(vgrp, vg_ref[:, g * ch:(g + 1) * ch]).astype(BF16)
        wmix = jnp.where(causal, ws_ref[g], 0.0).astype(BF16)
        for n in range(tm // blk):
            mixed = _dot(wmix, vn[n * blk:(n + 1) * blk, :]) + bs_ref[g]
            ya_ref[n * blk:(n + 1) * blk, g * ch:(g + 1) * ch] = (
                gu[n * blk:(n + 1) * blk, g * ch:(g + 1) * ch] * mixed).astype(BF16)


def _even_in(x, g, w, bf, vgain, ws, bs, *, seq):
    t, d = x.shape
    tm = min(TOK_TILE, seq)
    aw = vgain.shape[1]
    qw = w.shape[1] - 2 * aw - LANES
    blk = ws.shape[1]
    return pl.pallas_call(
        functools.partial(_even_in_kernel, tiles_per_batch=seq // tm),
        grid=(t // tm,),
        in_specs=[
            pl.BlockSpec((tm, d), lambda i: (i, 0)),
            _const_spec((1, d)),
            _const_spec(w.shape),
            _const_spec((1, LANES)),
            _const_spec((1, aw)),
            _const_spec(ws.shape),
            _const_spec(bs.shape),
        ],
        out_specs=[
            pl.BlockSpec((tm, aw), lambda i: (i, 0)),
            pl.BlockSpec((tm, qw), lambda i: (i, 0)),
            pl.BlockSpec((tm, 2 * N_HEADS * LANES), lambda i: (i, 0)),
        ],
        out_shape=[
            jax.ShapeDtypeStruct((t, aw), BF16),
            jax.ShapeDtypeStruct((t, qw), BF16),
            jax.ShapeDtypeStruct((t, 2 * N_HEADS * LANES), BF16),
        ],
        scratch_shapes=[pltpu.VMEM((8, LANES), F32)],
        compiler_params=_params("arbitrary"),
        name="even_in",
    )(x, g, w, bf, vgain, ws, bs)


def _tri_pairs(nblk):
    qi = np.array([i for i in range(nblk) for _ in range(i + 1)], np.int32)
    kj = np.array([j for i in range(nblk) for j in range(i + 1)], np.int32)
    return jnp.asarray(qi), jnp.asarray(kj)


def _softmax_strips(s_ref, p_ref, alpha_ref, m_ref, h, bias_rows=None):
    rows, tk = s_ref.shape
    for r0 in range(0, rows, ATT_STRIP):
        rs = slice(r0, r0 + ATT_STRIP)
        s = s_ref[rs, :]
        if bias_rows is not None:
            s = s + bias_rows(r0)
        m_prev = m_ref[h, rs, :]
        m_new = jnp.maximum(m_prev, jnp.max(s, axis=-1, keepdims=True))
        alpha_ref[rs, :] = jnp.exp(m_prev - m_new)
        m_ref[h, rs, :] = m_new
        p_ref[rs, :] = jnp.exp((s - jnp.concatenate([m_new] * (tk // LANES), axis=-1)).astype(BF16))


def _accumulate(p_ref, alpha_ref, v, ones_ref, acc_ref, h):
    alpha = alpha_ref[...]
    va = jnp.concatenate([v, ones_ref[...]], axis=-1)
    acc_ref[h] = jnp.concatenate([alpha, alpha], axis=-1) * acc_ref[h] + _dot(p_ref[...], va)


def _fox_kernel(qi_ref, kj_ref, q_ref, aq_ref, k_ref, ak_ref, v_ref, ones_ref, mask_ref, o_ref,
                s_ref, p_ref, alpha_ref, m_ref, acc_ref):
    step = pl.program_id(1)
    i = qi_ref[step]
    j = kj_ref[step]

    @pl.when(j == 0)
    def _():
        m_ref[...] = jnp.full_like(m_ref, NEG)
        acc_ref[...] = jnp.zeros_like(acc_ref)

    def sweep(masked):
        for h in range(N_HEADS):
            hs = slice(h * LANES, (h + 1) * LANES)
            qa = jnp.concatenate([q_ref[:, hs], aq_ref[:, hs]], axis=-1)
            ka = jnp.concatenate([k_ref[:, hs], ak_ref[:, hs]], axis=-1)
            s_ref[...] = _dot_nt(qa, ka)
            bias_rows = (lambda r0: mask_ref[r0:r0 + ATT_STRIP, :]) if masked else None
            _softmax_strips(s_ref, p_ref, alpha_ref, m_ref, h, bias_rows)
            _accumulate(p_ref, alpha_ref, v_ref[:, hs], ones_ref, acc_ref, h)

    @pl.when(j < i)
    def _():
        sweep(False)

    @pl.when(j == i)
    def _():
        sweep(True)
        for h in range(N_HEADS):
            acc = acc_ref[h]
            o_ref[:, h * LANES:(h + 1) * LANES] = (
                acc[:, :LANES] / acc[:, LANES:LANES + 1]).astype(o_ref.dtype)


def _ones_column(rows):
    ones = np.zeros((rows, LANES), np.float32)
    ones[:, 0] = 1.0
    return jnp.asarray(ones, BF16)


def _fox_attention(qkv, aug, *, batch, seq):
    t = qkv.shape[0]
    w = N_HEADS * LANES
    blk = min(ATT_BLOCK, seq)
    nblk = seq // blk
    qi, kj = _tri_pairs(nblk)
    r = np.arange(blk)
    mask = jnp.asarray(np.where(r[None, :] <= r[:, None], 0.0, NEG), F32)
    grid_spec = pltpu.PrefetchScalarGridSpec(
        num_scalar_prefetch=2,
        grid=(batch, qi.shape[0]),
        in_specs=[
            pl.BlockSpec((blk, w), lambda b, s, qi, kj: (b * nblk + qi[s], 0)),
            pl.BlockSpec((blk, w), lambda b, s, qi, kj: (b * nblk + qi[s], 0)),
            pl.BlockSpec((blk, w), lambda b, s, qi, kj: (b * nblk + kj[s], 1)),
            pl.BlockSpec((blk, w), lambda b, s, qi, kj: (b * nblk + kj[s], 1)),
            pl.BlockSpec((blk, w), lambda b, s, qi, kj: (b * nblk + kj[s], 2)),
            pl.BlockSpec((blk, LANES), lambda b, s, qi, kj: (0, 0)),
            pl.BlockSpec((blk, blk), lambda b, s, qi, kj: (0, 0)),
        ],
        out_specs=pl.BlockSpec((blk, w), lambda b, s, qi, kj: (b * nblk + qi[s], 0)),
        scratch_shapes=[
            pltpu.VMEM((blk, blk), F32),
            pltpu.VMEM((blk, blk), BF16),
            pltpu.VMEM((blk, LANES), F32),
            pltpu.VMEM((N_HEADS, blk, LANES), F32),
            pltpu.VMEM((N_HEADS, blk, 2 * LANES), F32),
        ],
    )
    return pl.pallas_call(
        _fox_kernel,
        grid_spec=grid_spec,
        out_shape=jax.ShapeDtypeStruct((t, w), BF16),
        compiler_params=_params("arbitrary", "arbitrary"),
        name="fox_attention",
    )(qi, kj, qkv, aug, qkv, aug, qkv, _ones_column(blk), mask)


def _diff_kernel(qi_ref, kj_ref, lam_ref, q_ref, k_ref, v_ref, ones_ref, bias_ref, gain_ref, o_ref,
                 qs_ref, s_ref, p_ref, alpha_ref, m_ref, acc_ref, *, out_scale):
    step = pl.program_id(1)
    i = qi_ref[step]
    j = kj_ref[step]
    tq = q_ref.shape[0]
    half = LANES // 2

    @pl.when(j == 0)
    def _():
        m_ref[...] = jnp.full_like(m_ref, NEG)
        acc_ref[...] = jnp.zeros_like(acc_ref)
        lane = lax.broadcasted_iota(I32, (tq, LANES), 1)
        zero = jnp.zeros((tq, LANES), BF16)
        for h in range(N_HEADS):
            q = q_ref[:, h * LANES:(h + 1) * LANES]
            qs_ref[h, 0:tq, :] = jnp.where(lane < half, q, zero)
            qs_ref[h, tq:2 * tq, :] = jnp.where(lane < half, zero, q)

    def sweep(near):
        for h in range(N_HEADS):
            hs = slice(h * LANES, (h + 1) * LANES)
            s_ref[...] = _dot_nt(qs_ref[h], k_ref[:, hs])
            bias_rows = ((lambda r0, h=h: bias_ref[h, i - j, pl.ds(r0 % tq, ATT_STRIP), :])
                         if near else None)
            _softmax_strips(s_ref, p_ref, alpha_ref, m_ref, h, bias_rows)
            _accumulate(p_ref, alpha_ref, v_ref[:, hs], ones_ref, acc_ref, h)

    @pl.when(j < i - 1)
    def _():
        sweep(False)

    @pl.when(j >= i - 1)
    def _():
        sweep(True)

    @pl.when(j == i)
    def _():
        for h in range(N_HEADS):
            hs = slice(h * LANES, (h + 1) * LANES)
            acc = acc_ref[h]
            o = acc[:, :LANES] / acc[:, LANES:LANES + 1]
            o = o[0:tq, :] - lam_ref[0, 0] * o[tq:2 * tq, :]
            o_ref[:, hs] = (_rmsnorm(o, gain_ref[:, hs]) * out_scale).astype(o_ref.dtype)


def _diff_attention(qkv, lam, bias, gain, *, batch, seq, out_scale):
    t = qkv.shape[0]
    w = N_HEADS * LANES
    blk = bias.shape[2]
    nblk = seq // blk
    qi, kj = _tri_pairs(nblk)
    grid_spec = pltpu.PrefetchScalarGridSpec(
        num_scalar_prefetch=2,
        grid=(batch, qi.shape[0]),
        in_specs=[
            pl.BlockSpec(memory_space=pltpu.SMEM),
            pl.BlockSpec((blk, w), lambda b, s, qi, kj: (b * nblk + qi[s], 0)),
            pl.BlockSpec((blk, w), lambda b, s, qi, kj: (b * nblk + kj[s], 1)),
            pl.BlockSpec((blk, w), lambda b, s, qi, kj: (b * nblk + kj[s], 2)),
            pl.BlockSpec((blk, LANES), lambda b, s, qi, kj: (0, 0)),
            pl.BlockSpec(bias.shape, lambda b, s, qi, kj: (0, 0, 0, 0)),
            pl.BlockSpec((1, w), lambda b, s, qi, kj: (0, 0)),
        ],
        out_specs=pl.BlockSpec((blk, w), lambda b, s, qi, kj: (b * nblk + qi[s], 0)),
        scratch_shapes=[
            pltpu.VMEM((N_HEADS, 2 * blk, LANES), BF16),
            pltpu.VMEM((2 * blk, blk), F32),
            pltpu.VMEM((2 * blk, blk), BF16),
            pltpu.VMEM((2 * blk, LANES), F32),
            pltpu.VMEM((N_HEADS, 2 * blk, LANES), F32),
            pltpu.VMEM((N_HEADS, 2 * blk, 2 * LANES), F32),
        ],
    )
    return pl.pallas_call(
        functools.partial(_diff_kernel, out_scale=out_scale),
        grid_spec=grid_spec,
        out_shape=jax.ShapeDtypeStruct((t, w), BF16),
        compiler_params=_params("arbitrary", "arbitrary"),
        name="diff_attention",
    )(qi, kj, lam, qkv, qkv, qkv, _ones_column(blk), bias, gain)


def _rel_bucket(rel):
    n_half = REL_BUCKETS // 2
    max_exact = n_half // 2
    ret = jnp.where(rel > 0, n_half, 0)
    n = jnp.abs(rel)
    nf = jnp.maximum(n, 1).astype(F32)
    large = max_exact + (jnp.log(nf / max_exact) / math.log(REL_MAX_DIST / max_exact)
                         * (n_half - max_exact)).astype(I32)
    large = jnp.minimum(large, n_half - 1)
    return ret + jnp.where(n < max_exact, n, large)


def _rel_bias_kernel(table_ref, idx_ref, o_ref):
    h = pl.program_id(0)
    idx = idx_ref[0]
    far = table_ref[REL_BUCKETS // 2 - 1, h]
    out = jnp.full(idx.shape, NEG, F32)
    for b in range(REL_BUCKETS):
        out = jnp.where(idx == b, table_ref[b, h] - far, out)
    o_ref[0, 0] = out


def _diff_bias_blocks(rel_table, blk):
    assert blk >= REL_MAX_DIST
    r = jnp.arange(blk)[:, None]
    c = jnp.arange(blk)[None, :]
    diag = jnp.where((c // CHUNK) <= (r // CHUNK), _rel_bucket(c - r), -1)
    prev = _rel_bucket(c - r - blk)
    idx = jnp.stack([diag, prev], axis=0).astype(I32)
    n_heads = rel_table.shape[1]
    return pl.pallas_call(
        _rel_bias_kernel,
        grid=(n_heads, 2),
        in_specs=[
            pl.BlockSpec(memory_space=pltpu.SMEM),
            pl.BlockSpec((1, blk, blk), lambda h, s: (s, 0, 0)),
        ],
        out_specs=pl.BlockSpec((1, 1, blk, blk), lambda h, s: (h, s, 0, 0)),
        out_shape=jax.ShapeDtypeStruct((n_heads, 2, blk, blk), F32),
        compiler_params=_params("arbitrary", "arbitrary"),
        name="rel_bias",
    )(rel_table.astype(F32), idx)


def _odd_in_kernel(x_ref, g_ref, w_ref, cw_ref, qkv_ref, yd_ref, zbuf_ref, *, tiles_per_batch):
    i = pl.program_id(0)
    tm = x_ref.shape[0]
    qw = qkv_ref.shape[1]
    dw = yd_ref.shape[1]
    xn = _rmsnorm(x_ref[...], g_ref[...]).astype(BF16)
    qkv_ref[...] = _dot(xn, w_ref[:, 0:qw]).astype(BF16)
    hh = _dot(xn, w_ref[:, qw:qw + dw])
    gb = _dot(xn, w_ref[:, qw + dw:qw + 2 * dw])
    gc = _dot(xn, w_ref[:, qw + 2 * dw:qw + 3 * dw])
    z = gc * hh

    @pl.when(i % tiles_per_batch == 0)
    def _():
        zbuf_ref[0:8, :] = jnp.zeros((8, dw), F32)

    @pl.when(i % tiles_per_batch != 0)
    def _():
        zbuf_ref[0:8, :] = zbuf_ref[tm:tm + 8, :]

    zbuf_ref[8:tm + 8, :] = z
    y = (cw_ref[0:1, :] * zbuf_ref[6:tm + 6, :] + cw_ref[1:2, :] * zbuf_ref[7:tm + 7, :]
         + cw_ref[2:3, :] * z)
    yd_ref[...] = (gb * y).astype(BF16)


def _odd_in(x, g, w, cw, *, seq, qw):
    t, d = x.shape
    tm = min(TOK_TILE, seq)
    dw = cw.shape[1]
    return pl.pallas_call(
        functools.partial(_odd_in_kernel, tiles_per_batch=seq // tm),
        grid=(t // tm,),
        in_specs=[
            pl.BlockSpec((tm, d), lambda i: (i, 0)),
            _const_spec((1, d)),
            _const_spec(w.shape),
            _const_spec(cw.shape),
        ],
        out_specs=[
            pl.BlockSpec((tm, qw), lambda i: (i, 0)),
            pl.BlockSpec((tm, dw), lambda i: (i, 0)),
        ],
        out_shape=[
            jax.ShapeDtypeStruct((t, qw), BF16),
            jax.ShapeDtypeStruct((t, dw), BF16),
        ],
        scratch_shapes=[pltpu.VMEM((tm + 8, dw), F32)],
        compiler_params=_params("arbitrary"),
        name="odd_in",
    )(x, g, w, cw)


def _mem_kv_kernel(mem_ref, g_ref, w_ref, k_ref, v_ref):
    d = mem_ref.shape[2]
    mn = _rmsnorm(mem_ref[0], g_ref[...]).astype(BF16)
    k_ref[0] = _dot(mn, w_ref[:, 0:d]).astype(BF16)
    v_ref[0] = _dot(mn, w_ref[:, d:2 * d]).astype(BF16)


def _mem_kv(mem, g, w):
    b, m, d = mem.shape
    return pl.pallas_call(
        _mem_kv_kernel,
        grid=(b,),
        in_specs=[
            pl.BlockSpec((1, m, d), lambda i: (i, 0, 0)),
            _const_spec((1, d)),
            _const_spec(w.shape),
        ],
        out_specs=[pl.BlockSpec((1, m, d), lambda i: (i, 0, 0))] * 2,
        out_shape=[jax.ShapeDtypeStruct((b, m, d), BF16)] * 2,
        compiler_params=_params("arbitrary"),
        name="mem_kv",
    )(mem, g, w)


def _post_mixer_kernel(x_ref, ya_ref, yb_ref, wout_ref, g_ref, wq_ref, k_ref, v_ref, wo_ref, o_ref):
    wa = ya_ref.shape[1]
    d = x_ref.shape[1]
    hd = d // N_HEADS
    x1 = x_ref[...] + _dot(ya_ref[...], wout_ref[0:wa, :]) + _dot(yb_ref[...], wout_ref[wa:, :])
    xn = _rmsnorm(x1, g_ref[...]).astype(BF16)
    q = _dot(xn, wq_ref[...]).astype(BF16)
    heads = []
    for h in range(N_HEADS):
        lg = _dot_nt(q[:, h * hd:(h + 1) * hd], k_ref[0, :, h * hd:(h + 1) * hd])
        p = jnp.exp(lg - jnp.max(lg, axis=-1, keepdims=True))
        p = p / jnp.sum(p, axis=-1, keepdims=True)
        heads.append(_dot(p.astype(BF16), v_ref[0, :, h * hd:(h + 1) * hd]).astype(BF16))
    o_ref[...] = x1 + _dot(jnp.concatenate(heads, axis=-1), wo_ref[...])


def _post_mixer(x, ya, yb, wout, g, wq, kmem, vmem, wo, *, seq):
    t, d = x.shape
    tm = min(TOK_TILE, seq)
    nt = seq // tm
    m = kmem.shape[1]
    wa = ya.shape[1]
    return pl.pallas_call(
        _post_mixer_kernel,
        grid=(t // tm,),
        in_specs=[
            pl.BlockSpec((tm, d), lambda i: (i, 0)),
            pl.BlockSpec((tm, wa), lambda i: (i, 0)),
            pl.BlockSpec((tm, yb.shape[1]), lambda i: (i, 0)),
            _const_spec(wout.shape),
            _const_spec((1, d)),
            _const_spec(wq.shape),
            pl.BlockSpec((1, m, d), lambda i: (i // nt, 0, 0)),
            pl.BlockSpec((1, m, d), lambda i: (i // nt, 0, 0)),
            _const_spec(wo.shape),
        ],
        out_specs=pl.BlockSpec((tm, d), lambda i: (i, 0)),
        out_shape=jax.ShapeDtypeStruct((t, d), F32),
        compiler_params=_params("arbitrary"),
        name="post_mixer",
    )(x, ya, yb, wout, g, wq, kmem, vmem, wo)


def _ffn_kernel(x_ref, g_ref, wg_ref, wu_ref, wd_ref, o_ref):
    x = x_ref[...]
    h = _rmsnorm(x, g_ref[...]).astype(BF16)
    ff = wg_ref.shape[1]
    acc = x
    for c0 in range(0, ff, FFN_CHUNK):
        c1 = min(c0 + FFN_CHUNK, ff)
        a = _dot(h, wg_ref[:, c0:c1])
        u = _dot(h, wu_ref[:, c0:c1])
        acc = acc + _dot((jax.nn.silu(a) * u).astype(BF16), wd_ref[c0:c1, :])
    o_ref[...] = acc


def _ffn(x, g, wg, wu, wd, *, seq):
    t, d = x.shape
    tm = min(TOK_TILE, seq)
    return pl.pallas_call(
        _ffn_kernel,
        grid=(t // tm,),
        in_specs=[
            pl.BlockSpec((tm, d), lambda i: (i, 0)),
            _const_spec((1, d)),
            _const_spec(wg.shape),
            _const_spec(wu.shape),
            _const_spec(wd.shape),
        ],
        out_specs=pl.BlockSpec((tm, d), lambda i: (i, 0)),
        out_shape=jax.ShapeDtypeStruct((t, d), F32),
        compiler_params=_params("arbitrary"),
        name="ffn",
    )(x, g, wg, wu, wd)


META_I1, META_I2, META_R1, META_R2, META_G1, META_G2 = range(6)


def _router_kernel(x_ref, g_ref, wr_ref, meta_ref, cnt_ref, carry_ref):
    i = pl.program_id(0)
    tm = x_ref.shape[0]
    h = _rmsnorm(x_ref[...], g_ref[...])
    logits = jnp.dot(h, wr_ref[...], preferred_element_type=F32, precision=lax.Precision.HIGHEST)
    lane = lax.broadcasted_iota(I32, logits.shape, 1)
    logits = jnp.where(lane < N_EXPERTS, logits, NEG)
    m1 = jnp.max(logits, axis=-1, keepdims=True)
    i1 = jnp.min(jnp.where(logits == m1, lane, LANES), axis=-1, keepdims=True)
    rest = jnp.where(lane == i1, NEG, logits)
    m2 = jnp.max(rest, axis=-1, keepdims=True)
    i2 = jnp.min(jnp.where(rest == m2, lane, LANES), axis=-1, keepdims=True)
    e = jnp.exp(m2 - m1)
    g1 = 1.0 / (1.0 + e)
    g2 = e / (1.0 + e)

    sel = jnp.where((lane == i1) | (lane == i2), 1.0, 0.0)
    row = lax.broadcasted_iota(I32, (tm, tm), 0)
    col = lax.broadcasted_iota(I32, (tm, tm), 1)
    strict = jnp.where(col < row, 1.0, 0.0).astype(BF16)

    @pl.when(i == 0)
    def _():
        carry_ref[...] = jnp.zeros_like(carry_ref)

    rank = _dot(strict, sel.astype(BF16)) + carry_ref[0:1, :]
    total = rank[tm - 1:tm, :] + sel[tm - 1:tm, :]
    carry_ref[...] = jnp.broadcast_to(total, carry_ref.shape)
    cnt_ref[...] = jnp.broadcast_to(total, cnt_ref.shape)
    r1 = jnp.sum(jnp.where(lane == i1, rank, 0.0), axis=-1, keepdims=True)
    r2 = jnp.sum(jnp.where(lane == i2, rank, 0.0), axis=-1, keepdims=True)
    meta = jnp.zeros_like(logits)
    for slot, val in ((META_I1, i1.astype(F32)), (META_I2, i2.astype(F32)), (META_R1, r1),
                      (META_R2, r2), (META_G1, g1), (META_G2, g2)):
        meta = jnp.where(lane == slot, val, meta)
    meta_ref[...] = meta


def _router(x, g, wr, *, seq):
    t, d = x.shape
    tm = min(TOK_TILE, seq)
    return pl.pallas_call(
        _router_kernel,
        grid=(t // tm,),
        in_specs=[
            pl.BlockSpec((tm, d), lambda i: (i, 0)),
            _const_spec((1, d)),
            _const_spec(wr.shape),
        ],
        out_specs=[
            pl.BlockSpec((tm, LANES), lambda i: (i, 0)),
            _const_spec((8, LANES)),
        ],
        out_shape=[
            jax.ShapeDtypeStruct((t, LANES), F32),
            jax.ShapeDtypeStruct((8, LANES), F32),
        ],
        scratch_shapes=[pltpu.VMEM((8, LANES), F32)],
        compiler_params=_params("arbitrary"),
        name="router",
    )(x, g, wr)


def _row_copy(src, src_row, dst, dst_row, sem):
    return pltpu.make_async_copy(src.at[pl.ds(src_row, 1)], dst.at[pl.ds(dst_row, 1)], sem)


def _dispatch_kernel(pos_ref, pad_ref, x_ref, g_ref, xs_ref, h_ref, zrow_ref, sem_ref, zsem_ref):
    c = pl.program_id(0)
    n_chunks = pl.num_programs(0)
    tc = x_ref.shape[0]
    slot = c % 2

    @pl.when(c == 0)
    def _():
        zrow_ref[...] = jnp.zeros_like(zrow_ref)
        for e in range(N_EXPERTS):
            start = pad_ref[2 * e]
            count = pad_ref[2 * e + 1] - start

            def zissue(r, carry, start=start):
                _row_copy(zrow_ref, 0, xs_ref, start + r, zsem_ref).start()
                return carry

            def zwait(r, carry):
                _row_copy(zrow_ref, 0, xs_ref, 0, zsem_ref).wait()
                return carry

            lax.fori_loop(0, count, zissue, 0)
            lax.fori_loop(0, count, zwait, 0)

        tail = pad_ref[2 * N_EXPERTS - 1]
        groups = (xs_ref.shape[0] - tail) // 8

        def tissue(r, carry):
            start = pl.multiple_of(tail + 8 * r, 8)
            pltpu.make_async_copy(zrow_ref, xs_ref.at[pl.ds(start, 8)], zsem_ref).start()
            return carry

        def twait(r, carry):
            pltpu.make_async_copy(zrow_ref, xs_ref.at[pl.ds(0, 8)], zsem_ref).wait()
            return carry

        lax.fori_loop(0, groups, tissue, 0)
        lax.fori_loop(0, groups, twait, 0)

    h_ref[slot] = _rmsnorm(x_ref[...], g_ref[...])

    def issue(r, carry):
        tok = c * tc + r
        _row_copy(h_ref.at[slot], r, xs_ref, pos_ref[2 * tok], sem_ref.at[slot]).start()
        _row_copy(h_ref.at[slot], r, xs_ref, pos_ref[2 * tok + 1], sem_ref.at[slot]).start()
        return carry

    def drain(s):
        def body(r, carry):
            _row_copy(h_ref.at[s], 0, xs_ref, 0, sem_ref.at[s]).wait()
            _row_copy(h_ref.at[s], 0, xs_ref, 0, sem_ref.at[s]).wait()
            return carry
        lax.fori_loop(0, tc, body, 0)

    lax.fori_loop(0, tc, issue, 0)

    @pl.when(c > 0)
    def _():
        drain(1 - slot)

    @pl.when(c == n_chunks - 1)
    def _():
        drain(slot)


def _dispatch(pos, pad, x, g, n_rows):
    t, d = x.shape
    tc = DISPATCH_CHUNK
    grid_spec = pltpu.PrefetchScalarGridSpec(
        num_scalar_prefetch=2,
        grid=(t // tc,),
        in_specs=[
            pl.BlockSpec((tc, d), lambda c, pos, pad: (c, 0)),
            pl.BlockSpec((1, d), lambda c, pos, pad: (0, 0)),
        ],
        out_specs=pl.BlockSpec(memory_space=pl.ANY),
        scratch_shapes=[
            pltpu.VMEM((2, tc, d), F32),
            pltpu.VMEM((8, d), F32),
            pltpu.SemaphoreType.DMA((2,)),
            pltpu.SemaphoreType.DMA(()),
        ],
    )
    return pl.pallas_call(
        _dispatch_kernel,
        grid_spec=grid_spec,
        out_shape=jax.ShapeDtypeStruct((n_rows, d), F32),
        compiler_params=_params("arbitrary"),
        name="moe_dispatch",
    )(pos, pad, x, g)


def _experts_kernel(te_ref, nused_ref, xs_ref, wg_ref, wu_ref, wd_ref, y_ref, xb_ref, acc_ref):
    p = pl.program_id(0)
    f = pl.program_id(1)

    @pl.when((p >= nused_ref[0]) & (f == 0))
    def _():
        y_ref[...] = jnp.zeros_like(y_ref)

    @pl.when(p < nused_ref[0])
    def _():
        @pl.when(f == 0)
        def _():
            xb_ref[...] = xs_ref[...].astype(BF16)
            acc_ref[...] = jnp.zeros_like(acc_ref)

        xb = xb_ref[...]
        a = _dot(xb, wg_ref[0])
        u = _dot(xb, wu_ref[0])
        acc_ref[...] += _dot((jax.nn.silu(a) * u).astype(BF16), wd_ref[0])

        @pl.when(f == pl.num_programs(1) - 1)
        def _():
            y_ref[...] = acc_ref[...]


def _experts(tile_expert, n_used, xs, wg, wu, wd):
    n_rows, d = xs.shape
    ff = wg.shape[2]
    tm = MOE_TILE
    tf = MOE_FF_CHUNK
    n_tiles = n_rows // tm

    def x_map(p, f, te, nu):
        return (jnp.minimum(p, nu[0] - 1), 0)

    grid_spec = pltpu.PrefetchScalarGridSpec(
        num_scalar_prefetch=2,
        grid=(n_tiles, ff // tf),
        in_specs=[
            pl.BlockSpec((tm, d), x_map),
            pl.BlockSpec((1, d, tf), lambda p, f, te, nu: (te[p], 0, f)),
            pl.BlockSpec((1, d, tf), lambda p, f, te, nu: (te[p], 0, f)),
            pl.BlockSpec((1, tf, d), lambda p, f, te, nu: (te[p], f, 0)),
        ],
        out_specs=pl.BlockSpec((tm, d), lambda p, f, te, nu: (p, 0)),
        scratch_shapes=[pltpu.VMEM((tm, d), BF16), pltpu.VMEM((tm, d), F32)],
    )
    return pl.pallas_call(
        _experts_kernel,
        grid_spec=grid_spec,
        out_shape=jax.ShapeDtypeStruct((n_rows, d), F32),
        compiler_params=_params("arbitrary", "arbitrary"),
        name="moe_experts",
    )(tile_expert, n_used, xs, wg, wu, wd)


def _combine_kernel(pos_ref, y_ref, x_ref, meta_ref, g_ref, o_ref, buf_ref, sem_ref):
    i = pl.program_id(0)
    n = pl.num_programs(0)
    tc = x_ref.shape[0]

    def fetch(tile, slot):
        def body(r, carry):
            tok = tile * tc + r
            for k in range(2):
                pltpu.make_async_copy(y_ref.at[pl.ds(pos_ref[2 * tok + k], 1)],
                                      buf_ref.at[slot, k, pl.ds(r, 1)], sem_ref.at[slot]).start()
            return carry
        lax.fori_loop(0, tc, body, 0)

    @pl.when(i == 0)
    def _():
        fetch(0, 0)

    @pl.when(i + 1 < n)
    def _():
        fetch(i + 1, (i + 1) % 2)

    slot = i % 2

    def drain(r, carry):
        for k in range(2):
            pltpu.make_async_copy(y_ref.at[pl.ds(0, 1)], buf_ref.at[slot, k, pl.ds(0, 1)],
                                  sem_ref.at[slot]).wait()
        return carry

    lax.fori_loop(0, tc, drain, 0)
    meta = meta_ref[...]
    g1 = meta[:, META_G1:META_G1 + 1]
    g2 = meta[:, META_G2:META_G2 + 1]
    x = x_ref[...] + g1 * buf_ref[slot, 0] + g2 * buf_ref[slot, 1]
    o_ref[...] = _rmsnorm(x, g_ref[...])


def _combine(pos, y, x, meta, g):
    t, d = x.shape
    tc = COMBINE_TILE
    grid_spec = pltpu.PrefetchScalarGridSpec(
        num_scalar_prefetch=1,
        grid=(t // tc,),
        in_specs=[
            pl.BlockSpec(memory_space=pl.ANY),
            pl.BlockSpec((tc, d), lambda i, pos: (i, 0)),
            pl.BlockSpec((tc, LANES), lambda i, pos: (i, 0)),
            pl.BlockSpec((1, d), lambda i, pos: (0, 0)),
        ],
        out_specs=pl.BlockSpec((tc, d), lambda i, pos: (i, 0)),
        scratch_shapes=[pltpu.VMEM((2, 2, tc, d), F32), pltpu.SemaphoreType.DMA((2,))],
    )
    return pl.pallas_call(
        _combine_kernel,
        grid_spec=grid_spec,
        out_shape=jax.ShapeDtypeStruct((t, d), F32),
        compiler_params=_params("arbitrary"),
        name="moe_combine",
    )(pos, y, x, meta, g)


def _moe_plan(meta, counts, n_tiles):
    cnt = counts[0, :N_EXPERTS].astype(I32)
    tiles = (cnt + MOE_TILE - 1) // MOE_TILE
    tile_end = jnp.cumsum(tiles)
    row_off = (tile_end - tiles) * MOE_TILE
    n_used = tile_end[-1:]
    experts = jnp.arange(N_EXPERTS, dtype=I32)

    def rows(idx_col, rank_col):
        idx = meta[:, idx_col].astype(I32)
        off = jnp.sum(jnp.where(idx[:, None] == experts[None, :], row_off[None, :], 0), axis=1)
        return off + meta[:, rank_col].astype(I32)

    pos = jnp.stack([rows(META_I1, META_R1), rows(META_I2, META_R2)], axis=1).reshape(-1)
    tile_ids = jnp.arange(n_tiles, dtype=I32)
    te = jnp.sum((tile_end[None, :] <= tile_ids[:, None]).astype(I32), axis=1)
    last = jnp.sum((tile_end <= n_used - 1).astype(I32))
    te = jnp.minimum(te, last).astype(I32)
    pad = jnp.stack([row_off + cnt, row_off + tiles * MOE_TILE], axis=1).reshape(-1).astype(I32)
    return pos.astype(I32), pad, te, n_used.astype(I32)


def _moe_and_final_norm(x, g_ffn, wr, wg, wu, wd, g_final, *, seq):
    t, d = x.shape
    n_tiles = 2 * t // MOE_TILE + N_EXPERTS
    meta, counts = _router(x, g_ffn, wr, seq=seq)
    pos, pad, te, n_used = _moe_plan(meta, counts, n_tiles)
    xs = _dispatch(pos, pad, x, g_ffn, n_tiles * MOE_TILE)
    y = _experts(te, n_used, xs, wg, wu, wd)
    return _combine(pos, y, x, meta, g_final)


def _row(v):
    return v.reshape(1, -1).astype(F32)


def kernel(x, mem, norm_mix, norm_mem_q, norm_mem_kv, norm_ffn, norm_final, even_w_in, fox_b_f, gmlp_v_gain, gmlp_w_s, gmlp_b_s, even_w_out, odd_w_in, diff_lambda_q1, diff_lambda_k1, diff_lambda_q2, diff_lambda_k2, diff_subln_gain, conv_w, odd_w_out, rel_bias, mem_w_q, mem_w_kv, mem_w_o, ffn_w_gate, ffn_w_up, ffn_w_down, router_w, moe_w_gate, moe_w_up, moe_w_down):
    batch, seq, d = x.shape
    t = batch * seq
    depth = norm_mix.shape[0]
    assert depth == 2 and seq % ATT_BLOCK == 0 and seq % TOK_TILE == 0
    xf = x.reshape(t, d)
    hd = d // N_HEADS
    aw = gmlp_v_gain.shape[1]
    n_blk = gmlp_w_s.shape[2]

    w_in = even_w_in[0]
    q0 = 2 * aw
    bw = N_HEADS * LANES
    scale = jnp.ones((w_in.shape[1],), F32).at[q0:q0 + bw].set(float(LANES) ** -0.5)
    w_in = jnp.pad(w_in * scale, ((0, 0), (0, LANES - N_HEADS))).astype(BF16)
    bf = jnp.pad(fox_b_f[0], (0, LANES - N_HEADS)).reshape(1, LANES)
    bs = jnp.broadcast_to(gmlp_b_s[0][:, :, None], (A_GROUPS, n_blk, aw // A_GROUPS)).astype(F32)
    ya, qkv, aug = _even_in(xf, _row(norm_mix[0]), w_in, bf, _row(gmlp_v_gain[0]),
                            gmlp_w_s[0], bs, seq=seq)
    yb = _fox_attention(qkv, aug, batch=batch, seq=seq)

    def mem_attention(layer, x_in, ya_, yb_, w_out):
        km, vm = _mem_kv(mem, _row(norm_mem_kv[layer]), mem_w_kv[layer].astype(BF16))
        wq = (mem_w_q[layer] * float(hd) ** -0.5).astype(BF16)
        return _post_mixer(x_in, ya_, yb_, w_out.astype(BF16), _row(norm_mem_q[layer]), wq, km, vm,
                           mem_w_o[layer].astype(BF16), seq=seq)

    xf = mem_attention(0, xf, ya, yb, even_w_out[0])
    xf = _ffn(xf, _row(norm_ffn[0]), ffn_w_gate[0].astype(BF16), ffn_w_up[0].astype(BF16),
              ffn_w_down[0].astype(BF16), seq=seq)

    w_in = odd_w_in[0]
    qk_dim = LANES // 2
    qk_w = 2 * N_HEADS * qk_dim
    perm = np.arange(qk_w).reshape(2, N_HEADS, qk_dim).transpose(1, 0, 2).reshape(-1)
    cols = np.concatenate([perm, qk_w + perm, np.arange(2 * qk_w, w_in.shape[1])])
    scale = jnp.ones((w_in.shape[1],), F32).at[:qk_w].set(float(qk_dim) ** -0.5)
    w_in = (w_in * scale)[:, cols].astype(BF16)
    qkv, yd = _odd_in(xf, _row(norm_mix[1]), w_in, conv_w[0].astype(F32), seq=seq,
                      qw=2 * qk_w + N_HEADS * LANES)
    lam_init = 0.8 - 0.6 * math.exp(-0.3 * 1)
    lam = (jnp.exp(jnp.sum(diff_lambda_q1[0] * diff_lambda_k1[0]))
           - jnp.exp(jnp.sum(diff_lambda_q2[0] * diff_lambda_k2[0])) + lam_init)
    bias = _diff_bias_blocks(rel_bias, min(ATT_BLOCK, seq))
    yc = _diff_attention(qkv, lam.reshape(1, 1).astype(F32), bias, _row(diff_subln_gain[0]),
                         batch=batch, seq=seq, out_scale=1.0 - lam_init)
    xf = mem_attention(1, xf, yc, yd, odd_w_out[0])
    wr = jnp.pad(router_w[0], ((0, 0), (0, LANES - N_EXPERTS))).astype(F32)
    out = _moe_and_final_norm(xf, _row(norm_ffn[1]), wr, moe_w_gate[0].astype(BF16),
                              moe_w_up[0].astype(BF16), moe_w_down[0].astype(BF16),
                              _row(norm_final), seq=seq)
    return out.reshape(batch, seq, d)
```

```python
import functools
import math

import numpy as np
import jax
import jax.numpy as jnp
from jax import lax
from jax.experimental import pallas as pl
from jax.experimental.pallas import tpu as pltpu

F32 = jnp.float32
BF16 = jnp.bfloat16
I32 = jnp.int32

EPS = 1e-6
NEG = -1e30
LANES = 128
VMEM_LIMIT = 56 * 1024 * 1024

CHUNK = 64
A_GROUPS = 4
N_HEADS = 4
N_EXPERTS = 8
REL_BUCKETS = 32
REL_MAX_DIST = 128

TOK_TILE = 512
ATT_BLOCK = 512
ATT_STRIP = 32
MOE_TILE = 512
MOE_FF_CHUNK = 1792
FFN_CHUNK = 512
DISPATCH_CHUNK = 256
COMBINE_TILE = 256
DMA_UNROLL = 8


def _dot(a, b):
    return jnp.dot(a, b, preferred_element_type=F32)


def _dot_nt(a, b):
    return lax.dot_general(a, b, (((1,), (1,)), ((), ())), preferred_element_type=F32)


def _rmsnorm(x, g):
    return x * lax.rsqrt(jnp.mean(x * x, axis=-1, keepdims=True) + EPS) * g


def _params(*sem):
    return pltpu.CompilerParams(dimension_semantics=sem, vmem_limit_bytes=VMEM_LIMIT)


def _const_spec(shape):
    return pl.BlockSpec(shape, lambda *_: (0,) * len(shape))


def _split3(v):
    a1 = v.astype(BF16)
    r1 = v - a1.astype(F32)
    a2 = r1.astype(BF16)
    a3 = (r1 - a2.astype(F32)).astype(BF16)
    return a1, a2, a3


def _even_in_kernel(x_ref, g_ref, w_ref, bf_ref, vg_ref, ws_ref, bs_ref,
                    ya_ref, qkv_ref, aug_ref, carry_ref, *, tiles_per_batch):
    i = pl.program_id(0)
    tm = x_ref.shape[0]
    aw = ya_ref.shape[1]
    qw = qkv_ref.shape[1]
    xn = _rmsnorm(x_ref[...], g_ref[...]).astype(BF16)

    qkv_ref[...] = _dot(xn, w_ref[:, 2 * aw:2 * aw + qw]).astype(BF16)

    fl = _dot(xn, w_ref[:, 2 * aw + qw:]) + bf_ref[...]
    ls = jnp.minimum(fl, 0.0) - jnp.log1p(jnp.exp(-jnp.abs(fl)))
    row = lax.broadcasted_iota(I32, (tm, tm), 0)
    col = lax.broadcasted_iota(I32, (tm, tm), 1)
    tri = jnp.where(col <= row, 1.0, 0.0).astype(BF16)
    a1, a2, a3 = _split3(ls)
    csum = _dot(tri, a1) + _dot(tri, a2) + _dot(tri, a3)

    @pl.when(i % tiles_per_batch == 0)
    def _():
        carry_ref[...] = jnp.zeros_like(carry_ref)

    csum = csum + carry_ref[0:1, :]
    carry_ref[...] = jnp.broadcast_to(csum[tm - 1:tm, :], carry_ref.shape)

    lane = lax.broadcasted_iota(I32, (tm, LANES), 1)
    for h in range(N_HEADS):
        c1, c2, c3 = (piece.astype(F32)
                      for piece in _split3(jnp.broadcast_to(csum[:, h:h + 1], (tm, LANES))))
        aq = jnp.where(lane == 0, c1, jnp.where(lane == 1, c2, jnp.where(lane == 2, c3,
                       jnp.where(lane < 6, 1.0, 0.0))))
        ak = jnp.where(lane == 3, -c1, jnp.where(lane == 4, -c2, jnp.where(lane == 5, -c3,
                       jnp.where(lane < 3, 1.0, 0.0))))
        aug_ref[:, h * LANES:(h + 1) * LANES] = aq.astype(BF16)
        aug_ref[:, (N_HEADS + h) * LANES:(N_HEADS + h + 1) * LANES] = ak.astype(BF16)

    gu = jax.nn.gelu(_dot(xn, w_ref[:, 0:aw]), approximate=True)
    gv = jax.nn.gelu(_dot(xn, w_ref[:, aw:2 * aw]), approximate=True)
    blk = ws_ref.shape[1]
    ch = aw // A_GROUPS
    r = lax.broadcasted_iota(I32, (blk, blk), 0)
    c = lax.broadcasted_iota(I32, (blk, blk), 1)
    causal = (c // CHUNK) <= (r // CHUNK)
    for g in range(A_GROUPS):
        vgrp = gv[:, g * ch:(g + 1) * ch]
        vn = _rmsnorm(vgrp, vg_ref[:, g * ch:(g + 1) * ch]).astype(BF16)
        wmix = jnp.where(causal, ws_ref[g], 0.0).astype(BF16)
        for n in range(tm // blk):
            mixed = _dot(wmix, vn[n * blk:(n + 1) * blk, :]) + bs_ref[g]
            ya_ref[n * blk:(n + 1) * blk, g * ch:(g + 1) * ch] = (
                gu[n * blk:(n + 1) * blk, g * ch:(g + 1) * ch] * mixed).astype(BF16)


def _even_in(x, g, w, bf, vgain, ws, bs, *, seq):
    t, d = x.shape
    tm = min(TOK_TILE, seq)
    aw = vgain.shape[1]
    qw = w.shape[1] - 2 * aw - LANES
    blk = ws.shape[1]
    return pl.pallas_call(
        functools.partial(_even_in_kernel, tiles_per_batch=seq // tm),
        grid=(t // tm,),
        in_specs=[
            pl.BlockSpec((tm, d), lambda i: (i, 0)),
            _const_spec((1, d)),
            _const_spec(w.shape),
            _const_spec((1, LANES)),
            _const_spec((1, aw)),
            _const_spec(ws.shape),
            _const_spec(bs.shape),
        ],
        out_specs=[
            pl.BlockSpec((tm, aw), lambda i: (i, 0)),
            pl.BlockSpec((tm, qw), lambda i: (i, 0)),
            pl.BlockSpec((tm, 2 * N_HEADS * LANES), lambda i: (i, 0)),
        ],
        out_shape=[
            jax.ShapeDtypeStruct((t, aw), BF16),
            jax.ShapeDtypeStruct((t, qw), BF16),
            jax.ShapeDtypeStruct((t, 2 * N_HEADS * LANES), BF16),
        ],
        scratch_shapes=[pltpu.VMEM((8, LANES), F32)],
        compiler_params=_params("arbitrary"),
        name="even_in",
    )(x, g, w, bf, vgain, ws, bs)


def _tri_pairs(nblk):
    qi = np.array([i for i in range(nblk) for _ in range(i + 1)], np.int32)
    kj = np.array([j for i in range(nblk) for j in range(i + 1)], np.int32)
    return jnp.asarray(qi), jnp.asarray(kj)


def _softmax_strips(s_ref, p_ref, alpha_ref, m_ref, h, bias_rows=None):
    rows, tk = s_ref.shape
    for r0 in range(0, rows, ATT_STRIP):
        rs = slice(r0, r0 + ATT_STRIP)
        s = s_ref[rs, :]
        if bias_rows is not None:
            s = s + bias_rows(r0)
        m_prev = m_ref[h, rs, :]
        m_new = jnp.maximum(m_prev, jnp.max(s, axis=-1, keepdims=True))
        alpha_ref[rs, :] = jnp.exp(m_prev - m_new)
        m_ref[h, rs, :] = m_new
        p_ref[rs, :] = jnp.exp((s - jnp.concatenate([m_new] * (tk // LANES), axis=-1)).astype(BF16))


def _accumulate(p_ref, alpha_ref, v, ones_ref, acc_ref, h):
    alpha = alpha_ref[...]
    va = jnp.concatenate([v, ones_ref[...]], axis=-1)
    acc_ref[h] = jnp.concatenate([alpha, alpha], axis=-1) * acc_ref[h] + _dot(p_ref[...], va)


def _fox_kernel(qi_ref, kj_ref, q_ref, aq_ref, k_ref, ak_ref, v_ref, ones_ref, mask_ref, o_ref,
                s_ref, p_ref, alpha_ref, m_ref, acc_ref):
    step = pl.program_id(1)
    i = qi_ref[step]
    j = kj_ref[step]

    @pl.when(j == 0)
    def _():
        m_ref[...] = jnp.full_like(m_ref, NEG)
        acc_ref[...] = jnp.zeros_like(acc_ref)

    def logits(h):
        hs = slice(h * LANES, (h + 1) * LANES)
        qa = jnp.concatenate([q_ref[:, hs], aq_ref[:, hs]], axis=-1)
        ka = jnp.concatenate([k_ref[:, hs], ak_ref[:, hs]], axis=-1)
        s_ref[h % 2] = _dot_nt(qa, ka)

    def sweep(masked):
        logits(0)
        for h in range(N_HEADS):
            hs = slice(h * LANES, (h + 1) * LANES)
            sb = h % 2
            if h + 1 < N_HEADS:
                logits(h + 1)
            bias_rows = (lambda r0: mask_ref[r0:r0 + ATT_STRIP, :]) if masked else None
            _softmax_strips(s_ref.at[sb], p_ref.at[sb], alpha_ref.at[sb], m_ref, h, bias_rows)
            _accumulate(p_ref.at[sb], alpha_ref.at[sb], v_ref[:, hs], ones_ref, acc_ref, h)

    @pl.when(j < i)
    def _():
        sweep(False)

    @pl.when(j == i)
    def _():
        sweep(True)
        for h in range(N_HEADS):
            acc = acc_ref[h]
            o_ref[:, h * LANES:(h + 1) * LANES] = (
                acc[:, :LANES] / acc[:, LANES:LANES + 1]).astype(o_ref.dtype)


def _ones_column(rows):
    ones = np.zeros((rows, LANES), np.float32)
    ones[:, 0] = 1.0
    return jnp.asarray(ones, BF16)


def _fox_attention(qkv, aug, *, batch, seq):
    t = qkv.shape[0]
    w = N_HEADS * LANES
    blk = min(ATT_BLOCK, seq)
    nblk = seq // blk
    qi, kj = _tri_pairs(nblk)
    r = np.arange(blk)
    mask = jnp.asarray(np.where(r[None, :] <= r[:, None], 0.0, NEG), F32)
    grid_spec = pltpu.PrefetchScalarGridSpec(
        num_scalar_prefetch=2,
        grid=(batch, qi.shape[0]),
        in_specs=[
            pl.BlockSpec((blk, w), lambda b, s, qi, kj: (b * nblk + qi[s], 0)),
            pl.BlockSpec((blk, w), lambda b, s, qi, kj: (b * nblk + qi[s], 0)),
            pl.BlockSpec((blk, w), lambda b, s, qi, kj: (b * nblk + kj[s], 1)),
            pl.BlockSpec((blk, w), lambda b, s, qi, kj: (b * nblk + kj[s], 1)),
            pl.BlockSpec((blk, w), lambda b, s, qi, kj: (b * nblk + kj[s], 2)),
            pl.BlockSpec((blk, LANES), lambda b, s, qi, kj: (0, 0)),
            pl.BlockSpec((blk, blk), lambda b, s, qi, kj: (0, 0)),
        ],
        out_specs=pl.BlockSpec((blk, w), lambda b, s, qi, kj: (b * nblk + qi[s], 0)),
        scratch_shapes=[
            pltpu.VMEM((2, blk, blk), F32),
            pltpu.VMEM((2, blk, blk), BF16),
            pltpu.VMEM((2, blk, LANES), F32),
            pltpu.VMEM((N_HEADS, blk, LANES), F32),
            pltpu.VMEM((N_HEADS, blk, 2 * LANES), F32),
        ],
    )
    return pl.pallas_call(
        _fox_kernel,
        grid_spec=grid_spec,
        out_shape=jax.ShapeDtypeStruct((t, w), BF16),
        compiler_params=_params("arbitrary", "arbitrary"),
        name="fox_attention",
    )(qi, kj, qkv, aug, qkv, aug, qkv, _ones_column(blk), mask)


def _diff_kernel(qi_ref, kj_ref, lam_ref, q_ref, k_ref, v_ref, ones_ref, bias_ref, gain_ref, o_ref,
                 qs_ref, s_ref, p_ref, alpha_ref, m_ref, acc_ref, *, out_scale):
    step = pl.program_id(1)
    i = qi_ref[step]
    j = kj_ref[step]
    tq = q_ref.shape[0]
    half = LANES // 2

    @pl.when(j == 0)
    def _():
        m_ref[...] = jnp.full_like(m_ref, NEG)
        acc_ref[...] = jnp.zeros_like(acc_ref)
        lane = lax.broadcasted_iota(I32, (tq, LANES), 1)
        zero = jnp.zeros((tq, LANES), BF16)
        for h in range(N_HEADS):
            q = q_ref[:, h * LANES:(h + 1) * LANES]
            qs_ref[h, 0:tq, :] = jnp.where(lane < half, q, zero)
            qs_ref[h, tq:2 * tq, :] = jnp.where(lane < half, zero, q)

    def logits(h):
        s_ref[h % 2] = _dot_nt(qs_ref[h], k_ref[:, h * LANES:(h + 1) * LANES])

    def sweep(near):
        logits(0)
        for h in range(N_HEADS):
            hs = slice(h * LANES, (h + 1) * LANES)
            sb = h % 2
            if h + 1 < N_HEADS:
                logits(h + 1)
            bias_rows = ((lambda r0, h=h: bias_ref[h, i - j, pl.ds(r0 % tq, ATT_STRIP), :])
                         if near else None)
            _softmax_strips(s_ref.at[sb], p_ref.at[sb], alpha_ref.at[sb], m_ref, h, bias_rows)
            _accumulate(p_ref.at[sb], alpha_ref.at[sb], v_ref[:, hs], ones_ref, acc_ref, h)

    @pl.when(j < i - 1)
    def _():
        sweep(False)

    @pl.when(j >= i - 1)
    def _():
        sweep(True)

    @pl.when(j == i)
    def _():
        for h in range(N_HEADS):
            hs = slice(h * LANES, (h + 1) * LANES)
            acc = acc_ref[h]
            o = acc[:, :LANES] / acc[:, LANES:LANES + 1]
            o = o[0:tq, :] - lam_ref[0, 0] * o[tq:2 * tq, :]
            o_ref[:, hs] = (_rmsnorm(o, gain_ref[:, hs]) * out_scale).astype(o_ref.dtype)


def _diff_attention(qkv, lam, bias, gain, *, batch, seq, out_scale):
    t = qkv.shape[0]
    w = N_HEADS * LANES
    blk = bias.shape[2]
    nblk = seq // blk
    qi, kj = _tri_pairs(nblk)
    grid_spec = pltpu.PrefetchScalarGridSpec(
        num_scalar_prefetch=2,
        grid=(batch, qi.shape[0]),
        in_specs=[
            pl.BlockSpec(memory_space=pltpu.SMEM),
            pl.BlockSpec((blk, w), lambda b, s, qi, kj: (b * nblk + qi[s], 0)),
            pl.BlockSpec((blk, w), lambda b, s, qi, kj: (b * nblk + kj[s], 1)),
            pl.BlockSpec((blk, w), lambda b, s, qi, kj: (b * nblk + kj[s], 2)),
            pl.BlockSpec((blk, LANES), lambda b, s, qi, kj: (0, 0)),
            pl.BlockSpec(bias.shape, lambda b, s, qi, kj: (0, 0, 0, 0)),
            pl.BlockSpec((1, w), lambda b, s, qi, kj: (0, 0)),
        ],
        out_specs=pl.BlockSpec((blk, w), lambda b, s, qi, kj: (b * nblk + qi[s], 0)),
        scratch_shapes=[
            pltpu.VMEM((N_HEADS, 2 * blk, LANES), BF16),
            pltpu.VMEM((2, 2 * blk, blk), F32),
            pltpu.VMEM((2, 2 * blk, blk), BF16),
            pltpu.VMEM((2, 2 * blk, LANES), F32),
            pltpu.VMEM((N_HEADS, 2 * blk, LANES), F32),
            pltpu.VMEM((N_HEADS, 2 * blk, 2 * LANES), F32),
        ],
    )
    return pl.pallas_call(
        functools.partial(_diff_kernel, out_scale=out_scale),
        grid_spec=grid_spec,
        out_shape=jax.ShapeDtypeStruct((t, w), BF16),
        compiler_params=_params("arbitrary", "arbitrary"),
        name="diff_attention",
    )(qi, kj, lam, qkv, qkv, qkv, _ones_column(blk), bias, gain)


def _rel_bucket(rel):
    n_half = REL_BUCKETS // 2
    max_exact = n_half // 2
    ret = jnp.where(rel > 0, n_half, 0)
    n = jnp.abs(rel)
    nf = jnp.maximum(n, 1).astype(F32)
    large = max_exact + (jnp.log(nf / max_exact) / math.log(REL_MAX_DIST / max_exact)
                         * (n_half - max_exact)).astype(I32)
    large = jnp.minimum(large, n_half - 1)
    return ret + jnp.where(n < max_exact, n, large)


def _rel_bias_kernel(table_ref, idx_ref, o_ref):
    h = pl.program_id(0)
    idx = idx_ref[0]
    far = table_ref[REL_BUCKETS // 2 - 1, h]
    out = jnp.full(idx.shape, NEG, F32)
    for b in range(REL_BUCKETS):
        out = jnp.where(idx == b, table_ref[b, h] - far, out)
    o_ref[0, 0] = out


def _diff_bias_blocks(rel_table, blk):
    assert blk >= REL_MAX_DIST
    r = jnp.arange(blk)[:, None]
    c = jnp.arange(blk)[None, :]
    diag = jnp.where((c // CHUNK) <= (r // CHUNK), _rel_bucket(c - r), -1)
    prev = _rel_bucket(c - r - blk)
    idx = jnp.stack([diag, prev], axis=0).astype(I32)
    n_heads = rel_table.shape[1]
    return pl.pallas_call(
        _rel_bias_kernel,
        grid=(n_heads, 2),
        in_specs=[
            pl.BlockSpec(memory_space=pltpu.SMEM),
            pl.BlockSpec((1, blk, blk), lambda h, s: (s, 0, 0)),
        ],
        out_specs=pl.BlockSpec((1, 1, blk, blk), lambda h, s: (h, s, 0, 0)),
        out_shape=jax.ShapeDtypeStruct((n_heads, 2, blk, blk), F32),
        compiler_params=_params("arbitrary", "arbitrary"),
        name="rel_bias",
    )(rel_table.astype(F32), idx)


def _odd_in_kernel(x_ref, g_ref, w_ref, cw_ref, qkv_ref, yd_ref, zbuf_ref, *, tiles_per_batch):
    i = pl.program_id(0)
    tm = x_ref.shape[0]
    qw = qkv_ref.shape[1]
    dw = yd_ref.shape[1]
    xn = _rmsnorm(x_ref[...], g_ref[...]).astype(BF16)
    qkv_ref[...] = _dot(xn, w_ref[:, 0:qw]).astype(BF16)
    hh = _dot(xn, w_ref[:, qw:qw + dw])
    gb = _dot(xn, w_ref[:, qw + dw:qw + 2 * dw])
    gc = _dot(xn, w_ref[:, qw + 2 * dw:qw + 3 * dw])
    z = gc * hh

    @pl.when(i % tiles_per_batch == 0)
    def _():
        zbuf_ref[0:8, :] = jnp.zeros((8, dw), F32)

    @pl.when(i % tiles_per_batch != 0)
    def _():
        zbuf_ref[0:8, :] = zbuf_ref[tm:tm + 8, :]

    zbuf_ref[8:tm + 8, :] = z
    y = (cw_ref[0:1, :] * zbuf_ref[6:tm + 6, :] + cw_ref[1:2, :] * zbuf_ref[7:tm + 7, :]
         + cw_ref[2:3, :] * z)
    yd_ref[...] = (gb * y).astype(BF16)


def _odd_in(x, g, w, cw, *, seq, qw):
    t, d = x.shape
    tm = min(TOK_TILE, seq)
    dw = cw.shape[1]
    return pl.pallas_call(
        functools.partial(_odd_in_kernel, tiles_per_batch=seq // tm),
        grid=(t // tm,),
        in_specs=[
            pl.BlockSpec((tm, d), lambda i: (i, 0)),
            _const_spec((1, d)),
            _const_spec(w.shape),
            _const_spec(cw.shape),
        ],
        out_specs=[
            pl.BlockSpec((tm, qw), lambda i: (i, 0)),
            pl.BlockSpec((tm, dw), lambda i: (i, 0)),
        ],
        out_shape=[
            jax.ShapeDtypeStruct((t, qw), BF16),
            jax.ShapeDtypeStruct((t, dw), BF16),
        ],
        scratch_shapes=[pltpu.VMEM((tm + 8, dw), F32)],
        compiler_params=_params("arbitrary"),
        name="odd_in",
    )(x, g, w, cw)


def _mem_kv_kernel(mem_ref, g_ref, w_ref, k_ref, v_ref):
    d = mem_ref.shape[2]
    mn = _rmsnorm(mem_ref[0], g_ref[...]).astype(BF16)
    k_ref[0] = _dot(mn, w_ref[:, 0:d]).astype(BF16)
    v_ref[0] = _dot(mn, w_ref[:, d:2 * d]).astype(BF16)


def _mem_kv(mem, g, w):
    b, m, d = mem.shape
    return pl.pallas_call(
        _mem_kv_kernel,
        grid=(b,),
        in_specs=[
            pl.BlockSpec((1, m, d), lambda i: (i, 0, 0)),
            _const_spec((1, d)),
            _const_spec(w.shape),
        ],
        out_specs=[pl.BlockSpec((1, m, d), lambda i: (i, 0, 0))] * 2,
        out_shape=[jax.ShapeDtypeStruct((b, m, d), BF16)] * 2,
        compiler_params=_params("arbitrary"),
        name="mem_kv",
    )(mem, g, w)


def _post_mixer_kernel(x_ref, ya_ref, yb_ref, wout_ref, g_ref, wq_ref, k_ref, v_ref, wo_ref, o_ref):
    wa = ya_ref.shape[1]
    d = x_ref.shape[1]
    hd = d // N_HEADS
    x1 = x_ref[...] + _dot(ya_ref[...], wout_ref[0:wa, :]) + _dot(yb_ref[...], wout_ref[wa:, :])
    xn = _rmsnorm(x1, g_ref[...]).astype(BF16)
    q = _dot(xn, wq_ref[...]).astype(BF16)
    heads = []
    for h in range(N_HEADS):
        lg = _dot_nt(q[:, h * hd:(h + 1) * hd], k_ref[0, :, h * hd:(h + 1) * hd])
        p = jnp.exp(lg - jnp.max(lg, axis=-1, keepdims=True))
        p = p / jnp.sum(p, axis=-1, keepdims=True)
        heads.append(_dot(p.astype(BF16), v_ref[0, :, h * hd:(h + 1) * hd]).astype(BF16))
    o_ref[...] = x1 + _dot(jnp.concatenate(heads, axis=-1), wo_ref[...])


def _post_mixer(x, ya, yb, wout, g, wq, kmem, vmem, wo, *, seq):
    t, d = x.shape
    tm = min(TOK_TILE, seq)
    nt = seq // tm
    m = kmem.shape[1]
    wa = ya.shape[1]
    return pl.pallas_call(
        _post_mixer_kernel,
        grid=(t // tm,),
        in_specs=[
            pl.BlockSpec((tm, d), lambda i: (i, 0)),
            pl.BlockSpec((tm, wa), lambda i: (i, 0)),
            pl.BlockSpec((tm, yb.shape[1]), lambda i: (i, 0)),
            _const_spec(wout.shape),
            _const_spec((1, d)),
            _const_spec(wq.shape),
            pl.BlockSpec((1, m, d), lambda i: (i // nt, 0, 0)),
            pl.BlockSpec((1, m, d), lambda i: (i // nt, 0, 0)),
            _const_spec(wo.shape),
        ],
        out_specs=pl.BlockSpec((tm, d), lambda i: (i, 0)),
        out_shape=jax.ShapeDtypeStruct((t, d), F32),
        compiler_params=_params("arbitrary"),
        name="post_mixer",
    )(x, ya, yb, wout, g, wq, kmem, vmem, wo)


def _ffn_kernel(x_ref, g_ref, wg_ref, wu_ref, wd_ref, o_ref):
    x = x_ref[...]
    h = _rmsnorm(x, g_ref[...]).astype(BF16)
    ff = wg_ref.shape[1]
    acc = x
    for c0 in range(0, ff, FFN_CHUNK):
        c1 = min(c0 + FFN_CHUNK, ff)
        a = _dot(h, wg_ref[:, c0:c1])
        u = _dot(h, wu_ref[:, c0:c1])
        acc = acc + _dot((jax.nn.silu(a) * u).astype(BF16), wd_ref[c0:c1, :])
    o_ref[...] = acc


def _ffn(x, g, wg, wu, wd, *, seq):
    t, d = x.shape
    tm = min(TOK_TILE, seq)
    return pl.pallas_call(
        _ffn_kernel,
        grid=(t // tm,),
        in_specs=[
            pl.BlockSpec((tm, d), lambda i: (i, 0)),
            _const_spec((1, d)),
            _const_spec(wg.shape),
            _const_spec(wu.shape),
            _const_spec(wd.shape),
        ],
        out_specs=pl.BlockSpec((tm, d), lambda i: (i, 0)),
        out_shape=jax.ShapeDtypeStruct((t, d), F32),
        compiler_params=_params("arbitrary"),
        name="ffn",
    )(x, g, wg, wu, wd)


META_I1, META_I2, META_R1, META_R2, META_G1, META_G2 = range(6)


def _router_kernel(x_ref, g_ref, wr_ref, meta_ref, cnt_ref, carry_ref):
    i = pl.program_id(0)
    tm = x_ref.shape[0]
    h = _rmsnorm(x_ref[...], g_ref[...])
    hp = _split3(h)
    wp = _split3(wr_ref[...])
    logits = sum(_dot(hp[a], wp[b]) for a, b in ((2, 0), (0, 2), (1, 1), (1, 0), (0, 1), (0, 0)))
    lane = lax.broadcasted_iota(I32, logits.shape, 1)
    logits = jnp.where(lane < N_EXPERTS, logits, NEG)
    m1 = jnp.max(logits, axis=-1, keepdims=True)
    i1 = jnp.min(jnp.where(logits == m1, lane, LANES), axis=-1, keepdims=True)
    rest = jnp.where(lane == i1, NEG, logits)
    m2 = jnp.max(rest, axis=-1, keepdims=True)
    i2 = jnp.min(jnp.where(rest == m2, lane, LANES), axis=-1, keepdims=True)
    e = jnp.exp(m2 - m1)
    g1 = 1.0 / (1.0 + e)
    g2 = e / (1.0 + e)

    sel = jnp.where((lane == i1) | (lane == i2), 1.0, 0.0)
    row = lax.broadcasted_iota(I32, (tm, tm), 0)
    col = lax.broadcasted_iota(I32, (tm, tm), 1)
    strict = jnp.where(col < row, 1.0, 0.0).astype(BF16)

    @pl.when(i == 0)
    def _():
        carry_ref[...] = jnp.zeros_like(carry_ref)

    rank = _dot(strict, sel.astype(BF16)) + carry_ref[0:1, :]
    total = rank[tm - 1:tm, :] + sel[tm - 1:tm, :]
    carry_ref[...] = jnp.broadcast_to(total, carry_ref.shape)
    cnt_ref[...] = jnp.broadcast_to(total, cnt_ref.shape)
    r1 = jnp.sum(jnp.where(lane == i1, rank, 0.0), axis=-1, keepdims=True)
    r2 = jnp.sum(jnp.where(lane == i2, rank, 0.0), axis=-1, keepdims=True)
    meta = jnp.zeros_like(logits)
    for slot, val in ((META_I1, i1.astype(F32)), (META_I2, i2.astype(F32)), (META_R1, r1),
                      (META_R2, r2), (META_G1, g1), (META_G2, g2)):
        meta = jnp.where(lane == slot, val, meta)
    meta_ref[...] = meta


def _router(x, g, wr, *, seq):
    t, d = x.shape
    tm = min(TOK_TILE, seq)
    return pl.pallas_call(
        _router_kernel,
        grid=(t // tm,),
        in_specs=[
            pl.BlockSpec((tm, d), lambda i: (i, 0)),
            _const_spec((1, d)),
            _const_spec(wr.shape),
        ],
        out_specs=[
            pl.BlockSpec((tm, LANES), lambda i: (i, 0)),
            _const_spec((8, LANES)),
        ],
        out_shape=[
            jax.ShapeDtypeStruct((t, LANES), F32),
            jax.ShapeDtypeStruct((8, LANES), F32),
        ],
        scratch_shapes=[pltpu.VMEM((8, LANES), F32)],
        compiler_params=_params("arbitrary"),
        name="router",
    )(x, g, wr)


def _row_copy(src, src_row, dst, dst_row, sem):
    return pltpu.make_async_copy(src.at[pl.ds(src_row, 1)], dst.at[pl.ds(dst_row, 1)], sem)


def _dispatch_kernel(pos_ref, pad_ref, x_ref, g_ref, xs_ref, h_ref, zrow_ref, sem_ref, zsem_ref):
    c = pl.program_id(0)
    n_chunks = pl.num_programs(0)
    tc = x_ref.shape[0]
    slot = c % 2

    @pl.when(c == 0)
    def _():
        zrow_ref[...] = jnp.zeros_like(zrow_ref)
        for e in range(N_EXPERTS):
            start = pad_ref[2 * e]
            count = pad_ref[2 * e + 1] - start

            def zissue(r, carry, start=start):
                _row_copy(zrow_ref, 0, xs_ref, start + r, zsem_ref).start()
                return carry

            def zwait(r, carry):
                _row_copy(zrow_ref, 0, xs_ref, 0, zsem_ref).wait()
                return carry

            lax.fori_loop(0, count, zissue, 0)
            lax.fori_loop(0, count, zwait, 0)

        tail = pad_ref[2 * N_EXPERTS - 1]
        groups = (xs_ref.shape[0] - tail) // 8

        def tissue(r, carry):
            start = pl.multiple_of(tail + 8 * r, 8)
            pltpu.make_async_copy(zrow_ref, xs_ref.at[pl.ds(start, 8)], zsem_ref).start()
            return carry

        def twait(r, carry):
            pltpu.make_async_copy(zrow_ref, xs_ref.at[pl.ds(0, 8)], zsem_ref).wait()
            return carry

        lax.fori_loop(0, groups, tissue, 0)
        lax.fori_loop(0, groups, twait, 0)

    h_ref[slot] = _rmsnorm(x_ref[...], g_ref[...])

    def issue(r, carry):
        tok = c * tc + r
        _row_copy(h_ref.at[slot], r, xs_ref, pos_ref[2 * tok], sem_ref.at[slot]).start()
        _row_copy(h_ref.at[slot], r, xs_ref, pos_ref[2 * tok + 1], sem_ref.at[slot]).start()
        return carry

    def drain(s):
        for _ in range(2):
            pltpu.make_async_copy(h_ref.at[s], xs_ref.at[pl.ds(0, tc)], sem_ref.at[s]).wait()

    lax.fori_loop(0, tc, issue, 0, unroll=DMA_UNROLL)

    @pl.when(c > 0)
    def _():
        drain(1 - slot)

    @pl.when(c == n_chunks - 1)
    def _():
        drain(slot)


def _dispatch(pos, pad, x, g, n_rows):
    t, d = x.shape
    tc = DISPATCH_CHUNK
    grid_spec = pltpu.PrefetchScalarGridSpec(
        num_scalar_prefetch=2,
        grid=(t // tc,),
        in_specs=[
            pl.BlockSpec((tc, d), lambda c, pos, pad: (c, 0)),
            pl.BlockSpec((1, d), lambda c, pos, pad: (0, 0)),
        ],
        out_specs=pl.BlockSpec(memory_space=pl.ANY),
        scratch_shapes=[
            pltpu.VMEM((2, tc, d), F32),
            pltpu.VMEM((8, d), F32),
            pltpu.SemaphoreType.DMA((2,)),
            pltpu.SemaphoreType.DMA(()),
        ],
    )
    return pl.pallas_call(
        _dispatch_kernel,
        grid_spec=grid_spec,
        out_shape=jax.ShapeDtypeStruct((n_rows, d), F32),
        compiler_params=_params("arbitrary"),
        name="moe_dispatch",
    )(pos, pad, x, g)


def _experts_kernel(te_ref, nused_ref, xs_ref, wg_ref, wu_ref, wd_ref, y_ref, xb_ref, acc_ref):
    p = pl.program_id(0)
    f = pl.program_id(1)

    @pl.when((p >= nused_ref[0]) & (f == 0))
    def _():
        y_ref[...] = jnp.zeros_like(y_ref)

    @pl.when(p < nused_ref[0])
    def _():
        @pl.when(f == 0)
        def _():
            xb_ref[...] = xs_ref[...].astype(BF16)
            acc_ref[...] = jnp.zeros_like(acc_ref)

        xb = xb_ref[...]
        a = _dot(xb, wg_ref[0])
        u = _dot(xb, wu_ref[0])
        acc_ref[...] += _dot((jax.nn.silu(a) * u).astype(BF16), wd_ref[0])

        @pl.when(f == pl.num_programs(1) - 1)
        def _():
            y_ref[...] = acc_ref[...]


def _experts(tile_expert, n_used, xs, wg, wu, wd):
    n_rows, d = xs.shape
    ff = wg.shape[2]
    tm = MOE_TILE
    tf = MOE_FF_CHUNK
    n_tiles = n_rows // tm

    def x_map(p, f, te, nu):
        return (jnp.minimum(p, nu[0] - 1), 0)

    grid_spec = pltpu.PrefetchScalarGridSpec(
        num_scalar_prefetch=2,
        grid=(n_tiles, ff // tf),
        in_specs=[
            pl.BlockSpec((tm, d), x_map),
            pl.BlockSpec((1, d, tf), lambda p, f, te, nu: (te[p], 0, f)),
            pl.BlockSpec((1, d, tf), lambda p, f, te, nu: (te[p], 0, f)),
            pl.BlockSpec((1, tf, d), lambda p, f, te, nu: (te[p], f, 0)),
        ],
        out_specs=pl.BlockSpec((tm, d), lambda p, f, te, nu: (p, 0)),
        scratch_shapes=[pltpu.VMEM((tm, d), BF16), pltpu.VMEM((tm, d), F32)],
    )
    return pl.pallas_call(
        _experts_kernel,
        grid_spec=grid_spec,
        out_shape=jax.ShapeDtypeStruct((n_rows, d), F32),
        compiler_params=_params("arbitrary", "arbitrary"),
        name="moe_experts",
    )(tile_expert, n_used, xs, wg, wu, wd)


def _combine_kernel(pos_ref, y_ref, x_ref, meta_ref, g_ref, o_ref, buf_ref, sem_ref):
    i = pl.program_id(0)
    n = pl.num_programs(0)
    tc = x_ref.shape[0]

    def fetch(tile, slot):
        def body(r, carry):
            tok = tile * tc + r
            for k in range(2):
                pltpu.make_async_copy(y_ref.at[pl.ds(pos_ref[2 * tok + k], 1)],
                                      buf_ref.at[slot, k, pl.ds(r, 1)], sem_ref.at[slot]).start()
            return carry
        lax.fori_loop(0, tc, body, 0, unroll=DMA_UNROLL)

    @pl.when(i == 0)
    def _():
        fetch(0, 0)

    @pl.when(i + 1 < n)
    def _():
        fetch(i + 1, (i + 1) % 2)

    slot = i % 2

    for k in range(2):
        pltpu.make_async_copy(y_ref.at[pl.ds(0, tc)], buf_ref.at[slot, k], sem_ref.at[slot]).wait()
    meta = meta_ref[...]
    g1 = meta[:, META_G1:META_G1 + 1]
    g2 = meta[:, META_G2:META_G2 + 1]
    x = x_ref[...] + g1 * buf_ref[slot, 0] + g2 * buf_ref[slot, 1]
    o_ref[...] = _rmsnorm(x, g_ref[...])


def _combine(pos, y, x, meta, g):
    t, d = x.shape
    tc = COMBINE_TILE
    grid_spec = pltpu.PrefetchScalarGridSpec(
        num_scalar_prefetch=1,
        grid=(t // tc,),
        in_specs=[
            pl.BlockSpec(memory_space=pl.ANY),
            pl.BlockSpec((tc, d), lambda i, pos: (i, 0)),
            pl.BlockSpec((tc, LANES), lambda i, pos: (i, 0)),
            pl.BlockSpec((1, d), lambda i, pos: (0, 0)),
        ],
        out_specs=pl.BlockSpec((tc, d), lambda i, pos: (i, 0)),
        scratch_shapes=[pltpu.VMEM((2, 2, tc, d), F32), pltpu.SemaphoreType.DMA((2,))],
    )
    return pl.pallas_call(
        _combine_kernel,
        grid_spec=grid_spec,
        out_shape=jax.ShapeDtypeStruct((t, d), F32),
        compiler_params=_params("arbitrary"),
        name="moe_combine",
    )(pos, y, x, meta, g)


def _moe_plan(meta, counts, n_tiles):
    cnt = counts[0, :N_EXPERTS].astype(I32)
    tiles = (cnt + MOE_TILE - 1) // MOE_TILE
    tile_end = jnp.cumsum(tiles)
    row_off = (tile_end - tiles) * MOE_TILE
    n_used = tile_end[-1:]
    experts = jnp.arange(N_EXPERTS, dtype=I32)

    def rows(idx_col, rank_col):
        idx = meta[:, idx_col].astype(I32)
        off = jnp.sum(jnp.where(idx[:, None] == experts[None, :], row_off[None, :], 0), axis=1)
        return off + meta[:, rank_col].astype(I32)

    pos = jnp.stack([rows(META_I1, META_R1), rows(META_I2, META_R2)], axis=1).reshape(-1)
    tile_ids = jnp.arange(n_tiles, dtype=I32)
    te = jnp.sum((tile_end[None, :] <= tile_ids[:, None]).astype(I32), axis=1)
    last = jnp.sum((tile_end <= n_used - 1).astype(I32))
    te = jnp.minimum(te, last).astype(I32)
    pad = jnp.stack([row_off + cnt, row_off + tiles * MOE_TILE], axis=1).reshape(-1).astype(I32)
    return pos.astype(I32), pad, te, n_used.astype(I32)


def _moe_and_final_norm(x, g_ffn, wr, wg, wu, wd, g_final, *, seq):
    t, d = x.shape
    n_tiles = 2 * t // MOE_TILE + N_EXPERTS
    meta, counts = _router(x, g_ffn, wr, seq=seq)
    pos, pad, te, n_used = _moe_plan(meta, counts, n_tiles)
    xs = _dispatch(pos, pad, x, g_ffn, n_tiles * MOE_TILE)
    y = _experts(te, n_used, xs, wg, wu, wd)
    return _combine(pos, y, x, meta, g_final)


def _row(v):
    return v.reshape(1, -1).astype(F32)


def kernel(x, mem, norm_mix, norm_mem_q, norm_mem_kv, norm_ffn, norm_final, even_w_in, fox_b_f, gmlp_v_gain, gmlp_w_s, gmlp_b_s, even_w_out, odd_w_in, diff_lambda_q1, diff_lambda_k1, diff_lambda_q2, diff_lambda_k2, diff_subln_gain, conv_w, odd_w_out, rel_bias, mem_w_q, mem_w_kv, mem_w_o, ffn_w_gate, ffn_w_up, ffn_w_down, router_w, moe_w_gate, moe_w_up, moe_w_down):
    batch, seq, d = x.shape
    t = batch * seq
    depth = norm_mix.shape[0]
    assert depth == 2 and seq % ATT_BLOCK == 0 and seq % TOK_TILE == 0
    xf = x.reshape(t, d)
    hd = d // N_HEADS
    aw = gmlp_v_gain.shape[1]
    n_blk = gmlp_w_s.shape[2]

    w_in = even_w_in[0]
    q0 = 2 * aw
    bw = N_HEADS * LANES
    scale = jnp.ones((w_in.shape[1],), F32).at[q0:q0 + bw].set(float(LANES) ** -0.5)
    w_in = jnp.pad(w_in * scale, ((0, 0), (0, LANES - N_HEADS))).astype(BF16)
    bf = jnp.pad(fox_b_f[0], (0, LANES - N_HEADS)).reshape(1, LANES)
    bs = jnp.broadcast_to(gmlp_b_s[0][:, :, None], (A_GROUPS, n_blk, aw // A_GROUPS)).astype(F32)
    ya, qkv, aug = _even_in(xf, _row(norm_mix[0]), w_in, bf, _row(gmlp_v_gain[0]),
                            gmlp_w_s[0], bs, seq=seq)
    yb = _fox_attention(qkv, aug, batch=batch, seq=seq)

    def mem_attention(layer, x_in, ya_, yb_, w_out):
        km, vm = _mem_kv(mem, _row(norm_mem_kv[layer]), mem_w_kv[layer].astype(BF16))
        wq = (mem_w_q[layer] * float(hd) ** -0.5).astype(BF16)
        return _post_mixer(x_in, ya_, yb_, w_out.astype(BF16), _row(norm_mem_q[layer]), wq, km, vm,
                           mem_w_o[layer].astype(BF16), seq=seq)

    xf = mem_attention(0, xf, ya, yb, even_w_out[0])
    xf = _ffn(xf, _row(norm_ffn[0]), ffn_w_gate[0].astype(BF16), ffn_w_up[0].astype(BF16),
              ffn_w_down[0].astype(BF16), seq=seq)

    w_in = odd_w_in[0]
    qk_dim = LANES // 2
    qk_w = 2 * N_HEADS * qk_dim
    perm = np.arange(qk_w).reshape(2, N_HEADS, qk_dim).transpose(1, 0, 2).reshape(-1)
    cols = np.concatenate([perm, qk_w + perm, np.arange(2 * qk_w, w_in.shape[1])])
    scale = jnp.ones((w_in.shape[1],), F32).at[:qk_w].set(float(qk_dim) ** -0.5)
    w_in = (w_in * scale)[:, cols].astype(BF16)
    qkv, yd = _odd_in(xf, _row(norm_mix[1]), w_in, conv_w[0].astype(F32), seq=seq,
                      qw=2 * qk_w + N_HEADS * LANES)
    lam_init = 0.8 - 0.6 * math.exp(-0.3 * 1)
    lam = (jnp.exp(jnp.sum(diff_lambda_q1[0] * diff_lambda_k1[0]))
           - jnp.exp(jnp.sum(diff_lambda_q2[0] * diff_lambda_k2[0])) + lam_init)
    bias = _diff_bias_blocks(rel_bias, min(ATT_BLOCK, seq))
    yc = _diff_attention(qkv, lam.reshape(1, 1).astype(F32), bias, _row(diff_subln_gain[0]),
                         batch=batch, seq=seq, out_scale=1.0 - lam_init)
    xf = mem_attention(1, xf, yc, yd, odd_w_out[0])
    wr = jnp.pad(router_w[0], ((0, 0), (0, LANES - N_EXPERTS))).astype(F32)
    out = _moe_and_final_norm(xf, _row(norm_ffn[1]), wr, moe_w_gate[0].astype(BF16),
                              moe_w_up[0].astype(BF16), moe_w_down[0].astype(BF16),
                              _row(norm_final), seq=seq)
    return out.reshape(batch, seq, d)
```

```python
import functools
import math

import numpy as np
import jax
import jax.numpy as jnp
from jax import lax
from jax.experimental import pallas as pl
from jax.experimental.pallas import tpu as pltpu

F32 = jnp.float32
BF16 = jnp.bfloat16
I32 = jnp.int32

EPS = 1e-6
NEG = -1e30
LANES = 128
VMEM_LIMIT = 56 * 1024 * 1024

CHUNK = 64
A_GROUPS = 4
N_HEADS = 4
N_EXPERTS = 8
REL_BUCKETS = 32
REL_MAX_DIST = 128

TOK_TILE = 512
ATT_BLOCK = 512
ATT_STRIP = 64
MOE_TILE = 1024
MOE_FF_CHUNK = 896
FFN_CHUNK = 512
DISPATCH_CHUNK = 256
COMBINE_TILE = 256
DMA_UNROLL = 8


def _dot(a, b):
    return jnp.dot(a, b, preferred_element_type=F32)


def _dot_nt(a, b):
    return lax.dot_general(a, b, (((1,), (1,)), ((), ())), preferred_element_type=F32)


def _rmsnorm(x, g):
    return x * lax.rsqrt(jnp.mean(x * x, axis=-1, keepdims=True) + EPS) * g


def _params(*sem):
    return pltpu.CompilerParams(dimension_semantics=sem, vmem_limit_bytes=VMEM_LIMIT)


def _const_spec(shape):
    return pl.BlockSpec(shape, lambda *_: (0,) * len(shape))


def _split3(v):
    a1 = v.astype(BF16)
    r1 = v - a1.astype(F32)
    a2 = r1.astype(BF16)
    a3 = (r1 - a2.astype(F32)).astype(BF16)
    return a1, a2, a3


def _even_in_kernel(x_ref, g_ref, w_ref, bf_ref, vg_ref, ws_ref, bs_ref,
                    ya_ref, qkv_ref, aug_ref, carry_ref, *, tiles_per_batch):
    i = pl.program_id(0)
    tm = x_ref.shape[0]
    aw = ya_ref.shape[1]
    qw = qkv_ref.shape[1]
    xn = _rmsnorm(x_ref[...], g_ref[...]).astype(BF16)

    qkv_ref[...] = _dot(xn, w_ref[:, 2 * aw:2 * aw + qw]).astype(BF16)

    fl = _dot(xn, w_ref[:, 2 * aw + qw:]) + bf_ref[...]
    ls = jnp.minimum(fl, 0.0) - jnp.log1p(jnp.exp(-jnp.abs(fl)))
    row = lax.broadcasted_iota(I32, (tm, tm), 0)
    col = lax.broadcasted_iota(I32, (tm, tm), 1)
    tri = jnp.where(col <= row, 1.0, 0.0).astype(BF16)
    a1, a2, a3 = _split3(ls)
    csum = _dot(tri, a1) + _dot(tri, a2) + _dot(tri, a3)

    @pl.when(i % tiles_per_batch == 0)
    def _():
        carry_ref[...] = jnp.zeros_like(carry_ref)

    csum = csum + carry_ref[0:1, :]
    carry_ref[...] = jnp.broadcast_to(csum[tm - 1:tm, :], carry_ref.shape)

    lane = lax.broadcasted_iota(I32, (tm, LANES), 1)
    for h in range(N_HEADS):
        c1, c2, c3 = (piece.astype(F32)
                      for piece in _split3(jnp.broadcast_to(csum[:, h:h + 1], (tm, LANES))))
        aq = jnp.where(lane == 0, c1, jnp.where(lane == 1, c2, jnp.where(lane == 2, c3,
                       jnp.where(lane < 6, 1.0, 0.0))))
        ak = jnp.where(lane == 3, -c1, jnp.where(lane == 4, -c2, jnp.where(lane == 5, -c3,
                       jnp.where(lane < 3, 1.0, 0.0))))
        aug_ref[:, h * LANES:(h + 1) * LANES] = aq.astype(BF16)
        aug_ref[:, (N_HEADS + h) * LANES:(N_HEADS + h + 1) * LANES] = ak.astype(BF16)

    gu = jax.nn.gelu(_dot(xn, w_ref[:, 0:aw]), approximate=True)
    gv = jax.nn.gelu(_dot(xn, w_ref[:, aw:2 * aw]), approximate=True)
    blk = ws_ref.shape[1]
    ch = aw // A_GROUPS
    r = lax.broadcasted_iota(I32, (blk, blk), 0)
    c = lax.broadcasted_iota(I32, (blk, blk), 1)
    causal = (c // CHUNK) <= (r // CHUNK)
    for g in range(A_GROUPS):
        vgrp = gv[:, g * ch:(g + 1) * ch]
        vn = _rmsnorm(vgrp, vg_ref[:, g * ch:(g + 1) * ch]).astype(BF16)
        wmix = jnp.where(causal, ws_ref[g], 0.0).astype(BF16)
        for n in range(tm // blk):
            mixed = _dot(wmix, vn[n * blk:(n + 1) * blk, :]) + bs_ref[g]
            ya_ref[n * blk:(n + 1) * blk, g * ch:(g + 1) * ch] = (
                gu[n * blk:(n + 1) * blk, g * ch:(g + 1) * ch] * mixed).astype(BF16)


def _even_in(x, g, w, bf, vgain, ws, bs, *, seq):
    t, d = x.shape
    tm = min(TOK_TILE, seq)
    aw = vgain.shape[1]
    qw = w.shape[1] - 2 * aw - LANES
    blk = ws.shape[1]
    return pl.pallas_call(
        functools.partial(_even_in_kernel, tiles_per_batch=seq // tm),
        grid=(t // tm,),
        in_specs=[
            pl.BlockSpec((tm, d), lambda i: (i, 0)),
            _const_spec((1, d)),
            _const_spec(w.shape),
            _const_spec((1, LANES)),
            _const_spec((1, aw)),
            _const_spec(ws.shape),
            _const_spec(bs.shape),
        ],
        out_specs=[
            pl.BlockSpec((tm, aw), lambda i: (i, 0)),
            pl.BlockSpec((tm, qw), lambda i: (i, 0)),
            pl.BlockSpec((tm, 2 * N_HEADS * LANES), lambda i: (i, 0)),
        ],
        out_shape=[
            jax.ShapeDtypeStruct((t, aw), BF16),
            jax.ShapeDtypeStruct((t, qw), BF16),
            jax.ShapeDtypeStruct((t, 2 * N_HEADS * LANES), BF16),
        ],
        scratch_shapes=[pltpu.VMEM((8, LANES), F32)],
        compiler_params=_params("arbitrary"),
        name="even_in",
    )(x, g, w, bf, vgain, ws, bs)


def _tri_pairs(nblk):
    qi = np.array([i for i in range(nblk) for _ in range(i + 1)], np.int32)
    kj = np.array([j for i in range(nblk) for j in range(i + 1)], np.int32)
    return jnp.asarray(qi), jnp.asarray(kj)


def _softmax_strips(logits, p_ref, alpha_ref, m_ref, h, bias_rows=None):
    rows, tk = logits.shape
    for r0 in range(0, rows, ATT_STRIP):
        rs = slice(r0, r0 + ATT_STRIP)
        s = logits[rs, :]
        if bias_rows is not None:
            s = s + bias_rows(r0)
        m_prev = m_ref[h, rs, :]
        m_new = jnp.maximum(m_prev, jnp.max(s, axis=-1, keepdims=True))
        alpha_ref[rs, :] = jnp.exp(m_prev - m_new)
        m_ref[h, rs, :] = m_new
        p_ref[rs, :] = jnp.exp((s - jnp.concatenate([m_new] * (tk // LANES), axis=-1)).astype(BF16))


def _accumulate(p_ref, alpha_ref, v, ones_ref, acc_ref, h):
    alpha = alpha_ref[...]
    va = jnp.concatenate([v, ones_ref[...]], axis=-1)
    acc_ref[h] = jnp.concatenate([alpha, alpha], axis=-1) * acc_ref[h] + _dot(p_ref[...], va)


def _fox_kernel(qi_ref, kj_ref, q_ref, aq_ref, k_ref, ak_ref, v_ref, ones_ref, mask_ref, o_ref,
                p_ref, alpha_ref, m_ref, acc_ref):
    step = pl.program_id(1)
    i = qi_ref[step]
    j = kj_ref[step]

    @pl.when(j == 0)
    def _():
        m_ref[...] = jnp.full_like(m_ref, NEG)
        acc_ref[...] = jnp.zeros_like(acc_ref)

    def logits(h):
        hs = slice(h * LANES, (h + 1) * LANES)
        qa = jnp.concatenate([q_ref[:, hs], aq_ref[:, hs]], axis=-1)
        ka = jnp.concatenate([k_ref[:, hs], ak_ref[:, hs]], axis=-1)
        return _dot_nt(qa, ka)

    def sweep(masked):
        nxt = logits(0)
        for h in range(N_HEADS):
            hs = slice(h * LANES, (h + 1) * LANES)
            sb = h % 2
            cur = nxt
            if h + 1 < N_HEADS:
                nxt = logits(h + 1)
            bias_rows = (lambda r0: mask_ref[r0:r0 + ATT_STRIP, :]) if masked else None
            _softmax_strips(cur, p_ref.at[sb], alpha_ref.at[sb], m_ref, h, bias_rows)
            _accumulate(p_ref.at[sb], alpha_ref.at[sb], v_ref[:, hs], ones_ref, acc_ref, h)

    @pl.when(j < i)
    def _():
        sweep(False)

    @pl.when(j == i)
    def _():
        sweep(True)
        for h in range(N_HEADS):
            acc = acc_ref[h]
            o_ref[:, h * LANES:(h + 1) * LANES] = (
                acc[:, :LANES] / acc[:, LANES:LANES + 1]).astype(o_ref.dtype)


def _ones_column(rows):
    ones = np.zeros((rows, LANES), np.float32)
    ones[:, 0] = 1.0
    return jnp.asarray(ones, BF16)


def _fox_attention(qkv, aug, *, batch, seq):
    t = qkv.shape[0]
    w = N_HEADS * LANES
    blk = min(ATT_BLOCK, seq)
    nblk = seq // blk
    qi, kj = _tri_pairs(nblk)
    r = np.arange(blk)
    mask = jnp.asarray(np.where(r[None, :] <= r[:, None], 0.0, NEG), F32)
    grid_spec = pltpu.PrefetchScalarGridSpec(
        num_scalar_prefetch=2,
        grid=(batch, qi.shape[0]),
        in_specs=[
            pl.BlockSpec((blk, w), lambda b, s, qi, kj: (b * nblk + qi[s], 0)),
            pl.BlockSpec((blk, w), lambda b, s, qi, kj: (b * nblk + qi[s], 0)),
            pl.BlockSpec((blk, w), lambda b, s, qi, kj: (b * nblk + kj[s], 1)),
            pl.BlockSpec((blk, w), lambda b, s, qi, kj: (b * nblk + kj[s], 1)),
            pl.BlockSpec((blk, w), lambda b, s, qi, kj: (b * nblk + kj[s], 2)),
            pl.BlockSpec((blk, LANES), lambda b, s, qi, kj: (0, 0)),
            pl.BlockSpec((blk, blk), lambda b, s, qi, kj: (0, 0)),
        ],
        out_specs=pl.BlockSpec((blk, w), lambda b, s, qi, kj: (b * nblk + qi[s], 0)),
        scratch_shapes=[
            pltpu.VMEM((2, blk, blk), BF16),
            pltpu.VMEM((2, blk, LANES), F32),
            pltpu.VMEM((N_HEADS, blk, LANES), F32),
            pltpu.VMEM((N_HEADS, blk, 2 * LANES), F32),
        ],
    )
    return pl.pallas_call(
        _fox_kernel,
        grid_spec=grid_spec,
        out_shape=jax.ShapeDtypeStruct((t, w), BF16),
        compiler_params=_params("arbitrary", "arbitrary"),
        name="fox_attention",
    )(qi, kj, qkv, aug, qkv, aug, qkv, _ones_column(blk), mask)


def _diff_kernel(qi_ref, kj_ref, lam_ref, q_ref, k_ref, v_ref, ones_ref, bias_ref, gain_ref, o_ref,
                 qs_ref, p_ref, alpha_ref, m_ref, acc_ref, *, out_scale):
    step = pl.program_id(1)
    i = qi_ref[step]
    j = kj_ref[step]
    tq = q_ref.shape[0]
    half = LANES // 2

    @pl.when(j == 0)
    def _():
        m_ref[...] = jnp.full_like(m_ref, NEG)
        acc_ref[...] = jnp.zeros_like(acc_ref)
        lane = lax.broadcasted_iota(I32, (tq, LANES), 1)
        zero = jnp.zeros((tq, LANES), BF16)
        for h in range(N_HEADS):
            q = q_ref[:, h * LANES:(h + 1) * LANES]
            qs_ref[h, 0:tq, :] = jnp.where(lane < half, q, zero)
            qs_ref[h, tq:2 * tq, :] = jnp.where(lane < half, zero, q)

    def logits(h):
        return _dot_nt(qs_ref[h], k_ref[:, h * LANES:(h + 1) * LANES])

    def sweep(near):
        nxt = logits(0)
        for h in range(N_HEADS):
            hs = slice(h * LANES, (h + 1) * LANES)
            sb = h % 2
            cur = nxt
            if h + 1 < N_HEADS:
                nxt = logits(h + 1)
            bias_rows = ((lambda r0, h=h: bias_ref[h, i - j, pl.ds(r0 % tq, ATT_STRIP), :])
                         if near else None)
            _softmax_strips(cur, p_ref.at[sb], alpha_ref.at[sb], m_ref, h, bias_rows)
            _accumulate(p_ref.at[sb], alpha_ref.at[sb], v_ref[:, hs], ones_ref, acc_ref, h)

    @pl.when(j < i - 1)
    def _():
        sweep(False)

    @pl.when(j >= i - 1)
    def _():
        sweep(True)

    @pl.when(j == i)
    def _():
        for h in range(N_HEADS):
            hs = slice(h * LANES, (h + 1) * LANES)
            acc = acc_ref[h]
            o = acc[:, :LANES] / acc[:, LANES:LANES + 1]
            o = o[0:tq, :] - lam_ref[0, 0] * o[tq:2 * tq, :]
            o_ref[:, hs] = (_rmsnorm(o, gain_ref[:, hs]) * out_scale).astype(o_ref.dtype)


def _diff_attention(qkv, lam, bias, gain, *, batch, seq, out_scale):
    t = qkv.shape[0]
    w = N_HEADS * LANES
    blk = bias.shape[2]
    nblk = seq // blk
    qi, kj = _tri_pairs(nblk)
    grid_spec = pltpu.PrefetchScalarGridSpec(
        num_scalar_prefetch=2,
        grid=(batch, qi.shape[0]),
        in_specs=[
            pl.BlockSpec(memory_space=pltpu.SMEM),
            pl.BlockSpec((blk, w), lambda b, s, qi, kj: (b * nblk + qi[s], 0)),
            pl.BlockSpec((blk, w), lambda b, s, qi, kj: (b * nblk + kj[s], 1)),
            pl.BlockSpec((blk, w), lambda b, s, qi, kj: (b * nblk + kj[s], 2)),
            pl.BlockSpec((blk, LANES), lambda b, s, qi, kj: (0, 0)),
            pl.BlockSpec(bias.shape, lambda b, s, qi, kj: (0, 0, 0, 0)),
            pl.BlockSpec((1, w), lambda b, s, qi, kj: (0, 0)),
        ],
        out_specs=pl.BlockSpec((blk, w), lambda b, s, qi, kj: (b * nblk + qi[s], 0)),
        scratch_shapes=[
            pltpu.VMEM((N_HEADS, 2 * blk, LANES), BF16),
            pltpu.VMEM((2, 2 * blk, blk), BF16),
            pltpu.VMEM((2, 2 * blk, LANES), F32),
            pltpu.VMEM((N_HEADS, 2 * blk, LANES), F32),
            pltpu.VMEM((N_HEADS, 2 * blk, 2 * LANES), F32),
        ],
    )
    return pl.pallas_call(
        functools.partial(_diff_kernel, out_scale=out_scale),
        grid_spec=grid_spec,
        out_shape=jax.ShapeDtypeStruct((t, w), BF16),
        compiler_params=_params("arbitrary", "arbitrary"),
        name="diff_attention",
    )(qi, kj, lam, qkv, qkv, qkv, _ones_column(blk), bias, gain)


def _rel_bucket(rel):
    n_half = REL_BUCKETS // 2
    max_exact = n_half // 2
    ret = jnp.where(rel > 0, n_half, 0)
    n = jnp.abs(rel)
    nf = jnp.maximum(n, 1).astype(F32)
    large = max_exact + (jnp.log(nf / max_exact) / math.log(REL_MAX_DIST / max_exact)
                         * (n_half - max_exact)).astype(I32)
    large = jnp.minimum(large, n_half - 1)
    return ret + jnp.where(n < max_exact, n, large)


def _rel_bias_kernel(table_ref, idx_ref, o_ref):
    h = pl.program_id(0)
    idx = idx_ref[0]
    far = table_ref[REL_BUCKETS // 2 - 1, h]
    out = jnp.full(idx.shape, NEG, F32)
    for b in range(REL_BUCKETS):
        out = jnp.where(idx == b, table_ref[b, h] - far, out)
    o_ref[0, 0] = out


def _diff_bias_blocks(rel_table, blk):
    assert blk >= REL_MAX_DIST
    r = jnp.arange(blk)[:, None]
    c = jnp.arange(blk)[None, :]
    diag = jnp.where((c // CHUNK) <= (r // CHUNK), _rel_bucket(c - r), -1)
    prev = _rel_bucket(c - r - blk)
    idx = jnp.stack([diag, prev], axis=0).astype(I32)
    n_heads = rel_table.shape[1]
    return pl.pallas_call(
        _rel_bias_kernel,
        grid=(n_heads, 2),
        in_specs=[
            pl.BlockSpec(memory_space=pltpu.SMEM),
            pl.BlockSpec((1, blk, blk), lambda h, s: (s, 0, 0)),
        ],
        out_specs=pl.BlockSpec((1, 1, blk, blk), lambda h, s: (h, s, 0, 0)),
        out_shape=jax.ShapeDtypeStruct((n_heads, 2, blk, blk), F32),
        compiler_params=_params("arbitrary", "arbitrary"),
        name="rel_bias",
    )(rel_table.astype(F32), idx)


def _odd_in_kernel(x_ref, g_ref, w_ref, cw_ref, qkv_ref, yd_ref, zbuf_ref, *, tiles_per_batch):
    i = pl.program_id(0)
    tm = x_ref.shape[0]
    qw = qkv_ref.shape[1]
    dw = yd_ref.shape[1]
    xn = _rmsnorm(x_ref[...], g_ref[...]).astype(BF16)
    qkv_ref[...] = _dot(xn, w_ref[:, 0:qw]).astype(BF16)
    hh = _dot(xn, w_ref[:, qw:qw + dw])
    gb = _dot(xn, w_ref[:, qw + dw:qw + 2 * dw])
    gc = _dot(xn, w_ref[:, qw + 2 * dw:qw + 3 * dw])
    z = gc * hh

    @pl.when(i % tiles_per_batch == 0)
    def _():
        zbuf_ref[0:8, :] = jnp.zeros((8, dw), F32)

    @pl.when(i % tiles_per_batch != 0)
    def _():
        zbuf_ref[0:8, :] = zbuf_ref[tm:tm + 8, :]

    zbuf_ref[8:tm + 8, :] = z
    y = (cw_ref[0:1, :] * zbuf_ref[6:tm + 6, :] + cw_ref[1:2, :] * zbuf_ref[7:tm + 7, :]
         + cw_ref[2:3, :] * z)
    yd_ref[...] = (gb * y).astype(BF16)


def _odd_in(x, g, w, cw, *, seq, qw):
    t, d = x.shape
    tm = min(TOK_TILE, seq)
    dw = cw.shape[1]
    return pl.pallas_call(
        functools.partial(_odd_in_kernel, tiles_per_batch=seq // tm),
        grid=(t // tm,),
        in_specs=[
            pl.BlockSpec((tm, d), lambda i: (i, 0)),
            _const_spec((1, d)),
            _const_spec(w.shape),
            _const_spec(cw.shape),
        ],
        out_specs=[
            pl.BlockSpec((tm, qw), lambda i: (i, 0)),
            pl.BlockSpec((tm, dw), lambda i: (i, 0)),
        ],
        out_shape=[
            jax.ShapeDtypeStruct((t, qw), BF16),
            jax.ShapeDtypeStruct((t, dw), BF16),
        ],
        scratch_shapes=[pltpu.VMEM((tm + 8, dw), F32)],
        compiler_params=_params("arbitrary"),
        name="odd_in",
    )(x, g, w, cw)


def _mem_kv_kernel(mem_ref, g_ref, w_ref, k_ref, v_ref):
    d = mem_ref.shape[2]
    mn = _rmsnorm(mem_ref[0], g_ref[...]).astype(BF16)
    k_ref[0] = _dot(mn, w_ref[:, 0:d]).astype(BF16)
    v_ref[0] = _dot(mn, w_ref[:, d:2 * d]).astype(BF16)


def _mem_kv(mem, g, w):
    b, m, d = mem.shape
    return pl.pallas_call(
        _mem_kv_kernel,
        grid=(b,),
        in_specs=[
            pl.BlockSpec((1, m, d), lambda i: (i, 0, 0)),
            _const_spec((1, d)),
            _const_spec(w.shape),
        ],
        out_specs=[pl.BlockSpec((1, m, d), lambda i: (i, 0, 0))] * 2,
        out_shape=[jax.ShapeDtypeStruct((b, m, d), BF16)] * 2,
        compiler_params=_params("arbitrary"),
        name="mem_kv",
    )(mem, g, w)


def _post_mixer_kernel(x_ref, ya_ref, yb_ref, wout_ref, g_ref, wq_ref, k_ref, v_ref, wo_ref, o_ref):
    wa = ya_ref.shape[1]
    d = x_ref.shape[1]
    hd = d // N_HEADS
    x1 = x_ref[...] + _dot(ya_ref[...], wout_ref[0:wa, :]) + _dot(yb_ref[...], wout_ref[wa:, :])
    xn = _rmsnorm(x1, g_ref[...]).astype(BF16)
    q = _dot(xn, wq_ref[...]).astype(BF16)
    heads = []
    for h in range(N_HEADS):
        lg = _dot_nt(q[:, h * hd:(h + 1) * hd], k_ref[0, :, h * hd:(h + 1) * hd])
        p = jnp.exp(lg - jnp.max(lg, axis=-1, keepdims=True))
        p = p / jnp.sum(p, axis=-1, keepdims=True)
        heads.append(_dot(p.astype(BF16), v_ref[0, :, h * hd:(h + 1) * hd]).astype(BF16))
    o_ref[...] = x1 + _dot(jnp.concatenate(heads, axis=-1), wo_ref[...])


def _post_mixer(x, ya, yb, wout, g, wq, kmem, vmem, wo, *, seq):
    t, d = x.shape
    tm = min(TOK_TILE, seq)
    nt = seq // tm
    m = kmem.shape[1]
    wa = ya.shape[1]
    return pl.pallas_call(
        _post_mixer_kernel,
        grid=(t // tm,),
        in_specs=[
            pl.BlockSpec((tm, d), lambda i: (i, 0)),
            pl.BlockSpec((tm, wa), lambda i: (i, 0)),
            pl.BlockSpec((tm, yb.shape[1]), lambda i: (i, 0)),
            _const_spec(wout.shape),
            _const_spec((1, d)),
            _const_spec(wq.shape),
            pl.BlockSpec((1, m, d), lambda i: (i // nt, 0, 0)),
            pl.BlockSpec((1, m, d), lambda i: (i // nt, 0, 0)),
            _const_spec(wo.shape),
        ],
        out_specs=pl.BlockSpec((tm, d), lambda i: (i, 0)),
        out_shape=jax.ShapeDtypeStruct((t, d), F32),
        compiler_params=_params("arbitrary"),
        name="post_mixer",
    )(x, ya, yb, wout, g, wq, kmem, vmem, wo)


def _ffn_kernel(x_ref, g_ref, wg_ref, wu_ref, wd_ref, o_ref):
    x = x_ref[...]
    h = _rmsnorm(x, g_ref[...]).astype(BF16)
    ff = wg_ref.shape[1]
    acc = x
    for c0 in range(0, ff, FFN_CHUNK):
        c1 = min(c0 + FFN_CHUNK, ff)
        a = _dot(h, wg_ref[:, c0:c1])
        u = _dot(h, wu_ref[:, c0:c1])
        acc = acc + _dot((jax.nn.silu(a) * u).astype(BF16), wd_ref[c0:c1, :])
    o_ref[...] = acc


def _ffn(x, g, wg, wu, wd, *, seq):
    t, d = x.shape
    tm = min(TOK_TILE, seq)
    return pl.pallas_call(
        _ffn_kernel,
        grid=(t // tm,),
        in_specs=[
            pl.BlockSpec((tm, d), lambda i: (i, 0)),
            _const_spec((1, d)),
            _const_spec(wg.shape),
            _const_spec(wu.shape),
            _const_spec(wd.shape),
        ],
        out_specs=pl.BlockSpec((tm, d), lambda i: (i, 0)),
        out_shape=jax.ShapeDtypeStruct((t, d), F32),
        compiler_params=_params("arbitrary"),
        name="ffn",
    )(x, g, wg, wu, wd)


META_I1, META_I2, META_R1, META_R2, META_G1, META_G2 = range(6)


def _router_kernel(x_ref, g_ref, wr_ref, meta_ref, cnt_ref, carry_ref):
    i = pl.program_id(0)
    tm = x_ref.shape[0]
    h = _rmsnorm(x_ref[...], g_ref[...])
    hp = _split3(h)
    wp = _split3(wr_ref[...])
    logits = sum(_dot(hp[a], wp[b]) for a, b in ((2, 0), (0, 2), (1, 1), (1, 0), (0, 1), (0, 0)))
    lane = lax.broadcasted_iota(I32, logits.shape, 1)
    logits = jnp.where(lane < N_EXPERTS, logits, NEG)
    m1 = jnp.max(logits, axis=-1, keepdims=True)
    i1 = jnp.min(jnp.where(logits == m1, lane, LANES), axis=-1, keepdims=True)
    rest = jnp.where(lane == i1, NEG, logits)
    m2 = jnp.max(rest, axis=-1, keepdims=True)
    i2 = jnp.min(jnp.where(rest == m2, lane, LANES), axis=-1, keepdims=True)
    e = jnp.exp(m2 - m1)
    g1 = 1.0 / (1.0 + e)
    g2 = e / (1.0 + e)

    sel = jnp.where((lane == i1) | (lane == i2), 1.0, 0.0)
    row = lax.broadcasted_iota(I32, (tm, tm), 0)
    col = lax.broadcasted_iota(I32, (tm, tm), 1)
    strict = jnp.where(col < row, 1.0, 0.0).astype(BF16)

    @pl.when(i == 0)
    def _():
        carry_ref[...] = jnp.zeros_like(carry_ref)

    rank = _dot(strict, sel.astype(BF16)) + carry_ref[0:1, :]
    total = rank[tm - 1:tm, :] + sel[tm - 1:tm, :]
    carry_ref[...] = jnp.broadcast_to(total, carry_ref.shape)
    cnt_ref[...] = jnp.broadcast_to(total, cnt_ref.shape)
    r1 = jnp.sum(jnp.where(lane == i1, rank, 0.0), axis=-1, keepdims=True)
    r2 = jnp.sum(jnp.where(lane == i2, rank, 0.0), axis=-1, keepdims=True)
    meta = jnp.zeros_like(logits)
    for slot, val in ((META_I1, i1.astype(F32)), (META_I2, i2.astype(F32)), (META_R1, r1),
                      (META_R2, r2), (META_G1, g1), (META_G2, g2)):
        meta = jnp.where(lane == slot, val, meta)
    meta_ref[...] = meta


def _router(x, g, wr, *, seq):
    t, d = x.shape
    tm = min(TOK_TILE, seq)
    return pl.pallas_call(
        _router_kernel,
        grid=(t // tm,),
        in_specs=[
            pl.BlockSpec((tm, d), lambda i: (i, 0)),
            _const_spec((1, d)),
            _const_spec(wr.shape),
        ],
        out_specs=[
            pl.BlockSpec((tm, LANES), lambda i: (i, 0)),
            _const_spec((8, LANES)),
        ],
        out_shape=[
            jax.ShapeDtypeStruct((t, LANES), F32),
            jax.ShapeDtypeStruct((8, LANES), F32),
        ],
        scratch_shapes=[pltpu.VMEM((8, LANES), F32)],
        compiler_params=_params("arbitrary"),
        name="router",
    )(x, g, wr)


def _row_copy(src, src_row, dst, dst_row, sem):
    return pltpu.make_async_copy(src.at[pl.ds(src_row, 1)], dst.at[pl.ds(dst_row, 1)], sem)


def _dispatch_kernel(pos_ref, pad_ref, x_ref, g_ref, xs_ref, h_ref, zrow_ref, sem_ref, zsem_ref):
    c = pl.program_id(0)
    n_chunks = pl.num_programs(0)
    tc = x_ref.shape[0]
    slot = c % 2

    @pl.when(c == 0)
    def _():
        zrow_ref[...] = jnp.zeros_like(zrow_ref)
        for e in range(N_EXPERTS):
            start = pad_ref[2 * e]
            count = pad_ref[2 * e + 1] - start

            def zissue(r, carry, start=start):
                _row_copy(zrow_ref, 0, xs_ref, start + r, zsem_ref).start()
                return carry

            def zwait(r, carry):
                _row_copy(zrow_ref, 0, xs_ref, 0, zsem_ref).wait()
                return carry

            lax.fori_loop(0, count, zissue, 0)
            lax.fori_loop(0, count, zwait, 0)

        tail = pad_ref[2 * N_EXPERTS - 1]
        groups = (xs_ref.shape[0] - tail) // 8

        def tissue(r, carry):
            start = pl.multiple_of(tail + 8 * r, 8)
            pltpu.make_async_copy(zrow_ref, xs_ref.at[pl.ds(start, 8)], zsem_ref).start()
            return carry

        def twait(r, carry):
            pltpu.make_async_copy(zrow_ref, xs_ref.at[pl.ds(0, 8)], zsem_ref).wait()
            return carry

        lax.fori_loop(0, groups, tissue, 0)
        lax.fori_loop(0, groups, twait, 0)

    h_ref[slot] = _rmsnorm(x_ref[...], g_ref[...])

    def issue(r, carry):
        tok = c * tc + r
        _row_copy(h_ref.at[slot], r, xs_ref, pos_ref[2 * tok], sem_ref.at[slot]).start()
        _row_copy(h_ref.at[slot], r, xs_ref, pos_ref[2 * tok + 1], sem_ref.at[slot]).start()
        return carry

    def drain(s):
        for _ in range(2):
            pltpu.make_async_copy(h_ref.at[s], xs_ref.at[pl.ds(0, tc)], sem_ref.at[s]).wait()

    lax.fori_loop(0, tc, issue, 0, unroll=DMA_UNROLL)

    @pl.when(c > 0)
    def _():
        drain(1 - slot)

    @pl.when(c == n_chunks - 1)
    def _():
        drain(slot)


def _dispatch(pos, pad, x, g, n_rows):
    t, d = x.shape
    tc = DISPATCH_CHUNK
    grid_spec = pltpu.PrefetchScalarGridSpec(
        num_scalar_prefetch=2,
        grid=(t // tc,),
        in_specs=[
            pl.BlockSpec((tc, d), lambda c, pos, pad: (c, 0)),
            pl.BlockSpec((1, d), lambda c, pos, pad: (0, 0)),
        ],
        out_specs=pl.BlockSpec(memory_space=pl.ANY),
        scratch_shapes=[
            pltpu.VMEM((2, tc, d), F32),
            pltpu.VMEM((8, d), F32),
            pltpu.SemaphoreType.DMA((2,)),
            pltpu.SemaphoreType.DMA(()),
        ],
    )
    return pl.pallas_call(
        _dispatch_kernel,
        grid_spec=grid_spec,
        out_shape=jax.ShapeDtypeStruct((n_rows, d), F32),
        compiler_params=_params("arbitrary"),
        name="moe_dispatch",
    )(pos, pad, x, g)


def _experts_kernel(te_ref, rows_ref, nused_ref, xs_ref, wg_ref, wu_ref, wd_ref, y_ref, xb_ref):
    p = pl.program_id(0)
    f = pl.program_id(1)
    rows = rows_ref[p]
    half = MOE_TILE // 2

    @pl.when(f == 0)
    def _():
        y_ref[...] = jnp.zeros_like(y_ref)

    @pl.when(rows > 0)
    def _():
        @pl.when(f == 0)
        def _():
            xb_ref[...] = xs_ref[...].astype(BF16)

        tf = wg_ref.shape[2]
        wgu = jnp.concatenate([wg_ref[0].astype(BF16), wu_ref[0].astype(BF16)], axis=1)
        wd = wd_ref[0].astype(BF16)

        def run(r0):
            gu = _dot(xb_ref[r0:r0 + half, :], wgu)
            act = (jax.nn.silu(gu[:, :tf]) * gu[:, tf:]).astype(BF16)
            y_ref[r0:r0 + half, :] += _dot(act, wd)

        run(0)

        @pl.when(rows > half)
        def _():
            run(half)


def _experts(tile_expert, tile_rows, n_used, xs, wg, wu, wd):
    n_rows, d = xs.shape
    ff = wg.shape[2]
    tm = MOE_TILE
    tf = MOE_FF_CHUNK
    n_tiles = n_rows // tm

    def x_map(p, f, te, tr, nu):
        return (jnp.minimum(p, nu[0] - 1), 0)

    grid_spec = pltpu.PrefetchScalarGridSpec(
        num_scalar_prefetch=3,
        grid=(n_tiles, ff // tf),
        in_specs=[
            pl.BlockSpec((tm, d), x_map),
            pl.BlockSpec((1, d, tf), lambda p, f, te, tr, nu: (te[p], 0, f)),
            pl.BlockSpec((1, d, tf), lambda p, f, te, tr, nu: (te[p], 0, f)),
            pl.BlockSpec((1, tf, d), lambda p, f, te, tr, nu: (te[p], f, 0)),
        ],
        out_specs=pl.BlockSpec((tm, d), lambda p, f, te, tr, nu: (p, 0)),
        scratch_shapes=[pltpu.VMEM((tm, d), BF16)],
    )
    return pl.pallas_call(
        _experts_kernel,
        grid_spec=grid_spec,
        out_shape=jax.ShapeDtypeStruct((n_rows, d), F32),
        compiler_params=_params("arbitrary", "arbitrary"),
        name="moe_experts",
    )(tile_expert, tile_rows, n_used, xs, wg, wu, wd)


def _combine_kernel(pos_ref, y_ref, x_ref, meta_ref, g_ref, o_ref, buf_ref, sem_ref):
    i = pl.program_id(0)
    n = pl.num_programs(0)
    tc = x_ref.shape[0]

    def fetch(tile, slot):
        def body(r, carry):
            tok = tile * tc + r
            for k in range(2):
                pltpu.make_async_copy(y_ref.at[pl.ds(pos_ref[2 * tok + k], 1)],
                                      buf_ref.at[slot, k, pl.ds(r, 1)], sem_ref.at[slot]).start()
            return carry
        lax.fori_loop(0, tc, body, 0, unroll=DMA_UNROLL)

    @pl.when(i == 0)
    def _():
        fetch(0, 0)

    @pl.when(i + 1 < n)
    def _():
        fetch(i + 1, (i + 1) % 2)

    slot = i % 2

    for k in range(2):
        pltpu.make_async_copy(y_ref.at[pl.ds(0, tc)], buf_ref.at[slot, k], sem_ref.at[slot]).wait()
    meta = meta_ref[...]
    g1 = meta[:, META_G1:META_G1 + 1]
    g2 = meta[:, META_G2:META_G2 + 1]
    x = x_ref[...] + g1 * buf_ref[slot, 0] + g2 * buf_ref[slot, 1]
    o_ref[...] = _rmsnorm(x, g_ref[...])


def _combine(pos, y, x, meta, g):
    t, d = x.shape
    tc = COMBINE_TILE
    grid_spec = pltpu.PrefetchScalarGridSpec(
        num_scalar_prefetch=1,
        grid=(t // tc,),
        in_specs=[
            pl.BlockSpec(memory_space=pl.ANY),
            pl.BlockSpec((tc, d), lambda i, pos: (i, 0)),
            pl.BlockSpec((tc, LANES), lambda i, pos: (i, 0)),
            pl.BlockSpec((1, d), lambda i, pos: (0, 0)),
        ],
        out_specs=pl.BlockSpec((tc, d), lambda i, pos: (i, 0)),
        scratch_shapes=[pltpu.VMEM((2, 2, tc, d), F32), pltpu.SemaphoreType.DMA((2,))],
    )
    return pl.pallas_call(
        _combine_kernel,
        grid_spec=grid_spec,
        out_shape=jax.ShapeDtypeStruct((t, d), F32),
        compiler_params=_params("arbitrary"),
        name="moe_combine",
    )(pos, y, x, meta, g)


def _moe_plan(meta, counts, n_tiles):
    cnt = counts[0, :N_EXPERTS].astype(I32)
    tiles = (cnt + MOE_TILE - 1) // MOE_TILE
    tile_end = jnp.cumsum(tiles)
    row_off = (tile_end - tiles) * MOE_TILE
    n_used = tile_end[-1:]
    experts = jnp.arange(N_EXPERTS, dtype=I32)

    def rows(idx_col, rank_col):
        idx = meta[:, idx_col].astype(I32)
        off = jnp.sum(jnp.where(idx[:, None] == experts[None, :], row_off[None, :], 0), axis=1)
        return off + meta[:, rank_col].astype(I32)

    pos = jnp.stack([rows(META_I1, META_R1), rows(META_I2, META_R2)], axis=1).reshape(-1)
    tile_ids = jnp.arange(n_tiles, dtype=I32)
    te = jnp.sum((tile_end[None, :] <= tile_ids[:, None]).astype(I32), axis=1)
    mine = te[:, None] == experts[None, :]
    first_row = tile_ids * MOE_TILE - jnp.sum(jnp.where(mine, row_off[None, :], 0), axis=1)
    tile_rows = jnp.clip(jnp.sum(jnp.where(mine, cnt[None, :], 0), axis=1) - first_row, 0, MOE_TILE)
    last = jnp.sum((tile_end <= n_used - 1).astype(I32))
    te = jnp.minimum(te, last).astype(I32)
    pad = jnp.stack([row_off + cnt, row_off + tiles * MOE_TILE], axis=1).reshape(-1).astype(I32)
    return pos.astype(I32), pad, te, tile_rows.astype(I32), n_used.astype(I32)


def _moe_and_final_norm(x, g_ffn, wr, wg, wu, wd, g_final, *, seq):
    t, d = x.shape
    n_tiles = 2 * t // MOE_TILE + N_EXPERTS
    meta, counts = _router(x, g_ffn, wr, seq=seq)
    pos, pad, te, tile_rows, n_used = _moe_plan(meta, counts, n_tiles)
    xs = _dispatch(pos, pad, x, g_ffn, n_tiles * MOE_TILE)
    y = _experts(te, tile_rows, n_used, xs, wg, wu, wd)
    return _combine(pos, y, x, meta, g_final)


def _row(v):
    return v.reshape(1, -1).astype(F32)


def kernel(x, mem, norm_mix, norm_mem_q, norm_mem_kv, norm_ffn, norm_final, even_w_in, fox_b_f, gmlp_v_gain, gmlp_w_s, gmlp_b_s, even_w_out, odd_w_in, diff_lambda_q1, diff_lambda_k1, diff_lambda_q2, diff_lambda_k2, diff_subln_gain, conv_w, odd_w_out, rel_bias, mem_w_q, mem_w_kv, mem_w_o, ffn_w_gate, ffn_w_up, ffn_w_down, router_w, moe_w_gate, moe_w_up, moe_w_down):
    batch, seq, d = x.shape
    t = batch * seq
    depth = norm_mix.shape[0]
    assert depth == 2 and seq % ATT_BLOCK == 0 and seq % TOK_TILE == 0
    xf = x.reshape(t, d)
    hd = d // N_HEADS
    aw = gmlp_v_gain.shape[1]
    n_blk = gmlp_w_s.shape[2]

    w_in = even_w_in[0]
    q0 = 2 * aw
    bw = N_HEADS * LANES
    scale = jnp.ones((w_in.shape[1],), F32).at[q0:q0 + bw].set(float(LANES) ** -0.5)
    w_in = jnp.pad(w_in * scale, ((0, 0), (0, LANES - N_HEADS))).astype(BF16)
    bf = jnp.pad(fox_b_f[0], (0, LANES - N_HEADS)).reshape(1, LANES)
    bs = jnp.broadcast_to(gmlp_b_s[0][:, :, None], (A_GROUPS, n_blk, aw // A_GROUPS)).astype(F32)
    ya, qkv, aug = _even_in(xf, _row(norm_mix[0]), w_in, bf, _row(gmlp_v_gain[0]),
                            gmlp_w_s[0], bs, seq=seq)
    yb = _fox_attention(qkv, aug, batch=batch, seq=seq)

    def mem_attention(layer, x_in, ya_, yb_, w_out):
        km, vm = _mem_kv(mem, _row(norm_mem_kv[layer]), mem_w_kv[layer].astype(BF16))
        wq = (mem_w_q[layer] * float(hd) ** -0.5).astype(BF16)
        return _post_mixer(x_in, ya_, yb_, w_out.astype(BF16), _row(norm_mem_q[layer]), wq, km, vm,
                           mem_w_o[layer].astype(BF16), seq=seq)

    xf = mem_attention(0, xf, ya, yb, even_w_out[0])
    xf = _ffn(xf, _row(norm_ffn[0]), ffn_w_gate[0].astype(BF16), ffn_w_up[0].astype(BF16),
              ffn_w_down[0].astype(BF16), seq=seq)

    w_in = odd_w_in[0]
    qk_dim = LANES // 2
    qk_w = 2 * N_HEADS * qk_dim
    perm = np.arange(qk_w).reshape(2, N_HEADS, qk_dim).transpose(1, 0, 2).reshape(-1)
    cols = np.concatenate([perm, qk_w + perm, np.arange(2 * qk_w, w_in.shape[1])])
    scale = jnp.ones((w_in.shape[1],), F32).at[:qk_w].set(float(qk_dim) ** -0.5)
    w_in = (w_in * scale)[:, cols].astype(BF16)
    qkv, yd = _odd_in(xf, _row(norm_mix[1]), w_in, conv_w[0].astype(F32), seq=seq,
                      qw=2 * qk_w + N_HEADS * LANES)
    lam_init = 0.8 - 0.6 * math.exp(-0.3 * 1)
    lam = (jnp.exp(jnp.sum(diff_lambda_q1[0] * diff_lambda_k1[0]))
           - jnp.exp(jnp.sum(diff_lambda_q2[0] * diff_lambda_k2[0])) + lam_init)
    bias = _diff_bias_blocks(rel_bias, min(ATT_BLOCK, seq))
    yc = _diff_attention(qkv, lam.reshape(1, 1).astype(F32), bias, _row(diff_subln_gain[0]),
                         batch=batch, seq=seq, out_scale=1.0 - lam_init)
    xf = mem_attention(1, xf, yc, yd, odd_w_out[0])
    wr = jnp.pad(router_w[0], ((0, 0), (0, LANES - N_EXPERTS))).astype(F32)
    out = _moe_and_final_norm(xf, _row(norm_ffn[1]), wr, moe_w_gate[0], moe_w_up[0], moe_w_down[0],
                              _row(norm_final), seq=seq)
    return out.reshape(batch, seq, d)
```

```python
import functools
import math

import numpy as np
import jax
import jax.numpy as jnp
from jax import lax
from jax.experimental import pallas as pl
from jax.experimental.pallas import tpu as pltpu

F32 = jnp.float32
BF16 = jnp.bfloat16
I32 = jnp.int32

EPS = 1e-6
NEG = -1e30
LANES = 128
VMEM_LIMIT = 56 * 1024 * 1024

CHUNK = 64
A_GROUPS = 4
N_HEADS = 4
N_EXPERTS = 8
REL_BUCKETS = 32
REL_MAX_DIST = 128

TOK_TILE = 512
FOX_BLOCK = 1024
ATT_BLOCK = 512
ATT_STRIP = 64
MOE_TILE = 1024
MOE_FF_CHUNK = 896
FFN_CHUNK = 512
DISPATCH_CHUNK = 256
COMBINE_TILE = 256
DMA_UNROLL = 8


def _dot(a, b):
    return jnp.dot(a, b, preferred_element_type=F32)


def _dot_nt(a, b):
    return lax.dot_general(a, b, (((1,), (1,)), ((), ())), preferred_element_type=F32)


def _rmsnorm(x, g):
    return x * lax.rsqrt(jnp.mean(x * x, axis=-1, keepdims=True) + EPS) * g


def _params(*sem):
    return pltpu.CompilerParams(dimension_semantics=sem, vmem_limit_bytes=VMEM_LIMIT)


def _const_spec(shape):
    return pl.BlockSpec(shape, lambda *_: (0,) * len(shape))


def _split3(v):
    a1 = v.astype(BF16)
    r1 = v - a1.astype(F32)
    a2 = r1.astype(BF16)
    a3 = (r1 - a2.astype(F32)).astype(BF16)
    return a1, a2, a3


def _even_in_kernel(x_ref, g_ref, w_ref, bf_ref, vg_ref, ws_ref, bs_ref,
                    ya_ref, qkv_ref, aug_ref, carry_ref, *, tiles_per_batch):
    i = pl.program_id(0)
    tm = x_ref.shape[0]
    aw = ya_ref.shape[1]
    qw = qkv_ref.shape[1]
    xn = _rmsnorm(x_ref[...], g_ref[...]).astype(BF16)

    qkv_ref[...] = _dot(xn, w_ref[:, 2 * aw:2 * aw + qw]).astype(BF16)

    fl = _dot(xn, w_ref[:, 2 * aw + qw:]) + bf_ref[...]
    ls = jnp.minimum(fl, 0.0) - jnp.log1p(jnp.exp(-jnp.abs(fl)))
    row = lax.broadcasted_iota(I32, (tm, tm), 0)
    col = lax.broadcasted_iota(I32, (tm, tm), 1)
    tri = jnp.where(col <= row, 1.0, 0.0).astype(BF16)
    a1, a2, a3 = _split3(ls)
    csum = _dot(tri, a1) + _dot(tri, a2) + _dot(tri, a3)

    @pl.when(i % tiles_per_batch == 0)
    def _():
        carry_ref[...] = jnp.zeros_like(carry_ref)

    csum = csum + carry_ref[0:1, :]
    carry_ref[...] = jnp.broadcast_to(csum[tm - 1:tm, :], carry_ref.shape)

    lane = lax.broadcasted_iota(I32, (tm, LANES), 1)
    for h in range(N_HEADS):
        c1, c2, c3 = (piece.astype(F32)
                      for piece in _split3(jnp.broadcast_to(csum[:, h:h + 1], (tm, LANES))))
        aq = jnp.where(lane == 0, c1, jnp.where(lane == 1, c2, jnp.where(lane == 2, c3,
                       jnp.where(lane < 6, 1.0, 0.0))))
        ak = jnp.where(lane == 3, -c1, jnp.where(lane == 4, -c2, jnp.where(lane == 5, -c3,
                       jnp.where(lane < 3, 1.0, 0.0))))
        aug_ref[:, h * LANES:(h + 1) * LANES] = aq.astype(BF16)
        aug_ref[:, (N_HEADS + h) * LANES:(N_HEADS + h + 1) * LANES] = ak.astype(BF16)

    gu = jax.nn.gelu(_dot(xn, w_ref[:, 0:aw]), approximate=True)
    gv = jax.nn.gelu(_dot(xn, w_ref[:, aw:2 * aw]), approximate=True)
    blk = ws_ref.shape[1]
    ch = aw // A_GROUPS
    r = lax.broadcasted_iota(I32, (blk, blk), 0)
    c = lax.broadcasted_iota(I32, (blk, blk), 1)
    causal = (c // CHUNK) <= (r // CHUNK)
    for g in range(A_GROUPS):
        vgrp = gv[:, g * ch:(g + 1) * ch]
        vn = _rmsnorm(vgrp, vg_ref[:, g * ch:(g + 1) * ch]).astype(BF16)
        wmix = jnp.where(causal, ws_ref[g], 0.0).astype(BF16)
        for n in range(tm // blk):
            mixed = _dot(wmix, vn[n * blk:(n + 1) * blk, :]) + bs_ref[g]
            ya_ref[n * blk:(n + 1) * blk, g * ch:(g + 1) * ch] = (
                gu[n * blk:(n + 1) * blk, g * ch:(g + 1) * ch] * mixed).astype(BF16)


def _even_in(x, g, w, bf, vgain, ws, bs, *, seq):
    t, d = x.shape
    tm = min(TOK_TILE, seq)
    aw = vgain.shape[1]
    qw = w.shape[1] - 2 * aw - LANES
    blk = ws.shape[1]
    return pl.pallas_call(
        functools.partial(_even_in_kernel, tiles_per_batch=seq // tm),
        grid=(t // tm,),
        in_specs=[
            pl.BlockSpec((tm, d), lambda i: (i, 0)),
            _const_spec((1, d)),
            _const_spec(w.shape),
            _const_spec((1, LANES)),
            _const_spec((1, aw)),
            _const_spec(ws.shape),
            _const_spec(bs.shape),
        ],
        out_specs=[
            pl.BlockSpec((tm, aw), lambda i: (i, 0)),
            pl.BlockSpec((tm, qw), lambda i: (i, 0)),
            pl.BlockSpec((tm, 2 * N_HEADS * LANES), lambda i: (i, 0)),
        ],
        out_shape=[
            jax.ShapeDtypeStruct((t, aw), BF16),
            jax.ShapeDtypeStruct((t, qw), BF16),
            jax.ShapeDtypeStruct((t, 2 * N_HEADS * LANES), BF16),
        ],
        scratch_shapes=[pltpu.VMEM((8, LANES), F32)],
        compiler_params=_params("arbitrary"),
        name="even_in",
    )(x, g, w, bf, vgain, ws, bs)


def _tri_pairs(nblk):
    qi = np.array([i for i in range(nblk) for _ in range(i + 1)], np.int32)
    kj = np.array([j for i in range(nblk) for j in range(i + 1)], np.int32)
    return jnp.asarray(qi), jnp.asarray(kj)


def _softmax_strips(logits, p_ref, alpha_ref, m_ref, h, add_bias=None):
    rows, tk = logits.shape
    for r0 in range(0, rows, ATT_STRIP):
        rs = slice(r0, r0 + ATT_STRIP)
        s = logits[rs, :]
        if add_bias is not None:
            s = add_bias(s, r0)
        m_prev = m_ref[h, rs, :]
        m_new = jnp.maximum(m_prev, jnp.max(s, axis=-1, keepdims=True))
        alpha_ref[rs, :] = jnp.exp(m_prev - m_new)
        m_ref[h, rs, :] = m_new
        p_ref[rs, 0:tk] = jnp.exp(
            (s - jnp.concatenate([m_new] * (tk // LANES), axis=-1)).astype(BF16))


def _accumulate(p_ref, alpha_ref, v, ones, acc_ref, h):
    alpha = alpha_ref[...]
    va = jnp.concatenate([v, ones], axis=-1)
    acc_ref[h] = (jnp.concatenate([alpha, alpha], axis=-1) * acc_ref[h]
                  + _dot(p_ref[:, 0:v.shape[0]], va))


def _fox_kernel(qi_ref, kj_ref, q_ref, aq_ref, k_ref, ak_ref, v_ref, ones_ref, mask_ref, o_ref,
                p_ref, alpha_ref, m_ref, acc_ref):
    step = pl.program_id(1)
    i = qi_ref[step]
    j = kj_ref[step]

    @pl.when(j == 0)
    def _():
        m_ref[...] = jnp.full_like(m_ref, NEG)
        acc_ref[...] = jnp.zeros_like(acc_ref)

    def logits(h):
        hs = slice(h * LANES, (h + 1) * LANES)
        qa = jnp.concatenate([q_ref[:, hs], aq_ref[:, hs]], axis=-1)
        ka = jnp.concatenate([k_ref[:, hs], ak_ref[:, hs]], axis=-1)
        return _dot_nt(qa, ka)

    def sweep(masked):
        nxt = logits(0)
        for h in range(N_HEADS):
            hs = slice(h * LANES, (h + 1) * LANES)
            sb = h % 2
            cur = nxt
            if h + 1 < N_HEADS:
                nxt = logits(h + 1)
            add_mask = (lambda s, r0: s + mask_ref[r0:r0 + ATT_STRIP, :]) if masked else None
            _softmax_strips(cur, p_ref.at[sb], alpha_ref.at[sb], m_ref, h, add_mask)
            _accumulate(p_ref.at[sb], alpha_ref.at[sb], v_ref[:, hs], ones_ref[...], acc_ref, h)

    @pl.when(j < i)
    def _():
        sweep(False)

    @pl.when(j == i)
    def _():
        sweep(True)
        for h in range(N_HEADS):
            acc = acc_ref[h]
            o_ref[:, h * LANES:(h + 1) * LANES] = (
                acc[:, :LANES] / acc[:, LANES:LANES + 1]).astype(o_ref.dtype)


def _ones_column(rows):
    ones = np.zeros((rows, LANES), np.float32)
    ones[:, 0] = 1.0
    return jnp.asarray(ones, BF16)


def _fox_attention(qkv, aug, *, batch, seq):
    t = qkv.shape[0]
    w = N_HEADS * LANES
    blk = min(FOX_BLOCK, seq)
    nblk = seq // blk
    qi, kj = _tri_pairs(nblk)
    r = np.arange(blk)
    mask = jnp.asarray(np.where(r[None, :] <= r[:, None], 0.0, NEG), F32)
    grid_spec = pltpu.PrefetchScalarGridSpec(
        num_scalar_prefetch=2,
        grid=(batch, qi.shape[0]),
        in_specs=[
            pl.BlockSpec((blk, w), lambda b, s, qi, kj: (b * nblk + qi[s], 0)),
            pl.BlockSpec((blk, w), lambda b, s, qi, kj: (b * nblk + qi[s], 0)),
            pl.BlockSpec((blk, w), lambda b, s, qi, kj: (b * nblk + kj[s], 1)),
            pl.BlockSpec((blk, w), lambda b, s, qi, kj: (b * nblk + kj[s], 1)),
            pl.BlockSpec((blk, w), lambda b, s, qi, kj: (b * nblk + kj[s], 2)),
            pl.BlockSpec((blk, LANES), lambda b, s, qi, kj: (0, 0)),
            pl.BlockSpec((blk, blk), lambda b, s, qi, kj: (0, 0)),
        ],
        out_specs=pl.BlockSpec((blk, w), lambda b, s, qi, kj: (b * nblk + qi[s], 0)),
        scratch_shapes=[
            pltpu.VMEM((2, blk, blk), BF16),
            pltpu.VMEM((2, blk, LANES), F32),
            pltpu.VMEM((N_HEADS, blk, LANES), F32),
            pltpu.VMEM((N_HEADS, blk, 2 * LANES), F32),
        ],
    )
    return pl.pallas_call(
        _fox_kernel,
        grid_spec=grid_spec,
        out_shape=jax.ShapeDtypeStruct((t, w), BF16),
        compiler_params=_params("arbitrary", "arbitrary"),
        name="fox_attention",
    )(qi, kj, qkv, aug, qkv, aug, qkv, _ones_column(blk), mask)


def _diff_pairs(nq):
    qi = np.array([i for i in range(nq) for _ in range(i // 2 + 1)], np.int32)
    kj = np.array([j for i in range(nq) for j in range(i // 2 + 1)], np.int32)
    return jnp.asarray(qi), jnp.asarray(kj)


def _diff_kernel(qi_ref, kj_ref, lam_ref, q_ref, k_ref, v_ref, ones_ref, bias_ref, gain_ref, o_ref,
                 qs_ref, p_ref, alpha_ref, m_ref, acc_ref, *, out_scale):
    step = pl.program_id(1)
    i = qi_ref[step]
    j = kj_ref[step]
    tq = q_ref.shape[0]
    half = LANES // 2
    jd = i // 2
    odd = (i % 2) == 1

    @pl.when(j == 0)
    def _():
        m_ref[...] = jnp.full_like(m_ref, NEG)
        acc_ref[...] = jnp.zeros_like(acc_ref)
        lane = lax.broadcasted_iota(I32, (tq, LANES), 1)
        zero = jnp.zeros((tq, LANES), BF16)
        for h in range(N_HEADS):
            q = q_ref[:, h * LANES:(h + 1) * LANES]
            qs_ref[h, 0:tq, :] = jnp.where(lane < half, q, zero)
            qs_ref[h, tq:2 * tq, :] = jnp.where(lane < half, zero, q)

    def sweep(tk, add_bias):
        def logits(h):
            return _dot_nt(qs_ref[h], k_ref[0:tk, h * LANES:(h + 1) * LANES])

        nxt = logits(0)
        for h in range(N_HEADS):
            hs = slice(h * LANES, (h + 1) * LANES)
            sb = h % 2
            cur = nxt
            if h + 1 < N_HEADS:
                nxt = logits(h + 1)
            bias_fn = None if add_bias is None else functools.partial(add_bias, h)
            _softmax_strips(cur, p_ref.at[sb], alpha_ref.at[sb], m_ref, h, bias_fn)
            _accumulate(p_ref.at[sb], alpha_ref.at[sb], v_ref[0:tk, hs], ones_ref[0:tk, :], acc_ref, h)

    def bias_rows(h, which, r0):
        return bias_ref[h, which, pl.ds(r0 % tq, ATT_STRIP), :]

    def odd_diag(h, s, r0):
        return jnp.concatenate([s[:, :tq] + bias_rows(h, 1, r0), s[:, tq:] + bias_rows(h, 0, r0)],
                               axis=-1)

    def even_prev(h, s, r0):
        return jnp.concatenate([s[:, :tq], s[:, tq:] + bias_rows(h, 1, r0)], axis=-1)

    def even_diag(h, s, r0):
        return s + bias_rows(h, 0, r0)

    @pl.when((j < jd - 1) | ((j == jd - 1) & odd))
    def _():
        sweep(2 * tq, None)

    @pl.when((j == jd - 1) & jnp.logical_not(odd))
    def _():
        sweep(2 * tq, even_prev)

    @pl.when((j == jd) & odd)
    def _():
        sweep(2 * tq, odd_diag)

    @pl.when((j == jd) & jnp.logical_not(odd))
    def _():
        sweep(tq, even_diag)

    @pl.when(j == jd)
    def _():
        for h in range(N_HEADS):
            hs = slice(h * LANES, (h + 1) * LANES)
            acc = acc_ref[h]
            o = acc[:, :LANES] / acc[:, LANES:LANES + 1]
            o = o[0:tq, :] - lam_ref[0, 0] * o[tq:2 * tq, :]
            o_ref[:, hs] = (_rmsnorm(o, gain_ref[:, hs]) * out_scale).astype(o_ref.dtype)


def _diff_attention(qkv, lam, bias, gain, *, batch, seq, out_scale):
    t = qkv.shape[0]
    w = N_HEADS * LANES
    blk = bias.shape[2]
    kblk = 2 * blk
    nq = seq // blk
    nk = seq // kblk
    qi, kj = _diff_pairs(nq)
    grid_spec = pltpu.PrefetchScalarGridSpec(
        num_scalar_prefetch=2,
        grid=(batch, qi.shape[0]),
        in_specs=[
            pl.BlockSpec(memory_space=pltpu.SMEM),
            pl.BlockSpec((blk, w), lambda b, s, qi, kj: (b * nq + qi[s], 0)),
            pl.BlockSpec((kblk, w), lambda b, s, qi, kj: (b * nk + kj[s], 1)),
            pl.BlockSpec((kblk, w), lambda b, s, qi, kj: (b * nk + kj[s], 2)),
            pl.BlockSpec((kblk, LANES), lambda b, s, qi, kj: (0, 0)),
            pl.BlockSpec(bias.shape, lambda b, s, qi, kj: (0, 0, 0, 0)),
            pl.BlockSpec((1, w), lambda b, s, qi, kj: (0, 0)),
        ],
        out_specs=pl.BlockSpec((blk, w), lambda b, s, qi, kj: (b * nq + qi[s], 0)),
        scratch_shapes=[
            pltpu.VMEM((N_HEADS, 2 * blk, LANES), BF16),
            pltpu.VMEM((2, 2 * blk, kblk), BF16),
            pltpu.VMEM((2, 2 * blk, LANES), F32),
            pltpu.VMEM((N_HEADS, 2 * blk, LANES), F32),
            pltpu.VMEM((N_HEADS, 2 * blk, 2 * LANES), F32),
        ],
    )
    return pl.pallas_call(
        functools.partial(_diff_kernel, out_scale=out_scale),
        grid_spec=grid_spec,
        out_shape=jax.ShapeDtypeStruct((t, w), BF16),
        compiler_params=_params("arbitrary", "arbitrary"),
        name="diff_attention",
    )(qi, kj, lam, qkv, qkv, qkv, _ones_column(kblk), bias, gain)


def _rel_bucket(rel):
    n_half = REL_BUCKETS // 2
    max_exact = n_half // 2
    ret = jnp.where(rel > 0, n_half, 0)
    n = jnp.abs(rel)
    nf = jnp.maximum(n, 1).astype(F32)
    large = max_exact + (jnp.log(nf / max_exact) / math.log(REL_MAX_DIST / max_exact)
                         * (n_half - max_exact)).astype(I32)
    large = jnp.minimum(large, n_half - 1)
    return ret + jnp.where(n < max_exact, n, large)


def _rel_bias_kernel(table_ref, idx_ref, o_ref):
    h = pl.program_id(0)
    idx = idx_ref[0]
    far = table_ref[REL_BUCKETS // 2 - 1, h]
    out = jnp.full(idx.shape, NEG, F32)
    for b in range(REL_BUCKETS):
        out = jnp.where(idx == b, table_ref[b, h] - far, out)
    o_ref[0, 0] = out


def _diff_bias_blocks(rel_table, blk):
    assert blk >= REL_MAX_DIST
    r = jnp.arange(blk)[:, None]
    c = jnp.arange(blk)[None, :]
    diag = jnp.where((c // CHUNK) <= (r // CHUNK), _rel_bucket(c - r), -1)
    prev = _rel_bucket(c - r - blk)
    idx = jnp.stack([diag, prev], axis=0).astype(I32)
    n_heads = rel_table.shape[1]
    return pl.pallas_call(
        _rel_bias_kernel,
        grid=(n_heads, 2),
        in_specs=[
            pl.BlockSpec(memory_space=pltpu.SMEM),
            pl.BlockSpec((1, blk, blk), lambda h, s: (s, 0, 0)),
        ],
        out_specs=pl.BlockSpec((1, 1, blk, blk), lambda h, s: (h, s, 0, 0)),
        out_shape=jax.ShapeDtypeStruct((n_heads, 2, blk, blk), F32),
        compiler_params=_params("arbitrary", "arbitrary"),
        name="rel_bias",
    )(rel_table.astype(F32), idx)


def _odd_in_kernel(x_ref, g_ref, w_ref, cw_ref, qkv_ref, yd_ref, zbuf_ref, *, tiles_per_batch):
    i = pl.program_id(0)
    tm = x_ref.shape[0]
    qw = qkv_ref.shape[1]
    dw = yd_ref.shape[1]
    xn = _rmsnorm(x_ref[...], g_ref[...]).astype(BF16)
    qkv_ref[...] = _dot(xn, w_ref[:, 0:qw]).astype(BF16)
    hh = _dot(xn, w_ref[:, qw:qw + dw])
    gb = _dot(xn, w_ref[:, qw + dw:qw + 2 * dw])
    gc = _dot(xn, w_ref[:, qw + 2 * dw:qw + 3 * dw])
    z = gc * hh

    @pl.when(i % tiles_per_batch == 0)
    def _():
        zbuf_ref[0:8, :] = jnp.zeros((8, dw), F32)

    @pl.when(i % tiles_per_batch != 0)
    def _():
        zbuf_ref[0:8, :] = zbuf_ref[tm:tm + 8, :]

    zbuf_ref[8:tm + 8, :] = z
    y = (cw_ref[0:1, :] * zbuf_ref[6:tm + 6, :] + cw_ref[1:2, :] * zbuf_ref[7:tm + 7, :]
         + cw_ref[2:3, :] * z)
    yd_ref[...] = (gb * y).astype(BF16)


def _odd_in(x, g, w, cw, *, seq, qw):
    t, d = x.shape
    tm = min(TOK_TILE, seq)
    dw = cw.shape[1]
    return pl.pallas_call(
        functools.partial(_odd_in_kernel, tiles_per_batch=seq // tm),
        grid=(t // tm,),
        in_specs=[
            pl.BlockSpec((tm, d), lambda i: (i, 0)),
            _const_spec((1, d)),
            _const_spec(w.shape),
            _const_spec(cw.shape),
        ],
        out_specs=[
            pl.BlockSpec((tm, qw), lambda i: (i, 0)),
            pl.BlockSpec((tm, dw), lambda i: (i, 0)),
        ],
        out_shape=[
            jax.ShapeDtypeStruct((t, qw), BF16),
            jax.ShapeDtypeStruct((t, dw), BF16),
        ],
        scratch_shapes=[pltpu.VMEM((tm + 8, dw), F32)],
        compiler_params=_params("arbitrary"),
        name="odd_in",
    )(x, g, w, cw)


def _mem_kv_kernel(mem_ref, g_ref, w_ref, k_ref, v_ref):
    d = mem_ref.shape[2]
    mn = _rmsnorm(mem_ref[0], g_ref[...]).astype(BF16)
    k_ref[0] = _dot(mn, w_ref[:, 0:d]).astype(BF16)
    v_ref[0] = _dot(mn, w_ref[:, d:2 * d]).astype(BF16)


def _mem_kv(mem, g, w):
    b, m, d = mem.shape
    return pl.pallas_call(
        _mem_kv_kernel,
        grid=(b,),
        in_specs=[
            pl.BlockSpec((1, m, d), lambda i: (i, 0, 0)),
            _const_spec((1, d)),
            _const_spec(w.shape),
        ],
        out_specs=[pl.BlockSpec((1, m, d), lambda i: (i, 0, 0))] * 2,
        out_shape=[jax.ShapeDtypeStruct((b, m, d), BF16)] * 2,
        compiler_params=_params("arbitrary"),
        name="mem_kv",
    )(mem, g, w)


def _post_mixer_kernel(x_ref, ya_ref, yb_ref, wout_ref, g_ref, wq_ref, k_ref, v_ref, wo_ref, o_ref):
    wa = ya_ref.shape[1]
    d = x_ref.shape[1]
    hd = d // N_HEADS
    x1 = x_ref[...] + _dot(ya_ref[...], wout_ref[0:wa, :]) + _dot(yb_ref[...], wout_ref[wa:, :])
    xn = _rmsnorm(x1, g_ref[...]).astype(BF16)
    q = _dot(xn, wq_ref[...]).astype(BF16)
    heads = []
    for h in range(N_HEADS):
        lg = _dot_nt(q[:, h * hd:(h + 1) * hd], k_ref[0, :, h * hd:(h + 1) * hd])
        p = jnp.exp(lg - jnp.max(lg, axis=-1, keepdims=True))
        p = p / jnp.sum(p, axis=-1, keepdims=True)
        heads.append(_dot(p.astype(BF16), v_ref[0, :, h * hd:(h + 1) * hd]).astype(BF16))
    o_ref[...] = x1 + _dot(jnp.concatenate(heads, axis=-1), wo_ref[...])


def _post_mixer(x, ya, yb, wout, g, wq, kmem, vmem, wo, *, seq):
    t, d = x.shape
    tm = min(TOK_TILE, seq)
    nt = seq // tm
    m = kmem.shape[1]
    wa = ya.shape[1]
    return pl.pallas_call(
        _post_mixer_kernel,
        grid=(t // tm,),
        in_specs=[
            pl.BlockSpec((tm, d), lambda i: (i, 0)),
            pl.BlockSpec((tm, wa), lambda i: (i, 0)),
            pl.BlockSpec((tm, yb.shape[1]), lambda i: (i, 0)),
            _const_spec(wout.shape),
            _const_spec((1, d)),
            _const_spec(wq.shape),
            pl.BlockSpec((1, m, d), lambda i: (i // nt, 0, 0)),
            pl.BlockSpec((1, m, d), lambda i: (i // nt, 0, 0)),
            _const_spec(wo.shape),
        ],
        out_specs=pl.BlockSpec((tm, d), lambda i: (i, 0)),
        out_shape=jax.ShapeDtypeStruct((t, d), F32),
        compiler_params=_params("arbitrary"),
        name="post_mixer",
    )(x, ya, yb, wout, g, wq, kmem, vmem, wo)


def _ffn_kernel(x_ref, g_ref, wg_ref, wu_ref, wd_ref, o_ref):
    x = x_ref[...]
    h = _rmsnorm(x, g_ref[...]).astype(BF16)
    ff = wg_ref.shape[1]
    acc = x
    for c0 in range(0, ff, FFN_CHUNK):
        c1 = min(c0 + FFN_CHUNK, ff)
        a = _dot(h, wg_ref[:, c0:c1])
        u = _dot(h, wu_ref[:, c0:c1])
        acc = acc + _dot((jax.nn.silu(a) * u).astype(BF16), wd_ref[c0:c1, :])
    o_ref[...] = acc


def _ffn(x, g, wg, wu, wd, *, seq):
    t, d = x.shape
    tm = min(TOK_TILE, seq)
    return pl.pallas_call(
        _ffn_kernel,
        grid=(t // tm,),
        in_specs=[
            pl.BlockSpec((tm, d), lambda i: (i, 0)),
            _const_spec((1, d)),
            _const_spec(wg.shape),
            _const_spec(wu.shape),
            _const_spec(wd.shape),
        ],
        out_specs=pl.BlockSpec((tm, d), lambda i: (i, 0)),
        out_shape=jax.ShapeDtypeStruct((t, d), F32),
        compiler_params=_params("arbitrary"),
        name="ffn",
    )(x, g, wg, wu, wd)


META_I1, META_I2, META_R1, META_R2, META_G1, META_G2 = range(6)


def _router_kernel(x_ref, g_ref, wr_ref, meta_ref, cnt_ref, carry_ref):
    i = pl.program_id(0)
    tm = x_ref.shape[0]
    h = _rmsnorm(x_ref[...], g_ref[...])
    hp = _split3(h)
    wp = _split3(wr_ref[...])
    logits = sum(_dot(hp[a], wp[b]) for a, b in ((2, 0), (0, 2), (1, 1), (1, 0), (0, 1), (0, 0)))
    lane = lax.broadcasted_iota(I32, logits.shape, 1)
    logits = jnp.where(lane < N_EXPERTS, logits, NEG)
    m1 = jnp.max(logits, axis=-1, keepdims=True)
    i1 = jnp.min(jnp.where(logits == m1, lane, LANES), axis=-1, keepdims=True)
    rest = jnp.where(lane == i1, NEG, logits)
    m2 = jnp.max(rest, axis=-1, keepdims=True)
    i2 = jnp.min(jnp.where(rest == m2, lane, LANES), axis=-1, keepdims=True)
    e = jnp.exp(m2 - m1)
    g1 = 1.0 / (1.0 + e)
    g2 = e / (1.0 + e)

    sel = jnp.where((lane == i1) | (lane == i2), 1.0, 0.0)
    row = lax.broadcasted_iota(I32, (tm, tm), 0)
    col = lax.broadcasted_iota(I32, (tm, tm), 1)
    strict = jnp.where(col < row, 1.0, 0.0).astype(BF16)

    @pl.when(i == 0)
    def _():
        carry_ref[...] = jnp.zeros_like(carry_ref)

    rank = _dot(strict, sel.astype(BF16)) + carry_ref[0:1, :]
    total = rank[tm - 1:tm, :] + sel[tm - 1:tm, :]
    carry_ref[...] = jnp.broadcast_to(total, carry_ref.shape)
    cnt_ref[...] = jnp.broadcast_to(total, cnt_ref.shape)
    r1 = jnp.sum(jnp.where(lane == i1, rank, 0.0), axis=-1, keepdims=True)
    r2 = jnp.sum(jnp.where(lane == i2, rank, 0.0), axis=-1, keepdims=True)
    meta = jnp.zeros_like(logits)
    for slot, val in ((META_I1, i1.astype(F32)), (META_I2, i2.astype(F32)), (META_R1, r1),
                      (META_R2, r2), (META_G1, g1), (META_G2, g2)):
        meta = jnp.where(lane == slot, val, meta)
    meta_ref[...] = meta


def _router(x, g, wr, *, seq):
    t, d = x.shape
    tm = min(TOK_TILE, seq)
    return pl.pallas_call(
        _router_kernel,
        grid=(t // tm,),
        in_specs=[
            pl.BlockSpec((tm, d), lambda i: (i, 0)),
            _const_spec((1, d)),
            _const_spec(wr.shape),
        ],
        out_specs=[
            pl.BlockSpec((tm, LANES), lambda i: (i, 0)),
            _const_spec((8, LANES)),
        ],
        out_shape=[
            jax.ShapeDtypeStruct((t, LANES), F32),
            jax.ShapeDtypeStruct((8, LANES), F32),
        ],
        scratch_shapes=[pltpu.VMEM((8, LANES), F32)],
        compiler_params=_params("arbitrary"),
        name="router",
    )(x, g, wr)


def _row_copy(src, src_row, dst, dst_row, sem):
    return pltpu.make_async_copy(src.at[pl.ds(src_row, 1)], dst.at[pl.ds(dst_row, 1)], sem)


def _dispatch_kernel(pos_ref, pad_ref, x_ref, g_ref, xs_ref, h_ref, zrow_ref, sem_ref, zsem_ref):
    c = pl.program_id(0)
    n_chunks = pl.num_programs(0)
    tc = x_ref.shape[0]
    slot = c % 2

    @pl.when(c == 0)
    def _():
        zrow_ref[...] = jnp.zeros_like(zrow_ref)
        for e in range(N_EXPERTS):
            start = pad_ref[2 * e]
            count = pad_ref[2 * e + 1] - start

            def zissue(r, carry, start=start):
                _row_copy(zrow_ref, 0, xs_ref, start + r, zsem_ref).start()
                return carry

            def zwait(r, carry):
                _row_copy(zrow_ref, 0, xs_ref, 0, zsem_ref).wait()
                return carry

            lax.fori_loop(0, count, zissue, 0)
            lax.fori_loop(0, count, zwait, 0)

        tail = pad_ref[2 * N_EXPERTS - 1]
        groups = (xs_ref.shape[0] - tail) // 8

        def tissue(r, carry):
            start = pl.multiple_of(tail + 8 * r, 8)
            pltpu.make_async_copy(zrow_ref, xs_ref.at[pl.ds(start, 8)], zsem_ref).start()
            return carry

        def twait(r, carry):
            pltpu.make_async_copy(zrow_ref, xs_ref.at[pl.ds(0, 8)], zsem_ref).wait()
            return carry

        lax.fori_loop(0, groups, tissue, 0)
        lax.fori_loop(0, groups, twait, 0)

    h_ref[slot] = _rmsnorm(x_ref[...], g_ref[...])

    def issue(r, carry):
        tok = c * tc + r
        _row_copy(h_ref.at[slot], r, xs_ref, pos_ref[2 * tok], sem_ref.at[slot]).start()
        _row_copy(h_ref.at[slot], r, xs_ref, pos_ref[2 * tok + 1], sem_ref.at[slot]).start()
        return carry

    def drain(s):
        for _ in range(2):
            pltpu.make_async_copy(h_ref.at[s], xs_ref.at[pl.ds(0, tc)], sem_ref.at[s]).wait()

    lax.fori_loop(0, tc, issue, 0, unroll=DMA_UNROLL)

    @pl.when(c > 0)
    def _():
        drain(1 - slot)

    @pl.when(c == n_chunks - 1)
    def _():
        drain(slot)


def _dispatch(pos, pad, x, g, n_rows):
    t, d = x.shape
    tc = DISPATCH_CHUNK
    grid_spec = pltpu.PrefetchScalarGridSpec(
        num_scalar_prefetch=2,
        grid=(t // tc,),
        in_specs=[
            pl.BlockSpec((tc, d), lambda c, pos, pad: (c, 0)),
            pl.BlockSpec((1, d), lambda c, pos, pad: (0, 0)),
        ],
        out_specs=pl.BlockSpec(memory_space=pl.ANY),
        scratch_shapes=[
            pltpu.VMEM((2, tc, d), F32),
            pltpu.VMEM((8, d), F32),
            pltpu.SemaphoreType.DMA((2,)),
            pltpu.SemaphoreType.DMA(()),
        ],
    )
    return pl.pallas_call(
        _dispatch_kernel,
        grid_spec=grid_spec,
        out_shape=jax.ShapeDtypeStruct((n_rows, d), F32),
        compiler_params=_params("arbitrary"),
        name="moe_dispatch",
    )(pos, pad, x, g)


def _experts_kernel(te_ref, rows_ref, nused_ref, xs_ref, wg_ref, wu_ref, wd_ref, y_ref, xb_ref):
    p = pl.program_id(0)
    f = pl.program_id(1)
    rows = rows_ref[p]
    half = MOE_TILE // 2

    @pl.when(f == 0)
    def _():
        y_ref[...] = jnp.zeros_like(y_ref)

    @pl.when(rows > 0)
    def _():
        @pl.when(f == 0)
        def _():
            xb_ref[...] = xs_ref[...].astype(BF16)

        tf = wg_ref.shape[2]
        wgu = jnp.concatenate([wg_ref[0].astype(BF16), wu_ref[0].astype(BF16)], axis=1)
        wd = wd_ref[0].astype(BF16)

        def run(r0):
            gu = _dot(xb_ref[r0:r0 + half, :], wgu)
            act = (jax.nn.silu(gu[:, :tf]) * gu[:, tf:]).astype(BF16)
            y_ref[r0:r0 + half, :] += _dot(act, wd)

        run(0)

        @pl.when(rows > half)
        def _():
            run(half)


def _experts(tile_expert, tile_rows, n_used, xs, wg, wu, wd):
    n_rows, d = xs.shape
    ff = wg.shape[2]
    tm = MOE_TILE
    tf = MOE_FF_CHUNK
    n_tiles = n_rows // tm

    def x_map(p, f, te, tr, nu):
        return (jnp.minimum(p, nu[0] - 1), 0)

    grid_spec = pltpu.PrefetchScalarGridSpec(
        num_scalar_prefetch=3,
        grid=(n_tiles, ff // tf),
        in_specs=[
            pl.BlockSpec((tm, d), x_map),
            pl.BlockSpec((1, d, tf), lambda p, f, te, tr, nu: (te[p], 0, f)),
            pl.BlockSpec((1, d, tf), lambda p, f, te, tr, nu: (te[p], 0, f)),
            pl.BlockSpec((1, tf, d), lambda p, f, te, tr, nu: (te[p], f, 0)),
        ],
        out_specs=pl.BlockSpec((tm, d), lambda p, f, te, tr, nu: (p, 0)),
        scratch_shapes=[pltpu.VMEM((tm, d), BF16)],
    )
    return pl.pallas_call(
        _experts_kernel,
        grid_spec=grid_spec,
        out_shape=jax.ShapeDtypeStruct((n_rows, d), F32),
        compiler_params=_params("arbitrary", "arbitrary"),
        name="moe_experts",
    )(tile_expert, tile_rows, n_used, xs, wg, wu, wd)


def _combine_kernel(pos_ref, y_ref, x_ref, meta_ref, g_ref, o_ref, buf_ref, sem_ref):
    i = pl.program_id(0)
    n = pl.num_programs(0)
    tc = x_ref.shape[0]

    def fetch(tile, slot):
        def body(r, carry):
            tok = tile * tc + r
            for k in range(2):
                pltpu.make_async_copy(y_ref.at[pl.ds(pos_ref[2 * tok + k], 1)],
                                      buf_ref.at[slot, k, pl.ds(r, 1)], sem_ref.at[slot]).start()
            return carry
        lax.fori_loop(0, tc, body, 0, unroll=DMA_UNROLL)

    @pl.when(i == 0)
    def _():
        fetch(0, 0)

    @pl.when(i + 1 < n)
    def _():
        fetch(i + 1, (i + 1) % 2)

    slot = i % 2

    for k in range(2):
        pltpu.make_async_copy(y_ref.at[pl.ds(0, tc)], buf_ref.at[slot, k], sem_ref.at[slot]).wait()
    meta = meta_ref[...]
    g1 = meta[:, META_G1:META_G1 + 1]
    g2 = meta[:, META_G2:META_G2 + 1]
    x = x_ref[...] + g1 * buf_ref[slot, 0] + g2 * buf_ref[slot, 1]
    o_ref[...] = _rmsnorm(x, g_ref[...])


def _combine(pos, y, x, meta, g):
    t, d = x.shape
    tc = COMBINE_TILE
    grid_spec = pltpu.PrefetchScalarGridSpec(
        num_scalar_prefetch=1,
        grid=(t // tc,),
        in_specs=[
            pl.BlockSpec(memory_space=pl.ANY),
            pl.BlockSpec((tc, d), lambda i, pos: (i, 0)),
            pl.BlockSpec((tc, LANES), lambda i, pos: (i, 0)),
            pl.BlockSpec((1, d), lambda i, pos: (0, 0)),
        ],
        out_specs=pl.BlockSpec((tc, d), lambda i, pos: (i, 0)),
        scratch_shapes=[pltpu.VMEM((2, 2, tc, d), F32), pltpu.SemaphoreType.DMA((2,))],
    )
    return pl.pallas_call(
        _combine_kernel,
        grid_spec=grid_spec,
        out_shape=jax.ShapeDtypeStruct((t, d), F32),
        compiler_params=_params("arbitrary"),
        name="moe_combine",
    )(pos, y, x, meta, g)


def _moe_plan(meta, counts, n_tiles):
    cnt = counts[0, :N_EXPERTS].astype(I32)
    tiles = (cnt + MOE_TILE - 1) // MOE_TILE
    tile_end = jnp.cumsum(tiles)
    row_off = (tile_end - tiles) * MOE_TILE
    n_used = tile_end[-1:]
    experts = jnp.arange(N_EXPERTS, dtype=I32)

    def rows(idx_col, rank_col):
        idx = meta[:, idx_col].astype(I32)
        off = jnp.sum(jnp.where(idx[:, None] == experts[None, :], row_off[None, :], 0), axis=1)
        return off + meta[:, rank_col].astype(I32)

    pos = jnp.stack([rows(META_I1, META_R1), rows(META_I2, META_R2)], axis=1).reshape(-1)
    tile_ids = jnp.arange(n_tiles, dtype=I32)
    te = jnp.sum((tile_end[None, :] <= tile_ids[:, None]).astype(I32), axis=1)
    mine = te[:, None] == experts[None, :]
    first_row = tile_ids * MOE_TILE - jnp.sum(jnp.where(mine, row_off[None, :], 0), axis=1)
    tile_rows = jnp.clip(jnp.sum(jnp.where(mine, cnt[None, :], 0), axis=1) - first_row, 0, MOE_TILE)
    last = jnp.sum((tile_end <= n_used - 1).astype(I32))
    te = jnp.minimum(te, last).astype(I32)
    pad = jnp.stack([row_off + cnt, row_off + tiles * MOE_TILE], axis=1).reshape(-1).astype(I32)
    return pos.astype(I32), pad, te, tile_rows.astype(I32), n_used.astype(I32)


def _moe_and_final_norm(x, g_ffn, wr, wg, wu, wd, g_final, *, seq):
    t, d = x.shape
    n_tiles = 2 * t // MOE_TILE + N_EXPERTS
    meta, counts = _router(x, g_ffn, wr, seq=seq)
    pos, pad, te, tile_rows, n_used = _moe_plan(meta, counts, n_tiles)
    xs = _dispatch(pos, pad, x, g_ffn, n_tiles * MOE_TILE)
    y = _experts(te, tile_rows, n_used, xs, wg, wu, wd)
    return _combine(pos, y, x, meta, g_final)


def _row(v):
    return v.reshape(1, -1).astype(F32)


def kernel(x, mem, norm_mix, norm_mem_q, norm_mem_kv, norm_ffn, norm_final, even_w_in, fox_b_f, gmlp_v_gain, gmlp_w_s, gmlp_b_s, even_w_out, odd_w_in, diff_lambda_q1, diff_lambda_k1, diff_lambda_q2, diff_lambda_k2, diff_subln_gain, conv_w, odd_w_out, rel_bias, mem_w_q, mem_w_kv, mem_w_o, ffn_w_gate, ffn_w_up, ffn_w_down, router_w, moe_w_gate, moe_w_up, moe_w_down):
    batch, seq, d = x.shape
    t = batch * seq
    depth = norm_mix.shape[0]
    assert depth == 2 and seq % (2 * ATT_BLOCK) == 0 and seq % TOK_TILE == 0
    xf = x.reshape(t, d)
    hd = d // N_HEADS
    aw = gmlp_v_gain.shape[1]
    n_blk = gmlp_w_s.shape[2]

    w_in = even_w_in[0]
    q0 = 2 * aw
    bw = N_HEADS * LANES
    scale = jnp.ones((w_in.shape[1],), F32).at[q0:q0 + bw].set(float(LANES) ** -0.5)
    w_in = jnp.pad(w_in * scale, ((0, 0), (0, LANES - N_HEADS))).astype(BF16)
    bf = jnp.pad(fox_b_f[0], (0, LANES - N_HEADS)).reshape(1, LANES)
    bs = jnp.broadcast_to(gmlp_b_s[0][:, :, None], (A_GROUPS, n_blk, aw // A_GROUPS)).astype(F32)
    ya, qkv, aug = _even_in(xf, _row(norm_mix[0]), w_in, bf, _row(gmlp_v_gain[0]),
                            gmlp_w_s[0], bs, seq=seq)
    yb = _fox_attention(qkv, aug, batch=batch, seq=seq)

    def mem_attention(layer, x_in, ya_, yb_, w_out):
        km, vm = _mem_kv(mem, _row(norm_mem_kv[layer]), mem_w_kv[layer].astype(BF16))
        wq = (mem_w_q[layer] * float(hd) ** -0.5).astype(BF16)
        return _post_mixer(x_in, ya_, yb_, w_out.astype(BF16), _row(norm_mem_q[layer]), wq, km, vm,
                           mem_w_o[layer].astype(BF16), seq=seq)

    xf = mem_attention(0, xf, ya, yb, even_w_out[0])
    xf = _ffn(xf, _row(norm_ffn[0]), ffn_w_gate[0].astype(BF16), ffn_w_up[0].astype(BF16),
              ffn_w_down[0].astype(BF16), seq=seq)

    w_in = odd_w_in[0]
    qk_dim = LANES // 2
    qk_w = 2 * N_HEADS * qk_dim
    perm = np.arange(qk_w).reshape(2, N_HEADS, qk_dim).transpose(1, 0, 2).reshape(-1)
    cols = np.concatenate([perm, qk_w + perm, np.arange(2 * qk_w, w_in.shape[1])])
    scale = jnp.ones((w_in.shape[1],), F32).at[:qk_w].set(float(qk_dim) ** -0.5)
    w_in = (w_in * scale)[:, cols].astype(BF16)
    qkv, yd = _odd_in(xf, _row(norm_mix[1]), w_in, conv_w[0].astype(F32), seq=seq,
                      qw=2 * qk_w + N_HEADS * LANES)
    lam_init = 0.8 - 0.6 * math.exp(-0.3 * 1)
    lam = (jnp.exp(jnp.sum(diff_lambda_q1[0] * diff_lambda_k1[0]))
           - jnp.exp(jnp.sum(diff_lambda_q2[0] * diff_lambda_k2[0])) + lam_init)
    bias = _diff_bias_blocks(rel_bias, min(ATT_BLOCK, seq))
    yc = _diff_attention(qkv, lam.reshape(1, 1).astype(F32), bias, _row(diff_subln_gain[0]),
                         batch=batch, seq=seq, out_scale=1.0 - lam_init)
    xf = mem_attention(1, xf, yc, yd, odd_w_out[0])
    wr = jnp.pad(router_w[0], ((0, 0), (0, LANES - N_EXPERTS))).astype(F32)
    out = _moe_and_final_norm(xf, _row(norm_ffn[1]), wr, moe_w_gate[0], moe_w_up[0], moe_w_down[0],
                              _row(norm_final), seq=seq)
    return out.reshape(batch, seq, d)
```

```python
import functools
import math

import numpy as np
import jax
import jax.numpy as jnp
from jax import lax
from jax.experimental import pallas as pl
from jax.experimental.pallas import tpu as pltpu

F32 = jnp.float32
BF16 = jnp.bfloat16
I32 = jnp.int32

EPS = 1e-6
NEG = -1e30
LANES = 128
VMEM_LIMIT = 56 * 1024 * 1024

CHUNK = 64
A_GROUPS = 4
N_HEADS = 4
N_EXPERTS = 8
REL_BUCKETS = 32
REL_MAX_DIST = 128

TOK_TILE = 512
FOX_BLOCK = 1024
ATT_BLOCK = 512
ATT_STRIP = 64
MOE_TILE = 1024
MOE_FF_CHUNK = 896
FFN_CHUNK = 512
INVERT_CHUNK = 512
COMBINE_TILE = 256
DMA_UNROLL = 8


def _dot(a, b):
    return jnp.dot(a, b, preferred_element_type=F32)


def _dot_nt(a, b):
    return lax.dot_general(a, b, (((1,), (1,)), ((), ())), preferred_element_type=F32)


def _rmsnorm(x, g):
    return x * lax.rsqrt(jnp.mean(x * x, axis=-1, keepdims=True) + EPS) * g


def _params(*sem):
    return pltpu.CompilerParams(dimension_semantics=sem, vmem_limit_bytes=VMEM_LIMIT)


def _const_spec(shape):
    return pl.BlockSpec(shape, lambda *_: (0,) * len(shape))


def _split3(v):
    a1 = v.astype(BF16)
    r1 = v - a1.astype(F32)
    a2 = r1.astype(BF16)
    a3 = (r1 - a2.astype(F32)).astype(BF16)
    return a1, a2, a3


def _even_in_kernel(x_ref, g_ref, w_ref, bf_ref, vg_ref, ws_ref, bs_ref,
                    ya_ref, qkv_ref, aug_ref, carry_ref, *, tiles_per_batch):
    i = pl.program_id(0)
    tm = x_ref.shape[0]
    aw = ya_ref.shape[1]
    qw = qkv_ref.shape[1]
    xn = _rmsnorm(x_ref[...], g_ref[...]).astype(BF16)

    qkv_ref[...] = _dot(xn, w_ref[:, 2 * aw:2 * aw + qw]).astype(BF16)

    fl = _dot(xn, w_ref[:, 2 * aw + qw:]) + bf_ref[...]
    ls = jnp.minimum(fl, 0.0) - jnp.log1p(jnp.exp(-jnp.abs(fl)))
    row = lax.broadcasted_iota(I32, (tm, tm), 0)
    col = lax.broadcasted_iota(I32, (tm, tm), 1)
    tri = jnp.where(col <= row, 1.0, 0.0).astype(BF16)
    a1, a2, a3 = _split3(ls)
    csum = _dot(tri, a1) + _dot(tri, a2) + _dot(tri, a3)

    @pl.when(i % tiles_per_batch == 0)
    def _():
        carry_ref[...] = jnp.zeros_like(carry_ref)

    csum = csum + carry_ref[0:1, :]
    carry_ref[...] = jnp.broadcast_to(csum[tm - 1:tm, :], carry_ref.shape)

    lane = lax.broadcasted_iota(I32, (tm, LANES), 1)
    for h in range(N_HEADS):
        c1, c2, c3 = (piece.astype(F32)
                      for piece in _split3(jnp.broadcast_to(csum[:, h:h + 1], (tm, LANES))))
        aq = jnp.where(lane == 0, c1, jnp.where(lane == 1, c2, jnp.where(lane == 2, c3,
                       jnp.where(lane < 6, 1.0, 0.0))))
        ak = jnp.where(lane == 3, -c1, jnp.where(lane == 4, -c2, jnp.where(lane == 5, -c3,
                       jnp.where(lane < 3, 1.0, 0.0))))
        aug_ref[:, h * LANES:(h + 1) * LANES] = aq.astype(BF16)
        aug_ref[:, (N_HEADS + h) * LANES:(N_HEADS + h + 1) * LANES] = ak.astype(BF16)

    gu = jax.nn.gelu(_dot(xn, w_ref[:, 0:aw]), approximate=True)
    gv = jax.nn.gelu(_dot(xn, w_ref[:, aw:2 * aw]), approximate=True)
    blk = ws_ref.shape[1]
    ch = aw // A_GROUPS
    r = lax.broadcasted_iota(I32, (blk, blk), 0)
    c = lax.broadcasted_iota(I32, (blk, blk), 1)
    causal = (c // CHUNK) <= (r // CHUNK)
    for g in range(A_GROUPS):
        vgrp = gv[:, g * ch:(g + 1) * ch]
        vn = _rmsnorm(vgrp, vg_ref[:, g * ch:(g + 1) * ch]).astype(BF16)
        wmix = jnp.where(causal, ws_ref[g], 0.0).astype(BF16)
        for n in range(tm // blk):
            mixed = _dot(wmix, vn[n * blk:(n + 1) * blk, :]) + bs_ref[g]
            ya_ref[n * blk:(n + 1) * blk, g * ch:(g + 1) * ch] = (
                gu[n * blk:(n + 1) * blk, g * ch:(g + 1) * ch] * mixed).astype(BF16)


def _even_in(x, g, w, bf, vgain, ws, bs, *, seq):
    t, d = x.shape
    tm = min(TOK_TILE, seq)
    aw = vgain.shape[1]
    qw = w.shape[1] - 2 * aw - LANES
    blk = ws.shape[1]
    return pl.pallas_call(
        functools.partial(_even_in_kernel, tiles_per_batch=seq // tm),
        grid=(t // tm,),
        in_specs=[
            pl.BlockSpec((tm, d), lambda i: (i, 0)),
            _const_spec((1, d)),
            _const_spec(w.shape),
            _const_spec((1, LANES)),
            _const_spec((1, aw)),
            _const_spec(ws.shape),
            _const_spec(bs.shape),
        ],
        out_specs=[
            pl.BlockSpec((tm, aw), lambda i: (i, 0)),
            pl.BlockSpec((tm, qw), lambda i: (i, 0)),
            pl.BlockSpec((tm, 2 * N_HEADS * LANES), lambda i: (i, 0)),
        ],
        out_shape=[
            jax.ShapeDtypeStruct((t, aw), BF16),
            jax.ShapeDtypeStruct((t, qw), BF16),
            jax.ShapeDtypeStruct((t, 2 * N_HEADS * LANES), BF16),
        ],
        scratch_shapes=[pltpu.VMEM((8, LANES), F32)],
        compiler_params=_params("arbitrary"),
        name="even_in",
    )(x, g, w, bf, vgain, ws, bs)


def _tri_pairs(nblk):
    qi = np.array([i for i in range(nblk) for _ in range(i + 1)], np.int32)
    kj = np.array([j for i in range(nblk) for j in range(i + 1)], np.int32)
    return jnp.asarray(qi), jnp.asarray(kj)


def _softmax_strips(logits, p_ref, alpha_ref, m_ref, h, add_bias=None):
    rows, tk = logits.shape
    for r0 in range(0, rows, ATT_STRIP):
        rs = slice(r0, r0 + ATT_STRIP)
        s = logits[rs, :]
        if add_bias is not None:
            s = add_bias(s, r0)
        m_prev = m_ref[h, rs, :]
        m_new = jnp.maximum(m_prev, jnp.max(s, axis=-1, keepdims=True))
        alpha_ref[rs, :] = jnp.exp(m_prev - m_new)
        m_ref[h, rs, :] = m_new
        p_ref[rs, 0:tk] = jnp.exp(
            (s - jnp.concatenate([m_new] * (tk // LANES), axis=-1)).astype(BF16))


def _accumulate(p_ref, alpha_ref, v, ones, acc_ref, h):
    alpha = alpha_ref[...]
    va = jnp.concatenate([v, ones], axis=-1)
    acc_ref[h] = (jnp.concatenate([alpha, alpha], axis=-1) * acc_ref[h]
                  + _dot(p_ref[:, 0:v.shape[0]], va))


def _fox_kernel(qi_ref, kj_ref, q_ref, aq_ref, k_ref, ak_ref, v_ref, ones_ref, mask_ref, o_ref,
                p_ref, alpha_ref, m_ref, acc_ref):
    step = pl.program_id(1)
    i = qi_ref[step]
    j = kj_ref[step]

    @pl.when(j == 0)
    def _():
        m_ref[...] = jnp.full_like(m_ref, NEG)
        acc_ref[...] = jnp.zeros_like(acc_ref)

    def logits(h):
        hs = slice(h * LANES, (h + 1) * LANES)
        qa = jnp.concatenate([q_ref[:, hs], aq_ref[:, hs]], axis=-1)
        ka = jnp.concatenate([k_ref[:, hs], ak_ref[:, hs]], axis=-1)
        return _dot_nt(qa, ka)

    def sweep(masked):
        nxt = logits(0)
        for h in range(N_HEADS):
            hs = slice(h * LANES, (h + 1) * LANES)
            sb = h % 2
            cur = nxt
            if h + 1 < N_HEADS:
                nxt = logits(h + 1)
            add_mask = (lambda s, r0: s + mask_ref[r0:r0 + ATT_STRIP, :]) if masked else None
            _softmax_strips(cur, p_ref.at[sb], alpha_ref.at[sb], m_ref, h, add_mask)
            _accumulate(p_ref.at[sb], alpha_ref.at[sb], v_ref[:, hs], ones_ref[...], acc_ref, h)

    @pl.when(j < i)
    def _():
        sweep(False)

    @pl.when(j == i)
    def _():
        sweep(True)
        for h in range(N_HEADS):
            acc = acc_ref[h]
            o_ref[:, h * LANES:(h + 1) * LANES] = (
                acc[:, :LANES] / acc[:, LANES:LANES + 1]).astype(o_ref.dtype)


def _ones_column(rows):
    ones = np.zeros((rows, LANES), np.float32)
    ones[:, 0] = 1.0
    return jnp.asarray(ones, BF16)


def _fox_attention(qkv, aug, *, batch, seq):
    t = qkv.shape[0]
    w = N_HEADS * LANES
    blk = min(FOX_BLOCK, seq)
    nblk = seq // blk
    qi, kj = _tri_pairs(nblk)
    r = np.arange(blk)
    mask = jnp.asarray(np.where(r[None, :] <= r[:, None], 0.0, NEG), F32)
    grid_spec = pltpu.PrefetchScalarGridSpec(
        num_scalar_prefetch=2,
        grid=(batch, qi.shape[0]),
        in_specs=[
            pl.BlockSpec((blk, w), lambda b, s, qi, kj: (b * nblk + qi[s], 0)),
            pl.BlockSpec((blk, w), lambda b, s, qi, kj: (b * nblk + qi[s], 0)),
            pl.BlockSpec((blk, w), lambda b, s, qi, kj: (b * nblk + kj[s], 1)),
            pl.BlockSpec((blk, w), lambda b, s, qi, kj: (b * nblk + kj[s], 1)),
            pl.BlockSpec((blk, w), lambda b, s, qi, kj: (b * nblk + kj[s], 2)),
            pl.BlockSpec((blk, LANES), lambda b, s, qi, kj: (0, 0)),
            pl.BlockSpec((blk, blk), lambda b, s, qi, kj: (0, 0)),
        ],
        out_specs=pl.BlockSpec((blk, w), lambda b, s, qi, kj: (b * nblk + qi[s], 0)),
        scratch_shapes=[
            pltpu.VMEM((2, blk, blk), BF16),
            pltpu.VMEM((2, blk, LANES), F32),
            pltpu.VMEM((N_HEADS, blk, LANES), F32),
            pltpu.VMEM((N_HEADS, blk, 2 * LANES), F32),
        ],
    )
    return pl.pallas_call(
        _fox_kernel,
        grid_spec=grid_spec,
        out_shape=jax.ShapeDtypeStruct((t, w), BF16),
        compiler_params=_params("arbitrary", "arbitrary"),
        name="fox_attention",
    )(qi, kj, qkv, aug, qkv, aug, qkv, _ones_column(blk), mask)


def _diff_pairs(nq):
    qi = np.array([i for i in range(nq) for _ in range(i // 2 + 1)], np.int32)
    kj = np.array([j for i in range(nq) for j in range(i // 2 + 1)], np.int32)
    return jnp.asarray(qi), jnp.asarray(kj)


def _diff_kernel(qi_ref, kj_ref, lam_ref, q_ref, k_ref, v_ref, ones_ref, bias_ref, gain_ref, o_ref,
                 qs_ref, p_ref, alpha_ref, m_ref, acc_ref, *, out_scale):
    step = pl.program_id(1)
    i = qi_ref[step]
    j = kj_ref[step]
    tq = q_ref.shape[0]
    half = LANES // 2
    jd = i // 2
    odd = (i % 2) == 1

    @pl.when(j == 0)
    def _():
        m_ref[...] = jnp.full_like(m_ref, NEG)
        acc_ref[...] = jnp.zeros_like(acc_ref)
        lane = lax.broadcasted_iota(I32, (tq, LANES), 1)
        zero = jnp.zeros((tq, LANES), BF16)
        for h in range(N_HEADS):
            q = q_ref[:, h * LANES:(h + 1) * LANES]
            qs_ref[h, 0:tq, :] = jnp.where(lane < half, q, zero)
            qs_ref[h, tq:2 * tq, :] = jnp.where(lane < half, zero, q)

    def sweep(tk, add_bias):
        def logits(h):
            return _dot_nt(qs_ref[h], k_ref[0:tk, h * LANES:(h + 1) * LANES])

        nxt = logits(0)
        for h in range(N_HEADS):
            hs = slice(h * LANES, (h + 1) * LANES)
            sb = h % 2
            cur = nxt
            if h + 1 < N_HEADS:
                nxt = logits(h + 1)
            bias_fn = None if add_bias is None else functools.partial(add_bias, h)
            _softmax_strips(cur, p_ref.at[sb], alpha_ref.at[sb], m_ref, h, bias_fn)
            _accumulate(p_ref.at[sb], alpha_ref.at[sb], v_ref[0:tk, hs], ones_ref[0:tk, :], acc_ref, h)

    def bias_rows(h, which, r0):
        return bias_ref[h, which, pl.ds(r0 % tq, ATT_STRIP), :]

    def odd_diag(h, s, r0):
        return jnp.concatenate([s[:, :tq] + bias_rows(h, 1, r0), s[:, tq:] + bias_rows(h, 0, r0)],
                               axis=-1)

    def even_prev(h, s, r0):
        return jnp.concatenate([s[:, :tq], s[:, tq:] + bias_rows(h, 1, r0)], axis=-1)

    def even_diag(h, s, r0):
        return s + bias_rows(h, 0, r0)

    @pl.when((j < jd - 1) | ((j == jd - 1) & odd))
    def _():
        sweep(2 * tq, None)

    @pl.when((j == jd - 1) & jnp.logical_not(odd))
    def _():
        sweep(2 * tq, even_prev)

    @pl.when((j == jd) & odd)
    def _():
        sweep(2 * tq, odd_diag)

    @pl.when((j == jd) & jnp.logical_not(odd))
    def _():
        sweep(tq, even_diag)

    @pl.when(j == jd)
    def _():
        for h in range(N_HEADS):
            hs = slice(h * LANES, (h + 1) * LANES)
            acc = acc_ref[h]
            o = acc[:, :LANES] / acc[:, LANES:LANES + 1]
            o = o[0:tq, :] - lam_ref[0, 0] * o[tq:2 * tq, :]
            o_ref[:, hs] = (_rmsnorm(o, gain_ref[:, hs]) * out_scale).astype(o_ref.dtype)


def _diff_attention(qkv, lam, bias, gain, *, batch, seq, out_scale):
    t = qkv.shape[0]
    w = N_HEADS * LANES
    blk = bias.shape[2]
    kblk = 2 * blk
    nq = seq // blk
    nk = seq // kblk
    qi, kj = _diff_pairs(nq)
    grid_spec = pltpu.PrefetchScalarGridSpec(
        num_scalar_prefetch=2,
        grid=(batch, qi.shape[0]),
        in_specs=[
            pl.BlockSpec(memory_space=pltpu.SMEM),
            pl.BlockSpec((blk, w), lambda b, s, qi, kj: (b * nq + qi[s], 0)),
            pl.BlockSpec((kblk, w), lambda b, s, qi, kj: (b * nk + kj[s], 1)),
            pl.BlockSpec((kblk, w), lambda b, s, qi, kj: (b * nk + kj[s], 2)),
            pl.BlockSpec((kblk, LANES), lambda b, s, qi, kj: (0, 0)),
            pl.BlockSpec(bias.shape, lambda b, s, qi, kj: (0, 0, 0, 0)),
            pl.BlockSpec((1, w), lambda b, s, qi, kj: (0, 0)),
        ],
        out_specs=pl.BlockSpec((blk, w), lambda b, s, qi, kj: (b * nq + qi[s], 0)),
        scratch_shapes=[
            pltpu.VMEM((N_HEADS, 2 * blk, LANES), BF16),
            pltpu.VMEM((2, 2 * blk, kblk), BF16),
            pltpu.VMEM((2, 2 * blk, LANES), F32),
            pltpu.VMEM((N_HEADS, 2 * blk, LANES), F32),
            pltpu.VMEM((N_HEADS, 2 * blk, 2 * LANES), F32),
        ],
    )
    return pl.pallas_call(
        functools.partial(_diff_kernel, out_scale=out_scale),
        grid_spec=grid_spec,
        out_shape=jax.ShapeDtypeStruct((t, w), BF16),
        compiler_params=_params("arbitrary", "arbitrary"),
        name="diff_attention",
    )(qi, kj, lam, qkv, qkv, qkv, _ones_column(kblk), bias, gain)


def _rel_bucket(rel):
    n_half = REL_BUCKETS // 2
    max_exact = n_half // 2
    ret = jnp.where(rel > 0, n_half, 0)
    n = jnp.abs(rel)
    nf = jnp.maximum(n, 1).astype(F32)
    large = max_exact + (jnp.log(nf / max_exact) / math.log(REL_MAX_DIST / max_exact)
                         * (n_half - max_exact)).astype(I32)
    large = jnp.minimum(large, n_half - 1)
    return ret + jnp.where(n < max_exact, n, large)


def _rel_bias_kernel(table_ref, idx_ref, o_ref):
    h = pl.program_id(0)
    idx = idx_ref[0]
    far = table_ref[REL_BUCKETS // 2 - 1, h]
    out = jnp.full(idx.shape, NEG, F32)
    for b in range(REL_BUCKETS):
        out = jnp.where(idx == b, table_ref[b, h] - far, out)
    o_ref[0, 0] = out


def _diff_bias_blocks(rel_table, blk):
    assert blk >= REL_MAX_DIST
    r = jnp.arange(blk)[:, None]
    c = jnp.arange(blk)[None, :]
    diag = jnp.where((c // CHUNK) <= (r // CHUNK), _rel_bucket(c - r), -1)
    prev = _rel_bucket(c - r - blk)
    idx = jnp.stack([diag, prev], axis=0).astype(I32)
    n_heads = rel_table.shape[1]
    return pl.pallas_call(
        _rel_bias_kernel,
        grid=(n_heads, 2),
        in_specs=[
            pl.BlockSpec(memory_space=pltpu.SMEM),
            pl.BlockSpec((1, blk, blk), lambda h, s: (s, 0, 0)),
        ],
        out_specs=pl.BlockSpec((1, 1, blk, blk), lambda h, s: (h, s, 0, 0)),
        out_shape=jax.ShapeDtypeStruct((n_heads, 2, blk, blk), F32),
        compiler_params=_params("arbitrary", "arbitrary"),
        name="rel_bias",
    )(rel_table.astype(F32), idx)


def _odd_in_kernel(x_ref, g_ref, w_ref, cw_ref, qkv_ref, yd_ref, zbuf_ref, *, tiles_per_batch):
    i = pl.program_id(0)
    tm = x_ref.shape[0]
    qw = qkv_ref.shape[1]
    dw = yd_ref.shape[1]
    xn = _rmsnorm(x_ref[...], g_ref[...]).astype(BF16)
    qkv_ref[...] = _dot(xn, w_ref[:, 0:qw]).astype(BF16)
    hh = _dot(xn, w_ref[:, qw:qw + dw])
    gb = _dot(xn, w_ref[:, qw + dw:qw + 2 * dw])
    gc = _dot(xn, w_ref[:, qw + 2 * dw:qw + 3 * dw])
    z = gc * hh

    @pl.when(i % tiles_per_batch == 0)
    def _():
        zbuf_ref[0:8, :] = jnp.zeros((8, dw), F32)

    @pl.when(i % tiles_per_batch != 0)
    def _():
        zbuf_ref[0:8, :] = zbuf_ref[tm:tm + 8, :]

    zbuf_ref[8:tm + 8, :] = z
    y = (cw_ref[0:1, :] * zbuf_ref[6:tm + 6, :] + cw_ref[1:2, :] * zbuf_ref[7:tm + 7, :]
         + cw_ref[2:3, :] * z)
    yd_ref[...] = (gb * y).astype(BF16)


def _odd_in(x, g, w, cw, *, seq, qw):
    t, d = x.shape
    tm = min(TOK_TILE, seq)
    dw = cw.shape[1]
    return pl.pallas_call(
        functools.partial(_odd_in_kernel, tiles_per_batch=seq // tm),
        grid=(t // tm,),
        in_specs=[
            pl.BlockSpec((tm, d), lambda i: (i, 0)),
            _const_spec((1, d)),
            _const_spec(w.shape),
            _const_spec(cw.shape),
        ],
        out_specs=[
            pl.BlockSpec((tm, qw), lambda i: (i, 0)),
            pl.BlockSpec((tm, dw), lambda i: (i, 0)),
        ],
        out_shape=[
            jax.ShapeDtypeStruct((t, qw), BF16),
            jax.ShapeDtypeStruct((t, dw), BF16),
        ],
        scratch_shapes=[pltpu.VMEM((tm + 8, dw), F32)],
        compiler_params=_params("arbitrary"),
        name="odd_in",
    )(x, g, w, cw)


def _mem_kv_kernel(mem_ref, g_ref, w_ref, k_ref, v_ref):
    d = mem_ref.shape[2]
    mn = _rmsnorm(mem_ref[0], g_ref[...]).astype(BF16)
    k_ref[0] = _dot(mn, w_ref[:, 0:d]).astype(BF16)
    v_ref[0] = _dot(mn, w_ref[:, d:2 * d]).astype(BF16)


def _mem_kv(mem, g, w):
    b, m, d = mem.shape
    return pl.pallas_call(
        _mem_kv_kernel,
        grid=(b,),
        in_specs=[
            pl.BlockSpec((1, m, d), lambda i: (i, 0, 0)),
            _const_spec((1, d)),
            _const_spec(w.shape),
        ],
        out_specs=[pl.BlockSpec((1, m, d), lambda i: (i, 0, 0))] * 2,
        out_shape=[jax.ShapeDtypeStruct((b, m, d), BF16)] * 2,
        compiler_params=_params("arbitrary"),
        name="mem_kv",
    )(mem, g, w)


def _post_mixer_kernel(x_ref, ya_ref, yb_ref, wout_ref, g_ref, wq_ref, k_ref, v_ref, wo_ref, o_ref):
    wa = ya_ref.shape[1]
    d = x_ref.shape[1]
    hd = d // N_HEADS
    x1 = x_ref[...] + _dot(ya_ref[...], wout_ref[0:wa, :]) + _dot(yb_ref[...], wout_ref[wa:, :])
    xn = _rmsnorm(x1, g_ref[...]).astype(BF16)
    q = _dot(xn, wq_ref[...]).astype(BF16)
    heads = []
    for h in range(N_HEADS):
        lg = _dot_nt(q[:, h * hd:(h + 1) * hd], k_ref[0, :, h * hd:(h + 1) * hd])
        p = jnp.exp(lg - jnp.max(lg, axis=-1, keepdims=True))
        p = p / jnp.sum(p, axis=-1, keepdims=True)
        heads.append(_dot(p.astype(BF16), v_ref[0, :, h * hd:(h + 1) * hd]).astype(BF16))
    o_ref[...] = x1 + _dot(jnp.concatenate(heads, axis=-1), wo_ref[...])


def _post_mixer(x, ya, yb, wout, g, wq, kmem, vmem, wo, *, seq):
    t, d = x.shape
    tm = min(TOK_TILE, seq)
    nt = seq // tm
    m = kmem.shape[1]
    wa = ya.shape[1]
    return pl.pallas_call(
        _post_mixer_kernel,
        grid=(t // tm,),
        in_specs=[
            pl.BlockSpec((tm, d), lambda i: (i, 0)),
            pl.BlockSpec((tm, wa), lambda i: (i, 0)),
            pl.BlockSpec((tm, yb.shape[1]), lambda i: (i, 0)),
            _const_spec(wout.shape),
            _const_spec((1, d)),
            _const_spec(wq.shape),
            pl.BlockSpec((1, m, d), lambda i: (i // nt, 0, 0)),
            pl.BlockSpec((1, m, d), lambda i: (i // nt, 0, 0)),
            _const_spec(wo.shape),
        ],
        out_specs=pl.BlockSpec((tm, d), lambda i: (i, 0)),
        out_shape=jax.ShapeDtypeStruct((t, d), F32),
        compiler_params=_params("arbitrary"),
        name="post_mixer",
    )(x, ya, yb, wout, g, wq, kmem, vmem, wo)


def _ffn_kernel(x_ref, g_ref, wg_ref, wu_ref, wd_ref, o_ref):
    x = x_ref[...]
    h = _rmsnorm(x, g_ref[...]).astype(BF16)
    ff = wg_ref.shape[1]
    acc = x
    for c0 in range(0, ff, FFN_CHUNK):
        c1 = min(c0 + FFN_CHUNK, ff)
        a = _dot(h, wg_ref[:, c0:c1])
        u = _dot(h, wu_ref[:, c0:c1])
        acc = acc + _dot((jax.nn.silu(a) * u).astype(BF16), wd_ref[c0:c1, :])
    o_ref[...] = acc


def _ffn(x, g, wg, wu, wd, *, seq):
    t, d = x.shape
    tm = min(TOK_TILE, seq)
    return pl.pallas_call(
        _ffn_kernel,
        grid=(t // tm,),
        in_specs=[
            pl.BlockSpec((tm, d), lambda i: (i, 0)),
            _const_spec((1, d)),
            _const_spec(wg.shape),
            _const_spec(wu.shape),
            _const_spec(wd.shape),
        ],
        out_specs=pl.BlockSpec((tm, d), lambda i: (i, 0)),
        out_shape=jax.ShapeDtypeStruct((t, d), F32),
        compiler_params=_params("arbitrary"),
        name="ffn",
    )(x, g, wg, wu, wd)


META_I1, META_I2, META_R1, META_R2, META_G1, META_G2 = range(6)


def _router_kernel(x_ref, g_ref, wr_ref, meta_ref, cnt_ref, carry_ref):
    i = pl.program_id(0)
    tm = x_ref.shape[0]
    h = _rmsnorm(x_ref[...], g_ref[...])
    hp = _split3(h)
    wp = _split3(wr_ref[...])
    logits = sum(_dot(hp[a], wp[b]) for a, b in ((2, 0), (0, 2), (1, 1), (1, 0), (0, 1), (0, 0)))
    lane = lax.broadcasted_iota(I32, logits.shape, 1)
    logits = jnp.where(lane < N_EXPERTS, logits, NEG)
    m1 = jnp.max(logits, axis=-1, keepdims=True)
    i1 = jnp.min(jnp.where(logits == m1, lane, LANES), axis=-1, keepdims=True)
    rest = jnp.where(lane == i1, NEG, logits)
    m2 = jnp.max(rest, axis=-1, keepdims=True)
    i2 = jnp.min(jnp.where(rest == m2, lane, LANES), axis=-1, keepdims=True)
    e = jnp.exp(m2 - m1)
    g1 = 1.0 / (1.0 + e)
    g2 = e / (1.0 + e)

    sel = jnp.where((lane == i1) | (lane == i2), 1.0, 0.0)
    row = lax.broadcasted_iota(I32, (tm, tm), 0)
    col = lax.broadcasted_iota(I32, (tm, tm), 1)
    strict = jnp.where(col < row, 1.0, 0.0).astype(BF16)

    @pl.when(i == 0)
    def _():
        carry_ref[...] = jnp.zeros_like(carry_ref)

    rank = _dot(strict, sel.astype(BF16)) + carry_ref[0:1, :]
    total = rank[tm - 1:tm, :] + sel[tm - 1:tm, :]
    carry_ref[...] = jnp.broadcast_to(total, carry_ref.shape)
    cnt_ref[...] = jnp.broadcast_to(total, cnt_ref.shape)
    r1 = jnp.sum(jnp.where(lane == i1, rank, 0.0), axis=-1, keepdims=True)
    r2 = jnp.sum(jnp.where(lane == i2, rank, 0.0), axis=-1, keepdims=True)
    meta = jnp.zeros_like(logits)
    for slot, val in ((META_I1, i1.astype(F32)), (META_I2, i2.astype(F32)), (META_R1, r1),
                      (META_R2, r2), (META_G1, g1), (META_G2, g2)):
        meta = jnp.where(lane == slot, val, meta)
    meta_ref[...] = meta


def _router(x, g, wr, *, seq):
    t, d = x.shape
    tm = min(TOK_TILE, seq)
    return pl.pallas_call(
        _router_kernel,
        grid=(t // tm,),
        in_specs=[
            pl.BlockSpec((tm, d), lambda i: (i, 0)),
            _const_spec((1, d)),
            _const_spec(wr.shape),
        ],
        out_specs=[
            pl.BlockSpec((tm, LANES), lambda i: (i, 0)),
            _const_spec((8, LANES)),
        ],
        out_shape=[
            jax.ShapeDtypeStruct((t, LANES), F32),
            jax.ShapeDtypeStruct((8, LANES), F32),
        ],
        scratch_shapes=[pltpu.VMEM((8, LANES), F32)],
        compiler_params=_params("arbitrary"),
        name="router",
    )(x, g, wr)


def _invert_kernel(pos_ref, pad_ref, tok_ref):
    c = pl.program_id(0)
    n_rows = tok_ref.shape[0]

    @pl.when(c == 0)
    def _():
        def fill(r, carry):
            tok_ref[r] = 0
            return carry

        for e in range(N_EXPERTS):
            stop = pad_ref[2 * e + 1] if e + 1 < N_EXPERTS else n_rows
            lax.fori_loop(pad_ref[2 * e], stop, fill, 0)

    def body(r, carry):
        tok = c * INVERT_CHUNK + r
        tok_ref[pos_ref[2 * tok]] = tok
        tok_ref[pos_ref[2 * tok + 1]] = tok
        return carry

    lax.fori_loop(0, INVERT_CHUNK, body, 0, unroll=DMA_UNROLL)


def _invert(pos, pad, n_rows):
    n_tokens = pos.shape[0] // 2
    grid_spec = pltpu.PrefetchScalarGridSpec(
        num_scalar_prefetch=2,
        grid=(n_tokens // INVERT_CHUNK,),
        in_specs=[],
        out_specs=pl.BlockSpec(memory_space=pltpu.SMEM),
    )
    return pl.pallas_call(
        _invert_kernel,
        grid_spec=grid_spec,
        out_shape=jax.ShapeDtypeStruct((n_rows,), I32),
        compiler_params=_params("arbitrary"),
        name="moe_invert",
    )(pos, pad)


def _experts_kernel(te_ref, rows_ref, tok_ref, x_hbm, g_ref, wg_ref, wu_ref, wd_ref, y_ref,
                    xg_ref, xb_ref, gsem_ref, *, n_f):
    p = pl.program_id(0)
    f = pl.program_id(1)
    n_tiles = pl.num_programs(0)
    rows = rows_ref[p]
    half = MOE_TILE // 2
    per_step = MOE_TILE // n_f
    slot = p % 2

    def row_gather(tile, r, dst_slot):
        return pltpu.make_async_copy(x_hbm.at[pl.ds(tok_ref[tile * MOE_TILE + r], 1)],
                                     xg_ref.at[dst_slot, pl.ds(r, 1)], gsem_ref.at[dst_slot])

    def wait_gather(dst_slot):
        pltpu.make_async_copy(x_hbm.at[pl.ds(0, MOE_TILE)], xg_ref.at[dst_slot],
                              gsem_ref.at[dst_slot]).wait()

    @pl.when(f == 0)
    def _():
        y_ref[...] = jnp.zeros_like(y_ref)

        @pl.when(p == 0)
        def _():
            def issue(r, carry):
                row_gather(0, r, 0).start()
                return carry
            lax.fori_loop(0, MOE_TILE, issue, 0, unroll=DMA_UNROLL)

        @pl.when((p == 0) | (rows_ref[jnp.maximum(p - 1, 0)] > 0))
        def _():
            wait_gather(slot)

    @pl.when(rows > 0)
    def _():
        @pl.when(f == 0)
        def _():
            xb_ref[...] = _rmsnorm(xg_ref[slot], g_ref[...]).astype(BF16)

        nxt = jnp.minimum(p + 1, n_tiles - 1)
        for r in range(per_step):
            row_gather(nxt, f * per_step + r, 1 - slot).start()

        tf = wg_ref.shape[2]
        wgu = jnp.concatenate([wg_ref[0].astype(BF16), wu_ref[0].astype(BF16)], axis=1)
        wd = wd_ref[0].astype(BF16)

        def run(r0):
            gu = _dot(xb_ref[r0:r0 + half, :], wgu)
            act = (jax.nn.silu(gu[:, :tf]) * gu[:, tf:]).astype(BF16)
            y_ref[r0:r0 + half, :] += _dot(act, wd)

        run(0)

        @pl.when(rows > half)
        def _():
            run(half)

        @pl.when((p == n_tiles - 1) & (f == n_f - 1))
        def _():
            wait_gather(1 - slot)


def _experts(tile_expert, tile_rows, tok, x, g, wg, wu, wd):
    n_rows = tok.shape[0]
    d = x.shape[1]
    ff = wg.shape[2]
    tm = MOE_TILE
    tf = MOE_FF_CHUNK
    n_f = ff // tf
    grid_spec = pltpu.PrefetchScalarGridSpec(
        num_scalar_prefetch=3,
        grid=(n_rows // tm, n_f),
        in_specs=[
            pl.BlockSpec(memory_space=pl.ANY),
            pl.BlockSpec((1, d), lambda p, f, te, tr, tok: (0, 0)),
            pl.BlockSpec((1, d, tf), lambda p, f, te, tr, tok: (te[p], 0, f)),
            pl.BlockSpec((1, d, tf), lambda p, f, te, tr, tok: (te[p], 0, f)),
            pl.BlockSpec((1, tf, d), lambda p, f, te, tr, tok: (te[p], f, 0)),
        ],
        out_specs=pl.BlockSpec((tm, d), lambda p, f, te, tr, tok: (p, 0)),
        scratch_shapes=[
            pltpu.VMEM((2, tm, d), F32),
            pltpu.VMEM((tm, d), BF16),
            pltpu.SemaphoreType.DMA((2,)),
        ],
    )
    return pl.pallas_call(
        functools.partial(_experts_kernel, n_f=n_f),
        grid_spec=grid_spec,
        out_shape=jax.ShapeDtypeStruct((n_rows, d), F32),
        compiler_params=_params("arbitrary", "arbitrary"),
        name="moe_experts",
    )(tile_expert, tile_rows, tok, x, g, wg, wu, wd)


def _combine_kernel(pos_ref, y_ref, x_ref, meta_ref, g_ref, o_ref, buf_ref, sem_ref):
    i = pl.program_id(0)
    n = pl.num_programs(0)
    tc = x_ref.shape[0]

    def fetch(tile, slot):
        def body(r, carry):
            tok = tile * tc + r
            for k in range(2):
                pltpu.make_async_copy(y_ref.at[pl.ds(pos_ref[2 * tok + k], 1)],
                                      buf_ref.at[slot, k, pl.ds(r, 1)], sem_ref.at[slot]).start()
            return carry
        lax.fori_loop(0, tc, body, 0, unroll=DMA_UNROLL)

    @pl.when(i == 0)
    def _():
        fetch(0, 0)

    @pl.when(i + 1 < n)
    def _():
        fetch(i + 1, (i + 1) % 2)

    slot = i % 2

    for k in range(2):
        pltpu.make_async_copy(y_ref.at[pl.ds(0, tc)], buf_ref.at[slot, k], sem_ref.at[slot]).wait()
    meta = meta_ref[...]
    g1 = meta[:, META_G1:META_G1 + 1]
    g2 = meta[:, META_G2:META_G2 + 1]
    x = x_ref[...] + g1 * buf_ref[slot, 0] + g2 * buf_ref[slot, 1]
    o_ref[...] = _rmsnorm(x, g_ref[...])


def _combine(pos, y, x, meta, g):
    t, d = x.shape
    tc = COMBINE_TILE
    grid_spec = pltpu.PrefetchScalarGridSpec(
        num_scalar_prefetch=1,
        grid=(t // tc,),
        in_specs=[
            pl.BlockSpec(memory_space=pl.ANY),
            pl.BlockSpec((tc, d), lambda i, pos: (i, 0)),
            pl.BlockSpec((tc, LANES), lambda i, pos: (i, 0)),
            pl.BlockSpec((1, d), lambda i, pos: (0, 0)),
        ],
        out_specs=pl.BlockSpec((tc, d), lambda i, pos: (i, 0)),
        scratch_shapes=[pltpu.VMEM((2, 2, tc, d), F32), pltpu.SemaphoreType.DMA((2,))],
    )
    return pl.pallas_call(
        _combine_kernel,
        grid_spec=grid_spec,
        out_shape=jax.ShapeDtypeStruct((t, d), F32),
        compiler_params=_params("arbitrary"),
        name="moe_combine",
    )(pos, y, x, meta, g)


def _moe_plan(meta, counts, n_tiles):
    cnt = counts[0, :N_EXPERTS].astype(I32)
    tiles = (cnt + MOE_TILE - 1) // MOE_TILE
    tile_end = jnp.cumsum(tiles)
    row_off = (tile_end - tiles) * MOE_TILE
    n_used = tile_end[-1:]
    experts = jnp.arange(N_EXPERTS, dtype=I32)

    def rows(idx_col, rank_col):
        idx = meta[:, idx_col].astype(I32)
        off = jnp.sum(jnp.where(idx[:, None] == experts[None, :], row_off[None, :], 0), axis=1)
        return off + meta[:, rank_col].astype(I32)

    pos = jnp.stack([rows(META_I1, META_R1), rows(META_I2, META_R2)], axis=1).reshape(-1)
    tile_ids = jnp.arange(n_tiles, dtype=I32)
    te = jnp.sum((tile_end[None, :] <= tile_ids[:, None]).astype(I32), axis=1)
    mine = te[:, None] == experts[None, :]
    first_row = tile_ids * MOE_TILE - jnp.sum(jnp.where(mine, row_off[None, :], 0), axis=1)
    tile_rows = jnp.clip(jnp.sum(jnp.where(mine, cnt[None, :], 0), axis=1) - first_row, 0, MOE_TILE)
    last = jnp.sum((tile_end <= n_used - 1).astype(I32))
    te = jnp.minimum(te, last).astype(I32)
    pad = jnp.stack([row_off + cnt, row_off + tiles * MOE_TILE], axis=1).reshape(-1).astype(I32)
    return pos.astype(I32), pad, te, tile_rows.astype(I32)


def _moe_and_final_norm(x, g_ffn, wr, wg, wu, wd, g_final, *, seq):
    t, d = x.shape
    n_tiles = 2 * t // MOE_TILE + N_EXPERTS
    meta, counts = _router(x, g_ffn, wr, seq=seq)
    pos, pad, te, tile_rows = _moe_plan(meta, counts, n_tiles)
    tok = _invert(pos, pad, n_tiles * MOE_TILE)
    y = _experts(te, tile_rows, tok, x, g_ffn, wg, wu, wd)
    return _combine(pos, y, x, meta, g_final)


def _row(v):
    return v.reshape(1, -1).astype(F32)


def kernel(x, mem, norm_mix, norm_mem_q, norm_mem_kv, norm_ffn, norm_final, even_w_in, fox_b_f, gmlp_v_gain, gmlp_w_s, gmlp_b_s, even_w_out, odd_w_in, diff_lambda_q1, diff_lambda_k1, diff_lambda_q2, diff_lambda_k2, diff_subln_gain, conv_w, odd_w_out, rel_bias, mem_w_q, mem_w_kv, mem_w_o, ffn_w_gate, ffn_w_up, ffn_w_down, router_w, moe_w_gate, moe_w_up, moe_w_down):
    batch, seq, d = x.shape
    t = batch * seq
    depth = norm_mix.shape[0]
    assert depth == 2 and seq % (2 * ATT_BLOCK) == 0 and seq % TOK_TILE == 0
    xf = x.reshape(t, d)
    hd = d // N_HEADS
    aw = gmlp_v_gain.shape[1]
    n_blk = gmlp_w_s.shape[2]

    w_in = even_w_in[0]
    q0 = 2 * aw
    bw = N_HEADS * LANES
    scale = jnp.ones((w_in.shape[1],), F32).at[q0:q0 + bw].set(float(LANES) ** -0.5)
    w_in = jnp.pad(w_in * scale, ((0, 0), (0, LANES - N_HEADS))).astype(BF16)
    bf = jnp.pad(fox_b_f[0], (0, LANES - N_HEADS)).reshape(1, LANES)
    bs = jnp.broadcast_to(gmlp_b_s[0][:, :, None], (A_GROUPS, n_blk, aw // A_GROUPS)).astype(F32)
    ya, qkv, aug = _even_in(xf, _row(norm_mix[0]), w_in, bf, _row(gmlp_v_gain[0]),
                            gmlp_w_s[0], bs, seq=seq)
    yb = _fox_attention(qkv, aug, batch=batch, seq=seq)

    def mem_attention(layer, x_in, ya_, yb_, w_out):
        km, vm = _mem_kv(mem, _row(norm_mem_kv[layer]), mem_w_kv[layer].astype(BF16))
        wq = (mem_w_q[layer] * float(hd) ** -0.5).astype(BF16)
        return _post_mixer(x_in, ya_, yb_, w_out.astype(BF16), _row(norm_mem_q[layer]), wq, km, vm,
                           mem_w_o[layer].astype(BF16), seq=seq)

    xf = mem_attention(0, xf, ya, yb, even_w_out[0])
    xf = _ffn(xf, _row(norm_ffn[0]), ffn_w_gate[0].astype(BF16), ffn_w_up[0].astype(BF16),
              ffn_w_down[0].astype(BF16), seq=seq)

    w_in = odd_w_in[0]
    qk_dim = LANES // 2
    qk_w = 2 * N_HEADS * qk_dim
    perm = np.arange(qk_w).reshape(2, N_HEADS, qk_dim).transpose(1, 0, 2).reshape(-1)
    cols = np.concatenate([perm, qk_w + perm, np.arange(2 * qk_w, w_in.shape[1])])
    scale = jnp.ones((w_in.shape[1],), F32).at[:qk_w].set(float(qk_dim) ** -0.5)
    w_in = (w_in * scale)[:, cols].astype(BF16)
    qkv, yd = _odd_in(xf, _row(norm_mix[1]), w_in, conv_w[0].astype(F32), seq=seq,
                      qw=2 * qk_w + N_HEADS * LANES)
    lam_init = 0.8 - 0.6 * math.exp(-0.3 * 1)
    lam = (jnp.exp(jnp.sum(diff_lambda_q1[0] * diff_lambda_k1[0]))
           - jnp.exp(jnp.sum(diff_lambda_q2[0] * diff_lambda_k2[0])) + lam_init)
    bias = _diff_bias_blocks(rel_bias, min(ATT_BLOCK, seq))
    yc = _diff_attention(qkv, lam.reshape(1, 1).astype(F32), bias, _row(diff_subln_gain[0]),
                         batch=batch, seq=seq, out_scale=1.0 - lam_init)
    xf = mem_attention(1, xf, yc, yd, odd_w_out[0])
    wr = jnp.pad(router_w[0], ((0, 0), (0, LANES - N_EXPERTS))).astype(F32)
    out = _moe_and_final_norm(xf, _row(norm_ffn[1]), wr, moe_w_gate[0], moe_w_up[0], moe_w_down[0],
                              _row(norm_final), seq=seq)
    return out.reshape(batch, seq, d)
```

```python
import functools
import math

import numpy as np
import jax
import jax.numpy as jnp
from jax import lax
from jax.experimental import pallas as pl
from jax.experimental.pallas import tpu as pltpu

F32 = jnp.float32
BF16 = jnp.bfloat16
I32 = jnp.int32

EPS = 1e-6
NEG = -1e30
LANES = 128
VMEM_LIMIT = 56 * 1024 * 1024

CHUNK = 64
A_GROUPS = 4
N_HEADS = 4
N_EXPERTS = 8
REL_BUCKETS = 32
REL_MAX_DIST = 128

TOK_TILE = 512
FOX_BLOCK = 1024
ATT_BLOCK = 512
ATT_STRIP = 64
MOE_TILE = 1024
MOE_FF_CHUNK = 896
FFN_CHUNK = 512
DISPATCH_CHUNK = 256
COMBINE_TILE = 256
DMA_UNROLL = 8


def _dot(a, b):
    return jnp.dot(a, b, preferred_element_type=F32)


def _dot_nt(a, b):
    return lax.dot_general(a, b, (((1,), (1,)), ((), ())), preferred_element_type=F32)


def _rmsnorm(x, g):
    return x * lax.rsqrt(jnp.mean(x * x, axis=-1, keepdims=True) + EPS) * g


def _params(*sem):
    return pltpu.CompilerParams(dimension_semantics=sem, vmem_limit_bytes=VMEM_LIMIT)


def _const_spec(shape):
    return pl.BlockSpec(shape, lambda *_: (0,) * len(shape), pipeline_mode=pl.Buffered(1))


def _split3(v):
    a1 = v.astype(BF16)
    r1 = v - a1.astype(F32)
    a2 = r1.astype(BF16)
    a3 = (r1 - a2.astype(F32)).astype(BF16)
    return a1, a2, a3


def _even_in_kernel(x_ref, g_ref, w_ref, bf_ref, vg_ref, ws_ref, bs_ref,
                    ya_ref, qkv_ref, aug_ref, carry_ref, *, tiles_per_batch):
    i = pl.program_id(0)
    tm = x_ref.shape[0]
    aw = ya_ref.shape[1]
    qw = qkv_ref.shape[1]
    xn = _rmsnorm(x_ref[...], g_ref[...]).astype(BF16)

    qkv_ref[...] = _dot(xn, w_ref[:, 2 * aw:2 * aw + qw]).astype(BF16)

    fl = _dot(xn, w_ref[:, 2 * aw + qw:]) + bf_ref[...]
    ls = jnp.minimum(fl, 0.0) - jnp.log1p(jnp.exp(-jnp.abs(fl)))
    row = lax.broadcasted_iota(I32, (tm, tm), 0)
    col = lax.broadcasted_iota(I32, (tm, tm), 1)
    tri = jnp.where(col <= row, 1.0, 0.0).astype(BF16)
    a1, a2, a3 = _split3(ls)
    csum = _dot(tri, a1) + _dot(tri, a2) + _dot(tri, a3)

    @pl.when(i % tiles_per_batch == 0)
    def _():
        carry_ref[...] = jnp.zeros_like(carry_ref)

    csum = csum + carry_ref[0:1, :]
    carry_ref[...] = jnp.broadcast_to(csum[tm - 1:tm, :], carry_ref.shape)

    lane = lax.broadcasted_iota(I32, (tm, LANES), 1)
    for h in range(N_HEADS):
        c1, c2, c3 = (piece.astype(F32)
                      for piece in _split3(jnp.broadcast_to(csum[:, h:h + 1], (tm, LANES))))
        aq = jnp.where(lane == 0, c1, jnp.where(lane == 1, c2, jnp.where(lane == 2, c3,
                       jnp.where(lane < 6, 1.0, 0.0))))
        ak = jnp.where(lane == 3, -c1, jnp.where(lane == 4, -c2, jnp.where(lane == 5, -c3,
                       jnp.where(lane < 3, 1.0, 0.0))))
        aug_ref[:, h * LANES:(h + 1) * LANES] = aq.astype(BF16)
        aug_ref[:, (N_HEADS + h) * LANES:(N_HEADS + h + 1) * LANES] = ak.astype(BF16)

    gu = jax.nn.gelu(_dot(xn, w_ref[:, 0:aw]), approximate=True)
    gv = jax.nn.gelu(_dot(xn, w_ref[:, aw:2 * aw]), approximate=True)
    blk = ws_ref.shape[1]
    ch = aw // A_GROUPS
    r = lax.broadcasted_iota(I32, (blk, blk), 0)
    c = lax.broadcasted_iota(I32, (blk, blk), 1)
    causal = (c // CHUNK) <= (r // CHUNK)
    for g in range(A_GROUPS):
        vgrp = gv[:, g * ch:(g + 1) * ch]
        vn = _rmsnorm(vgrp, vg_ref[:, g * ch:(g + 1) * ch]).astype(BF16)
        wmix = jnp.where(causal, ws_ref[g], 0.0).astype(BF16)
        for n in range(tm // blk):
            mixed = _dot(wmix, vn[n * blk:(n + 1) * blk, :]) + bs_ref[g]
            ya_ref[n * blk:(n + 1) * blk, g * ch:(g + 1) * ch] = (
                gu[n * blk:(n + 1) * blk, g * ch:(g + 1) * ch] * mixed).astype(BF16)


def _even_in(x, g, w, bf, vgain, ws, bs, *, seq):
    t, d = x.shape
    tm = min(TOK_TILE, seq)
    aw = vgain.shape[1]
    qw = w.shape[1] - 2 * aw - LANES
    blk = ws.shape[1]
    return pl.pallas_call(
        functools.partial(_even_in_kernel, tiles_per_batch=seq // tm),
        grid=(t // tm,),
        in_specs=[
            pl.BlockSpec((tm, d), lambda i: (i, 0)),
            _const_spec((1, d)),
            _const_spec(w.shape),
            _const_spec((1, LANES)),
            _const_spec((1, aw)),
            _const_spec(ws.shape),
            _const_spec(bs.shape),
        ],
        out_specs=[
            pl.BlockSpec((tm, aw), lambda i: (i, 0)),
            pl.BlockSpec((tm, qw), lambda i: (i, 0)),
            pl.BlockSpec((tm, 2 * N_HEADS * LANES), lambda i: (i, 0)),
        ],
        out_shape=[
            jax.ShapeDtypeStruct((t, aw), BF16),
            jax.ShapeDtypeStruct((t, qw), BF16),
            jax.ShapeDtypeStruct((t, 2 * N_HEADS * LANES), BF16),
        ],
        scratch_shapes=[pltpu.VMEM((8, LANES), F32)],
        compiler_params=_params("arbitrary"),
        name="even_in",
    )(x, g, w, bf, vgain, ws, bs)


def _tri_pairs(nblk):
    qi = np.array([i for i in range(nblk) for _ in range(i + 1)], np.int32)
    kj = np.array([j for i in range(nblk) for j in range(i + 1)], np.int32)
    return jnp.asarray(qi), jnp.asarray(kj)


def _softmax_strips(logits, p_ref, alpha_ref, m_ref, h, add_bias=None):
    rows, tk = logits.shape
    for r0 in range(0, rows, ATT_STRIP):
        rs = slice(r0, r0 + ATT_STRIP)
        s = logits[rs, :]
        if add_bias is not None:
            s = add_bias(s, r0)
        m_prev = m_ref[h, rs, :]
        m_new = jnp.maximum(m_prev, jnp.max(s, axis=-1, keepdims=True))
        alpha_ref[rs, :] = jnp.exp(m_prev - m_new)
        m_ref[h, rs, :] = m_new
        p_ref[rs, 0:tk] = jnp.exp(
            (s - jnp.concatenate([m_new] * (tk // LANES), axis=-1)).astype(BF16))


def _accumulate(p_ref, alpha_ref, v, ones, acc_ref, h):
    alpha = alpha_ref[...]
    va = jnp.concatenate([v, ones], axis=-1)
    acc_ref[h] = (jnp.concatenate([alpha, alpha], axis=-1) * acc_ref[h]
                  + _dot(p_ref[:, 0:v.shape[0]], va))


def _fox_kernel(qi_ref, kj_ref, q_ref, aq_ref, k_ref, ak_ref, v_ref, ones_ref, mask_ref, o_ref,
                p_ref, alpha_ref, m_ref, acc_ref):
    step = pl.program_id(1)
    i = qi_ref[step]
    j = kj_ref[step]

    @pl.when(j == 0)
    def _():
        m_ref[...] = jnp.full_like(m_ref, NEG)
        acc_ref[...] = jnp.zeros_like(acc_ref)

    def logits(h):
        hs = slice(h * LANES, (h + 1) * LANES)
        qa = jnp.concatenate([q_ref[:, hs], aq_ref[:, hs]], axis=-1)
        ka = jnp.concatenate([k_ref[:, hs], ak_ref[:, hs]], axis=-1)
        return _dot_nt(qa, ka)

    def sweep(masked):
        nxt = logits(0)
        for h in range(N_HEADS):
            hs = slice(h * LANES, (h + 1) * LANES)
            sb = h % 2
            cur = nxt
            if h + 1 < N_HEADS:
                nxt = logits(h + 1)
            add_mask = (lambda s, r0: s + mask_ref[r0:r0 + ATT_STRIP, :]) if masked else None
            _softmax_strips(cur, p_ref.at[sb], alpha_ref.at[sb], m_ref, h, add_mask)
            _accumulate(p_ref.at[sb], alpha_ref.at[sb], v_ref[:, hs], ones_ref[...], acc_ref, h)

    @pl.when(j < i)
    def _():
        sweep(False)

    @pl.when(j == i)
    def _():
        sweep(True)
        for h in range(N_HEADS):
            acc = acc_ref[h]
            o_ref[:, h * LANES:(h + 1) * LANES] = (
                acc[:, :LANES] / acc[:, LANES:LANES + 1]).astype(o_ref.dtype)


def _ones_column(rows):
    ones = np.zeros((rows, LANES), np.float32)
    ones[:, 0] = 1.0
    return jnp.asarray(ones, BF16)


def _fox_attention(qkv, aug, *, batch, seq):
    t = qkv.shape[0]
    w = N_HEADS * LANES
    blk = min(FOX_BLOCK, seq)
    nblk = seq // blk
    qi, kj = _tri_pairs(nblk)
    r = np.arange(blk)
    mask = jnp.asarray(np.where(r[None, :] <= r[:, None], 0.0, NEG), F32)
    grid_spec = pltpu.PrefetchScalarGridSpec(
        num_scalar_prefetch=2,
        grid=(batch, qi.shape[0]),
        in_specs=[
            pl.BlockSpec((blk, w), lambda b, s, qi, kj: (b * nblk + qi[s], 0)),
            pl.BlockSpec((blk, w), lambda b, s, qi, kj: (b * nblk + qi[s], 0)),
            pl.BlockSpec((blk, w), lambda b, s, qi, kj: (b * nblk + kj[s], 1)),
            pl.BlockSpec((blk, w), lambda b, s, qi, kj: (b * nblk + kj[s], 1)),
            pl.BlockSpec((blk, w), lambda b, s, qi, kj: (b * nblk + kj[s], 2)),
            pl.BlockSpec((blk, LANES), lambda b, s, qi, kj: (0, 0)),
            pl.BlockSpec((blk, blk), lambda b, s, qi, kj: (0, 0)),
        ],
        out_specs=pl.BlockSpec((blk, w), lambda b, s, qi, kj: (b * nblk + qi[s], 0)),
        scratch_shapes=[
            pltpu.VMEM((2, blk, blk), BF16),
            pltpu.VMEM((2, blk, LANES), F32),
            pltpu.VMEM((N_HEADS, blk, LANES), F32),
            pltpu.VMEM((N_HEADS, blk, 2 * LANES), F32),
        ],
    )
    return pl.pallas_call(
        _fox_kernel,
        grid_spec=grid_spec,
        out_shape=jax.ShapeDtypeStruct((t, w), BF16),
        compiler_params=_params("arbitrary", "arbitrary"),
        name="fox_attention",
    )(qi, kj, qkv, aug, qkv, aug, qkv, _ones_column(blk), mask)


def _diff_pairs(nq):
    qi = np.array([i for i in range(nq) for _ in range(i // 2 + 1)], np.int32)
    kj = np.array([j for i in range(nq) for j in range(i // 2 + 1)], np.int32)
    return jnp.asarray(qi), jnp.asarray(kj)


def _diff_kernel(qi_ref, kj_ref, lam_ref, q_ref, k_ref, v_ref, ones_ref, bias_ref, gain_ref, o_ref,
                 qs_ref, p_ref, alpha_ref, m_ref, acc_ref, *, out_scale):
    step = pl.program_id(1)
    i = qi_ref[step]
    j = kj_ref[step]
    tq = q_ref.shape[0]
    half = LANES // 2
    jd = i // 2
    odd = (i % 2) == 1

    @pl.when(j == 0)
    def _():
        m_ref[...] = jnp.full_like(m_ref, NEG)
        acc_ref[...] = jnp.zeros_like(acc_ref)
        lane = lax.broadcasted_iota(I32, (tq, LANES), 1)
        zero = jnp.zeros((tq, LANES), BF16)
        for h in range(N_HEADS):
            q = q_ref[:, h * LANES:(h + 1) * LANES]
            qs_ref[h, 0:tq, :] = jnp.where(lane < half, q, zero)
            qs_ref[h, tq:2 * tq, :] = jnp.where(lane < half, zero, q)

    def sweep(tk, add_bias):
        def logits(h):
            return _dot_nt(qs_ref[h], k_ref[0:tk, h * LANES:(h + 1) * LANES])

        nxt = logits(0)
        for h in range(N_HEADS):
            hs = slice(h * LANES, (h + 1) * LANES)
            sb = h % 2
            cur = nxt
            if h + 1 < N_HEADS:
                nxt = logits(h + 1)
            bias_fn = None if add_bias is None else functools.partial(add_bias, h)
            _softmax_strips(cur, p_ref.at[sb], alpha_ref.at[sb], m_ref, h, bias_fn)
            _accumulate(p_ref.at[sb], alpha_ref.at[sb], v_ref[0:tk, hs], ones_ref[0:tk, :], acc_ref, h)

    def bias_rows(h, which, r0):
        return bias_ref[h, which, pl.ds(r0 % tq, ATT_STRIP), :]

    def odd_diag(h, s, r0):
        return jnp.concatenate([s[:, :tq] + bias_rows(h, 1, r0), s[:, tq:] + bias_rows(h, 0, r0)],
                               axis=-1)

    def even_prev(h, s, r0):
        return jnp.concatenate([s[:, :tq], s[:, tq:] + bias_rows(h, 1, r0)], axis=-1)

    def even_diag(h, s, r0):
        return s + bias_rows(h, 0, r0)

    @pl.when((j < jd - 1) | ((j == jd - 1) & odd))
    def _():
        sweep(2 * tq, None)

    @pl.when((j == jd - 1) & jnp.logical_not(odd))
    def _():
        sweep(2 * tq, even_prev)

    @pl.when((j == jd) & odd)
    def _():
        sweep(2 * tq, odd_diag)

    @pl.when((j == jd) & jnp.logical_not(odd))
    def _():
        sweep(tq, even_diag)

    @pl.when(j == jd)
    def _():
        for h in range(N_HEADS):
            hs = slice(h * LANES, (h + 1) * LANES)
            acc = acc_ref[h]
            o = acc[:, :LANES] / acc[:, LANES:LANES + 1]
            o = o[0:tq, :] - lam_ref[0, 0] * o[tq:2 * tq, :]
            o_ref[:, hs] = (_rmsnorm(o, gain_ref[:, hs]) * out_scale).astype(o_ref.dtype)


def _diff_attention(qkv, lam, bias, gain, *, batch, seq, out_scale):
    t = qkv.shape[0]
    w = N_HEADS * LANES
    blk = bias.shape[2]
    kblk = 2 * blk
    nq = seq // blk
    nk = seq // kblk
    qi, kj = _diff_pairs(nq)
    grid_spec = pltpu.PrefetchScalarGridSpec(
        num_scalar_prefetch=2,
        grid=(batch, qi.shape[0]),
        in_specs=[
            pl.BlockSpec(memory_space=pltpu.SMEM),
            pl.BlockSpec((blk, w), lambda b, s, qi, kj: (b * nq + qi[s], 0)),
            pl.BlockSpec((kblk, w), lambda b, s, qi, kj: (b * nk + kj[s], 1)),
            pl.BlockSpec((kblk, w), lambda b, s, qi, kj: (b * nk + kj[s], 2)),
            pl.BlockSpec((kblk, LANES), lambda b, s, qi, kj: (0, 0)),
            pl.BlockSpec(bias.shape, lambda b, s, qi, kj: (0, 0, 0, 0)),
            pl.BlockSpec((1, w), lambda b, s, qi, kj: (0, 0)),
        ],
        out_specs=pl.BlockSpec((blk, w), lambda b, s, qi, kj: (b * nq + qi[s], 0)),
        scratch_shapes=[
            pltpu.VMEM((N_HEADS, 2 * blk, LANES), BF16),
            pltpu.VMEM((2, 2 * blk, kblk), BF16),
            pltpu.VMEM((2, 2 * blk, LANES), F32),
            pltpu.VMEM((N_HEADS, 2 * blk, LANES), F32),
            pltpu.VMEM((N_HEADS, 2 * blk, 2 * LANES), F32),
        ],
    )
    return pl.pallas_call(
        functools.partial(_diff_kernel, out_scale=out_scale),
        grid_spec=grid_spec,
        out_shape=jax.ShapeDtypeStruct((t, w), BF16),
        compiler_params=_params("arbitrary", "arbitrary"),
        name="diff_attention",
    )(qi, kj, lam, qkv, qkv, qkv, _ones_column(kblk), bias, gain)


def _rel_bucket(rel):
    n_half = REL_BUCKETS // 2
    max_exact = n_half // 2
    ret = jnp.where(rel > 0, n_half, 0)
    n = jnp.abs(rel)
    nf = jnp.maximum(n, 1).astype(F32)
    large = max_exact + (jnp.log(nf / max_exact) / math.log(REL_MAX_DIST / max_exact)
                         * (n_half - max_exact)).astype(I32)
    large = jnp.minimum(large, n_half - 1)
    return ret + jnp.where(n < max_exact, n, large)


def _rel_bias_kernel(table_ref, idx_ref, o_ref):
    h = pl.program_id(0)
    idx = idx_ref[0]
    far = table_ref[REL_BUCKETS // 2 - 1, h]
    out = jnp.full(idx.shape, NEG, F32)
    for b in range(REL_BUCKETS):
        out = jnp.where(idx == b, table_ref[b, h] - far, out)
    o_ref[0, 0] = out


def _diff_bias_blocks(rel_table, blk):
    assert blk >= REL_MAX_DIST
    r = jnp.arange(blk)[:, None]
    c = jnp.arange(blk)[None, :]
    diag = jnp.where((c // CHUNK) <= (r // CHUNK), _rel_bucket(c - r), -1)
    prev = _rel_bucket(c - r - blk)
    idx = jnp.stack([diag, prev], axis=0).astype(I32)
    n_heads = rel_table.shape[1]
    return pl.pallas_call(
        _rel_bias_kernel,
        grid=(n_heads, 2),
        in_specs=[
            pl.BlockSpec(memory_space=pltpu.SMEM),
            pl.BlockSpec((1, blk, blk), lambda h, s: (s, 0, 0)),
        ],
        out_specs=pl.BlockSpec((1, 1, blk, blk), lambda h, s: (h, s, 0, 0)),
        out_shape=jax.ShapeDtypeStruct((n_heads, 2, blk, blk), F32),
        compiler_params=_params("arbitrary", "arbitrary"),
        name="rel_bias",
    )(rel_table.astype(F32), idx)


def _odd_in_kernel(x_ref, g_ref, w_ref, cw_ref, qkv_ref, yd_ref, zbuf_ref, *, tiles_per_batch):
    i = pl.program_id(0)
    tm = x_ref.shape[0]
    qw = qkv_ref.shape[1]
    dw = yd_ref.shape[1]
    xn = _rmsnorm(x_ref[...], g_ref[...]).astype(BF16)
    qkv_ref[...] = _dot(xn, w_ref[:, 0:qw]).astype(BF16)
    hh = _dot(xn, w_ref[:, qw:qw + dw])
    gb = _dot(xn, w_ref[:, qw + dw:qw + 2 * dw])
    gc = _dot(xn, w_ref[:, qw + 2 * dw:qw + 3 * dw])
    z = gc * hh

    @pl.when(i % tiles_per_batch == 0)
    def _():
        zbuf_ref[0:8, :] = jnp.zeros((8, dw), F32)

    @pl.when(i % tiles_per_batch != 0)
    def _():
        zbuf_ref[0:8, :] = zbuf_ref[tm:tm + 8, :]

    zbuf_ref[8:tm + 8, :] = z
    y = (cw_ref[0:1, :] * zbuf_ref[6:tm + 6, :] + cw_ref[1:2, :] * zbuf_ref[7:tm + 7, :]
         + cw_ref[2:3, :] * z)
    yd_ref[...] = (gb * y).astype(BF16)


def _odd_in(x, g, w, cw, *, seq, qw):
    t, d = x.shape
    tm = min(TOK_TILE, seq)
    dw = cw.shape[1]
    return pl.pallas_call(
        functools.partial(_odd_in_kernel, tiles_per_batch=seq // tm),
        grid=(t // tm,),
        in_specs=[
            pl.BlockSpec((tm, d), lambda i: (i, 0)),
            _const_spec((1, d)),
            _const_spec(w.shape),
            _const_spec(cw.shape),
        ],
        out_specs=[
            pl.BlockSpec((tm, qw), lambda i: (i, 0)),
            pl.BlockSpec((tm, dw), lambda i: (i, 0)),
        ],
        out_shape=[
            jax.ShapeDtypeStruct((t, qw), BF16),
            jax.ShapeDtypeStruct((t, dw), BF16),
        ],
        scratch_shapes=[pltpu.VMEM((tm + 8, dw), F32)],
        compiler_params=_params("arbitrary"),
        name="odd_in",
    )(x, g, w, cw)


def _mem_kv_kernel(mem_ref, g_ref, w_ref, k_ref, v_ref):
    d = mem_ref.shape[2]
    mn = _rmsnorm(mem_ref[0], g_ref[...]).astype(BF16)
    k_ref[0] = _dot(mn, w_ref[:, 0:d].astype(BF16)).astype(BF16)
    v_ref[0] = _dot(mn, w_ref[:, d:2 * d].astype(BF16)).astype(BF16)


def _mem_kv(mem, g, w):
    b, m, d = mem.shape
    return pl.pallas_call(
        _mem_kv_kernel,
        grid=(b,),
        in_specs=[
            pl.BlockSpec((1, m, d), lambda i: (i, 0, 0)),
            _const_spec((1, d)),
            _const_spec(w.shape),
        ],
        out_specs=[pl.BlockSpec((1, m, d), lambda i: (i, 0, 0))] * 2,
        out_shape=[jax.ShapeDtypeStruct((b, m, d), BF16)] * 2,
        compiler_params=_params("arbitrary"),
        name="mem_kv",
    )(mem, g, w)


def _post_mixer_kernel(x_ref, ya_ref, yb_ref, wout_ref, g_ref, wq_ref, k_ref, v_ref, wo_ref, o_ref):
    wa = ya_ref.shape[1]
    d = x_ref.shape[1]
    hd = d // N_HEADS
    x1 = (x_ref[...] + _dot(ya_ref[...], wout_ref[0:wa, :].astype(BF16))
          + _dot(yb_ref[...], wout_ref[wa:, :].astype(BF16)))
    xn = _rmsnorm(x1, g_ref[...]).astype(BF16)
    q = (_dot(xn, wq_ref[...].astype(BF16)) * (float(hd) ** -0.5)).astype(BF16)
    heads = []
    for h in range(N_HEADS):
        lg = _dot_nt(q[:, h * hd:(h + 1) * hd], k_ref[0, :, h * hd:(h + 1) * hd])
        p = jnp.exp(lg - jnp.max(lg, axis=-1, keepdims=True))
        p = p / jnp.sum(p, axis=-1, keepdims=True)
        heads.append(_dot(p.astype(BF16), v_ref[0, :, h * hd:(h + 1) * hd]).astype(BF16))
    o_ref[...] = x1 + _dot(jnp.concatenate(heads, axis=-1), wo_ref[...].astype(BF16))


def _post_mixer(x, ya, yb, wout, g, wq, kmem, vmem, wo, *, seq):
    t, d = x.shape
    tm = min(TOK_TILE, seq)
    nt = seq // tm
    m = kmem.shape[1]
    wa = ya.shape[1]
    return pl.pallas_call(
        _post_mixer_kernel,
        grid=(t // tm,),
        in_specs=[
            pl.BlockSpec((tm, d), lambda i: (i, 0)),
            pl.BlockSpec((tm, wa), lambda i: (i, 0)),
            pl.BlockSpec((tm, yb.shape[1]), lambda i: (i, 0)),
            _const_spec(wout.shape),
            _const_spec((1, d)),
            _const_spec(wq.shape),
            pl.BlockSpec((1, m, d), lambda i: (i // nt, 0, 0)),
            pl.BlockSpec((1, m, d), lambda i: (i // nt, 0, 0)),
            _const_spec(wo.shape),
        ],
        out_specs=pl.BlockSpec((tm, d), lambda i: (i, 0)),
        out_shape=jax.ShapeDtypeStruct((t, d), F32),
        compiler_params=_params("arbitrary"),
        name="post_mixer",
    )(x, ya, yb, wout, g, wq, kmem, vmem, wo)


def _ffn_kernel(x_ref, g_ref, wg_ref, wu_ref, wd_ref, o_ref):
    x = x_ref[...]
    h = _rmsnorm(x, g_ref[...]).astype(BF16)
    ff = wg_ref.shape[1]
    acc = x
    for c0 in range(0, ff, FFN_CHUNK):
        c1 = min(c0 + FFN_CHUNK, ff)
        a = _dot(h, wg_ref[:, c0:c1].astype(BF16))
        u = _dot(h, wu_ref[:, c0:c1].astype(BF16))
        acc = acc + _dot((jax.nn.silu(a) * u).astype(BF16), wd_ref[c0:c1, :].astype(BF16))
    o_ref[...] = acc


def _ffn(x, g, wg, wu, wd, *, seq):
    t, d = x.shape
    tm = min(TOK_TILE, seq)
    return pl.pallas_call(
        _ffn_kernel,
        grid=(t // tm,),
        in_specs=[
            pl.BlockSpec((tm, d), lambda i: (i, 0)),
            _const_spec((1, d)),
            _const_spec(wg.shape),
            _const_spec(wu.shape),
            _const_spec(wd.shape),
        ],
        out_specs=pl.BlockSpec((tm, d), lambda i: (i, 0)),
        out_shape=jax.ShapeDtypeStruct((t, d), F32),
        compiler_params=_params("arbitrary"),
        name="ffn",
    )(x, g, wg, wu, wd)


META_I1, META_I2, META_R1, META_R2, META_G1, META_G2 = range(6)


def _router_kernel(x_ref, g_ref, wr_ref, meta_ref, metat_ref, cnt_ref, carry_ref):
    i = pl.program_id(0)
    tm = x_ref.shape[0]
    h = _rmsnorm(x_ref[...], g_ref[...])
    hp = _split3(h)
    wp = _split3(wr_ref[...])
    logits = _dot(hp[1], wp[0]) + _dot(hp[0], wp[1]) + _dot(hp[0], wp[0])
    lane = lax.broadcasted_iota(I32, logits.shape, 1)
    logits = jnp.where(lane < N_EXPERTS, logits, NEG)
    m1 = jnp.max(logits, axis=-1, keepdims=True)
    i1 = jnp.min(jnp.where(logits == m1, lane, LANES), axis=-1, keepdims=True)
    rest = jnp.where(lane == i1, NEG, logits)
    m2 = jnp.max(rest, axis=-1, keepdims=True)
    i2 = jnp.min(jnp.where(rest == m2, lane, LANES), axis=-1, keepdims=True)
    e = jnp.exp(m2 - m1)
    g1 = 1.0 / (1.0 + e)
    g2 = e / (1.0 + e)

    sel = jnp.where((lane == i1) | (lane == i2), 1.0, 0.0)
    row = lax.broadcasted_iota(I32, (tm, tm), 0)
    col = lax.broadcasted_iota(I32, (tm, tm), 1)
    strict = jnp.where(col < row, 1.0, 0.0).astype(BF16)

    @pl.when(i == 0)
    def _():
        carry_ref[...] = jnp.zeros_like(carry_ref)

    rank = _dot(strict, sel.astype(BF16)) + carry_ref[0:1, :]
    total = rank[tm - 1:tm, :] + sel[tm - 1:tm, :]
    carry_ref[...] = jnp.broadcast_to(total, carry_ref.shape)
    cnt_ref[...] = jnp.broadcast_to(total, cnt_ref.shape)
    r1 = jnp.sum(jnp.where(lane == i1, rank, 0.0), axis=-1, keepdims=True)
    r2 = jnp.sum(jnp.where(lane == i2, rank, 0.0), axis=-1, keepdims=True)
    meta = jnp.zeros_like(logits)
    for slot, val in ((META_I1, i1.astype(F32)), (META_I2, i2.astype(F32)), (META_R1, r1),
                      (META_R2, r2), (META_G1, g1), (META_G2, g2)):
        meta = jnp.where(lane == slot, val, meta)
    meta_ref[...] = meta
    metat_ref[...] = meta.T[0:8, :]


def _router(x, g, wr, *, seq):
    t, d = x.shape
    tm = min(TOK_TILE, seq)
    return pl.pallas_call(
        _router_kernel,
        grid=(t // tm,),
        in_specs=[
            pl.BlockSpec((tm, d), lambda i: (i, 0)),
            _const_spec((1, d)),
            _const_spec(wr.shape),
        ],
        out_specs=[
            pl.BlockSpec((tm, LANES), lambda i: (i, 0)),
            pl.BlockSpec((8, tm), lambda i: (0, i)),
            pl.BlockSpec((8, LANES), lambda i: (0, 0)),
        ],
        out_shape=[
            jax.ShapeDtypeStruct((t, LANES), F32),
            jax.ShapeDtypeStruct((8, t), F32),
            jax.ShapeDtypeStruct((8, LANES), F32),
        ],
        scratch_shapes=[pltpu.VMEM((8, LANES), F32)],
        compiler_params=_params("arbitrary"),
        name="router",
    )(x, g, wr)


def _row_copy(src, src_row, dst, dst_row, sem):
    return pltpu.make_async_copy(src.at[pl.ds(src_row, 1)], dst.at[pl.ds(dst_row, 1)], sem)


def _dispatch_kernel(pos_ref, pad_ref, x_ref, g_ref, xs_ref, h_ref, zrow_ref, sem_ref, zsem_ref):
    c = pl.program_id(0)
    n_chunks = pl.num_programs(0)
    tc = x_ref.shape[0]
    slot = c % 2

    @pl.when(c == 0)
    def _():
        zrow_ref[...] = jnp.zeros_like(zrow_ref)
        for e in range(N_EXPERTS):
            start = pad_ref[2 * e]
            count = pad_ref[2 * e + 1] - start

            def zissue(r, carry, start=start):
                _row_copy(zrow_ref, 0, xs_ref, start + r, zsem_ref).start()
                return carry

            def zwait(r, carry):
                _row_copy(zrow_ref, 0, xs_ref, 0, zsem_ref).wait()
                return carry

            lax.fori_loop(0, count, zissue, 0)
            lax.fori_loop(0, count, zwait, 0)

        tail = pad_ref[2 * N_EXPERTS - 1]
        groups = (xs_ref.shape[0] - tail) // 8

        def tissue(r, carry):
            start = pl.multiple_of(tail + 8 * r, 8)
            pltpu.make_async_copy(zrow_ref, xs_ref.at[pl.ds(start, 8)], zsem_ref).start()
            return carry

        def twait(r, carry):
            pltpu.make_async_copy(zrow_ref, xs_ref.at[pl.ds(0, 8)], zsem_ref).wait()
            return carry

        lax.fori_loop(0, groups, tissue, 0)
        lax.fori_loop(0, groups, twait, 0)

    h_ref[slot] = _rmsnorm(x_ref[...], g_ref[...])

    def issue(r, carry):
        tok = c * tc + r
        _row_copy(h_ref.at[slot], r, xs_ref, pos_ref[tok], sem_ref.at[slot]).start()
        _row_copy(h_ref.at[slot], r, xs_ref, pos_ref[n_chunks * tc + tok], sem_ref.at[slot]).start()
        return carry

    def drain(s):
        for _ in range(2):
            pltpu.make_async_copy(h_ref.at[s], xs_ref.at[pl.ds(0, tc)], sem_ref.at[s]).wait()

    lax.fori_loop(0, tc, issue, 0, unroll=DMA_UNROLL)

    @pl.when(c > 0)
    def _():
        drain(1 - slot)

    @pl.when(c == n_chunks - 1)
    def _():
        drain(slot)


def _dispatch(pos, pad, x, g, n_rows):
    t, d = x.shape
    tc = DISPATCH_CHUNK
    grid_spec = pltpu.PrefetchScalarGridSpec(
        num_scalar_prefetch=2,
        grid=(t // tc,),
        in_specs=[
            pl.BlockSpec((tc, d), lambda c, pos, pad: (c, 0)),
            pl.BlockSpec((1, d), lambda c, pos, pad: (0, 0)),
        ],
        out_specs=pl.BlockSpec(memory_space=pl.ANY),
        scratch_shapes=[
            pltpu.VMEM((2, tc, d), F32),
            pltpu.VMEM((8, d), F32),
            pltpu.SemaphoreType.DMA((2,)),
            pltpu.SemaphoreType.DMA(()),
        ],
    )
    return pl.pallas_call(
        _dispatch_kernel,
        grid_spec=grid_spec,
        out_shape=jax.ShapeDtypeStruct((n_rows, d), F32),
        compiler_params=_params("arbitrary"),
        name="moe_dispatch",
    )(pos, pad, x, g)


def _experts_kernel(te_ref, rows_ref, nused_ref, xs_ref, wg_ref, wu_ref, wd_ref, y_ref, xb_ref):
    p = pl.program_id(0)
    f = pl.program_id(1)
    rows = rows_ref[p]
    half = MOE_TILE // 2

    @pl.when(f == 0)
    def _():
        y_ref[...] = jnp.zeros_like(y_ref)

    @pl.when(rows > 0)
    def _():
        @pl.when(f == 0)
        def _():
            xb_ref[...] = xs_ref[...].astype(BF16)

        tf = wg_ref.shape[2]
        wgu = jnp.concatenate([wg_ref[0].astype(BF16), wu_ref[0].astype(BF16)], axis=1)
        wd = wd_ref[0].astype(BF16)

        def run(r0):
            gu = _dot(xb_ref[r0:r0 + half, :], wgu)
            act = (jax.nn.silu(gu[:, :tf]) * gu[:, tf:]).astype(BF16)
            y_ref[r0:r0 + half, :] += _dot(act, wd)

        run(0)

        @pl.when(rows > half)
        def _():
            run(half)


def _experts(tile_expert, tile_rows, n_used, xs, wg, wu, wd):
    n_rows, d = xs.shape
    ff = wg.shape[2]
    tm = MOE_TILE
    tf = MOE_FF_CHUNK
    n_tiles = n_rows // tm

    def x_map(p, f, te, tr, nu):
        return (jnp.minimum(p, nu[0] - 1), 0)

    grid_spec = pltpu.PrefetchScalarGridSpec(
        num_scalar_prefetch=3,
        grid=(n_tiles, ff // tf),
        in_specs=[
            pl.BlockSpec((tm, d), x_map),
            pl.BlockSpec((1, d, tf), lambda p, f, te, tr, nu: (te[p], 0, f)),
            pl.BlockSpec((1, d, tf), lambda p, f, te, tr, nu: (te[p], 0, f)),
            pl.BlockSpec((1, tf, d), lambda p, f, te, tr, nu: (te[p], f, 0)),
        ],
        out_specs=pl.BlockSpec((tm, d), lambda p, f, te, tr, nu: (p, 0)),
        scratch_shapes=[pltpu.VMEM((tm, d), BF16)],
    )
    return pl.pallas_call(
        _experts_kernel,
        grid_spec=grid_spec,
        out_shape=jax.ShapeDtypeStruct((n_rows, d), F32),
        compiler_params=_params("arbitrary", "arbitrary"),
        name="moe_experts",
    )(tile_expert, tile_rows, n_used, xs, wg, wu, wd)


def _combine_kernel(pos_ref, y_ref, x_ref, meta_ref, g_ref, o_ref, buf_ref, sem_ref):
    i = pl.program_id(0)
    n = pl.num_programs(0)
    tc = x_ref.shape[0]

    def fetch(tile, slot):
        def body(r, carry):
            tok = tile * tc + r
            for k in range(2):
                pltpu.make_async_copy(y_ref.at[pl.ds(pos_ref[k * n * tc + tok], 1)],
                                      buf_ref.at[slot, k, pl.ds(r, 1)], sem_ref.at[slot]).start()
            return carry
        lax.fori_loop(0, tc, body, 0, unroll=DMA_UNROLL)

    @pl.when(i == 0)
    def _():
        fetch(0, 0)

    @pl.when(i + 1 < n)
    def _():
        fetch(i + 1, (i + 1) % 2)

    slot = i % 2

    for k in range(2):
        pltpu.make_async_copy(y_ref.at[pl.ds(0, tc)], buf_ref.at[slot, k], sem_ref.at[slot]).wait()
    meta = meta_ref[...]
    g1 = meta[:, META_G1:META_G1 + 1]
    g2 = meta[:, META_G2:META_G2 + 1]
    x = x_ref[...] + g1 * buf_ref[slot, 0] + g2 * buf_ref[slot, 1]
    o_ref[...] = _rmsnorm(x, g_ref[...])


def _combine(pos, y, x, meta, g):
    t, d = x.shape
    tc = COMBINE_TILE
    grid_spec = pltpu.PrefetchScalarGridSpec(
        num_scalar_prefetch=1,
        grid=(t // tc,),
        in_specs=[
            pl.BlockSpec(memory_space=pl.ANY),
            pl.BlockSpec((tc, d), lambda i, pos: (i, 0)),
            pl.BlockSpec((tc, LANES), lambda i, pos: (i, 0)),
            pl.BlockSpec((1, d), lambda i, pos: (0, 0)),
        ],
        out_specs=pl.BlockSpec((tc, d), lambda i, pos: (i, 0)),
        scratch_shapes=[pltpu.VMEM((2, 2, tc, d), F32), pltpu.SemaphoreType.DMA((2,))],
    )
    return pl.pallas_call(
        _combine_kernel,
        grid_spec=grid_spec,
        out_shape=jax.ShapeDtypeStruct((t, d), F32),
        compiler_params=_params("arbitrary"),
        name="moe_combine",
    )(pos, y, x, meta, g)


def _moe_plan(meta_t, counts, n_tiles):
    cnt = counts[0, :N_EXPERTS].astype(I32)
    tiles = (cnt + MOE_TILE - 1) // MOE_TILE
    tile_end = jnp.cumsum(tiles)
    row_off = (tile_end - tiles) * MOE_TILE
    n_used = tile_end[-1:]
    experts = jnp.arange(N_EXPERTS, dtype=I32)

    def rows(idx_row, rank_row):
        idx = meta_t[idx_row].astype(I32)
        off = jnp.sum(jnp.where(idx[None, :] == experts[:, None], row_off[:, None], 0), axis=0)
        return off + meta_t[rank_row].astype(I32)

    pos = jnp.concatenate([rows(META_I1, META_R1), rows(META_I2, META_R2)])
    tile_ids = jnp.arange(n_tiles, dtype=I32)
    te = jnp.sum((tile_end[None, :] <= tile_ids[:, None]).astype(I32), axis=1)
    mine = te[:, None] == experts[None, :]
    first_row = tile_ids * MOE_TILE - jnp.sum(jnp.where(mine, row_off[None, :], 0), axis=1)
    tile_rows = jnp.clip(jnp.sum(jnp.where(mine, cnt[None, :], 0), axis=1) - first_row, 0, MOE_TILE)
    last = jnp.sum((tile_end <= n_used - 1).astype(I32))
    te = jnp.minimum(te, last).astype(I32)
    pad = jnp.stack([row_off + cnt, row_off + tiles * MOE_TILE], axis=1).reshape(-1).astype(I32)
    return pos.astype(I32), pad, te, tile_rows.astype(I32), n_used.astype(I32)


def _moe_and_final_norm(x, g_ffn, wr, wg, wu, wd, g_final, *, seq):
    t, d = x.shape
    n_tiles = 2 * t // MOE_TILE + N_EXPERTS
    meta, meta_t, counts = _router(x, g_ffn, wr, seq=seq)
    pos, pad, te, tile_rows, n_used = _moe_plan(meta_t, counts, n_tiles)
    xs = _dispatch(pos, pad, x, g_ffn, n_tiles * MOE_TILE)
    y = _experts(te, tile_rows, n_used, xs, wg, wu, wd)
    return _combine(pos, y, x, meta, g_final)


def _row(v):
    return v.reshape(1, -1).astype(F32)


def kernel(x, mem, norm_mix, norm_mem_q, norm_mem_kv, norm_ffn, norm_final, even_w_in, fox_b_f, gmlp_v_gain, gmlp_w_s, gmlp_b_s, even_w_out, odd_w_in, diff_lambda_q1, diff_lambda_k1, diff_lambda_q2, diff_lambda_k2, diff_subln_gain, conv_w, odd_w_out, rel_bias, mem_w_q, mem_w_kv, mem_w_o, ffn_w_gate, ffn_w_up, ffn_w_down, router_w, moe_w_gate, moe_w_up, moe_w_down):
    batch, seq, d = x.shape
    t = batch * seq
    depth = norm_mix.shape[0]
    assert depth == 2 and seq % (2 * ATT_BLOCK) == 0 and seq % TOK_TILE == 0
    xf = x.reshape(t, d)
    hd = d // N_HEADS
    aw = gmlp_v_gain.shape[1]
    n_blk = gmlp_w_s.shape[2]

    w_in = even_w_in[0]
    q0 = 2 * aw
    bw = N_HEADS * LANES
    scale = jnp.ones((w_in.shape[1],), F32).at[q0:q0 + bw].set(float(LANES) ** -0.5)
    w_in = jnp.pad(w_in * scale, ((0, 0), (0, LANES - N_HEADS))).astype(BF16)
    bf = jnp.pad(fox_b_f[0], (0, LANES - N_HEADS)).reshape(1, LANES)
    bs = jnp.broadcast_to(gmlp_b_s[0][:, :, None], (A_GROUPS, n_blk, aw // A_GROUPS)).astype(F32)
    ya, qkv, aug = _even_in(xf, _row(norm_mix[0]), w_in, bf, _row(gmlp_v_gain[0]),
                            gmlp_w_s[0], bs, seq=seq)
    yb = _fox_attention(qkv, aug, batch=batch, seq=seq)

    def mem_attention(layer, x_in, ya_, yb_, w_out):
        km, vm = _mem_kv(mem, _row(norm_mem_kv[layer]), mem_w_kv[layer])
        return _post_mixer(x_in, ya_, yb_, w_out, _row(norm_mem_q[layer]), mem_w_q[layer], km, vm,
                           mem_w_o[layer], seq=seq)

    xf = mem_attention(0, xf, ya, yb, even_w_out[0])
    xf = _ffn(xf, _row(norm_ffn[0]), ffn_w_gate[0], ffn_w_up[0], ffn_w_down[0], seq=seq)

    w_in = odd_w_in[0]
    qk_dim = LANES // 2
    qk_w = 2 * N_HEADS * qk_dim
    def regroup(cols):
        return cols.reshape(d, 2, N_HEADS, qk_dim).transpose(0, 2, 1, 3).reshape(d, qk_w)

    w_in = jnp.concatenate([regroup(w_in[:, :qk_w]) * float(qk_dim) ** -0.5,
                            regroup(w_in[:, qk_w:2 * qk_w]), w_in[:, 2 * qk_w:]], axis=1).astype(BF16)
    qkv, yd = _odd_in(xf, _row(norm_mix[1]), w_in, conv_w[0].astype(F32), seq=seq,
                      qw=2 * qk_w + N_HEADS * LANES)
    lam_init = 0.8 - 0.6 * math.exp(-0.3 * 1)
    lam = (jnp.exp(jnp.sum(diff_lambda_q1[0] * diff_lambda_k1[0]))
           - jnp.exp(jnp.sum(diff_lambda_q2[0] * diff_lambda_k2[0])) + lam_init)
    bias = _diff_bias_blocks(rel_bias, min(ATT_BLOCK, seq))
    yc = _diff_attention(qkv, lam.reshape(1, 1).astype(F32), bias, _row(diff_subln_gain[0]),
                         batch=batch, seq=seq, out_scale=1.0 - lam_init)
    xf = mem_attention(1, xf, yc, yd, odd_w_out[0])
    wr = jnp.pad(router_w[0], ((0, 0), (0, LANES - N_EXPERTS))).astype(F32)
    out = _moe_and_final_norm(xf, _row(norm_ffn[1]), wr, moe_w_gate[0], moe_w_up[0], moe_w_down[0],
                              _row(norm_final), seq=seq)
    return out.reshape(batch, seq, d)
```

```python
import functools
import math

import numpy as np
import jax
import jax.numpy as jnp
from jax import lax
from jax.experimental import pallas as pl
from jax.experimental.pallas import tpu as pltpu

F32 = jnp.float32
BF16 = jnp.bfloat16
I32 = jnp.int32

EPS = 1e-6
NEG = -1e30
LANES = 128
VMEM_LIMIT = 56 * 1024 * 1024

CHUNK = 64
A_GROUPS = 4
N_HEADS = 4
N_EXPERTS = 8
REL_BUCKETS = 32
REL_MAX_DIST = 128

TOK_TILE = 512
FOX_BLOCK = 1024
ATT_BLOCK = 512
ATT_STRIP = 64
MOE_TILE = 1024
MOE_FF_CHUNK = 896
FFN_CHUNK = 512
DISPATCH_CHUNK = 256
COMBINE_TILE = 256
DMA_UNROLL = 8


def _dot(a, b):
    return jnp.dot(a, b, preferred_element_type=F32)


def _dot_nt(a, b):
    return lax.dot_general(a, b, (((1,), (1,)), ((), ())), preferred_element_type=F32)


def _rmsnorm(x, g):
    return x * lax.rsqrt(jnp.mean(x * x, axis=-1, keepdims=True) + EPS) * g


def _params(*sem):
    return pltpu.CompilerParams(dimension_semantics=sem, vmem_limit_bytes=VMEM_LIMIT)


def _const_spec(shape):
    return pl.BlockSpec(shape, lambda *_: (0,) * len(shape), pipeline_mode=pl.Buffered(1))


def _layer_spec(stacked, layer):
    rest = stacked.shape[1:]
    return pl.BlockSpec((None,) + rest, lambda *_: (layer,) + (0,) * len(rest),
                        pipeline_mode=pl.Buffered(1))


def _split3(v):
    a1 = v.astype(BF16)
    r1 = v - a1.astype(F32)
    a2 = r1.astype(BF16)
    a3 = (r1 - a2.astype(F32)).astype(BF16)
    return a1, a2, a3


def _even_in_kernel(x_ref, g_ref, w_ref, wf_ref, bf_ref, vg_ref, ws_ref, bs_ref,
                    ya_ref, qkv_ref, aug_ref, carry_ref, *, tiles_per_batch):
    i = pl.program_id(0)
    tm = x_ref.shape[0]
    aw = ya_ref.shape[1]
    qw = qkv_ref.shape[1]
    xn = _rmsnorm(x_ref[...], g_ref[...]).astype(BF16)

    hw = N_HEADS * LANES
    q = _dot(xn, w_ref[:, 2 * aw:2 * aw + hw].astype(BF16)) * (float(LANES) ** -0.5)
    qkv_ref[:, 0:hw] = q.astype(BF16)
    qkv_ref[:, hw:qw] = _dot(xn, w_ref[:, 2 * aw + hw:2 * aw + qw].astype(BF16)).astype(BF16)

    fl = _dot(xn, wf_ref[...].astype(BF16)) + bf_ref[...]
    ls = jnp.minimum(fl, 0.0) - jnp.log1p(jnp.exp(-jnp.abs(fl)))
    row = lax.broadcasted_iota(I32, (tm, tm), 0)
    col = lax.broadcasted_iota(I32, (tm, tm), 1)
    tri = jnp.where(col <= row, 1.0, 0.0).astype(BF16)
    a1, a2, a3 = _split3(ls)
    csum = _dot(tri, a1) + _dot(tri, a2) + _dot(tri, a3)

    @pl.when(i % tiles_per_batch == 0)
    def _():
        carry_ref[...] = jnp.zeros_like(carry_ref)

    csum = csum + carry_ref[0:1, :]
    carry_ref[...] = jnp.broadcast_to(csum[tm - 1:tm, :], carry_ref.shape)

    lane = lax.broadcasted_iota(I32, (tm, LANES), 1)
    for h in range(N_HEADS):
        c1, c2, c3 = (piece.astype(F32)
                      for piece in _split3(jnp.broadcast_to(csum[:, h:h + 1], (tm, LANES))))
        aq = jnp.where(lane == 0, c1, jnp.where(lane == 1, c2, jnp.where(lane == 2, c3,
                       jnp.where(lane < 6, 1.0, 0.0))))
        ak = jnp.where(lane == 3, -c1, jnp.where(lane == 4, -c2, jnp.where(lane == 5, -c3,
                       jnp.where(lane < 3, 1.0, 0.0))))
        aug_ref[:, h * LANES:(h + 1) * LANES] = aq.astype(BF16)
        aug_ref[:, (N_HEADS + h) * LANES:(N_HEADS + h + 1) * LANES] = ak.astype(BF16)

    gu = jax.nn.gelu(_dot(xn, w_ref[:, 0:aw].astype(BF16)), approximate=True)
    gv = jax.nn.gelu(_dot(xn, w_ref[:, aw:2 * aw].astype(BF16)), approximate=True)
    blk = ws_ref.shape[1]
    ch = aw // A_GROUPS
    r = lax.broadcasted_iota(I32, (blk, blk), 0)
    c = lax.broadcasted_iota(I32, (blk, blk), 1)
    causal = (c // CHUNK) <= (r // CHUNK)
    for g in range(A_GROUPS):
        vgrp = gv[:, g * ch:(g + 1) * ch]
        vn = _rmsnorm(vgrp, vg_ref[:, g * ch:(g + 1) * ch]).astype(BF16)
        wmix = jnp.where(causal, ws_ref[g], 0.0).astype(BF16)
        for n in range(tm // blk):
            mixed = _dot(wmix, vn[n * blk:(n + 1) * blk, :]) + bs_ref[g]
            ya_ref[n * blk:(n + 1) * blk, g * ch:(g + 1) * ch] = (
                gu[n * blk:(n + 1) * blk, g * ch:(g + 1) * ch] * mixed).astype(BF16)


def _even_in(x, g, w_stacked, wf, bf, vgain, ws, bs, *, seq):
    t, d = x.shape
    tm = min(TOK_TILE, seq)
    aw = vgain.shape[1]
    qw = 3 * N_HEADS * LANES
    blk = ws.shape[1]
    return pl.pallas_call(
        functools.partial(_even_in_kernel, tiles_per_batch=seq // tm),
        grid=(t // tm,),
        in_specs=[
            pl.BlockSpec((tm, d), lambda i: (i, 0)),
            _const_spec((1, d)),
            _layer_spec(w_stacked, 0),
            _const_spec(wf.shape),
            _const_spec((1, LANES)),
            _const_spec((1, aw)),
            _const_spec(ws.shape),
            _const_spec(bs.shape),
        ],
        out_specs=[
            pl.BlockSpec((tm, aw), lambda i: (i, 0)),
            pl.BlockSpec((tm, qw), lambda i: (i, 0)),
            pl.BlockSpec((tm, 2 * N_HEADS * LANES), lambda i: (i, 0)),
        ],
        out_shape=[
            jax.ShapeDtypeStruct((t, aw), BF16),
            jax.ShapeDtypeStruct((t, qw), BF16),
            jax.ShapeDtypeStruct((t, 2 * N_HEADS * LANES), BF16),
        ],
        scratch_shapes=[pltpu.VMEM((8, LANES), F32)],
        compiler_params=_params("arbitrary"),
        name="even_in",
    )(x, g, w_stacked, wf, bf, vgain, ws, bs)


def _tri_pairs(nblk):
    qi = np.array([i for i in range(nblk) for _ in range(i + 1)], np.int32)
    kj = np.array([j for i in range(nblk) for j in range(i + 1)], np.int32)
    return jnp.asarray(qi), jnp.asarray(kj)


def _softmax_strips(logits, p_ref, alpha_ref, m_ref, h, add_bias=None):
    rows, tk = logits.shape
    for r0 in range(0, rows, ATT_STRIP):
        rs = slice(r0, r0 + ATT_STRIP)
        s = logits[rs, :]
        if add_bias is not None:
            s = add_bias(s, r0)
        m_prev = m_ref[h, rs, :]
        m_new = jnp.maximum(m_prev, jnp.max(s, axis=-1, keepdims=True))
        alpha_ref[rs, :] = jnp.exp(m_prev - m_new)
        m_ref[h, rs, :] = m_new
        p_ref[rs, 0:tk] = jnp.exp(
            (s - jnp.concatenate([m_new] * (tk // LANES), axis=-1)).astype(BF16))


def _accumulate(p_ref, alpha_ref, v, ones, acc_ref, h):
    alpha = alpha_ref[...]
    va = jnp.concatenate([v, ones], axis=-1)
    acc_ref[h] = (jnp.concatenate([alpha, alpha], axis=-1) * acc_ref[h]
                  + _dot(p_ref[:, 0:v.shape[0]], va))


def _fox_kernel(qi_ref, kj_ref, q_ref, aq_ref, k_ref, ak_ref, v_ref, ones_ref, mask_ref, o_ref,
                p_ref, alpha_ref, m_ref, acc_ref):
    step = pl.program_id(1)
    i = qi_ref[step]
    j = kj_ref[step]

    @pl.when(j == 0)
    def _():
        m_ref[...] = jnp.full_like(m_ref, NEG)
        acc_ref[...] = jnp.zeros_like(acc_ref)

    def logits(h):
        hs = slice(h * LANES, (h + 1) * LANES)
        qa = jnp.concatenate([q_ref[:, hs], aq_ref[:, hs]], axis=-1)
        ka = jnp.concatenate([k_ref[:, hs], ak_ref[:, hs]], axis=-1)
        return _dot_nt(qa, ka)

    def sweep(masked):
        nxt = logits(0)
        for h in range(N_HEADS):
            hs = slice(h * LANES, (h + 1) * LANES)
            sb = h % 2
            cur = nxt
            if h + 1 < N_HEADS:
                nxt = logits(h + 1)
            add_mask = (lambda s, r0: s + mask_ref[r0:r0 + ATT_STRIP, :]) if masked else None
            _softmax_strips(cur, p_ref.at[sb], alpha_ref.at[sb], m_ref, h, add_mask)
            _accumulate(p_ref.at[sb], alpha_ref.at[sb], v_ref[:, hs], ones_ref[...], acc_ref, h)

    @pl.when(j < i)
    def _():
        sweep(False)

    @pl.when(j == i)
    def _():
        sweep(True)
        for h in range(N_HEADS):
            acc = acc_ref[h]
            o_ref[:, h * LANES:(h + 1) * LANES] = (
                acc[:, :LANES] / acc[:, LANES:LANES + 1]).astype(o_ref.dtype)


def _ones_column(rows):
    ones = np.zeros((rows, LANES), np.float32)
    ones[:, 0] = 1.0
    return jnp.asarray(ones, BF16)


def _fox_attention(qkv, aug, *, batch, seq):
    t = qkv.shape[0]
    w = N_HEADS * LANES
    blk = min(FOX_BLOCK, seq)
    nblk = seq // blk
    qi, kj = _tri_pairs(nblk)
    r = np.arange(blk)
    mask = jnp.asarray(np.where(r[None, :] <= r[:, None], 0.0, NEG), F32)
    grid_spec = pltpu.PrefetchScalarGridSpec(
        num_scalar_prefetch=2,
        grid=(batch, qi.shape[0]),
        in_specs=[
            pl.BlockSpec((blk, w), lambda b, s, qi, kj: (b * nblk + qi[s], 0)),
            pl.BlockSpec((blk, w), lambda b, s, qi, kj: (b * nblk + qi[s], 0)),
            pl.BlockSpec((blk, w), lambda b, s, qi, kj: (b * nblk + kj[s], 1)),
            pl.BlockSpec((blk, w), lambda b, s, qi, kj: (b * nblk + kj[s], 1)),
            pl.BlockSpec((blk, w), lambda b, s, qi, kj: (b * nblk + kj[s], 2)),
            pl.BlockSpec((blk, LANES), lambda b, s, qi, kj: (0, 0)),
            pl.BlockSpec((blk, blk), lambda b, s, qi, kj: (0, 0)),
        ],
        out_specs=pl.BlockSpec((blk, w), lambda b, s, qi, kj: (b * nblk + qi[s], 0)),
        scratch_shapes=[
            pltpu.VMEM((2, blk, blk), BF16),
            pltpu.VMEM((2, blk, LANES), F32),
            pltpu.VMEM((N_HEADS, blk, LANES), F32),
            pltpu.VMEM((N_HEADS, blk, 2 * LANES), F32),
        ],
    )
    return pl.pallas_call(
        _fox_kernel,
        grid_spec=grid_spec,
        out_shape=jax.ShapeDtypeStruct((t, w), BF16),
        compiler_params=_params("arbitrary", "arbitrary"),
        name="fox_attention",
    )(qi, kj, qkv, aug, qkv, aug, qkv, _ones_column(blk), mask)


def _diff_pairs(nq):
    qi = np.array([i for i in range(nq) for _ in range(i // 2 + 1)], np.int32)
    kj = np.array([j for i in range(nq) for j in range(i // 2 + 1)], np.int32)
    return jnp.asarray(qi), jnp.asarray(kj)


def _diff_kernel(qi_ref, kj_ref, lam_ref, q_ref, k_ref, v_ref, ones_ref, bias_ref, gain_ref, o_ref,
                 qs_ref, p_ref, alpha_ref, m_ref, acc_ref, *, out_scale):
    step = pl.program_id(1)
    i = qi_ref[step]
    j = kj_ref[step]
    tq = q_ref.shape[0]
    half = LANES // 2
    jd = i // 2
    odd = (i % 2) == 1

    @pl.when(j == 0)
    def _():
        m_ref[...] = jnp.full_like(m_ref, NEG)
        acc_ref[...] = jnp.zeros_like(acc_ref)
        lane = lax.broadcasted_iota(I32, (tq, LANES), 1)
        zero = jnp.zeros((tq, LANES), BF16)
        for h in range(N_HEADS):
            q = q_ref[:, h * LANES:(h + 1) * LANES]
            qs_ref[h, 0:tq, :] = jnp.where(lane < half, q, zero)
            qs_ref[h, tq:2 * tq, :] = jnp.where(lane < half, zero, q)

    def sweep(tk, add_bias):
        def logits(h):
            return _dot_nt(qs_ref[h], k_ref[0:tk, h * LANES:(h + 1) * LANES])

        nxt = logits(0)
        for h in range(N_HEADS):
            hs = slice(h * LANES, (h + 1) * LANES)
            sb = h % 2
            cur = nxt
            if h + 1 < N_HEADS:
                nxt = logits(h + 1)
            bias_fn = None if add_bias is None else functools.partial(add_bias, h)
            _softmax_strips(cur, p_ref.at[sb], alpha_ref.at[sb], m_ref, h, bias_fn)
            _accumulate(p_ref.at[sb], alpha_ref.at[sb], v_ref[0:tk, hs], ones_ref[0:tk, :], acc_ref, h)

    def bias_rows(h, which, r0):
        return bias_ref[h, which, pl.ds(r0 % tq, ATT_STRIP), :]

    def odd_diag(h, s, r0):
        return jnp.concatenate([s[:, :tq] + bias_rows(h, 1, r0), s[:, tq:] + bias_rows(h, 0, r0)],
                               axis=-1)

    def even_prev(h, s, r0):
        return jnp.concatenate([s[:, :tq], s[:, tq:] + bias_rows(h, 1, r0)], axis=-1)

    def even_diag(h, s, r0):
        return s + bias_rows(h, 0, r0)

    @pl.when((j < jd - 1) | ((j == jd - 1) & odd))
    def _():
        sweep(2 * tq, None)

    @pl.when((j == jd - 1) & jnp.logical_not(odd))
    def _():
        sweep(2 * tq, even_prev)

    @pl.when((j == jd) & odd)
    def _():
        sweep(2 * tq, odd_diag)

    @pl.when((j == jd) & jnp.logical_not(odd))
    def _():
        sweep(tq, even_diag)

    @pl.when(j == jd)
    def _():
        for h in range(N_HEADS):
            hs = slice(h * LANES, (h + 1) * LANES)
            acc = acc_ref[h]
            o = acc[:, :LANES] / acc[:, LANES:LANES + 1]
            o = o[0:tq, :] - lam_ref[0, 0] * o[tq:2 * tq, :]
            o_ref[:, hs] = (_rmsnorm(o, gain_ref[:, hs]) * out_scale).astype(o_ref.dtype)


def _diff_attention(qkv, lam, bias, gain, *, batch, seq, out_scale):
    t = qkv.shape[0]
    w = N_HEADS * LANES
    blk = bias.shape[2]
    kblk = 2 * blk
    nq = seq // blk
    nk = seq // kblk
    qi, kj = _diff_pairs(nq)
    grid_spec = pltpu.PrefetchScalarGridSpec(
        num_scalar_prefetch=2,
        grid=(batch, qi.shape[0]),
        in_specs=[
            pl.BlockSpec(memory_space=pltpu.SMEM),
            pl.BlockSpec((blk, w), lambda b, s, qi, kj: (b * nq + qi[s], 0)),
            pl.BlockSpec((kblk, w), lambda b, s, qi, kj: (b * nk + kj[s], 1)),
            pl.BlockSpec((kblk, w), lambda b, s, qi, kj: (b * nk + kj[s], 2)),
            pl.BlockSpec((kblk, LANES), lambda b, s, qi, kj: (0, 0)),
            pl.BlockSpec(bias.shape, lambda b, s, qi, kj: (0, 0, 0, 0)),
            pl.BlockSpec((1, w), lambda b, s, qi, kj: (0, 0)),
        ],
        out_specs=pl.BlockSpec((blk, w), lambda b, s, qi, kj: (b * nq + qi[s], 0)),
        scratch_shapes=[
            pltpu.VMEM((N_HEADS, 2 * blk, LANES), BF16),
            pltpu.VMEM((2, 2 * blk, kblk), BF16),
            pltpu.VMEM((2, 2 * blk, LANES), F32),
            pltpu.VMEM((N_HEADS, 2 * blk, LANES), F32),
            pltpu.VMEM((N_HEADS, 2 * blk, 2 * LANES), F32),
        ],
    )
    return pl.pallas_call(
        functools.partial(_diff_kernel, out_scale=out_scale),
        grid_spec=grid_spec,
        out_shape=jax.ShapeDtypeStruct((t, w), BF16),
        compiler_params=_params("arbitrary", "arbitrary"),
        name="diff_attention",
    )(qi, kj, lam, qkv, qkv, qkv, _ones_column(kblk), bias, gain)


def _rel_bucket(rel):
    n_half = REL_BUCKETS // 2
    max_exact = n_half // 2
    ret = jnp.where(rel > 0, n_half, 0)
    n = jnp.abs(rel)
    nf = jnp.maximum(n, 1).astype(F32)
    large = max_exact + (jnp.log(nf / max_exact) / math.log(REL_MAX_DIST / max_exact)
                         * (n_half - max_exact)).astype(I32)
    large = jnp.minimum(large, n_half - 1)
    return ret + jnp.where(n < max_exact, n, large)


def _rel_bias_kernel(table_ref, idx_ref, o_ref):
    h = pl.program_id(0)
    idx = idx_ref[0]
    far = table_ref[REL_BUCKETS // 2 - 1, h]
    out = jnp.full(idx.shape, NEG, F32)
    for b in range(REL_BUCKETS):
        out = jnp.where(idx == b, table_ref[b, h] - far, out)
    o_ref[0, 0] = out


def _diff_bias_blocks(rel_table, blk):
    assert blk >= REL_MAX_DIST
    r = jnp.arange(blk)[:, None]
    c = jnp.arange(blk)[None, :]
    diag = jnp.where((c // CHUNK) <= (r // CHUNK), _rel_bucket(c - r), -1)
    prev = _rel_bucket(c - r - blk)
    idx = jnp.stack([diag, prev], axis=0).astype(I32)
    n_heads = rel_table.shape[1]
    return pl.pallas_call(
        _rel_bias_kernel,
        grid=(n_heads, 2),
        in_specs=[
            pl.BlockSpec(memory_space=pltpu.SMEM),
            pl.BlockSpec((1, blk, blk), lambda h, s: (s, 0, 0)),
        ],
        out_specs=pl.BlockSpec((1, 1, blk, blk), lambda h, s: (h, s, 0, 0)),
        out_shape=jax.ShapeDtypeStruct((n_heads, 2, blk, blk), F32),
        compiler_params=_params("arbitrary", "arbitrary"),
        name="rel_bias",
    )(rel_table.astype(F32), idx)


def _odd_in_kernel(x_ref, g_ref, wqk_ref, w_ref, cw_ref, qkv_ref, yd_ref, zbuf_ref, *,
                   tiles_per_batch):
    i = pl.program_id(0)
    tm = x_ref.shape[0]
    qw = qkv_ref.shape[1]
    dw = yd_ref.shape[1]
    qkw = wqk_ref.shape[1]
    xn = _rmsnorm(x_ref[...], g_ref[...]).astype(BF16)
    qkv_ref[:, 0:qkw] = _dot(xn, wqk_ref[...]).astype(BF16)
    qkv_ref[:, qkw:qw] = _dot(xn, w_ref[:, qkw:qw].astype(BF16)).astype(BF16)
    hh = _dot(xn, w_ref[:, qw:qw + dw].astype(BF16))
    gb = _dot(xn, w_ref[:, qw + dw:qw + 2 * dw].astype(BF16))
    gc = _dot(xn, w_ref[:, qw + 2 * dw:qw + 3 * dw].astype(BF16))
    z = gc * hh

    @pl.when(i % tiles_per_batch == 0)
    def _():
        zbuf_ref[0:8, :] = jnp.zeros((8, dw), F32)

    @pl.when(i % tiles_per_batch != 0)
    def _():
        zbuf_ref[0:8, :] = zbuf_ref[tm:tm + 8, :]

    zbuf_ref[8:tm + 8, :] = z
    y = (cw_ref[0:1, :] * zbuf_ref[6:tm + 6, :] + cw_ref[1:2, :] * zbuf_ref[7:tm + 7, :]
         + cw_ref[2:3, :] * z)
    yd_ref[...] = (gb * y).astype(BF16)


def _odd_in(x, g, wqk, w_stacked, cw, *, seq, qw):
    t, d = x.shape
    tm = min(TOK_TILE, seq)
    dw = cw.shape[1]
    return pl.pallas_call(
        functools.partial(_odd_in_kernel, tiles_per_batch=seq // tm),
        grid=(t // tm,),
        in_specs=[
            pl.BlockSpec((tm, d), lambda i: (i, 0)),
            _const_spec((1, d)),
            _const_spec(wqk.shape),
            _layer_spec(w_stacked, 0),
            _const_spec(cw.shape),
        ],
        out_specs=[
            pl.BlockSpec((tm, qw), lambda i: (i, 0)),
            pl.BlockSpec((tm, dw), lambda i: (i, 0)),
        ],
        out_shape=[
            jax.ShapeDtypeStruct((t, qw), BF16),
            jax.ShapeDtypeStruct((t, dw), BF16),
        ],
        scratch_shapes=[pltpu.VMEM((tm + 8, dw), F32)],
        compiler_params=_params("arbitrary"),
        name="odd_in",
    )(x, g, wqk, w_stacked, cw)


def _mem_kv_kernel(mem_ref, g_ref, w_ref, k_ref, v_ref):
    d = mem_ref.shape[2]
    mn = _rmsnorm(mem_ref[0], g_ref[...]).astype(BF16)
    k_ref[0] = _dot(mn, w_ref[:, 0:d].astype(BF16)).astype(BF16)
    v_ref[0] = _dot(mn, w_ref[:, d:2 * d].astype(BF16)).astype(BF16)


def _mem_kv(mem, g, w_stacked, layer):
    b, m, d = mem.shape
    return pl.pallas_call(
        _mem_kv_kernel,
        grid=(b,),
        in_specs=[
            pl.BlockSpec((1, m, d), lambda i: (i, 0, 0)),
            _const_spec((1, d)),
            _layer_spec(w_stacked, layer),
        ],
        out_specs=[pl.BlockSpec((1, m, d), lambda i: (i, 0, 0))] * 2,
        out_shape=[jax.ShapeDtypeStruct((b, m, d), BF16)] * 2,
        compiler_params=_params("arbitrary"),
        name="mem_kv",
    )(mem, g, w_stacked)


def _post_mixer_kernel(x_ref, ya_ref, yb_ref, wout_ref, g_ref, wq_ref, k_ref, v_ref, wo_ref, o_ref):
    wa = ya_ref.shape[1]
    d = x_ref.shape[1]
    hd = d // N_HEADS
    x1 = (x_ref[...] + _dot(ya_ref[...], wout_ref[0:wa, :].astype(BF16))
          + _dot(yb_ref[...], wout_ref[wa:, :].astype(BF16)))
    xn = _rmsnorm(x1, g_ref[...]).astype(BF16)
    q = (_dot(xn, wq_ref[...].astype(BF16)) * (float(hd) ** -0.5)).astype(BF16)
    heads = []
    for h in range(N_HEADS):
        lg = _dot_nt(q[:, h * hd:(h + 1) * hd], k_ref[0, :, h * hd:(h + 1) * hd])
        p = jnp.exp(lg - jnp.max(lg, axis=-1, keepdims=True))
        p = p / jnp.sum(p, axis=-1, keepdims=True)
        heads.append(_dot(p.astype(BF16), v_ref[0, :, h * hd:(h + 1) * hd]).astype(BF16))
    o_ref[...] = x1 + _dot(jnp.concatenate(heads, axis=-1), wo_ref[...].astype(BF16))


def _post_mixer(x, ya, yb, wout_stacked, g, wq_stacked, kmem, vmem, wo_stacked, layer, *, seq):
    t, d = x.shape
    tm = min(TOK_TILE, seq)
    nt = seq // tm
    m = kmem.shape[1]
    wa = ya.shape[1]
    return pl.pallas_call(
        _post_mixer_kernel,
        grid=(t // tm,),
        in_specs=[
            pl.BlockSpec((tm, d), lambda i: (i, 0)),
            pl.BlockSpec((tm, wa), lambda i: (i, 0)),
            pl.BlockSpec((tm, yb.shape[1]), lambda i: (i, 0)),
            _layer_spec(wout_stacked, 0),
            _const_spec((1, d)),
            _layer_spec(wq_stacked, layer),
            pl.BlockSpec((1, m, d), lambda i: (i // nt, 0, 0)),
            pl.BlockSpec((1, m, d), lambda i: (i // nt, 0, 0)),
            _layer_spec(wo_stacked, layer),
        ],
        out_specs=pl.BlockSpec((tm, d), lambda i: (i, 0)),
        out_shape=jax.ShapeDtypeStruct((t, d), F32),
        compiler_params=_params("arbitrary"),
        name="post_mixer",
    )(x, ya, yb, wout_stacked, g, wq_stacked, kmem, vmem, wo_stacked)


def _ffn_kernel(x_ref, g_ref, wg_ref, wu_ref, wd_ref, o_ref):
    x = x_ref[...]
    h = _rmsnorm(x, g_ref[...]).astype(BF16)
    ff = wg_ref.shape[1]
    acc = x
    for c0 in range(0, ff, FFN_CHUNK):
        c1 = min(c0 + FFN_CHUNK, ff)
        a = _dot(h, wg_ref[:, c0:c1].astype(BF16))
        u = _dot(h, wu_ref[:, c0:c1].astype(BF16))
        acc = acc + _dot((jax.nn.silu(a) * u).astype(BF16), wd_ref[c0:c1, :].astype(BF16))
    o_ref[...] = acc


def _ffn(x, g, wg, wu, wd, *, seq):
    t, d = x.shape
    tm = min(TOK_TILE, seq)
    return pl.pallas_call(
        _ffn_kernel,
        grid=(t // tm,),
        in_specs=[
            pl.BlockSpec((tm, d), lambda i: (i, 0)),
            _const_spec((1, d)),
            _const_spec(wg.shape),
            _const_spec(wu.shape),
            _const_spec(wd.shape),
        ],
        out_specs=pl.BlockSpec((tm, d), lambda i: (i, 0)),
        out_shape=jax.ShapeDtypeStruct((t, d), F32),
        compiler_params=_params("arbitrary"),
        name="ffn",
    )(x, g, wg, wu, wd)


META_I1, META_I2, META_R1, META_R2, META_G1, META_G2 = range(6)


def _router_kernel(x_ref, g_ref, wr_ref, meta_ref, metat_ref, cnt_ref, carry_ref):
    i = pl.program_id(0)
    tm = x_ref.shape[0]
    h = _rmsnorm(x_ref[...], g_ref[...])
    hp = _split3(h)
    wp = _split3(wr_ref[...])
    logits = _dot(hp[1], wp[0]) + _dot(hp[0], wp[1]) + _dot(hp[0], wp[0])
    lane = lax.broadcasted_iota(I32, logits.shape, 1)
    logits = jnp.where(lane < N_EXPERTS, logits, NEG)
    m1 = jnp.max(logits, axis=-1, keepdims=True)
    i1 = jnp.min(jnp.where(logits == m1, lane, LANES), axis=-1, keepdims=True)
    rest = jnp.where(lane == i1, NEG, logits)
    m2 = jnp.max(rest, axis=-1, keepdims=True)
    i2 = jnp.min(jnp.where(rest == m2, lane, LANES), axis=-1, keepdims=True)
    e = jnp.exp(m2 - m1)
    g1 = 1.0 / (1.0 + e)
    g2 = e / (1.0 + e)

    sel = jnp.where((lane == i1) | (lane == i2), 1.0, 0.0)
    row = lax.broadcasted_iota(I32, (tm, tm), 0)
    col = lax.broadcasted_iota(I32, (tm, tm), 1)
    strict = jnp.where(col < row, 1.0, 0.0).astype(BF16)

    @pl.when(i == 0)
    def _():
        carry_ref[...] = jnp.zeros_like(carry_ref)

    rank = _dot(strict, sel.astype(BF16)) + carry_ref[0:1, :]
    total = rank[tm - 1:tm, :] + sel[tm - 1:tm, :]
    carry_ref[...] = jnp.broadcast_to(total, carry_ref.shape)
    cnt_ref[...] = jnp.broadcast_to(total, cnt_ref.shape)
    r1 = jnp.sum(jnp.where(lane == i1, rank, 0.0), axis=-1, keepdims=True)
    r2 = jnp.sum(jnp.where(lane == i2, rank, 0.0), axis=-1, keepdims=True)
    meta = jnp.zeros_like(logits)
    for slot, val in ((META_I1, i1.astype(F32)), (META_I2, i2.astype(F32)), (META_R1, r1),
                      (META_R2, r2), (META_G1, g1), (META_G2, g2)):
        meta = jnp.where(lane == slot, val, meta)
    meta_ref[...] = meta
    metat_ref[...] = meta.T[0:8, :]


def _router(x, g, wr, *, seq):
    t, d = x.shape
    tm = min(TOK_TILE, seq)
    return pl.pallas_call(
        _router_kernel,
        grid=(t // tm,),
        in_specs=[
            pl.BlockSpec((tm, d), lambda i: (i, 0)),
            _const_spec((1, d)),
            _const_spec(wr.shape),
        ],
        out_specs=[
            pl.BlockSpec((tm, LANES), lambda i: (i, 0)),
            pl.BlockSpec((8, tm), lambda i: (0, i)),
            pl.BlockSpec((8, LANES), lambda i: (0, 0)),
        ],
        out_shape=[
            jax.ShapeDtypeStruct((t, LANES), F32),
            jax.ShapeDtypeStruct((8, t), F32),
            jax.ShapeDtypeStruct((8, LANES), F32),
        ],
        scratch_shapes=[pltpu.VMEM((8, LANES), F32)],
        compiler_params=_params("arbitrary"),
        name="router",
    )(x, g, wr)


def _row_copy(src, src_row, dst, dst_row, sem):
    return pltpu.make_async_copy(src.at[pl.ds(src_row, 1)], dst.at[pl.ds(dst_row, 1)], sem)


def _dispatch_kernel(pos_ref, pad_ref, x_ref, g_ref, xs_ref, h_ref, zrow_ref, sem_ref, zsem_ref):
    c = pl.program_id(0)
    n_chunks = pl.num_programs(0)
    tc = x_ref.shape[0]
    slot = c % 2

    @pl.when(c == 0)
    def _():
        zrow_ref[...] = jnp.zeros_like(zrow_ref)
        for e in range(N_EXPERTS):
            start = pad_ref[2 * e]
            count = pad_ref[2 * e + 1] - start

            def zissue(r, carry, start=start):
                _row_copy(zrow_ref, 0, xs_ref, start + r, zsem_ref).start()
                return carry

            def zwait(r, carry):
                _row_copy(zrow_ref, 0, xs_ref, 0, zsem_ref).wait()
                return carry

            lax.fori_loop(0, count, zissue, 0)
            lax.fori_loop(0, count, zwait, 0)

        tail = pad_ref[2 * N_EXPERTS - 1]
        groups = (xs_ref.shape[0] - tail) // 8

        def tissue(r, carry):
            start = pl.multiple_of(tail + 8 * r, 8)
            pltpu.make_async_copy(zrow_ref, xs_ref.at[pl.ds(start, 8)], zsem_ref).start()
            return carry

        def twait(r, carry):
            pltpu.make_async_copy(zrow_ref, xs_ref.at[pl.ds(0, 8)], zsem_ref).wait()
            return carry

        lax.fori_loop(0, groups, tissue, 0)
        lax.fori_loop(0, groups, twait, 0)

    h_ref[slot] = _rmsnorm(x_ref[...], g_ref[...])

    def issue(r, carry):
        tok = c * tc + r
        _row_copy(h_ref.at[slot], r, xs_ref, pos_ref[tok], sem_ref.at[slot]).start()
        _row_copy(h_ref.at[slot], r, xs_ref, pos_ref[n_chunks * tc + tok], sem_ref.at[slot]).start()
        return carry

    def drain(s):
        for _ in range(2):
            pltpu.make_async_copy(h_ref.at[s], xs_ref.at[pl.ds(0, tc)], sem_ref.at[s]).wait()

    lax.fori_loop(0, tc, issue, 0, unroll=DMA_UNROLL)

    @pl.when(c > 0)
    def _():
        drain(1 - slot)

    @pl.when(c == n_chunks - 1)
    def _():
        drain(slot)


def _dispatch(pos, pad, x, g, n_rows):
    t, d = x.shape
    tc = DISPATCH_CHUNK
    grid_spec = pltpu.PrefetchScalarGridSpec(
        num_scalar_prefetch=2,
        grid=(t // tc,),
        in_specs=[
            pl.BlockSpec((tc, d), lambda c, pos, pad: (c, 0)),
            pl.BlockSpec((1, d), lambda c, pos, pad: (0, 0)),
        ],
        out_specs=pl.BlockSpec(memory_space=pl.ANY),
        scratch_shapes=[
            pltpu.VMEM((2, tc, d), F32),
            pltpu.VMEM((8, d), F32),
            pltpu.SemaphoreType.DMA((2,)),
            pltpu.SemaphoreType.DMA(()),
        ],
    )
    return pl.pallas_call(
        _dispatch_kernel,
        grid_spec=grid_spec,
        out_shape=jax.ShapeDtypeStruct((n_rows, d), F32),
        compiler_params=_params("arbitrary"),
        name="moe_dispatch",
    )(pos, pad, x, g)


def _experts_kernel(te_ref, rows_ref, nused_ref, xs_ref, wg_ref, wu_ref, wd_ref, y_ref, xb_ref):
    p = pl.program_id(0)
    f = pl.program_id(1)
    rows = rows_ref[p]
    half = MOE_TILE // 2

    @pl.when(f == 0)
    def _():
        y_ref[...] = jnp.zeros_like(y_ref)

    @pl.when(rows > 0)
    def _():
        @pl.when(f == 0)
        def _():
            xb_ref[...] = xs_ref[...].astype(BF16)

        tf = wg_ref.shape[2]
        wgu = jnp.concatenate([wg_ref[0].astype(BF16), wu_ref[0].astype(BF16)], axis=1)
        wd = wd_ref[0].astype(BF16)

        def run(r0):
            gu = _dot(xb_ref[r0:r0 + half, :], wgu)
            act = (jax.nn.silu(gu[:, :tf]) * gu[:, tf:]).astype(BF16)
            y_ref[r0:r0 + half, :] += _dot(act, wd)

        run(0)

        @pl.when(rows > half)
        def _():
            run(half)


def _experts(tile_expert, tile_rows, n_used, xs, wg, wu, wd):
    n_rows, d = xs.shape
    ff = wg.shape[2]
    tm = MOE_TILE
    tf = MOE_FF_CHUNK
    n_tiles = n_rows // tm

    def x_map(p, f, te, tr, nu):
        return (jnp.minimum(p, nu[0] - 1), 0)

    grid_spec = pltpu.PrefetchScalarGridSpec(
        num_scalar_prefetch=3,
        grid=(n_tiles, ff // tf),
        in_specs=[
            pl.BlockSpec((tm, d), x_map),
            pl.BlockSpec((1, d, tf), lambda p, f, te, tr, nu: (te[p], 0, f)),
            pl.BlockSpec((1, d, tf), lambda p, f, te, tr, nu: (te[p], 0, f)),
            pl.BlockSpec((1, tf, d), lambda p, f, te, tr, nu: (te[p], f, 0)),
        ],
        out_specs=pl.BlockSpec((tm, d), lambda p, f, te, tr, nu: (p, 0)),
        scratch_shapes=[pltpu.VMEM((tm, d), BF16)],
    )
    return pl.pallas_call(
        _experts_kernel,
        grid_spec=grid_spec,
        out_shape=jax.ShapeDtypeStruct((n_rows, d), F32),
        compiler_params=_params("arbitrary", "arbitrary"),
        name="moe_experts",
    )(tile_expert, tile_rows, n_used, xs, wg, wu, wd)


def _combine_kernel(pos_ref, y_ref, x_ref, meta_ref, g_ref, o_ref, buf_ref, sem_ref):
    i = pl.program_id(0)
    n = pl.num_programs(0)
    tc = x_ref.shape[0]

    def fetch(tile, slot):
        def body(r, carry):
            tok = tile * tc + r
            for k in range(2):
                pltpu.make_async_copy(y_ref.at[pl.ds(pos_ref[k * n * tc + tok], 1)],
                                      buf_ref.at[slot, k, pl.ds(r, 1)], sem_ref.at[slot]).start()
            return carry
        lax.fori_loop(0, tc, body, 0, unroll=DMA_UNROLL)

    @pl.when(i == 0)
    def _():
        fetch(0, 0)

    @pl.when(i + 1 < n)
    def _():
        fetch(i + 1, (i + 1) % 2)

    slot = i % 2

    for k in range(2):
        pltpu.make_async_copy(y_ref.at[pl.ds(0, tc)], buf_ref.at[slot, k], sem_ref.at[slot]).wait()
    meta = meta_ref[...]
    g1 = meta[:, META_G1:META_G1 + 1]
    g2 = meta[:, META_G2:META_G2 + 1]
    x = x_ref[...] + g1 * buf_ref[slot, 0] + g2 * buf_ref[slot, 1]
    o_ref[...] = _rmsnorm(x, g_ref[...])


def _combine(pos, y, x, meta, g):
    t, d = x.shape
    tc = COMBINE_TILE
    grid_spec = pltpu.PrefetchScalarGridSpec(
        num_scalar_prefetch=1,
        grid=(t // tc,),
        in_specs=[
            pl.BlockSpec(memory_space=pl.ANY),
            pl.BlockSpec((tc, d), lambda i, pos: (i, 0)),
            pl.BlockSpec((tc, LANES), lambda i, pos: (i, 0)),
            pl.BlockSpec((1, d), lambda i, pos: (0, 0)),
        ],
        out_specs=pl.BlockSpec((tc, d), lambda i, pos: (i, 0)),
        scratch_shapes=[pltpu.VMEM((2, 2, tc, d), F32), pltpu.SemaphoreType.DMA((2,))],
    )
    return pl.pallas_call(
        _combine_kernel,
        grid_spec=grid_spec,
        out_shape=jax.ShapeDtypeStruct((t, d), F32),
        compiler_params=_params("arbitrary"),
        name="moe_combine",
    )(pos, y, x, meta, g)


def _moe_plan(meta_t, counts, n_tiles):
    cnt = counts[0, :N_EXPERTS].astype(I32)
    tiles = (cnt + MOE_TILE - 1) // MOE_TILE
    tile_end = jnp.cumsum(tiles)
    row_off = (tile_end - tiles) * MOE_TILE
    n_used = tile_end[-1:]
    experts = jnp.arange(N_EXPERTS, dtype=I32)

    def rows(idx_row, rank_row):
        idx = meta_t[idx_row].astype(I32)
        off = jnp.sum(jnp.where(idx[None, :] == experts[:, None], row_off[:, None], 0), axis=0)
        return off + meta_t[rank_row].astype(I32)

    pos = jnp.concatenate([rows(META_I1, META_R1), rows(META_I2, META_R2)])
    tile_ids = jnp.arange(n_tiles, dtype=I32)
    te = jnp.sum((tile_end[None, :] <= tile_ids[:, None]).astype(I32), axis=1)
    mine = te[:, None] == experts[None, :]
    first_row = tile_ids * MOE_TILE - jnp.sum(jnp.where(mine, row_off[None, :], 0), axis=1)
    tile_rows = jnp.clip(jnp.sum(jnp.where(mine, cnt[None, :], 0), axis=1) - first_row, 0, MOE_TILE)
    last = jnp.sum((tile_end <= n_used - 1).astype(I32))
    te = jnp.minimum(te, last).astype(I32)
    pad = jnp.stack([row_off + cnt, row_off + tiles * MOE_TILE], axis=1).reshape(-1).astype(I32)
    return pos.astype(I32), pad, te, tile_rows.astype(I32), n_used.astype(I32)


def _moe_and_final_norm(x, g_ffn, wr, wg, wu, wd, g_final, *, seq):
    t, d = x.shape
    n_tiles = 2 * t // MOE_TILE + N_EXPERTS
    meta, meta_t, counts = _router(x, g_ffn, wr, seq=seq)
    pos, pad, te, tile_rows, n_used = _moe_plan(meta_t, counts, n_tiles)
    xs = _dispatch(pos, pad, x, g_ffn, n_tiles * MOE_TILE)
    y = _experts(te, tile_rows, n_used, xs, wg, wu, wd)
    return _combine(pos, y, x, meta, g_final)


def _row(v):
    return v.reshape(1, -1).astype(F32)


def kernel(x, mem, norm_mix, norm_mem_q, norm_mem_kv, norm_ffn, norm_final, even_w_in, fox_b_f, gmlp_v_gain, gmlp_w_s, gmlp_b_s, even_w_out, odd_w_in, diff_lambda_q1, diff_lambda_k1, diff_lambda_q2, diff_lambda_k2, diff_subln_gain, conv_w, odd_w_out, rel_bias, mem_w_q, mem_w_kv, mem_w_o, ffn_w_gate, ffn_w_up, ffn_w_down, router_w, moe_w_gate, moe_w_up, moe_w_down):
    batch, seq, d = x.shape
    t = batch * seq
    depth = norm_mix.shape[0]
    assert depth == 2 and seq % (2 * ATT_BLOCK) == 0 and seq % TOK_TILE == 0
    xf = x.reshape(t, d)
    hd = d // N_HEADS
    aw = gmlp_v_gain.shape[1]
    n_blk = gmlp_w_s.shape[2]

    gate_cols = even_w_in.shape[2] - N_HEADS
    wf = jnp.pad(even_w_in[0, :, gate_cols:], ((0, 0), (0, LANES - N_HEADS)))
    bf = jnp.pad(fox_b_f[0], (0, LANES - N_HEADS)).reshape(1, LANES)
    bs = jnp.broadcast_to(gmlp_b_s[0][:, :, None], (A_GROUPS, n_blk, aw // A_GROUPS)).astype(F32)
    ya, qkv, aug = _even_in(xf, _row(norm_mix[0]), even_w_in, wf, bf, _row(gmlp_v_gain[0]),
                            gmlp_w_s[0], bs, seq=seq)
    yb = _fox_attention(qkv, aug, batch=batch, seq=seq)

    def mem_attention(layer, x_in, ya_, yb_, w_out):
        km, vm = _mem_kv(mem, _row(norm_mem_kv[layer]), mem_w_kv, layer)
        return _post_mixer(x_in, ya_, yb_, w_out, _row(norm_mem_q[layer]), mem_w_q, km, vm,
                           mem_w_o, layer, seq=seq)

    xf = mem_attention(0, xf, ya, yb, even_w_out)
    xf = _ffn(xf, _row(norm_ffn[0]), ffn_w_gate[0], ffn_w_up[0], ffn_w_down[0], seq=seq)

    qk_dim = LANES // 2
    qk_w = 2 * N_HEADS * qk_dim

    def regroup(cols):
        return cols.reshape(d, 2, N_HEADS, qk_dim).transpose(0, 2, 1, 3).reshape(d, qk_w)

    wqk = jnp.concatenate([regroup(odd_w_in[0, :, :qk_w]) * float(qk_dim) ** -0.5,
                           regroup(odd_w_in[0, :, qk_w:2 * qk_w])], axis=1).astype(BF16)
    qkv, yd = _odd_in(xf, _row(norm_mix[1]), wqk, odd_w_in, conv_w[0].astype(F32), seq=seq,
                      qw=2 * qk_w + N_HEADS * LANES)
    lam_init = 0.8 - 0.6 * math.exp(-0.3 * 1)
    lam = (jnp.exp(jnp.sum(diff_lambda_q1[0] * diff_lambda_k1[0]))
           - jnp.exp(jnp.sum(diff_lambda_q2[0] * diff_lambda_k2[0])) + lam_init)
    bias = _diff_bias_blocks(rel_bias, min(ATT_BLOCK, seq))
    yc = _diff_attention(qkv, lam.reshape(1, 1).astype(F32), bias, _row(diff_subln_gain[0]),
                         batch=batch, seq=seq, out_scale=1.0 - lam_init)
    xf = mem_attention(1, xf, yc, yd, odd_w_out)
    wr = jnp.pad(router_w[0], ((0, 0), (0, LANES - N_EXPERTS))).astype(F32)
    out = _moe_and_final_norm(xf, _row(norm_ffn[1]), wr, moe_w_gate[0], moe_w_up[0], moe_w_down[0],
                              _row(norm_final), seq=seq)
    return out.reshape(batch, seq, d)
```

```python
import functools
import math

import numpy as np
import jax
import jax.numpy as jnp
from jax import lax
from jax.experimental import pallas as pl
from jax.experimental.pallas import tpu as pltpu

F32 = jnp.float32
BF16 = jnp.bfloat16
I32 = jnp.int32

EPS = 1e-6
NEG = -1e30
LANES = 128
VMEM_LIMIT = 56 * 1024 * 1024

CHUNK = 64
A_GROUPS = 4
N_HEADS = 4
N_EXPERTS = 8
REL_BUCKETS = 32
REL_MAX_DIST = 128

TOK_TILE = 1024
TOK_TILE_SMALL = 512
FOX_BLOCK = 1024
ATT_BLOCK = 512
ATT_STRIP = 64
MOE_TILE = 1024
MOE_FF_CHUNK = 896
FFN_CHUNK = 512
DISPATCH_CHUNK = 256
COMBINE_TILE = 256
DMA_UNROLL = 8


def _dot(a, b):
    return jnp.dot(a, b, preferred_element_type=F32)


def _dot_nt(a, b):
    return lax.dot_general(a, b, (((1,), (1,)), ((), ())), preferred_element_type=F32)


def _rmsnorm(x, g):
    return x * lax.rsqrt(jnp.mean(x * x, axis=-1, keepdims=True) + EPS) * g


def _params(*sem):
    return pltpu.CompilerParams(dimension_semantics=sem, vmem_limit_bytes=VMEM_LIMIT)


def _const_spec(shape):
    return pl.BlockSpec(shape, lambda *_: (0,) * len(shape), pipeline_mode=pl.Buffered(1))


def _layer_spec(stacked, layer):
    rest = stacked.shape[1:]
    return pl.BlockSpec((None,) + rest, lambda *_: (layer,) + (0,) * len(rest),
                        pipeline_mode=pl.Buffered(1))


def _split3(v):
    a1 = v.astype(BF16)
    r1 = v - a1.astype(F32)
    a2 = r1.astype(BF16)
    a3 = (r1 - a2.astype(F32)).astype(BF16)
    return a1, a2, a3


def _even_in_kernel(x_ref, g_ref, w_ref, wf_ref, bf_ref, vg_ref, ws_ref, bs_ref,
                    ya_ref, qkv_ref, aug_ref, carry_ref, *, tiles_per_batch):
    i = pl.program_id(0)
    tm = x_ref.shape[0]
    aw = ya_ref.shape[1]
    qw = qkv_ref.shape[1]

    @pl.when(i % tiles_per_batch == 0)
    def _():
        carry_ref[...] = jnp.zeros_like(carry_ref)

    xn = _rmsnorm(x_ref[...], g_ref[...]).astype(BF16)

    hw = N_HEADS * LANES
    q = _dot(xn, w_ref[:, 2 * aw:2 * aw + hw].astype(BF16)) * (float(LANES) ** -0.5)
    qkv_ref[:, 0:hw] = q.astype(BF16)
    qkv_ref[:, hw:qw] = _dot(xn, w_ref[:, 2 * aw + hw:2 * aw + qw].astype(BF16)).astype(BF16)

    fl = _dot(xn, wf_ref[...].astype(BF16)) + bf_ref[...]
    ls = jnp.minimum(fl, 0.0) - jnp.log1p(jnp.exp(-jnp.abs(fl)))
    row = lax.broadcasted_iota(I32, (tm, tm), 0)
    col = lax.broadcasted_iota(I32, (tm, tm), 1)
    tri = jnp.where(col <= row, 1.0, 0.0).astype(BF16)
    a1, a2, a3 = _split3(ls)
    csum = _dot(tri, a1) + _dot(tri, a2) + _dot(tri, a3)
    csum = csum + carry_ref[0:1, :]
    carry_ref[...] = jnp.broadcast_to(csum[tm - 1:tm, :], carry_ref.shape)

    lane = lax.broadcasted_iota(I32, (tm, LANES), 1)
    for h in range(N_HEADS):
        c1, c2, c3 = (piece.astype(F32)
                      for piece in _split3(jnp.broadcast_to(csum[:, h:h + 1], (tm, LANES))))
        aq = jnp.where(lane == 0, c1, jnp.where(lane == 1, c2, jnp.where(lane == 2, c3,
                       jnp.where(lane < 6, 1.0, 0.0))))
        ak = jnp.where(lane == 3, -c1, jnp.where(lane == 4, -c2, jnp.where(lane == 5, -c3,
                       jnp.where(lane < 3, 1.0, 0.0))))
        aug_ref[:, h * LANES:(h + 1) * LANES] = aq.astype(BF16)
        aug_ref[:, (N_HEADS + h) * LANES:(N_HEADS + h + 1) * LANES] = ak.astype(BF16)

    gu = jax.nn.gelu(_dot(xn, w_ref[:, 0:aw].astype(BF16)), approximate=True)
    gv = jax.nn.gelu(_dot(xn, w_ref[:, aw:2 * aw].astype(BF16)), approximate=True)
    blk = ws_ref.shape[1]
    ch = aw // A_GROUPS
    r = lax.broadcasted_iota(I32, (blk, blk), 0)
    c = lax.broadcasted_iota(I32, (blk, blk), 1)
    causal = (c // CHUNK) <= (r // CHUNK)
    for g in range(A_GROUPS):
        vgrp = gv[:, g * ch:(g + 1) * ch]
        vn = _rmsnorm(vgrp, vg_ref[:, g * ch:(g + 1) * ch]).astype(BF16)
        wmix = jnp.where(causal, ws_ref[g], 0.0).astype(BF16)
        for n in range(tm // blk):
            mixed = _dot(wmix, vn[n * blk:(n + 1) * blk, :]) + bs_ref[g]
            ya_ref[n * blk:(n + 1) * blk, g * ch:(g + 1) * ch] = (
                gu[n * blk:(n + 1) * blk, g * ch:(g + 1) * ch] * mixed).astype(BF16)


def _even_in(x, g, w_stacked, wf, bf, vgain, ws, bs, *, seq):
    t, d = x.shape
    tm = min(TOK_TILE_SMALL, seq)
    aw = vgain.shape[1]
    qw = 3 * N_HEADS * LANES
    blk = ws.shape[1]
    return pl.pallas_call(
        functools.partial(_even_in_kernel, tiles_per_batch=seq // tm),
        grid=(t // tm,),
        in_specs=[
            pl.BlockSpec((tm, d), lambda i: (i, 0)),
            _const_spec((1, d)),
            _layer_spec(w_stacked, 0),
            _const_spec(wf.shape),
            _const_spec((1, LANES)),
            _const_spec((1, aw)),
            _const_spec(ws.shape),
            _const_spec(bs.shape),
        ],
        out_specs=[
            pl.BlockSpec((tm, aw), lambda i: (i, 0)),
            pl.BlockSpec((tm, qw), lambda i: (i, 0)),
            pl.BlockSpec((tm, 2 * N_HEADS * LANES), lambda i: (i, 0)),
        ],
        out_shape=[
            jax.ShapeDtypeStruct((t, aw), BF16),
            jax.ShapeDtypeStruct((t, qw), BF16),
            jax.ShapeDtypeStruct((t, 2 * N_HEADS * LANES), BF16),
        ],
        scratch_shapes=[pltpu.VMEM((8, LANES), F32)],
        compiler_params=_params("arbitrary"),
        name="even_in",
    )(x, g, w_stacked, wf, bf, vgain, ws, bs)


def _tri_pairs(nblk):
    qi = np.array([i for i in range(nblk) for _ in range(i + 1)], np.int32)
    kj = np.array([j for i in range(nblk) for j in range(i + 1)], np.int32)
    return jnp.asarray(qi), jnp.asarray(kj)


def _softmax_strips(logits, p_ref, alpha_ref, m_ref, h, add_bias=None):
    rows, tk = logits.shape
    for r0 in range(0, rows, ATT_STRIP):
        rs = slice(r0, r0 + ATT_STRIP)
        s = logits[rs, :]
        if add_bias is not None:
            s = add_bias(s, r0)
        m_prev = m_ref[h, rs, :]
        m_new = jnp.maximum(m_prev, jnp.max(s, axis=-1, keepdims=True))
        alpha_ref[rs, :] = jnp.exp(m_prev - m_new)
        m_ref[h, rs, :] = m_new
        p_ref[rs, 0:tk] = jnp.exp(
            (s - jnp.concatenate([m_new] * (tk // LANES), axis=-1)).astype(BF16))


def _accumulate(p_ref, alpha_ref, v, ones, acc_ref, h):
    alpha = alpha_ref[...]
    va = jnp.concatenate([v, ones], axis=-1)
    acc_ref[h] = (jnp.concatenate([alpha, alpha], axis=-1) * acc_ref[h]
                  + _dot(p_ref[:, 0:v.shape[0]], va))


def _fox_kernel(qi_ref, kj_ref, q_ref, aq_ref, k_ref, ak_ref, v_ref, ones_ref, mask_ref, o_ref,
                p_ref, alpha_ref, m_ref, acc_ref):
    step = pl.program_id(1)
    i = qi_ref[step]
    j = kj_ref[step]

    @pl.when(j == 0)
    def _():
        m_ref[...] = jnp.full_like(m_ref, NEG)
        acc_ref[...] = jnp.zeros_like(acc_ref)

    def logits(h):
        hs = slice(h * LANES, (h + 1) * LANES)
        qa = jnp.concatenate([q_ref[:, hs], aq_ref[:, hs]], axis=-1)
        ka = jnp.concatenate([k_ref[:, hs], ak_ref[:, hs]], axis=-1)
        return _dot_nt(qa, ka)

    def sweep(masked):
        nxt = logits(0)
        for h in range(N_HEADS):
            hs = slice(h * LANES, (h + 1) * LANES)
            sb = h % 2
            cur = nxt
            if h + 1 < N_HEADS:
                nxt = logits(h + 1)
            add_mask = (lambda s, r0: s + mask_ref[r0:r0 + ATT_STRIP, :]) if masked else None
            _softmax_strips(cur, p_ref.at[sb], alpha_ref.at[sb], m_ref, h, add_mask)
            _accumulate(p_ref.at[sb], alpha_ref.at[sb], v_ref[:, hs], ones_ref[...], acc_ref, h)

    @pl.when(j < i)
    def _():
        sweep(False)

    @pl.when(j == i)
    def _():
        sweep(True)
        for h in range(N_HEADS):
            acc = acc_ref[h]
            o_ref[:, h * LANES:(h + 1) * LANES] = (
                acc[:, :LANES] / acc[:, LANES:LANES + 1]).astype(o_ref.dtype)


def _ones_column(rows):
    ones = np.zeros((rows, LANES), np.float32)
    ones[:, 0] = 1.0
    return jnp.asarray(ones, BF16)


def _fox_attention(qkv, aug, *, batch, seq):
    t = qkv.shape[0]
    w = N_HEADS * LANES
    blk = min(FOX_BLOCK, seq)
    nblk = seq // blk
    qi, kj = _tri_pairs(nblk)
    r = np.arange(blk)
    mask = jnp.asarray(np.where(r[None, :] <= r[:, None], 0.0, NEG), F32)
    grid_spec = pltpu.PrefetchScalarGridSpec(
        num_scalar_prefetch=2,
        grid=(batch, qi.shape[0]),
        in_specs=[
            pl.BlockSpec((blk, w), lambda b, s, qi, kj: (b * nblk + qi[s], 0)),
            pl.BlockSpec((blk, w), lambda b, s, qi, kj: (b * nblk + qi[s], 0)),
            pl.BlockSpec((blk, w), lambda b, s, qi, kj: (b * nblk + kj[s], 1)),
            pl.BlockSpec((blk, w), lambda b, s, qi, kj: (b * nblk + kj[s], 1)),
            pl.BlockSpec((blk, w), lambda b, s, qi, kj: (b * nblk + kj[s], 2)),
            pl.BlockSpec((blk, LANES), lambda b, s, qi, kj: (0, 0)),
            pl.BlockSpec((blk, blk), lambda b, s, qi, kj: (0, 0)),
        ],
        out_specs=pl.BlockSpec((blk, w), lambda b, s, qi, kj: (b * nblk + qi[s], 0)),
        scratch_shapes=[
            pltpu.VMEM((2, blk, blk), BF16),
            pltpu.VMEM((2, blk, LANES), F32),
            pltpu.VMEM((N_HEADS, blk, LANES), F32),
            pltpu.VMEM((N_HEADS, blk, 2 * LANES), F32),
        ],
    )
    return pl.pallas_call(
        _fox_kernel,
        grid_spec=grid_spec,
        out_shape=jax.ShapeDtypeStruct((t, w), BF16),
        compiler_params=_params("arbitrary", "arbitrary"),
        name="fox_attention",
    )(qi, kj, qkv, aug, qkv, aug, qkv, _ones_column(blk), mask)


def _diff_pairs(nq):
    qi = np.array([i for i in range(nq) for _ in range(i // 2 + 1)], np.int32)
    kj = np.array([j for i in range(nq) for j in range(i // 2 + 1)], np.int32)
    return jnp.asarray(qi), jnp.asarray(kj)


def _diff_kernel(qi_ref, kj_ref, lam_ref, q_ref, k_ref, v_ref, ones_ref, bias_ref, gain_ref, o_ref,
                 qs_ref, p_ref, alpha_ref, m_ref, acc_ref, *, out_scale):
    step = pl.program_id(1)
    i = qi_ref[step]
    j = kj_ref[step]
    tq = q_ref.shape[0]
    half = LANES // 2
    jd = i // 2
    odd = (i % 2) == 1

    @pl.when(j == 0)
    def _():
        m_ref[...] = jnp.full_like(m_ref, NEG)
        acc_ref[...] = jnp.zeros_like(acc_ref)
        lane = lax.broadcasted_iota(I32, (tq, LANES), 1)
        zero = jnp.zeros((tq, LANES), BF16)
        for h in range(N_HEADS):
            q = q_ref[:, h * LANES:(h + 1) * LANES]
            qs_ref[h, 0:tq, :] = jnp.where(lane < half, q, zero)
            qs_ref[h, tq:2 * tq, :] = jnp.where(lane < half, zero, q)

    def sweep(tk, add_bias):
        def logits(h):
            return _dot_nt(qs_ref[h], k_ref[0:tk, h * LANES:(h + 1) * LANES])

        nxt = logits(0)
        for h in range(N_HEADS):
            hs = slice(h * LANES, (h + 1) * LANES)
            sb = h % 2
            cur = nxt
            if h + 1 < N_HEADS:
                nxt = logits(h + 1)
            bias_fn = None if add_bias is None else functools.partial(add_bias, h)
            _softmax_strips(cur, p_ref.at[sb], alpha_ref.at[sb], m_ref, h, bias_fn)
            _accumulate(p_ref.at[sb], alpha_ref.at[sb], v_ref[0:tk, hs], ones_ref[0:tk, :], acc_ref, h)

    def bias_rows(h, which, r0):
        return bias_ref[h, which, pl.ds(r0 % tq, ATT_STRIP), :]

    def odd_diag(h, s, r0):
        return jnp.concatenate([s[:, :tq] + bias_rows(h, 1, r0), s[:, tq:] + bias_rows(h, 0, r0)],
                               axis=-1)

    def even_prev(h, s, r0):
        return jnp.concatenate([s[:, :tq], s[:, tq:] + bias_rows(h, 1, r0)], axis=-1)

    def even_diag(h, s, r0):
        return s + bias_rows(h, 0, r0)

    @pl.when((j < jd - 1) | ((j == jd - 1) & odd))
    def _():
        sweep(2 * tq, None)

    @pl.when((j == jd - 1) & jnp.logical_not(odd))
    def _():
        sweep(2 * tq, even_prev)

    @pl.when((j == jd) & odd)
    def _():
        sweep(2 * tq, odd_diag)

    @pl.when((j == jd) & jnp.logical_not(odd))
    def _():
        sweep(tq, even_diag)

    @pl.when(j == jd)
    def _():
        for h in range(N_HEADS):
            hs = slice(h * LANES, (h + 1) * LANES)
            acc = acc_ref[h]
            o = acc[:, :LANES] / acc[:, LANES:LANES + 1]
            o = o[0:tq, :] - lam_ref[0, 0] * o[tq:2 * tq, :]
            o_ref[:, hs] = (_rmsnorm(o, gain_ref[:, hs]) * out_scale).astype(o_ref.dtype)


def _diff_attention(qkv, lam, bias, gain, *, batch, seq, out_scale):
    t = qkv.shape[0]
    w = N_HEADS * LANES
    blk = bias.shape[2]
    kblk = 2 * blk
    nq = seq // blk
    nk = seq // kblk
    qi, kj = _diff_pairs(nq)
    grid_spec = pltpu.PrefetchScalarGridSpec(
        num_scalar_prefetch=2,
        grid=(batch, qi.shape[0]),
        in_specs=[
            pl.BlockSpec(memory_space=pltpu.SMEM),
            pl.BlockSpec((blk, w), lambda b, s, qi, kj: (b * nq + qi[s], 0)),
            pl.BlockSpec((kblk, w), lambda b, s, qi, kj: (b * nk + kj[s], 1)),
            pl.BlockSpec((kblk, w), lambda b, s, qi, kj: (b * nk + kj[s], 2)),
            pl.BlockSpec((kblk, LANES), lambda b, s, qi, kj: (0, 0)),
            pl.BlockSpec(bias.shape, lambda b, s, qi, kj: (0, 0, 0, 0)),
            pl.BlockSpec((1, w), lambda b, s, qi, kj: (0, 0)),
        ],
        out_specs=pl.BlockSpec((blk, w), lambda b, s, qi, kj: (b * nq + qi[s], 0)),
        scratch_shapes=[
            pltpu.VMEM((N_HEADS, 2 * blk, LANES), BF16),
            pltpu.VMEM((2, 2 * blk, kblk), BF16),
            pltpu.VMEM((2, 2 * blk, LANES), F32),
            pltpu.VMEM((N_HEADS, 2 * blk, LANES), F32),
            pltpu.VMEM((N_HEADS, 2 * blk, 2 * LANES), F32),
        ],
    )
    return pl.pallas_call(
        functools.partial(_diff_kernel, out_scale=out_scale),
        grid_spec=grid_spec,
        out_shape=jax.ShapeDtypeStruct((t, w), BF16),
        compiler_params=_params("arbitrary", "arbitrary"),
        name="diff_attention",
    )(qi, kj, lam, qkv, qkv, qkv, _ones_column(kblk), bias, gain)


def _rel_bucket(rel):
    n_half = REL_BUCKETS // 2
    max_exact = n_half // 2
    ret = jnp.where(rel > 0, n_half, 0)
    n = jnp.abs(rel)
    nf = jnp.maximum(n, 1).astype(F32)
    large = max_exact + (jnp.log(nf / max_exact) / math.log(REL_MAX_DIST / max_exact)
                         * (n_half - max_exact)).astype(I32)
    large = jnp.minimum(large, n_half - 1)
    return ret + jnp.where(n < max_exact, n, large)


def _rel_bias_kernel(table_ref, idx_ref, o_ref):
    h = pl.program_id(0)
    idx = idx_ref[0]
    far = table_ref[REL_BUCKETS // 2 - 1, h]
    out = jnp.full(idx.shape, NEG, F32)
    for b in range(REL_BUCKETS):
        out = jnp.where(idx == b, table_ref[b, h] - far, out)
    o_ref[0, 0] = out


def _diff_bias_blocks(rel_table, blk):
    assert blk >= REL_MAX_DIST
    r = jnp.arange(blk)[:, None]
    c = jnp.arange(blk)[None, :]
    diag = jnp.where((c // CHUNK) <= (r // CHUNK), _rel_bucket(c - r), -1)
    prev = _rel_bucket(c - r - blk)
    idx = jnp.stack([diag, prev], axis=0).astype(I32)
    n_heads = rel_table.shape[1]
    return pl.pallas_call(
        _rel_bias_kernel,
        grid=(n_heads, 2),
        in_specs=[
            pl.BlockSpec(memory_space=pltpu.SMEM),
            pl.BlockSpec((1, blk, blk), lambda h, s: (s, 0, 0)),
        ],
        out_specs=pl.BlockSpec((1, 1, blk, blk), lambda h, s: (h, s, 0, 0)),
        out_shape=jax.ShapeDtypeStruct((n_heads, 2, blk, blk), F32),
        compiler_params=_params("arbitrary", "arbitrary"),
        name="rel_bias",
    )(rel_table.astype(F32), idx)


def _odd_in_kernel(x_ref, g_ref, wqk_ref, w_ref, cw_ref, qkv_ref, yd_ref, zbuf_ref, *,
                   tiles_per_batch):
    i = pl.program_id(0)
    tm = x_ref.shape[0]
    qw = qkv_ref.shape[1]
    dw = yd_ref.shape[1]
    qkw = wqk_ref.shape[1]

    @pl.when(i % tiles_per_batch == 0)
    def _():
        zbuf_ref[tm:tm + 8, :] = jnp.zeros((8, dw), F32)

    xn = _rmsnorm(x_ref[...], g_ref[...]).astype(BF16)
    qkv_ref[:, 0:qkw] = _dot(xn, wqk_ref[...]).astype(BF16)
    qkv_ref[:, qkw:qw] = _dot(xn, w_ref[:, qkw:qw].astype(BF16)).astype(BF16)
    hh = _dot(xn, w_ref[:, qw:qw + dw].astype(BF16))
    gb = _dot(xn, w_ref[:, qw + dw:qw + 2 * dw].astype(BF16))
    gc = _dot(xn, w_ref[:, qw + 2 * dw:qw + 3 * dw].astype(BF16))
    z = gc * hh

    zbuf_ref[0:8, :] = zbuf_ref[tm:tm + 8, :]
    zbuf_ref[8:tm + 8, :] = z
    y = (cw_ref[0:1, :] * zbuf_ref[6:tm + 6, :] + cw_ref[1:2, :] * zbuf_ref[7:tm + 7, :]
         + cw_ref[2:3, :] * z)
    yd_ref[...] = (gb * y).astype(BF16)


def _odd_in(x, g, wqk, w_stacked, cw, *, seq, qw):
    t, d = x.shape
    tm = min(TOK_TILE, seq)
    dw = cw.shape[1]
    return pl.pallas_call(
        functools.partial(_odd_in_kernel, tiles_per_batch=seq // tm),
        grid=(t // tm,),
        in_specs=[
            pl.BlockSpec((tm, d), lambda i: (i, 0)),
            _const_spec((1, d)),
            _const_spec(wqk.shape),
            _layer_spec(w_stacked, 0),
            _const_spec(cw.shape),
        ],
        out_specs=[
            pl.BlockSpec((tm, qw), lambda i: (i, 0)),
            pl.BlockSpec((tm, dw), lambda i: (i, 0)),
        ],
        out_shape=[
            jax.ShapeDtypeStruct((t, qw), BF16),
            jax.ShapeDtypeStruct((t, dw), BF16),
        ],
        scratch_shapes=[pltpu.VMEM((tm + 8, dw), F32)],
        compiler_params=_params("arbitrary"),
        name="odd_in",
    )(x, g, wqk, w_stacked, cw)


def _mem_kv_kernel(mem_ref, g_ref, w_ref, k_ref, v_ref):
    d = mem_ref.shape[2]
    mn = _rmsnorm(mem_ref[0], g_ref[...]).astype(BF16)
    k_ref[0] = _dot(mn, w_ref[:, 0:d].astype(BF16)).astype(BF16)
    v_ref[0] = _dot(mn, w_ref[:, d:2 * d].astype(BF16)).astype(BF16)


def _mem_kv(mem, g, w_stacked, layer):
    b, m, d = mem.shape
    return pl.pallas_call(
        _mem_kv_kernel,
        grid=(b,),
        in_specs=[
            pl.BlockSpec((1, m, d), lambda i: (i, 0, 0)),
            _const_spec((1, d)),
            _layer_spec(w_stacked, layer),
        ],
        out_specs=[pl.BlockSpec((1, m, d), lambda i: (i, 0, 0))] * 2,
        out_shape=[jax.ShapeDtypeStruct((b, m, d), BF16)] * 2,
        compiler_params=_params("arbitrary"),
        name="mem_kv",
    )(mem, g, w_stacked)


def _post_mixer_kernel(x_ref, ya_ref, yb_ref, wout_ref, g_ref, wq_ref, k_ref, v_ref, wo_ref, o_ref):
    wa = ya_ref.shape[1]
    d = x_ref.shape[1]
    hd = d // N_HEADS
    x1 = (x_ref[...] + _dot(ya_ref[...], wout_ref[0:wa, :].astype(BF16))
          + _dot(yb_ref[...], wout_ref[wa:, :].astype(BF16)))
    xn = _rmsnorm(x1, g_ref[...]).astype(BF16)
    q = (_dot(xn, wq_ref[...].astype(BF16)) * (float(hd) ** -0.5)).astype(BF16)
    heads = []
    for h in range(N_HEADS):
        lg = _dot_nt(q[:, h * hd:(h + 1) * hd], k_ref[0, :, h * hd:(h + 1) * hd])
        p = jnp.exp(lg - jnp.max(lg, axis=-1, keepdims=True))
        p = p / jnp.sum(p, axis=-1, keepdims=True)
        heads.append(_dot(p.astype(BF16), v_ref[0, :, h * hd:(h + 1) * hd]).astype(BF16))
    o_ref[...] = x1 + _dot(jnp.concatenate(heads, axis=-1), wo_ref[...].astype(BF16))


def _post_mixer(x, ya, yb, wout_stacked, g, wq_stacked, kmem, vmem, wo_stacked, layer, *, seq):
    t, d = x.shape
    tm = min(TOK_TILE, seq)
    nt = seq // tm
    m = kmem.shape[1]
    wa = ya.shape[1]
    return pl.pallas_call(
        _post_mixer_kernel,
        grid=(t // tm,),
        in_specs=[
            pl.BlockSpec((tm, d), lambda i: (i, 0)),
            pl.BlockSpec((tm, wa), lambda i: (i, 0)),
            pl.BlockSpec((tm, yb.shape[1]), lambda i: (i, 0)),
            _layer_spec(wout_stacked, 0),
            _const_spec((1, d)),
            _layer_spec(wq_stacked, layer),
            pl.BlockSpec((1, m, d), lambda i: (i // nt, 0, 0)),
            pl.BlockSpec((1, m, d), lambda i: (i // nt, 0, 0)),
            _layer_spec(wo_stacked, layer),
        ],
        out_specs=pl.BlockSpec((tm, d), lambda i: (i, 0)),
        out_shape=jax.ShapeDtypeStruct((t, d), F32),
        compiler_params=_params("arbitrary"),
        name="post_mixer",
    )(x, ya, yb, wout_stacked, g, wq_stacked, kmem, vmem, wo_stacked)


def _ffn_kernel(x_ref, g_ref, wg_ref, wu_ref, wd_ref, o_ref):
    x = x_ref[...]
    h = _rmsnorm(x, g_ref[...]).astype(BF16)
    ff = wg_ref.shape[1]
    acc = x
    for c0 in range(0, ff, FFN_CHUNK):
        c1 = min(c0 + FFN_CHUNK, ff)
        a = _dot(h, wg_ref[:, c0:c1].astype(BF16))
        u = _dot(h, wu_ref[:, c0:c1].astype(BF16))
        acc = acc + _dot((jax.nn.silu(a) * u).astype(BF16), wd_ref[c0:c1, :].astype(BF16))
    o_ref[...] = acc


def _ffn(x, g, wg, wu, wd, *, seq):
    t, d = x.shape
    tm = min(TOK_TILE_SMALL, seq)
    return pl.pallas_call(
        _ffn_kernel,
        grid=(t // tm,),
        in_specs=[
            pl.BlockSpec((tm, d), lambda i: (i, 0)),
            _const_spec((1, d)),
            _const_spec(wg.shape),
            _const_spec(wu.shape),
            _const_spec(wd.shape),
        ],
        out_specs=pl.BlockSpec((tm, d), lambda i: (i, 0)),
        out_shape=jax.ShapeDtypeStruct((t, d), F32),
        compiler_params=_params("arbitrary"),
        name="ffn",
    )(x, g, wg, wu, wd)


META_I1, META_I2, META_R1, META_R2, META_G1, META_G2 = range(6)


def _router_kernel(x_ref, g_ref, wr_ref, meta_ref, metat_ref, cnt_ref, carry_ref):
    i = pl.program_id(0)
    tm = x_ref.shape[0]

    @pl.when(i == 0)
    def _():
        carry_ref[...] = jnp.zeros_like(carry_ref)

    h = _rmsnorm(x_ref[...], g_ref[...])
    hp = _split3(h)
    wp = _split3(wr_ref[...])
    logits = _dot(hp[1], wp[0]) + _dot(hp[0], wp[1]) + _dot(hp[0], wp[0])
    lane = lax.broadcasted_iota(I32, logits.shape, 1)
    logits = jnp.where(lane < N_EXPERTS, logits, NEG)
    m1 = jnp.max(logits, axis=-1, keepdims=True)
    i1 = jnp.min(jnp.where(logits == m1, lane, LANES), axis=-1, keepdims=True)
    rest = jnp.where(lane == i1, NEG, logits)
    m2 = jnp.max(rest, axis=-1, keepdims=True)
    i2 = jnp.min(jnp.where(rest == m2, lane, LANES), axis=-1, keepdims=True)
    e = jnp.exp(m2 - m1)
    g1 = 1.0 / (1.0 + e)
    g2 = e / (1.0 + e)

    sel = jnp.where((lane == i1) | (lane == i2), 1.0, 0.0)
    row = lax.broadcasted_iota(I32, (tm, tm), 0)
    col = lax.broadcasted_iota(I32, (tm, tm), 1)
    strict = jnp.where(col < row, 1.0, 0.0).astype(BF16)
    rank =_dot(strict, sel.astype(BF16)) + carry_ref[0:1, :]
    total = rank[tm - 1:tm, :] + sel[tm - 1:tm, :]
    carry_ref[...] = jnp.broadcast_to(total, carry_ref.shape)
    cnt_ref[...] = jnp.broadcast_to(total, cnt_ref.shape)
    r1 = jnp.sum(jnp.where(lane == i1, rank, 0.0), axis=-1, keepdims=True)
    r2 = jnp.sum(jnp.where(lane == i2, rank, 0.0), axis=-1, keepdims=True)
    meta = jnp.zeros_like(logits)
    for slot, val in ((META_I1, i1.astype(F32)), (META_I2, i2.astype(F32)), (META_R1, r1),
                      (META_R2, r2), (META_G1, g1), (META_G2, g2)):
        meta = jnp.where(lane == slot, val, meta)
    meta_ref[...] = meta
    metat_ref[...] = meta.T[0:8, :]


def _router(x, g, wr, *, seq):
    t, d = x.shape
    tm = min(TOK_TILE, seq)
    return pl.pallas_call(
        _router_kernel,
        grid=(t // tm,),
        in_specs=[
            pl.BlockSpec((tm, d), lambda i: (i, 0)),
            _const_spec((1, d)),
            _const_spec(wr.shape),
        ],
        out_specs=[
            pl.BlockSpec((tm, LANES), lambda i: (i, 0)),
            pl.BlockSpec((8, tm), lambda i: (0, i)),
            pl.BlockSpec((8, LANES), lambda i: (0, 0)),
        ],
        out_shape=[
            jax.ShapeDtypeStruct((t, LANES), F32),
            jax.ShapeDtypeStruct((8, t), F32),
            jax.ShapeDtypeStruct((8, LANES), F32),
        ],
        scratch_shapes=[pltpu.VMEM((8, LANES), F32)],
        compiler_params=_params("arbitrary"),
        name="router",
    )(x, g, wr)


def _row_copy(src, src_row, dst, dst_row, sem):
    return pltpu.make_async_copy(src.at[pl.ds(src_row, 1)], dst.at[pl.ds(dst_row, 1)], sem)


def _dispatch_kernel(pos_ref, pad_ref, x_ref, g_ref, xs_ref, h_ref, zrow_ref, sem_ref, zsem_ref):
    c = pl.program_id(0)
    n_chunks = pl.num_programs(0)
    tc = x_ref.shape[0]
    slot = c % 2

    @pl.when(c == 0)
    def _():
        zrow_ref[...] = jnp.zeros_like(zrow_ref)
        for e in range(N_EXPERTS):
            start = pad_ref[2 * e]
            count = pad_ref[2 * e + 1] - start

            def zissue(r, carry, start=start):
                _row_copy(zrow_ref, 0, xs_ref, start + r, zsem_ref).start()
                return carry

            def zwait(r, carry):
                _row_copy(zrow_ref, 0, xs_ref, 0, zsem_ref).wait()
                return carry

            lax.fori_loop(0, count, zissue, 0)
            lax.fori_loop(0, count, zwait, 0)

        tail = pad_ref[2 * N_EXPERTS - 1]
        groups = (xs_ref.shape[0] - tail) // 8

        def tissue(r, carry):
            start = pl.multiple_of(tail + 8 * r, 8)
            pltpu.make_async_copy(zrow_ref, xs_ref.at[pl.ds(start, 8)], zsem_ref).start()
            return carry

        def twait(r, carry):
            pltpu.make_async_copy(zrow_ref, xs_ref.at[pl.ds(0, 8)], zsem_ref).wait()
            return carry

        lax.fori_loop(0, groups, tissue, 0)
        lax.fori_loop(0, groups, twait, 0)

    h_ref[slot] = _rmsnorm(x_ref[...], g_ref[...])

    def issue(r, carry):
        tok = c * tc + r
        _row_copy(h_ref.at[slot], r, xs_ref, pos_ref[tok], sem_ref.at[slot]).start()
        _row_copy(h_ref.at[slot], r, xs_ref, pos_ref[n_chunks * tc + tok], sem_ref.at[slot]).start()
        return carry

    def drain(s):
        for _ in range(2):
            pltpu.make_async_copy(h_ref.at[s], xs_ref.at[pl.ds(0, tc)], sem_ref.at[s]).wait()

    lax.fori_loop(0, tc, issue, 0, unroll=DMA_UNROLL)

    @pl.when(c > 0)
    def _():
        drain(1 - slot)

    @pl.when(c == n_chunks - 1)
    def _():
        drain(slot)


def _dispatch(pos, pad, x, g, n_rows):
    t, d = x.shape
    tc = DISPATCH_CHUNK
    grid_spec = pltpu.PrefetchScalarGridSpec(
        num_scalar_prefetch=2,
        grid=(t // tc,),
        in_specs=[
            pl.BlockSpec((tc, d), lambda c, pos, pad: (c, 0)),
            pl.BlockSpec((1, d), lambda c, pos, pad: (0, 0)),
        ],
        out_specs=pl.BlockSpec(memory_space=pl.ANY),
        scratch_shapes=[
            pltpu.VMEM((2, tc, d), F32),
            pltpu.VMEM((8, d), F32),
            pltpu.SemaphoreType.DMA((2,)),
            pltpu.SemaphoreType.DMA(()),
        ],
    )
    return pl.pallas_call(
        _dispatch_kernel,
        grid_spec=grid_spec,
        out_shape=jax.ShapeDtypeStruct((n_rows, d), F32),
        compiler_params=_params("arbitrary"),
        name="moe_dispatch",
    )(pos, pad, x, g)


def _experts_kernel(te_ref, rows_ref, nused_ref, xs_ref, wg_ref, wu_ref, wd_ref, y_ref, xb_ref):
    p = pl.program_id(0)
    f = pl.program_id(1)
    rows = rows_ref[p]
    half = MOE_TILE // 2

    @pl.when(f == 0)
    def _():
        y_ref[...] = jnp.zeros_like(y_ref)

    @pl.when(rows > 0)
    def _():
        @pl.when(f == 0)
        def _():
            xb_ref[...] = xs_ref[...].astype(BF16)

        tf = wg_ref.shape[2]
        wgu = jnp.concatenate([wg_ref[0].astype(BF16), wu_ref[0].astype(BF16)], axis=1)
        wd = wd_ref[0].astype(BF16)

        def run(r0):
            gu = _dot(xb_ref[r0:r0 + half, :], wgu)
            act = (jax.nn.silu(gu[:, :tf]) * gu[:, tf:]).astype(BF16)
            y_ref[r0:r0 + half, :] += _dot(act, wd)

        run(0)

        @pl.when(rows > half)
        def _():
            run(half)


def _experts(tile_expert, tile_rows, n_used, xs, wg, wu, wd):
    n_rows, d = xs.shape
    ff = wg.shape[2]
    tm = MOE_TILE
    tf = MOE_FF_CHUNK
    n_tiles = n_rows // tm

    def x_map(p, f, te, tr, nu):
        return (jnp.minimum(p, nu[0] - 1), 0)

    grid_spec = pltpu.PrefetchScalarGridSpec(
        num_scalar_prefetch=3,
        grid=(n_tiles, ff // tf),
        in_specs=[
            pl.BlockSpec((tm, d), x_map),
            pl.BlockSpec((1, d, tf), lambda p, f, te, tr, nu: (te[p], 0, f)),
            pl.BlockSpec((1, d, tf), lambda p, f, te, tr, nu: (te[p], 0, f)),
            pl.BlockSpec((1, tf, d), lambda p, f, te, tr, nu: (te[p], f, 0)),
        ],
        out_specs=pl.BlockSpec((tm, d), lambda p, f, te, tr, nu: (p, 0)),
        scratch_shapes=[pltpu.VMEM((tm, d), BF16)],
    )
    return pl.pallas_call(
        _experts_kernel,
        grid_spec=grid_spec,
        out_shape=jax.ShapeDtypeStruct((n_rows, d), F32),
        compiler_params=_params("arbitrary", "arbitrary"),
        name="moe_experts",
    )(tile_expert, tile_rows, n_used, xs, wg, wu, wd)


def _combine_kernel(pos_ref, y_ref, x_ref, meta_ref, g_ref, o_ref, buf_ref, sem_ref):
    i = pl.program_id(0)
    n = pl.num_programs(0)
    tc = x_ref.shape[0]

    def fetch(tile, slot):
        def body(r, carry):
            tok = tile * tc + r
            for k in range(2):
                pltpu.make_async_copy(y_ref.at[pl.ds(pos_ref[k * n * tc + tok], 1)],
                                      buf_ref.at[slot, k, pl.ds(r, 1)], sem_ref.at[slot]).start()
            return carry
        lax.fori_loop(0, tc, body, 0, unroll=DMA_UNROLL)

    @pl.when(i == 0)
    def _():
        fetch(0, 0)

    @pl.when(i + 1 < n)
    def _():
        fetch(i + 1, (i + 1) % 2)

    slot = i % 2

    for k in range(2):
        pltpu.make_async_copy(y_ref.at[pl.ds(0, tc)], buf_ref.at[slot, k], sem_ref.at[slot]).wait()
    meta = meta_ref[...]
    g1 = meta[:, META_G1:META_G1 + 1]
    g2 = meta[:, META_G2:META_G2 + 1]
    x = x_ref[...] + g1 * buf_ref[slot, 0] + g2 * buf_ref[slot, 1]
    o_ref[...] = _rmsnorm(x, g_ref[...])


def _combine(pos, y, x, meta, g):
    t, d = x.shape
    tc = COMBINE_TILE
    grid_spec = pltpu.PrefetchScalarGridSpec(
        num_scalar_prefetch=1,
        grid=(t // tc,),
        in_specs=[
            pl.BlockSpec(memory_space=pl.ANY),
            pl.BlockSpec((tc, d), lambda i, pos: (i, 0)),
            pl.BlockSpec((tc, LANES), lambda i, pos: (i, 0)),
            pl.BlockSpec((1, d), lambda i, pos: (0, 0)),
        ],
        out_specs=pl.BlockSpec((tc, d), lambda i, pos: (i, 0)),
        scratch_shapes=[pltpu.VMEM((2, 2, tc, d), F32), pltpu.SemaphoreType.DMA((2,))],
    )
    return pl.pallas_call(
        _combine_kernel,
        grid_spec=grid_spec,
        out_shape=jax.ShapeDtypeStruct((t, d), F32),
        compiler_params=_params("arbitrary"),
        name="moe_combine",
    )(pos, y, x, meta, g)


def _moe_plan(meta_t, counts, n_tiles):
    cnt = counts[0, :N_EXPERTS].astype(I32)
    tiles = (cnt + MOE_TILE - 1) // MOE_TILE
    tile_end = jnp.cumsum(tiles)
    row_off = (tile_end - tiles) * MOE_TILE
    n_used = tile_end[-1:]
    experts = jnp.arange(N_EXPERTS, dtype=I32)

    def rows(idx_row, rank_row):
        idx = meta_t[idx_row].astype(I32)
        off = jnp.sum(jnp.where(idx[None, :] == experts[:, None], row_off[:, None], 0), axis=0)
        return off + meta_t[rank_row].astype(I32)

    pos = jnp.concatenate([rows(META_I1, META_R1), rows(META_I2, META_R2)])
    tile_ids = jnp.arange(n_tiles, dtype=I32)
    te = jnp.sum((tile_end[None, :] <= tile_ids[:, None]).astype(I32), axis=1)
    mine = te[:, None] == experts[None, :]
    first_row = tile_ids * MOE_TILE - jnp.sum(jnp.where(mine, row_off[None, :], 0), axis=1)
    tile_rows = jnp.clip(jnp.sum(jnp.where(mine, cnt[None, :], 0), axis=1) - first_row, 0, MOE_TILE)
    last = jnp.sum((tile_end <= n_used - 1).astype(I32))
    te = jnp.minimum(te, last).astype(I32)
    pad = jnp.stack([row_off + cnt, row_off + tiles * MOE_TILE], axis=1).reshape(-1).astype(I32)
    return pos.astype(I32), pad, te, tile_rows.astype(I32), n_used.astype(I32)


def _moe_and_final_norm(x, g_ffn, wr, wg, wu, wd, g_final, *, seq):
    t, d = x.shape
    n_tiles = 2 * t // MOE_TILE + N_EXPERTS
    meta, meta_t, counts = _router(x, g_ffn, wr, seq=seq)
    pos, pad, te, tile_rows, n_used = _moe_plan(meta_t, counts, n_tiles)
    xs = _dispatch(pos, pad, x, g_ffn, n_tiles * MOE_TILE)
    y = _experts(te, tile_rows, n_used, xs, wg, wu, wd)
    return _combine(pos, y, x, meta, g_final)


def _row(v):
    return v.reshape(1, -1).astype(F32)


def kernel(x, mem, norm_mix, norm_mem_q, norm_mem_kv, norm_ffn, norm_final, even_w_in, fox_b_f, gmlp_v_gain, gmlp_w_s, gmlp_b_s, even_w_out, odd_w_in, diff_lambda_q1, diff_lambda_k1, diff_lambda_q2, diff_lambda_k2, diff_subln_gain, conv_w, odd_w_out, rel_bias, mem_w_q, mem_w_kv, mem_w_o, ffn_w_gate, ffn_w_up, ffn_w_down, router_w, moe_w_gate, moe_w_up, moe_w_down):
    batch, seq, d = x.shape
    t = batch * seq
    depth = norm_mix.shape[0]
    assert depth == 2 and seq % (2 * ATT_BLOCK) == 0 and seq % TOK_TILE == 0
    xf = x.reshape(t, d)
    hd = d // N_HEADS
    aw = gmlp_v_gain.shape[1]
    n_blk = gmlp_w_s.shape[2]

    gate_cols = even_w_in.shape[2] - N_HEADS
    wf = jnp.pad(even_w_in[0, :, gate_cols:], ((0, 0), (0, LANES - N_HEADS)))
    bf = jnp.pad(fox_b_f[0], (0, LANES - N_HEADS)).reshape(1, LANES)
    bs = jnp.broadcast_to(gmlp_b_s[0][:, :, None], (A_GROUPS, n_blk, aw // A_GROUPS)).astype(F32)
    ya, qkv, aug = _even_in(xf, _row(norm_mix[0]), even_w_in, wf, bf, _row(gmlp_v_gain[0]),
                            gmlp_w_s[0], bs, seq=seq)
    yb = _fox_attention(qkv, aug, batch=batch, seq=seq)

    def mem_attention(layer, x_in, ya_, yb_, w_out):
        km, vm = _mem_kv(mem, _row(norm_mem_kv[layer]), mem_w_kv, layer)
        return _post_mixer(x_in, ya_, yb_, w_out, _row(norm_mem_q[layer]), mem_w_q, km, vm,
                           mem_w_o, layer, seq=seq)

    xf = mem_attention(0, xf, ya, yb, even_w_out)
    xf = _ffn(xf, _row(norm_ffn[0]), ffn_w_gate[0], ffn_w_up[0], ffn_w_down[0], seq=seq)

    qk_dim = LANES // 2
    qk_w = 2 * N_HEADS * qk_dim

    def regroup(cols):
        return cols.reshape(d, 2, N_HEADS, qk_dim).transpose(0, 2, 1, 3).reshape(d, qk_w)

    wqk = jnp.concatenate([regroup(odd_w_in[0, :, :qk_w]) * float(qk_dim) ** -0.5,
                           regroup(odd_w_in[0, :, qk_w:2 * qk_w])], axis=1).astype(BF16)
    qkv, yd = _odd_in(xf, _row(norm_mix[1]), wqk, odd_w_in, conv_w[0].astype(F32), seq=seq,
                      qw=2 * qk_w + N_HEADS * LANES)
    lam_init = 0.8 - 0.6 * math.exp(-0.3 * 1)
    lam = (jnp.exp(jnp.sum(diff_lambda_q1[0] * diff_lambda_k1[0]))
           - jnp.exp(jnp.sum(diff_lambda_q2[0] * diff_lambda_k2[0])) + lam_init)
    bias = _diff_bias_blocks(rel_bias, min(ATT_BLOCK, seq))
    yc = _diff_attention(qkv, lam.reshape(1, 1).astype(F32), bias, _row(diff_subln_gain[0]),
                         batch=batch, seq=seq, out_scale=1.0 - lam_init)
    xf = mem_attention(1, xf, yc, yd, odd_w_out)
    wr = jnp.pad(router_w[0], ((0, 0), (0, LANES - N_EXPERTS))).astype(F32)
    out = _moe_and_final_norm(xf, _row(norm_ffn[1]), wr, moe_w_gate[0], moe_w_up[0], moe_w_down[0],
                              _row(norm_final), seq=seq)
    return out.reshape(batch, seq, d)
```

```python
import functools
import math

import numpy as np
import jax
import jax.numpy as jnp
from jax import lax
from jax.experimental import pallas as pl
from jax.experimental.pallas import tpu as pltpu

F32 = jnp.float32
BF16 = jnp.bfloat16
I32 = jnp.int32

EPS = 1e-6
NEG = -1e30
LANES = 128
VMEM_LIMIT = 56 * 1024 * 1024

CHUNK = 64
A_GROUPS = 4
N_HEADS = 4
N_EXPERTS = 8
REL_BUCKETS = 32
REL_MAX_DIST = 128

TOK_TILE = 1024
TOK_TILE_SMALL = 512
FOX_BLOCK = 1024
ATT_BLOCK = 512
ATT_STRIP = 64
MOE_TILE = 1024
MOE_FF_CHUNK = 896
FFN_CHUNK = 512
DISPATCH_CHUNK = 256
COMBINE_TILE = 256
DMA_UNROLL = 8


def _dot(a, b):
    return jnp.dot(a, b, preferred_element_type=F32)


def _dot_nt(a, b):
    return lax.dot_general(a, b, (((1,), (1,)), ((), ())), preferred_element_type=F32)


def _rmsnorm(x, g):
    return x * lax.rsqrt(jnp.mean(x * x, axis=-1, keepdims=True) + EPS) * g


def _params(*sem):
    return pltpu.CompilerParams(dimension_semantics=sem, vmem_limit_bytes=VMEM_LIMIT)


def _const_spec(shape):
    return pl.BlockSpec(shape, lambda *_: (0,) * len(shape), pipeline_mode=pl.Buffered(1))


def _layer_spec(stacked, layer):
    rest = stacked.shape[1:]
    return pl.BlockSpec((None,) + rest, lambda *_: (layer,) + (0,) * len(rest),
                        pipeline_mode=pl.Buffered(1))


def _split3(v):
    a1 = v.astype(BF16)
    r1 = v - a1.astype(F32)
    a2 = r1.astype(BF16)
    a3 = (r1 - a2.astype(F32)).astype(BF16)
    return a1, a2, a3


def _even_in_kernel(x_ref, g_ref, w_ref, wf_ref, bf_ref, vg_ref, ws_ref, bs_ref,
                    ya_ref, qkv_ref, aug_ref, carry_ref, *, tiles_per_batch):
    i = pl.program_id(0)
    tm = x_ref.shape[0]
    aw = ya_ref.shape[1]
    qw = qkv_ref.shape[1]

    @pl.when(i % tiles_per_batch == 0)
    def _():
        carry_ref[...] = jnp.zeros_like(carry_ref)

    xn = _rmsnorm(x_ref[...], g_ref[...]).astype(BF16)

    hw = N_HEADS * LANES
    q = _dot(xn, w_ref[:, 2 * aw:2 * aw + hw].astype(BF16)) * (float(LANES) ** -0.5)
    qkv_ref[:, 0:hw] = q.astype(BF16)
    qkv_ref[:, hw:qw] = _dot(xn, w_ref[:, 2 * aw + hw:2 * aw + qw].astype(BF16)).astype(BF16)

    fl = _dot(xn, wf_ref[...].astype(BF16)) + bf_ref[...]
    ls = jnp.minimum(fl, 0.0) - jnp.log1p(jnp.exp(-jnp.abs(fl)))
    row = lax.broadcasted_iota(I32, (tm, tm), 0)
    col = lax.broadcasted_iota(I32, (tm, tm), 1)
    tri = jnp.where(col <= row, 1.0, 0.0).astype(BF16)
    a1, a2, a3 = _split3(ls)
    csum = _dot(tri, a1) + _dot(tri, a2) + _dot(tri, a3)
    csum = csum + carry_ref[0:1, :]
    carry_ref[...] = jnp.broadcast_to(csum[tm - 1:tm, :], carry_ref.shape)

    lane = lax.broadcasted_iota(I32, (tm, LANES), 1)
    for h in range(N_HEADS):
        c1, c2, c3 = (piece.astype(F32)
                      for piece in _split3(jnp.broadcast_to(csum[:, h:h + 1], (tm, LANES))))
        aq = jnp.where(lane == 0, c1, jnp.where(lane == 1, c2, jnp.where(lane == 2, c3,
                       jnp.where(lane < 6, 1.0, 0.0))))
        ak = jnp.where(lane == 3, -c1, jnp.where(lane == 4, -c2, jnp.where(lane == 5, -c3,
                       jnp.where(lane < 3, 1.0, 0.0))))
        aug_ref[:, h * LANES:(h + 1) * LANES] = aq.astype(BF16)
        aug_ref[:, (N_HEADS + h) * LANES:(N_HEADS + h + 1) * LANES] = ak.astype(BF16)

    gu = jax.nn.gelu(_dot(xn, w_ref[:, 0:aw].astype(BF16)), approximate=True)
    gv = jax.nn.gelu(_dot(xn, w_ref[:, aw:2 * aw].astype(BF16)), approximate=True)
    blk = ws_ref.shape[1]
    ch = aw // A_GROUPS
    r = lax.broadcasted_iota(I32, (blk, blk), 0)
    c = lax.broadcasted_iota(I32, (blk, blk), 1)
    causal = (c // CHUNK) <= (r // CHUNK)
    for g in range(A_GROUPS):
        vgrp = gv[:, g * ch:(g + 1) * ch]
        vn = _rmsnorm(vgrp, vg_ref[:, g * ch:(g + 1) * ch]).astype(BF16)
        wmix = jnp.where(causal, ws_ref[g], 0.0).astype(BF16)
        for n in range(tm // blk):
            mixed = _dot(wmix, vn[n * blk:(n + 1) * blk, :]) + bs_ref[g]
            ya_ref[n * blk:(n + 1) * blk, g * ch:(g + 1) * ch] = (
                gu[n * blk:(n + 1) * blk, g * ch:(g + 1) * ch] * mixed).astype(BF16)


def _even_in(x, g, w_stacked, wf, bf, vgain, ws, bs, *, seq):
    t, d = x.shape
    tm = min(TOK_TILE_SMALL, seq)
    aw = vgain.shape[1]
    qw = 3 * N_HEADS * LANES
    blk = ws.shape[1]
    return pl.pallas_call(
        functools.partial(_even_in_kernel, tiles_per_batch=seq // tm),
        grid=(t // tm,),
        in_specs=[
            pl.BlockSpec((tm, d), lambda i: (i, 0)),
            _const_spec((1, d)),
            _layer_spec(w_stacked, 0),
            _const_spec(wf.shape),
            _const_spec((1, LANES)),
            _const_spec((1, aw)),
            _const_spec(ws.shape),
            _const_spec(bs.shape),
        ],
        out_specs=[
            pl.BlockSpec((tm, aw), lambda i: (i, 0)),
            pl.BlockSpec((tm, qw), lambda i: (i, 0)),
            pl.BlockSpec((tm, 2 * N_HEADS * LANES), lambda i: (i, 0)),
        ],
        out_shape=[
            jax.ShapeDtypeStruct((t, aw), BF16),
            jax.ShapeDtypeStruct((t, qw), BF16),
            jax.ShapeDtypeStruct((t, 2 * N_HEADS * LANES), BF16),
        ],
        scratch_shapes=[pltpu.VMEM((8, LANES), F32)],
        compiler_params=_params("arbitrary"),
        name="even_in",
    )(x, g, w_stacked, wf, bf, vgain, ws, bs)


def _tri_pairs(nblk):
    qi = np.array([i for i in range(nblk) for _ in range(i + 1)], np.int32)
    kj = np.array([j for i in range(nblk) for j in range(i + 1)], np.int32)
    return jnp.asarray(qi), jnp.asarray(kj)


def _softmax_strips(logits, p_ref, alpha_ref, m_ref, h, add_bias=None):
    rows, tk = logits.shape
    for r0 in range(0, rows, ATT_STRIP):
        rs = slice(r0, r0 + ATT_STRIP)
        s = logits[rs, :]
        if add_bias is not None:
            s = add_bias(s, r0)
        m_prev = m_ref[h, rs, :]
        m_new = jnp.maximum(m_prev, jnp.max(s, axis=-1, keepdims=True))
        alpha_ref[rs, :] = jnp.exp(m_prev - m_new)
        m_ref[h, rs, :] = m_new
        p_ref[rs, 0:tk] = jnp.exp(
            (s - jnp.concatenate([m_new] * (tk // LANES), axis=-1)).astype(BF16))


def _accumulate(p_ref, alpha_ref, v, ones, acc_ref, h):
    alpha = alpha_ref[...]
    va = jnp.concatenate([v, ones], axis=-1)
    acc_ref[h] = (jnp.concatenate([alpha, alpha], axis=-1) * acc_ref[h]
                  + _dot(p_ref[:, 0:v.shape[0]], va))


def _fox_kernel(qi_ref, kj_ref, q_ref, aq_ref, k_ref, ak_ref, v_ref, ones_ref, mask_ref, o_ref,
                p_ref, alpha_ref, m_ref, acc_ref):
    step = pl.program_id(1)
    i = qi_ref[step]
    j = kj_ref[step]

    @pl.when(j == 0)
    def _():
        m_ref[...] = jnp.full_like(m_ref, NEG)
        acc_ref[...] = jnp.zeros_like(acc_ref)

    def logits(h):
        hs = slice(h * LANES, (h + 1) * LANES)
        qa = jnp.concatenate([q_ref[:, hs], aq_ref[:, hs]], axis=-1)
        ka = jnp.concatenate([k_ref[:, hs], ak_ref[:, hs]], axis=-1)
        return _dot_nt(qa, ka)

    def sweep(masked):
        nxt = logits(0)
        for h in range(N_HEADS):
            hs = slice(h * LANES, (h + 1) * LANES)
            sb = h % 2
            cur = nxt
            if h + 1 < N_HEADS:
                nxt = logits(h + 1)
            add_mask = (lambda s, r0: s + mask_ref[r0:r0 + ATT_STRIP, :]) if masked else None
            _softmax_strips(cur, p_ref.at[sb], alpha_ref.at[sb], m_ref, h, add_mask)
            _accumulate(p_ref.at[sb], alpha_ref.at[sb], v_ref[:, hs], ones_ref[...], acc_ref, h)

    @pl.when(j < i)
    def _():
        sweep(False)

    @pl.when(j == i)
    def _():
        sweep(True)
        for h in range(N_HEADS):
            acc = acc_ref[h]
            o_ref[:, h * LANES:(h + 1) * LANES] = (
                acc[:, :LANES] / acc[:, LANES:LANES + 1]).astype(o_ref.dtype)


def _ones_column(rows):
    ones = np.zeros((rows, LANES), np.float32)
    ones[:, 0] = 1.0
    return jnp.asarray(ones, BF16)


def _fox_attention(qkv, aug, *, batch, seq):
    t = qkv.shape[0]
    w = N_HEADS * LANES
    blk = min(FOX_BLOCK, seq)
    nblk = seq // blk
    qi, kj = _tri_pairs(nblk)
    r = np.arange(blk)
    mask = jnp.asarray(np.where(r[None, :] <= r[:, None], 0.0, NEG), F32)
    grid_spec = pltpu.PrefetchScalarGridSpec(
        num_scalar_prefetch=2,
        grid=(batch, qi.shape[0]),
        in_specs=[
            pl.BlockSpec((blk, w), lambda b, s, qi, kj: (b * nblk + qi[s], 0)),
            pl.BlockSpec((blk, w), lambda b, s, qi, kj: (b * nblk + qi[s], 0)),
            pl.BlockSpec((blk, w), lambda b, s, qi, kj: (b * nblk + kj[s], 1)),
            pl.BlockSpec((blk, w), lambda b, s, qi, kj: (b * nblk + kj[s], 1)),
            pl.BlockSpec((blk, w), lambda b, s, qi, kj: (b * nblk + kj[s], 2)),
            pl.BlockSpec((blk, LANES), lambda b, s, qi, kj: (0, 0)),
            pl.BlockSpec((blk, blk), lambda b, s, qi, kj: (0, 0)),
        ],
        out_specs=pl.BlockSpec((blk, w), lambda b, s, qi, kj: (b * nblk + qi[s], 0)),
        scratch_shapes=[
            pltpu.VMEM((2, blk, blk), BF16),
            pltpu.VMEM((2, blk, LANES), F32),
            pltpu.VMEM((N_HEADS, blk, LANES), F32),
            pltpu.VMEM((N_HEADS, blk, 2 * LANES), F32),
        ],
    )
    return pl.pallas_call(
        _fox_kernel,
        grid_spec=grid_spec,
        out_shape=jax.ShapeDtypeStruct((t, w), BF16),
        compiler_params=_params("arbitrary", "arbitrary"),
        name="fox_attention",
    )(qi, kj, qkv, aug, qkv, aug, qkv, _ones_column(blk), mask)


def _diff_pairs(nq):
    qi = np.array([i for i in range(nq) for _ in range(i // 2 + 1)], np.int32)
    kj = np.array([j for i in range(nq) for j in range(i // 2 + 1)], np.int32)
    return jnp.asarray(qi), jnp.asarray(kj)


def _diff_kernel(qi_ref, kj_ref, lam_ref, q_ref, k_ref, v_ref, ones_ref, bias_ref, gain_ref, o_ref,
                 qs_ref, p_ref, alpha_ref, m_ref, acc_ref, *, out_scale):
    step = pl.program_id(1)
    i = qi_ref[step]
    j = kj_ref[step]
    tq = q_ref.shape[0]
    half = LANES // 2
    jd = i // 2
    odd = (i % 2) == 1

    @pl.when(j == 0)
    def _():
        m_ref[...] = jnp.full_like(m_ref, NEG)
        acc_ref[...] = jnp.zeros_like(acc_ref)
        lane = lax.broadcasted_iota(I32, (tq, LANES), 1)
        zero = jnp.zeros((tq, LANES), BF16)
        for h in range(N_HEADS):
            q = q_ref[:, h * LANES:(h + 1) * LANES]
            qs_ref[h, 0:tq, :] = jnp.where(lane < half, q, zero)
            qs_ref[h, tq:2 * tq, :] = jnp.where(lane < half, zero, q)

    def sweep(tk, add_bias):
        def logits(h):
            return _dot_nt(qs_ref[h], k_ref[0:tk, h * LANES:(h + 1) * LANES])

        nxt = logits(0)
        for h in range(N_HEADS):
            hs = slice(h * LANES, (h + 1) * LANES)
            sb = h % 2
            cur = nxt
            if h + 1 < N_HEADS:
                nxt = logits(h + 1)
            bias_fn = None if add_bias is None else functools.partial(add_bias, h)
            _softmax_strips(cur, p_ref.at[sb], alpha_ref.at[sb], m_ref, h, bias_fn)
            _accumulate(p_ref.at[sb], alpha_ref.at[sb], v_ref[0:tk, hs], ones_ref[0:tk, :], acc_ref, h)

    def bias_rows(h, which, r0):
        return bias_ref[h, which, pl.ds(r0 % tq, ATT_STRIP), :]

    def odd_diag(h, s, r0):
        return jnp.concatenate([s[:, :tq] + bias_rows(h, 1, r0), s[:, tq:] + bias_rows(h, 0, r0)],
                               axis=-1)

    def even_prev(h, s, r0):
        return jnp.concatenate([s[:, :tq], s[:, tq:] + bias_rows(h, 1, r0)], axis=-1)

    def even_diag(h, s, r0):
        return s + bias_rows(h, 0, r0)

    @pl.when((j < jd - 1) | ((j == jd - 1) & odd))
    def _():
        sweep(2 * tq, None)

    @pl.when((j == jd - 1) & jnp.logical_not(odd))
    def _():
        sweep(2 * tq, even_prev)

    @pl.when((j == jd) & odd)
    def _():
        sweep(2 * tq, odd_diag)

    @pl.when((j == jd) & jnp.logical_not(odd))
    def _():
        sweep(tq, even_diag)

    @pl.when(j == jd)
    def _():
        for h in range(N_HEADS):
            hs = slice(h * LANES, (h + 1) * LANES)
            acc = acc_ref[h]
            o = acc[:, :LANES] / acc[:, LANES:LANES + 1]
            o = o[0:tq, :] - lam_ref[0, 0] * o[tq:2 * tq, :]
            o_ref[:, hs] = (_rmsnorm(o, gain_ref[:, hs]) * out_scale).astype(o_ref.dtype)


def _diff_attention(qkv, lam, bias, gain, *, batch, seq, out_scale):
    t = qkv.shape[0]
    w = N_HEADS * LANES
    blk = bias.shape[2]
    kblk = 2 * blk
    nq = seq // blk
    nk = seq // kblk
    qi, kj = _diff_pairs(nq)
    grid_spec = pltpu.PrefetchScalarGridSpec(
        num_scalar_prefetch=2,
        grid=(batch, qi.shape[0]),
        in_specs=[
            pl.BlockSpec(memory_space=pltpu.SMEM),
            pl.BlockSpec((blk, w), lambda b, s, qi, kj: (b * nq + qi[s], 0)),
            pl.BlockSpec((kblk, w), lambda b, s, qi, kj: (b * nk + kj[s], 1)),
            pl.BlockSpec((kblk, w), lambda b, s, qi, kj: (b * nk + kj[s], 2)),
            pl.BlockSpec((kblk, LANES), lambda b, s, qi, kj: (0, 0)),
            pl.BlockSpec(bias.shape, lambda b, s, qi, kj: (0, 0, 0, 0)),
            pl.BlockSpec((1, w), lambda b, s, qi, kj: (0, 0)),
        ],
        out_specs=pl.BlockSpec((blk, w), lambda b, s, qi, kj: (b * nq + qi[s], 0)),
        scratch_shapes=[
            pltpu.VMEM((N_HEADS, 2 * blk, LANES), BF16),
            pltpu.VMEM((2, 2 * blk, kblk), BF16),
            pltpu.VMEM((2, 2 * blk, LANES), F32),
            pltpu.VMEM((N_HEADS, 2 * blk, LANES), F32),
            pltpu.VMEM((N_HEADS, 2 * blk, 2 * LANES), F32),
        ],
    )
    return pl.pallas_call(
        functools.partial(_diff_kernel, out_scale=out_scale),
        grid_spec=grid_spec,
        out_shape=jax.ShapeDtypeStruct((t, w), BF16),
        compiler_params=_params("arbitrary", "arbitrary"),
        name="diff_attention",
    )(qi, kj, lam, qkv, qkv, qkv, _ones_column(kblk), bias, gain)


def _rel_bucket(rel):
    n_half = REL_BUCKETS // 2
    max_exact = n_half // 2
    ret = jnp.where(rel > 0, n_half, 0)
    n = jnp.abs(rel)
    nf = jnp.maximum(n, 1).astype(F32)
    large = max_exact + (jnp.log(nf / max_exact) / math.log(REL_MAX_DIST / max_exact)
                         * (n_half - max_exact)).astype(I32)
    large = jnp.minimum(large, n_half - 1)
    return ret + jnp.where(n < max_exact, n, large)


def _rel_bias_kernel(table_ref, idx_ref, o_ref):
    h = pl.program_id(0)
    idx = idx_ref[0]
    far = table_ref[REL_BUCKETS // 2 - 1, h]
    out = jnp.full(idx.shape, NEG, F32)
    for b in range(REL_BUCKETS):
        out = jnp.where(idx == b, table_ref[b, h] - far, out)
    o_ref[0, 0] = out


def _diff_bias_blocks(rel_table, blk):
    assert blk >= REL_MAX_DIST
    r = jnp.arange(blk)[:, None]
    c = jnp.arange(blk)[None, :]
    diag = jnp.where((c // CHUNK) <= (r // CHUNK), _rel_bucket(c - r), -1)
    prev = _rel_bucket(c - r - blk)
    idx = jnp.stack([diag, prev], axis=0).astype(I32)
    n_heads = rel_table.shape[1]
    return pl.pallas_call(
        _rel_bias_kernel,
        grid=(n_heads, 2),
        in_specs=[
            pl.BlockSpec(memory_space=pltpu.SMEM),
            pl.BlockSpec((1, blk, blk), lambda h, s: (s, 0, 0)),
        ],
        out_specs=pl.BlockSpec((1, 1, blk, blk), lambda h, s: (h, s, 0, 0)),
        out_shape=jax.ShapeDtypeStruct((n_heads, 2, blk, blk), F32),
        compiler_params=_params("arbitrary", "arbitrary"),
        name="rel_bias",
    )(rel_table.astype(F32), idx)


def _odd_in_kernel(x_ref, g_ref, wqk_ref, w_ref, cw_ref, qkv_ref, yd_ref, zbuf_ref, *,
                   tiles_per_batch):
    i = pl.program_id(0)
    tm = x_ref.shape[0]
    qw = qkv_ref.shape[1]
    dw = yd_ref.shape[1]
    qkw = wqk_ref.shape[1]

    @pl.when(i % tiles_per_batch == 0)
    def _():
        zbuf_ref[tm:tm + 8, :] = jnp.zeros((8, dw), F32)

    xn = _rmsnorm(x_ref[...], g_ref[...]).astype(BF16)
    qkv_ref[:, 0:qkw] = _dot(xn, wqk_ref[...]).astype(BF16)
    qkv_ref[:, qkw:qw] = _dot(xn, w_ref[:, qkw:qw].astype(BF16)).astype(BF16)
    hh = _dot(xn, w_ref[:, qw:qw + dw].astype(BF16))
    gb = _dot(xn, w_ref[:, qw + dw:qw + 2 * dw].astype(BF16))
    gc = _dot(xn, w_ref[:, qw + 2 * dw:qw + 3 * dw].astype(BF16))
    z = gc * hh

    zbuf_ref[0:8, :] = zbuf_ref[tm:tm + 8, :]
    zbuf_ref[8:tm + 8, :] = z
    y = (cw_ref[0:1, :] * zbuf_ref[6:tm + 6, :] + cw_ref[1:2, :] * zbuf_ref[7:tm + 7, :]
         + cw_ref[2:3, :] * z)
    yd_ref[...] = (gb * y).astype(BF16)


def _odd_in(x, g, wqk, w_stacked, cw, *, seq, qw):
    t, d = x.shape
    tm = min(TOK_TILE, seq)
    dw = cw.shape[1]
    return pl.pallas_call(
        functools.partial(_odd_in_kernel, tiles_per_batch=seq // tm),
        grid=(t // tm,),
        in_specs=[
            pl.BlockSpec((tm, d), lambda i: (i, 0)),
            _const_spec((1, d)),
            _const_spec(wqk.shape),
            _layer_spec(w_stacked, 0),
            _const_spec(cw.shape),
        ],
        out_specs=[
            pl.BlockSpec((tm, qw), lambda i: (i, 0)),
            pl.BlockSpec((tm, dw), lambda i: (i, 0)),
        ],
        out_shape=[
            jax.ShapeDtypeStruct((t, qw), BF16),
            jax.ShapeDtypeStruct((t, dw), BF16),
        ],
        scratch_shapes=[pltpu.VMEM((tm + 8, dw), F32)],
        compiler_params=_params("arbitrary"),
        name="odd_in",
    )(x, g, wqk, w_stacked, cw)


def _mem_kv_kernel(mem_ref, g_ref, w_ref, k_ref, v_ref):
    d = mem_ref.shape[2]
    mn = _rmsnorm(mem_ref[0], g_ref[...]).astype(BF16)
    k_ref[0] = _dot(mn, w_ref[:, 0:d].astype(BF16)).astype(BF16)
    v_ref[0] = _dot(mn, w_ref[:, d:2 * d].astype(BF16)).astype(BF16)


def _mem_kv(mem, g, w_stacked, layer):
    b, m, d = mem.shape
    return pl.pallas_call(
        _mem_kv_kernel,
        grid=(b,),
        in_specs=[
            pl.BlockSpec((1, m, d), lambda i: (i, 0, 0)),
            _const_spec((1, d)),
            _layer_spec(w_stacked, layer),
        ],
        out_specs=[pl.BlockSpec((1, m, d), lambda i: (i, 0, 0))] * 2,
        out_shape=[jax.ShapeDtypeStruct((b, m, d), BF16)] * 2,
        compiler_params=_params("arbitrary"),
        name="mem_kv",
    )(mem, g, w_stacked)


def _post_mixer_kernel(x_ref, ya_ref, yb_ref, wout_ref, g_ref, wq_ref, k_ref, v_ref, wo_ref, o_ref):
    wa = ya_ref.shape[1]
    d = x_ref.shape[1]
    hd = d // N_HEADS
    x1 = (x_ref[...] + _dot(ya_ref[...], wout_ref[0:wa, :].astype(BF16))
          + _dot(yb_ref[...], wout_ref[wa:, :].astype(BF16)))
    xn = _rmsnorm(x1, g_ref[...]).astype(BF16)
    q = (_dot(xn, wq_ref[...].astype(BF16)) * (float(hd) ** -0.5)).astype(BF16)
    heads = []
    for h in range(N_HEADS):
        lg = _dot_nt(q[:, h * hd:(h + 1) * hd], k_ref[0, :, h * hd:(h + 1) * hd])
        p = jnp.exp(lg - jnp.max(lg, axis=-1, keepdims=True))
        p = p / jnp.sum(p, axis=-1, keepdims=True)
        heads.append(_dot(p.astype(BF16), v_ref[0, :, h * hd:(h + 1) * hd]).astype(BF16))
    o_ref[...] = x1 + _dot(jnp.concatenate(heads, axis=-1), wo_ref[...].astype(BF16))


def _post_mixer(x, ya, yb, wout_stacked, g, wq_stacked, kmem, vmem, wo_stacked, layer, *, seq):
    t, d = x.shape
    tm = min(TOK_TILE, seq)
    nt = seq // tm
    m = kmem.shape[1]
    wa = ya.shape[1]
    return pl.pallas_call(
        _post_mixer_kernel,
        grid=(t // tm,),
        in_specs=[
            pl.BlockSpec((tm, d), lambda i: (i, 0)),
            pl.BlockSpec((tm, wa), lambda i: (i, 0)),
            pl.BlockSpec((tm, yb.shape[1]), lambda i: (i, 0)),
            _layer_spec(wout_stacked, 0),
            _const_spec((1, d)),
            _layer_spec(wq_stacked, layer),
            pl.BlockSpec((1, m, d), lambda i: (i // nt, 0, 0)),
            pl.BlockSpec((1, m, d), lambda i: (i // nt, 0, 0)),
            _layer_spec(wo_stacked, layer),
        ],
        out_specs=pl.BlockSpec((tm, d), lambda i: (i, 0)),
        out_shape=jax.ShapeDtypeStruct((t, d), F32),
        compiler_params=_params("arbitrary"),
        name="post_mixer",
    )(x, ya, yb, wout_stacked, g, wq_stacked, kmem, vmem, wo_stacked)


def _ffn_kernel(x_ref, g_ref, wg_ref, wu_ref, wd_ref, o_ref):
    x = x_ref[...]
    h = _rmsnorm(x, g_ref[...]).astype(BF16)
    ff = wg_ref.shape[1]
    acc = x
    for c0 in range(0, ff, FFN_CHUNK):
        c1 = min(c0 + FFN_CHUNK, ff)
        a = _dot(h, wg_ref[:, c0:c1].astype(BF16))
        u = _dot(h, wu_ref[:, c0:c1].astype(BF16))
        acc = acc + _dot((jax.nn.silu(a) * u).astype(BF16), wd_ref[c0:c1, :].astype(BF16))
    o_ref[...] = acc


def _ffn(x, g, wg, wu, wd, *, seq):
    t, d = x.shape
    tm = min(TOK_TILE_SMALL, seq)
    return pl.pallas_call(
        _ffn_kernel,
        grid=(t // tm,),
        in_specs=[
            pl.BlockSpec((tm, d), lambda i: (i, 0)),
            _const_spec((1, d)),
            _const_spec(wg.shape),
            _const_spec(wu.shape),
            _const_spec(wd.shape),
        ],
        out_specs=pl.BlockSpec((tm, d), lambda i: (i, 0)),
        out_shape=jax.ShapeDtypeStruct((t, d), F32),
        compiler_params=_params("arbitrary"),
        name="ffn",
    )(x, g, wg, wu, wd)


META_I1, META_I2, META_R1, META_R2, META_G1, META_G2 = range(6)


def _router_kernel(x_ref, g_ref, wr_ref, meta_ref, metat_ref, cnt_ref, carry_ref):
    i = pl.program_id(0)
    tm = x_ref.shape[0]

    @pl.when(i == 0)
    def _():
        carry_ref[...] = jnp.zeros_like(carry_ref)

    h = _rmsnorm(x_ref[...], g_ref[...])
    hp = _split3(h)
    wp = _split3(wr_ref[...])
    logits = _dot(hp[1], wp[0]) + _dot(hp[0], wp[1]) + _dot(hp[0], wp[0])
    lane = lax.broadcasted_iota(I32, logits.shape, 1)
    logits = jnp.where(lane < N_EXPERTS, logits, NEG)
    m1 = jnp.max(logits, axis=-1, keepdims=True)
    i1 = jnp.min(jnp.where(logits == m1, lane, LANES), axis=-1, keepdims=True)
    rest = jnp.where(lane == i1, NEG, logits)
    m2 = jnp.max(rest, axis=-1, keepdims=True)
    i2 = jnp.min(jnp.where(rest == m2, lane, LANES), axis=-1, keepdims=True)
    e = jnp.exp(m2 - m1)
    g1 = 1.0 / (1.0 + e)
    g2 = e / (1.0 + e)

    sel = jnp.where((lane == i1) | (lane == i2), 1.0, 0.0)
    row = lax.broadcasted_iota(I32, (tm, tm), 0)
    col = lax.broadcasted_iota(I32, (tm, tm), 1)
    strict = jnp.where(col < row, 1.0, 0.0).astype(BF16)
    rank =_dot(strict, sel.astype(BF16)) + carry_ref[0:1, :]
    total = rank[tm - 1:tm, :] + sel[tm - 1:tm, :]
    carry_ref[...] = jnp.broadcast_to(total, carry_ref.shape)
    cnt_ref[...] = jnp.broadcast_to(total, cnt_ref.shape)
    r1 = jnp.sum(jnp.where(lane == i1, rank, 0.0), axis=-1, keepdims=True)
    r2 = jnp.sum(jnp.where(lane == i2, rank, 0.0), axis=-1, keepdims=True)
    meta = jnp.zeros_like(logits)
    for slot, val in ((META_I1, i1.astype(F32)), (META_I2, i2.astype(F32)), (META_R1, r1),
                      (META_R2, r2), (META_G1, g1), (META_G2, g2)):
        meta = jnp.where(lane == slot, val, meta)
    meta_ref[...] = meta
    metat_ref[...] = meta.T[0:8, :]


def _router(x, g, wr, *, seq):
    t, d = x.shape
    tm = min(TOK_TILE, seq)
    return pl.pallas_call(
        _router_kernel,
        grid=(t // tm,),
        in_specs=[
            pl.BlockSpec((tm, d), lambda i: (i, 0)),
            _const_spec((1, d)),
            _const_spec(wr.shape),
        ],
        out_specs=[
            pl.BlockSpec((tm, LANES), lambda i: (i, 0)),
            pl.BlockSpec((8, tm), lambda i: (0, i)),
            pl.BlockSpec((8, LANES), lambda i: (0, 0)),
        ],
        out_shape=[
            jax.ShapeDtypeStruct((t, LANES), F32),
            jax.ShapeDtypeStruct((8, t), F32),
            jax.ShapeDtypeStruct((8, LANES), F32),
        ],
        scratch_shapes=[pltpu.VMEM((8, LANES), F32)],
        compiler_params=_params("arbitrary"),
        name="router",
    )(x, g, wr)


def _row_copy(src, src_row, dst, dst_row, sem):
    return pltpu.make_async_copy(src.at[pl.ds(src_row, 1)], dst.at[pl.ds(dst_row, 1)], sem)


def _dispatch_kernel(pos_ref, pad_ref, x_ref, g_ref, xs_ref, h_ref, zrow_ref, sem_ref, zsem_ref):
    c = pl.program_id(0)
    n_chunks = pl.num_programs(0)
    tc = x_ref.shape[0]
    slot = c % 2

    @pl.when(c == 0)
    def _():
        zrow_ref[...] = jnp.zeros_like(zrow_ref)
        for e in range(N_EXPERTS):
            start = pad_ref[2 * e]
            count = pad_ref[2 * e + 1] - start

            def zissue(r, carry, start=start):
                _row_copy(zrow_ref, 0, xs_ref, start + r, zsem_ref).start()
                return carry

            def zwait(r, carry):
                _row_copy(zrow_ref, 0, xs_ref, 0, zsem_ref).wait()
                return carry

            lax.fori_loop(0, count, zissue, 0)
            lax.fori_loop(0, count, zwait, 0)

        tail = pad_ref[2 * N_EXPERTS - 1]
        groups = (xs_ref.shape[0] - tail) // 8

        def tissue(r, carry):
            start = pl.multiple_of(tail + 8 * r, 8)
            pltpu.make_async_copy(zrow_ref, xs_ref.at[pl.ds(start, 8)], zsem_ref).start()
            return carry

        def twait(r, carry):
            pltpu.make_async_copy(zrow_ref, xs_ref.at[pl.ds(0, 8)], zsem_ref).wait()
            return carry

        lax.fori_loop(0, groups, tissue, 0)
        lax.fori_loop(0, groups, twait, 0)

    h_ref[slot] = _rmsnorm(x_ref[...], g_ref[...])

    def issue(r, carry):
        tok = c * tc + r
        _row_copy(h_ref.at[slot], r, xs_ref, pos_ref[tok], sem_ref.at[slot]).start(priority=0)
        _row_copy(h_ref.at[slot], r, xs_ref, pos_ref[n_chunks * tc + tok],
                  sem_ref.at[slot]).start(priority=1)
        return carry

    def drain(s):
        for _ in range(2):
            pltpu.make_async_copy(h_ref.at[s], xs_ref.at[pl.ds(0, tc)], sem_ref.at[s]).wait()

    lax.fori_loop(0, tc, issue, 0, unroll=DMA_UNROLL)

    @pl.when(c > 0)
    def _():
        drain(1 - slot)

    @pl.when(c == n_chunks - 1)
    def _():
        drain(slot)


def _dispatch(pos, pad, x, g, n_rows):
    t, d = x.shape
    tc = DISPATCH_CHUNK
    grid_spec = pltpu.PrefetchScalarGridSpec(
        num_scalar_prefetch=2,
        grid=(t // tc,),
        in_specs=[
            pl.BlockSpec((tc, d), lambda c, pos, pad: (c, 0)),
            pl.BlockSpec((1, d), lambda c, pos, pad: (0, 0)),
        ],
        out_specs=pl.BlockSpec(memory_space=pl.ANY),
        scratch_shapes=[
            pltpu.VMEM((2, tc, d), F32),
            pltpu.VMEM((8, d), F32),
            pltpu.SemaphoreType.DMA((2,)),
            pltpu.SemaphoreType.DMA(()),
        ],
    )
    return pl.pallas_call(
        _dispatch_kernel,
        grid_spec=grid_spec,
        out_shape=jax.ShapeDtypeStruct((n_rows, d), F32),
        compiler_params=_params("arbitrary"),
        name="moe_dispatch",
    )(pos, pad, x, g)


def _experts_kernel(te_ref, rows_ref, nused_ref, xs_ref, wg_ref, wu_ref, wd_ref, y_ref, xb_ref):
    p = pl.program_id(0)
    f = pl.program_id(1)
    rows = rows_ref[p]
    half = MOE_TILE // 2

    @pl.when(f == 0)
    def _():
        y_ref[...] = jnp.zeros_like(y_ref)

    @pl.when(rows > 0)
    def _():
        @pl.when(f == 0)
        def _():
            xb_ref[...] = xs_ref[...].astype(BF16)

        tf = wg_ref.shape[2]
        wgu = jnp.concatenate([wg_ref[0].astype(BF16), wu_ref[0].astype(BF16)], axis=1)
        wd = wd_ref[0].astype(BF16)

        def run(r0):
            gu = _dot(xb_ref[r0:r0 + half, :], wgu)
            act = (jax.nn.silu(gu[:, :tf]) * gu[:, tf:]).astype(BF16)
            y_ref[r0:r0 + half, :] += _dot(act, wd)

        run(0)

        @pl.when(rows > half)
        def _():
            run(half)


def _experts(tile_expert, tile_rows, n_used, xs, wg, wu, wd):
    n_rows, d = xs.shape
    ff = wg.shape[2]
    tm = MOE_TILE
    tf = MOE_FF_CHUNK
    n_tiles = n_rows // tm

    def x_map(p, f, te, tr, nu):
        return (jnp.minimum(p, nu[0] - 1), 0)

    grid_spec = pltpu.PrefetchScalarGridSpec(
        num_scalar_prefetch=3,
        grid=(n_tiles, ff // tf),
        in_specs=[
            pl.BlockSpec((tm, d), x_map),
            pl.BlockSpec((1, d, tf), lambda p, f, te, tr, nu: (te[p], 0, f)),
            pl.BlockSpec((1, d, tf), lambda p, f, te, tr, nu: (te[p], 0, f)),
            pl.BlockSpec((1, tf, d), lambda p, f, te, tr, nu: (te[p], f, 0)),
        ],
        out_specs=pl.BlockSpec((tm, d), lambda p, f, te, tr, nu: (p, 0)),
        scratch_shapes=[pltpu.VMEM((tm, d), BF16)],
    )
    return pl.pallas_call(
        _experts_kernel,
        grid_spec=grid_spec,
        out_shape=jax.ShapeDtypeStruct((n_rows, d), F32),
        compiler_params=_params("arbitrary", "arbitrary"),
        name="moe_experts",
    )(tile_expert, tile_rows, n_used, xs, wg, wu, wd)


def _combine_kernel(pos_ref, y_ref, x_ref, meta_ref, g_ref, o_ref, buf_ref, sem_ref):
    i = pl.program_id(0)
    n = pl.num_programs(0)
    tc = x_ref.shape[0]

    def fetch(tile, slot):
        def body(r, carry):
            tok = tile * tc + r
            for k in range(2):
                pltpu.make_async_copy(y_ref.at[pl.ds(pos_ref[k * n * tc + tok], 1)],
                                      buf_ref.at[slot, k, pl.ds(r, 1)],
                                      sem_ref.at[slot]).start(priority=k)
            return carry
        lax.fori_loop(0, tc, body, 0, unroll=DMA_UNROLL)

    @pl.when(i == 0)
    def _():
        fetch(0, 0)

    @pl.when(i + 1 < n)
    def _():
        fetch(i + 1, (i + 1) % 2)

    slot = i % 2

    for k in range(2):
        pltpu.make_async_copy(y_ref.at[pl.ds(0, tc)], buf_ref.at[slot, k], sem_ref.at[slot]).wait()
    meta = meta_ref[...]
    g1 = meta[:, META_G1:META_G1 + 1]
    g2 = meta[:, META_G2:META_G2 + 1]
    x = x_ref[...] + g1 * buf_ref[slot, 0] + g2 * buf_ref[slot, 1]
    o_ref[...] = _rmsnorm(x, g_ref[...])


def _combine(pos, y, x, meta, g):
    t, d = x.shape
    tc = COMBINE_TILE
    grid_spec = pltpu.PrefetchScalarGridSpec(
        num_scalar_prefetch=1,
        grid=(t // tc,),
        in_specs=[
            pl.BlockSpec(memory_space=pl.ANY),
            pl.BlockSpec((tc, d), lambda i, pos: (i, 0)),
            pl.BlockSpec((tc, LANES), lambda i, pos: (i, 0)),
            pl.BlockSpec((1, d), lambda i, pos: (0, 0)),
        ],
        out_specs=pl.BlockSpec((tc, d), lambda i, pos: (i, 0)),
        scratch_shapes=[pltpu.VMEM((2, 2, tc, d), F32), pltpu.SemaphoreType.DMA((2,))],
    )
    return pl.pallas_call(
        _combine_kernel,
        grid_spec=grid_spec,
        out_shape=jax.ShapeDtypeStruct((t, d), F32),
        compiler_params=_params("arbitrary"),
        name="moe_combine",
    )(pos, y, x, meta, g)


def _moe_plan(meta_t, counts, n_tiles):
    cnt = counts[0, :N_EXPERTS].astype(I32)
    tiles = (cnt + MOE_TILE - 1) // MOE_TILE
    tile_end = jnp.cumsum(tiles)
    row_off = (tile_end - tiles) * MOE_TILE
    n_used = tile_end[-1:]
    experts = jnp.arange(N_EXPERTS, dtype=I32)

    def rows(idx_row, rank_row):
        idx = meta_t[idx_row].astype(I32)
        off = jnp.sum(jnp.where(idx[None, :] == experts[:, None], row_off[:, None], 0), axis=0)
        return off + meta_t[rank_row].astype(I32)

    pos = jnp.concatenate([rows(META_I1, META_R1), rows(META_I2, META_R2)])
    tile_ids = jnp.arange(n_tiles, dtype=I32)
    te = jnp.sum((tile_end[None, :] <= tile_ids[:, None]).astype(I32), axis=1)
    mine = te[:, None] == experts[None, :]
    first_row = tile_ids * MOE_TILE - jnp.sum(jnp.where(mine, row_off[None, :], 0), axis=1)
    tile_rows = jnp.clip(jnp.sum(jnp.where(mine, cnt[None, :], 0), axis=1) - first_row, 0, MOE_TILE)
    last = jnp.sum((tile_end <= n_used - 1).astype(I32))
    te = jnp.minimum(te, last).astype(I32)
    pad = jnp.stack([row_off + cnt, row_off + tiles * MOE_TILE], axis=1).reshape(-1).astype(I32)
    return pos.astype(I32), pad, te, tile_rows.astype(I32), n_used.astype(I32)


def _moe_and_final_norm(x, g_ffn, wr, wg, wu, wd, g_final, *, seq):
    t, d = x.shape
    n_tiles = 2 * t // MOE_TILE + N_EXPERTS
    meta, meta_t, counts = _router(x, g_ffn, wr, seq=seq)
    pos, pad, te, tile_rows, n_used = _moe_plan(meta_t, counts, n_tiles)
    xs = _dispatch(pos, pad, x, g_ffn, n_tiles * MOE_TILE)
    y = _experts(te, tile_rows, n_used, xs, wg, wu, wd)
    return _combine(pos, y, x, meta, g_final)


def _row(v):
    return v.reshape(1, -1).astype(F32)


def kernel(x, mem, norm_mix, norm_mem_q, norm_mem_kv, norm_ffn, norm_final, even_w_in, fox_b_f, gmlp_v_gain, gmlp_w_s, gmlp_b_s, even_w_out, odd_w_in, diff_lambda_q1, diff_lambda_k1, diff_lambda_q2, diff_lambda_k2, diff_subln_gain, conv_w, odd_w_out, rel_bias, mem_w_q, mem_w_kv, mem_w_o, ffn_w_gate, ffn_w_up, ffn_w_down, router_w, moe_w_gate, moe_w_up, moe_w_down):
    batch, seq, d = x.shape
    t = batch * seq
    depth = norm_mix.shape[0]
    assert depth == 2 and seq % (2 * ATT_BLOCK) == 0 and seq % TOK_TILE == 0
    xf = x.reshape(t, d)
    hd = d // N_HEADS
    aw = gmlp_v_gain.shape[1]
    n_blk = gmlp_w_s.shape[2]

    gate_cols = even_w_in.shape[2] - N_HEADS
    wf = jnp.pad(even_w_in[0, :, gate_cols:], ((0, 0), (0, LANES - N_HEADS)))
    bf = jnp.pad(fox_b_f[0], (0, LANES - N_HEADS)).reshape(1, LANES)
    bs = jnp.broadcast_to(gmlp_b_s[0][:, :, None], (A_GROUPS, n_blk, aw // A_GROUPS)).astype(F32)
    ya, qkv, aug = _even_in(xf, _row(norm_mix[0]), even_w_in, wf, bf, _row(gmlp_v_gain[0]),
                            gmlp_w_s[0], bs, seq=seq)
    yb = _fox_attention(qkv, aug, batch=batch, seq=seq)

    def mem_attention(layer, x_in, ya_, yb_, w_out):
        km, vm = _mem_kv(mem, _row(norm_mem_kv[layer]), mem_w_kv, layer)
        return _post_mixer(x_in, ya_, yb_, w_out, _row(norm_mem_q[layer]), mem_w_q, km, vm,
                           mem_w_o, layer, seq=seq)

    xf = mem_attention(0, xf, ya, yb, even_w_out)
    xf = _ffn(xf, _row(norm_ffn[0]), ffn_w_gate[0], ffn_w_up[0], ffn_w_down[0], seq=seq)

    qk_dim = LANES // 2
    qk_w = 2 * N_HEADS * qk_dim

    def regroup(cols):
        return cols.reshape(d, 2, N_HEADS, qk_dim).transpose(0, 2, 1, 3).reshape(d, qk_w)

    wqk = jnp.concatenate([regroup(odd_w_in[0, :, :qk_w]) * float(qk_dim) ** -0.5,
                           regroup(odd_w_in[0, :, qk_w:2 * qk_w])], axis=1).astype(BF16)
    qkv, yd = _odd_in(xf, _row(norm_mix[1]), wqk, odd_w_in, conv_w[0].astype(F32), seq=seq,
                      qw=2 * qk_w + N_HEADS * LANES)
    lam_init = 0.8 - 0.6 * math.exp(-0.3 * 1)
    lam = (jnp.exp(jnp.sum(diff_lambda_q1[0] * diff_lambda_k1[0]))
           - jnp.exp(jnp.sum(diff_lambda_q2[0] * diff_lambda_k2[0])) + lam_init)
    bias = _diff_bias_blocks(rel_bias, min(ATT_BLOCK, seq))
    yc = _diff_attention(qkv, lam.reshape(1, 1).astype(F32), bias, _row(diff_subln_gain[0]),
                         batch=batch, seq=seq, out_scale=1.0 - lam_init)
    xf = mem_attention(1, xf, yc, yd, odd_w_out)
    wr = jnp.pad(router_w[0], ((0, 0), (0, LANES - N_EXPERTS))).astype(F32)
    out = _moe_and_final_norm(xf, _row(norm_ffn[1]), wr, moe_w_gate[0], moe_w_up[0], moe_w_down[0],
                              _row(norm_final), seq=seq)
    return out.reshape(batch, seq, d)
```

```python
import functools
import math

import numpy as np
import jax
import jax.numpy as jnp
from jax import lax
from jax.experimental import pallas as pl
from jax.experimental.pallas import tpu as pltpu

F32 = jnp.float32
BF16 = jnp.bfloat16
I32 = jnp.int32

EPS = 1e-6
NEG = -1e30
LANES = 128
VMEM_LIMIT = 56 * 1024 * 1024

CHUNK = 64
A_GROUPS = 4
N_HEADS = 4
N_EXPERTS = 8
REL_BUCKETS = 32
REL_MAX_DIST = 128

TOK_TILE = 1024
TOK_TILE_SMALL = 512
FOX_BLOCK = 1024
ATT_BLOCK = 512
ATT_STRIP = 64
LAG_LIMIT = 40.0
MOE_TILE = 1024
MOE_FF_CHUNK = 896
FFN_CHUNK = 512
DISPATCH_CHUNK = 256
COMBINE_TILE = 256
DMA_UNROLL = 8


def _dot(a, b):
    return jnp.dot(a, b, preferred_element_type=F32)


def _dot_nt(a, b):
    return lax.dot_general(a, b, (((1,), (1,)), ((), ())), preferred_element_type=F32)


def _rmsnorm(x, g):
    return x * lax.rsqrt(jnp.mean(x * x, axis=-1, keepdims=True) + EPS) * g


def _params(*sem):
    return pltpu.CompilerParams(dimension_semantics=sem, vmem_limit_bytes=VMEM_LIMIT)


def _const_spec(shape):
    return pl.BlockSpec(shape, lambda *_: (0,) * len(shape), pipeline_mode=pl.Buffered(1))


def _layer_spec(stacked, layer):
    rest = stacked.shape[1:]
    return pl.BlockSpec((None,) + rest, lambda *_: (layer,) + (0,) * len(rest),
                        pipeline_mode=pl.Buffered(1))


def _split3(v):
    a1 = v.astype(BF16)
    r1 = v - a1.astype(F32)
    a2 = r1.astype(BF16)
    a3 = (r1 - a2.astype(F32)).astype(BF16)
    return a1, a2, a3


def _even_in_kernel(x_ref, g_ref, w_ref, wf_ref, bf_ref, vg_ref, ws_ref, bs_ref,
                    ya_ref, qkv_ref, aug_ref, carry_ref, *, tiles_per_batch):
    i = pl.program_id(0)
    tm = x_ref.shape[0]
    aw = ya_ref.shape[1]
    qw = qkv_ref.shape[1]

    @pl.when(i % tiles_per_batch == 0)
    def _():
        carry_ref[...] = jnp.zeros_like(carry_ref)

    xn = _rmsnorm(x_ref[...], g_ref[...]).astype(BF16)

    hw = N_HEADS * LANES
    q = _dot(xn, w_ref[:, 2 * aw:2 * aw + hw].astype(BF16)) * (float(LANES) ** -0.5)
    qkv_ref[:, 0:hw] = q.astype(BF16)
    qkv_ref[:, hw:qw] = _dot(xn, w_ref[:, 2 * aw + hw:2 * aw + qw].astype(BF16)).astype(BF16)

    fl = _dot(xn, wf_ref[...].astype(BF16)) + bf_ref[...]
    ls = jnp.minimum(fl, 0.0) - jnp.log1p(jnp.exp(-jnp.abs(fl)))
    row = lax.broadcasted_iota(I32, (tm, tm), 0)
    col = lax.broadcasted_iota(I32, (tm, tm), 1)
    tri = jnp.where(col <= row, 1.0, 0.0).astype(BF16)
    a1, a2, a3 = _split3(ls)
    csum = _dot(tri, a1) + _dot(tri, a2) + _dot(tri, a3)
    csum = csum + carry_ref[0:1, :]
    carry_ref[...] = jnp.broadcast_to(csum[tm - 1:tm, :], carry_ref.shape)

    lane = lax.broadcasted_iota(I32, (tm, LANES), 1)
    for h in range(N_HEADS):
        c1, c2, c3 = (piece.astype(F32)
                      for piece in _split3(jnp.broadcast_to(csum[:, h:h + 1], (tm, LANES))))
        aq = jnp.where(lane == 0, c1, jnp.where(lane == 1, c2, jnp.where(lane == 2, c3,
                       jnp.where(lane < 6, 1.0, 0.0))))
        ak = jnp.where(lane == 3, -c1, jnp.where(lane == 4, -c2, jnp.where(lane == 5, -c3,
                       jnp.where(lane < 3, 1.0, 0.0))))
        aug_ref[:, h * LANES:(h + 1) * LANES] = aq.astype(BF16)
        aug_ref[:, (N_HEADS + h) * LANES:(N_HEADS + h + 1) * LANES] = ak.astype(BF16)

    gu = jax.nn.gelu(_dot(xn, w_ref[:, 0:aw].astype(BF16)), approximate=True)
    gv = jax.nn.gelu(_dot(xn, w_ref[:, aw:2 * aw].astype(BF16)), approximate=True)
    blk = ws_ref.shape[1]
    ch = aw // A_GROUPS
    r = lax.broadcasted_iota(I32, (blk, blk), 0)
    c = lax.broadcasted_iota(I32, (blk, blk), 1)
    causal = (c // CHUNK) <= (r // CHUNK)
    for g in range(A_GROUPS):
        vgrp = gv[:, g * ch:(g + 1) * ch]
        vn = _rmsnorm(vgrp, vg_ref[:, g * ch:(g + 1) * ch]).astype(BF16)
        wmix = jnp.where(causal, ws_ref[g], 0.0).astype(BF16)
        for n in range(tm // blk):
            mixed = _dot(wmix, vn[n * blk:(n + 1) * blk, :]) + bs_ref[g]
            ya_ref[n * blk:(n + 1) * blk, g * ch:(g + 1) * ch] = (
                gu[n * blk:(n + 1) * blk, g * ch:(g + 1) * ch] * mixed).astype(BF16)


def _even_in(x, g, w_stacked, wf, bf, vgain, ws, bs, *, seq):
    t, d = x.shape
    tm = min(TOK_TILE_SMALL, seq)
    aw = vgain.shape[1]
    qw = 3 * N_HEADS * LANES
    blk = ws.shape[1]
    return pl.pallas_call(
        functools.partial(_even_in_kernel, tiles_per_batch=seq // tm),
        grid=(t // tm,),
        in_specs=[
            pl.BlockSpec((tm, d), lambda i: (i, 0)),
            _const_spec((1, d)),
            _layer_spec(w_stacked, 0),
            _const_spec(wf.shape),
            _const_spec((1, LANES)),
            _const_spec((1, aw)),
            _const_spec(ws.shape),
            _const_spec(bs.shape),
        ],
        out_specs=[
            pl.BlockSpec((tm, aw), lambda i: (i, 0)),
            pl.BlockSpec((tm, qw), lambda i: (i, 0)),
            pl.BlockSpec((tm, 2 * N_HEADS * LANES), lambda i: (i, 0)),
        ],
        out_shape=[
            jax.ShapeDtypeStruct((t, aw), BF16),
            jax.ShapeDtypeStruct((t, qw), BF16),
            jax.ShapeDtypeStruct((t, 2 * N_HEADS * LANES), BF16),
        ],
        scratch_shapes=[pltpu.VMEM((8, LANES), F32)],
        compiler_params=_params("arbitrary"),
        name="even_in",
    )(x, g, w_stacked, wf, bf, vgain, ws, bs)


def _tri_pairs(nblk):
    qi = np.array([i for i in range(nblk) for _ in range(i + 1)], np.int32)
    kj = np.array([j for i in range(nblk) for j in range(i, -1, -1)], np.int32)
    return jnp.asarray(qi), jnp.asarray(kj)


def _softmax_strips(logits, p_ref, alpha_ref, m_in, m_out, add_bias=None):
    rows, tk = logits.shape
    for r0 in range(0, rows, ATT_STRIP):
        rs = slice(r0, r0 + ATT_STRIP)
        s = logits[rs, :]
        if add_bias is not None:
            s = add_bias(s, r0)
        m_cur = jnp.max(s, axis=-1, keepdims=True)
        if m_in is None:
            m_new = jnp.broadcast_to(m_cur, (ATT_STRIP, LANES))
        else:
            m_prev = m_in[rs, :]
            m_new = jnp.maximum(m_prev, m_cur)
            alpha_ref[rs, :] = jnp.exp(m_prev - m_new)
        m_out[rs, :] = m_new
        p_ref[rs, 0:tk] = jnp.exp(
            (s - jnp.concatenate([m_new] * (tk // LANES), axis=-1)).astype(BF16))


def _softmax_strips_lagged(logits, p_ref, alpha_ref, m_in, m_out, excess):
    rows, tk = logits.shape
    for r0 in range(0, rows, ATT_STRIP):
        rs = slice(r0, r0 + ATT_STRIP)
        s = logits[rs, :]
        m_prev = m_in[rs, :]
        p_ref[rs, 0:tk] = jnp.exp(
            (s - jnp.concatenate([m_prev] * (tk // LANES), axis=-1)).astype(BF16))
        m_cur = jnp.max(s, axis=-1, keepdims=True)
        m_new = jnp.maximum(m_prev, m_cur)
        alpha_ref[rs, :] = jnp.exp(m_prev - m_new)
        m_out[rs, :] = m_new
        excess = jnp.maximum(excess, m_cur - m_prev)
    return excess


def _accumulate(p_ref, alpha_ref, v, ones, acc_in, acc_out, lagged=False):
    pv = _dot(p_ref[:, 0:v.shape[0]], jnp.concatenate([v, ones], axis=-1))
    if acc_in is None:
        acc_out[...] = pv
        return
    alpha = alpha_ref[...]
    alpha2 = jnp.concatenate([alpha, alpha], axis=-1)
    acc_out[...] = alpha2 * (acc_in[...] + pv) if lagged else alpha2 * acc_in[...] + pv


def _fox_kernel(qi_ref, kj_ref, q_ref, aq_ref, k_ref, ak_ref, v_ref, ones_ref, mask_ref, o_ref,
                p_ref, alpha_ref, m_ref, acc_ref):
    step = pl.program_id(1)
    i = qi_ref[step]
    j = kj_ref[step]
    parity = (i - j) % 2

    def logits(h):
        hs = slice(h * LANES, (h + 1) * LANES)
        qa = jnp.concatenate([q_ref[:, hs], aq_ref[:, hs]], axis=-1)
        ka = jnp.concatenate([k_ref[:, hs], ak_ref[:, hs]], axis=-1)
        return _dot_nt(qa, ka)

    def sweep(mode, src, dst):
        excess = jnp.full((ATT_STRIP, LANES), NEG, F32)
        nxt = logits(0)
        for h in range(N_HEADS):
            hs = slice(h * LANES, (h + 1) * LANES)
            sb = h % 2
            cur = nxt
            if h + 1 < N_HEADS:
                nxt = logits(h + 1)
            p_h, alpha_h = p_ref.at[sb], alpha_ref.at[sb]
            m_out, acc_out = m_ref.at[dst, h], acc_ref.at[dst, h]
            if mode != "diagonal":
                m_in, acc_in = m_ref.at[src, h], acc_ref.at[src, h]
            if mode == "diagonal":
                add_mask = lambda s, r0: s + mask_ref[r0:r0 + ATT_STRIP, :]
                _softmax_strips(cur, p_h, alpha_h, None, m_out, add_mask)
                _accumulate(p_h, alpha_h, v_ref[:, hs], ones_ref[...], None, acc_out)
            elif mode == "lagged":
                excess = _softmax_strips_lagged(cur, p_h, alpha_h, m_in, m_out, excess)
                _accumulate(p_h, alpha_h, v_ref[:, hs], ones_ref[...], acc_in, acc_out, lagged=True)
            else:
                _softmax_strips(cur, p_h, alpha_h, m_in, m_out)
                _accumulate(p_h, alpha_h, v_ref[:, hs], ones_ref[...], acc_in, acc_out)
        return jnp.max(excess)

    @pl.when(j == i)
    def _():
        sweep("diagonal", None, 1)

    for src in (0, 1):
        @pl.when((j < i) & (parity == src))
        def _(src=src):
            worst = sweep("lagged", src, 1 - src)

            @pl.when(worst > LAG_LIMIT)
            def _():
                sweep("standard", src, 1 - src)

    @pl.when(j == 0)
    def _():
        for h in range(N_HEADS):
            acc = acc_ref[1 - parity, h]
            o_ref[:, h * LANES:(h + 1) * LANES] = (
                acc[:, :LANES] / acc[:, LANES:LANES + 1]).astype(o_ref.dtype)


def _ones_column(rows):
    ones = np.zeros((rows, LANES), np.float32)
    ones[:, 0] = 1.0
    return jnp.asarray(ones, BF16)


def _fox_attention(qkv, aug, *, batch, seq):
    t = qkv.shape[0]
    w = N_HEADS * LANES
    blk = min(FOX_BLOCK, seq)
    nblk = seq // blk
    qi, kj = _tri_pairs(nblk)
    r = np.arange(blk)
    mask = jnp.asarray(np.where(r[None, :] <= r[:, None], 0.0, NEG), F32)
    grid_spec = pltpu.PrefetchScalarGridSpec(
        num_scalar_prefetch=2,
        grid=(batch, qi.shape[0]),
        in_specs=[
            pl.BlockSpec((blk, w), lambda b, s, qi, kj: (b * nblk + qi[s], 0)),
            pl.BlockSpec((blk, w), lambda b, s, qi, kj: (b * nblk + qi[s], 0)),
            pl.BlockSpec((blk, w), lambda b, s, qi, kj: (b * nblk + kj[s], 1)),
            pl.BlockSpec((blk, w), lambda b, s, qi, kj: (b * nblk + kj[s], 1)),
            pl.BlockSpec((blk, w), lambda b, s, qi, kj: (b * nblk + kj[s], 2)),
            _const_spec((blk, LANES)),
            _const_spec((blk, blk)),
        ],
        out_specs=pl.BlockSpec((blk, w), lambda b, s, qi, kj: (b * nblk + qi[s], 0)),
        scratch_shapes=[
            pltpu.VMEM((2, blk, blk), BF16),
            pltpu.VMEM((2, blk, LANES), F32),
            pltpu.VMEM((2, N_HEADS, blk, LANES), F32),
            pltpu.VMEM((2, N_HEADS, blk, 2 * LANES), F32),
        ],
    )
    return pl.pallas_call(
        _fox_kernel,
        grid_spec=grid_spec,
        out_shape=jax.ShapeDtypeStruct((t, w), BF16),
        compiler_params=_params("arbitrary", "arbitrary"),
        name="fox_attention",
    )(qi, kj, qkv, aug, qkv, aug, qkv, _ones_column(blk), mask)


def _diff_pairs(nq):
    qi = np.array([i for i in range(nq) for _ in range(i // 2 + 1)], np.int32)
    kj = np.array([j for i in range(nq) for j in range(i // 2 + 1)], np.int32)
    return jnp.asarray(qi), jnp.asarray(kj)


def _diff_kernel(qi_ref, kj_ref, lam_ref, q_ref, k_ref, v_ref, ones_ref, bias_ref, gain_ref, o_ref,
                 qs_ref, p_ref, alpha_ref, m_ref, acc_ref, *, out_scale):
    step = pl.program_id(1)
    i = qi_ref[step]
    j = kj_ref[step]
    tq = q_ref.shape[0]
    half = LANES // 2
    jd = i // 2
    odd = (i % 2) == 1

    @pl.when(j == 0)
    def _():
        m_ref[...] = jnp.full_like(m_ref, NEG)
        acc_ref[...] = jnp.zeros_like(acc_ref)
        lane = lax.broadcasted_iota(I32, (tq, LANES), 1)
        zero = jnp.zeros((tq, LANES), BF16)
        for h in range(N_HEADS):
            q = q_ref[:, h * LANES:(h + 1) * LANES]
            qs_ref[h, 0:tq, :] = jnp.where(lane < half, q, zero)
            qs_ref[h, tq:2 * tq, :] = jnp.where(lane < half, zero, q)

    def sweep(tk, add_bias):
        def logits(h):
            return _dot_nt(qs_ref[h], k_ref[0:tk, h * LANES:(h + 1) * LANES])

        nxt = logits(0)
        for h in range(N_HEADS):
            hs = slice(h * LANES, (h + 1) * LANES)
            sb = h % 2
            cur = nxt
            if h + 1 < N_HEADS:
                nxt = logits(h + 1)
            bias_fn = None if add_bias is None else functools.partial(add_bias, h)
            _softmax_strips(cur, p_ref.at[sb], alpha_ref.at[sb], m_ref.at[h], m_ref.at[h], bias_fn)
            _accumulate(p_ref.at[sb], alpha_ref.at[sb], v_ref[0:tk, hs], ones_ref[0:tk, :],
                        acc_ref.at[h], acc_ref.at[h])

    def bias_rows(h, which, r0):
        return bias_ref[h, which, pl.ds(r0 % tq, ATT_STRIP), :]

    def odd_diag(h, s, r0):
        return jnp.concatenate([s[:, :tq] + bias_rows(h, 1, r0), s[:, tq:] + bias_rows(h, 0, r0)],
                               axis=-1)

    def even_prev(h, s, r0):
        return jnp.concatenate([s[:, :tq], s[:, tq:] + bias_rows(h, 1, r0)], axis=-1)

    def even_diag(h, s, r0):
        return s + bias_rows(h, 0, r0)

    @pl.when((j < jd - 1) | ((j == jd - 1) & odd))
    def _():
        sweep(2 * tq, None)

    @pl.when((j == jd - 1) & jnp.logical_not(odd))
    def _():
        sweep(2 * tq, even_prev)

    @pl.when((j == jd) & odd)
    def _():
        sweep(2 * tq, odd_diag)

    @pl.when((j == jd) & jnp.logical_not(odd))
    def _():
        sweep(tq, even_diag)

    @pl.when(j == jd)
    def _():
        for h in range(N_HEADS):
            hs = slice(h * LANES, (h + 1) * LANES)
            acc = acc_ref[h]
            o = acc[:, :LANES] / acc[:, LANES:LANES + 1]
            o = o[0:tq, :] - lam_ref[0, 0] * o[tq:2 * tq, :]
            o_ref[:, hs] = (_rmsnorm(o, gain_ref[:, hs]) * out_scale).astype(o_ref.dtype)


def _diff_attention(qkv, lam, bias, gain, *, batch, seq, out_scale):
    t = qkv.shape[0]
    w = N_HEADS * LANES
    blk = bias.shape[2]
    kblk = 2 * blk
    nq = seq // blk
    nk = seq // kblk
    qi, kj = _diff_pairs(nq)
    grid_spec = pltpu.PrefetchScalarGridSpec(
        num_scalar_prefetch=2,
        grid=(batch, qi.shape[0]),
        in_specs=[
            pl.BlockSpec(memory_space=pltpu.SMEM),
            pl.BlockSpec((blk, w), lambda b, s, qi, kj: (b * nq + qi[s], 0)),
            pl.BlockSpec((kblk, w), lambda b, s, qi, kj: (b * nk + kj[s], 1)),
            pl.BlockSpec((kblk, w), lambda b, s, qi, kj: (b * nk + kj[s], 2)),
            pl.BlockSpec((kblk, LANES), lambda b, s, qi, kj: (0, 0)),
            pl.BlockSpec(bias.shape, lambda b, s, qi, kj: (0, 0, 0, 0)),
            pl.BlockSpec((1, w), lambda b, s, qi, kj: (0, 0)),
        ],
        out_specs=pl.BlockSpec((blk, w), lambda b, s, qi, kj: (b * nq + qi[s], 0)),
        scratch_shapes=[
            pltpu.VMEM((N_HEADS, 2 * blk, LANES), BF16),
            pltpu.VMEM((2, 2 * blk, kblk), BF16),
            pltpu.VMEM((2, 2 * blk, LANES), F32),
            pltpu.VMEM((N_HEADS, 2 * blk, LANES), F32),
            pltpu.VMEM((N_HEADS, 2 * blk, 2 * LANES), F32),
        ],
    )
    return pl.pallas_call(
        functools.partial(_diff_kernel, out_scale=out_scale),
        grid_spec=grid_spec,
        out_shape=jax.ShapeDtypeStruct((t, w), BF16),
        compiler_params=_params("arbitrary", "arbitrary"),
        name="diff_attention",
    )(qi, kj, lam, qkv, qkv, qkv, _ones_column(kblk), bias, gain)


def _rel_bucket(rel):
    n_half = REL_BUCKETS // 2
    max_exact = n_half // 2
    ret = jnp.where(rel > 0, n_half, 0)
    n = jnp.abs(rel)
    nf = jnp.maximum(n, 1).astype(F32)
    large = max_exact + (jnp.log(nf / max_exact) / math.log(REL_MAX_DIST / max_exact)
                         * (n_half - max_exact)).astype(I32)
    large = jnp.minimum(large, n_half - 1)
    return ret + jnp.where(n < max_exact, n, large)


def _rel_bias_kernel(table_ref, idx_ref, o_ref):
    h = pl.program_id(0)
    idx = idx_ref[0]
    far = table_ref[REL_BUCKETS // 2 - 1, h]
    out = jnp.full(idx.shape, NEG, F32)
    for b in range(REL_BUCKETS):
        out = jnp.where(idx == b, table_ref[b, h] - far, out)
    o_ref[0, 0] = out


def _diff_bias_blocks(rel_table, blk):
    assert blk >= REL_MAX_DIST
    r = jnp.arange(blk)[:, None]
    c = jnp.arange(blk)[None, :]
    diag = jnp.where((c // CHUNK) <= (r // CHUNK), _rel_bucket(c - r), -1)
    prev = _rel_bucket(c - r - blk)
    idx = jnp.stack([diag, prev], axis=0).astype(I32)
    n_heads = rel_table.shape[1]
    return pl.pallas_call(
        _rel_bias_kernel,
        grid=(n_heads, 2),
        in_specs=[
            pl.BlockSpec(memory_space=pltpu.SMEM),
            pl.BlockSpec((1, blk, blk), lambda h, s: (s, 0, 0)),
        ],
        out_specs=pl.BlockSpec((1, 1, blk, blk), lambda h, s: (h, s, 0, 0)),
        out_shape=jax.ShapeDtypeStruct((n_heads, 2, blk, blk), F32),
        compiler_params=_params("arbitrary", "arbitrary"),
        name="rel_bias",
    )(rel_table.astype(F32), idx)


def _odd_in_kernel(x_ref, g_ref, wqk_ref, w_ref, cw_ref, qkv_ref, yd_ref, zbuf_ref, *,
                   tiles_per_batch):
    i = pl.program_id(0)
    tm = x_ref.shape[0]
    qw = qkv_ref.shape[1]
    dw = yd_ref.shape[1]
    qkw = wqk_ref.shape[1]

    @pl.when(i % tiles_per_batch == 0)
    def _():
        zbuf_ref[tm:tm + 8, :] = jnp.zeros((8, dw), F32)

    xn = _rmsnorm(x_ref[...], g_ref[...]).astype(BF16)
    qkv_ref[:, 0:qkw] = _dot(xn, wqk_ref[...]).astype(BF16)
    qkv_ref[:, qkw:qw] = _dot(xn, w_ref[:, qkw:qw].astype(BF16)).astype(BF16)
    hh = _dot(xn, w_ref[:, qw:qw + dw].astype(BF16))
    gb = _dot(xn, w_ref[:, qw + dw:qw + 2 * dw].astype(BF16))
    gc = _dot(xn, w_ref[:, qw + 2 * dw:qw + 3 * dw].astype(BF16))
    z = gc * hh

    zbuf_ref[0:8, :] = zbuf_ref[tm:tm + 8, :]
    zbuf_ref[8:tm + 8, :] = z
    y = (cw_ref[0:1, :] * zbuf_ref[6:tm + 6, :] + cw_ref[1:2, :] * zbuf_ref[7:tm + 7, :]
         + cw_ref[2:3, :] * z)
    yd_ref[...] = (gb * y).astype(BF16)


def _odd_in(x, g, wqk, w_stacked, cw, *, seq, qw):
    t, d = x.shape
    tm = min(TOK_TILE, seq)
    dw = cw.shape[1]
    return pl.pallas_call(
        functools.partial(_odd_in_kernel, tiles_per_batch=seq // tm),
        grid=(t // tm,),
        in_specs=[
            pl.BlockSpec((tm, d), lambda i: (i, 0)),
            _const_spec((1, d)),
            _const_spec(wqk.shape),
            _layer_spec(w_stacked, 0),
            _const_spec(cw.shape),
        ],
        out_specs=[
            pl.BlockSpec((tm, qw), lambda i: (i, 0)),
            pl.BlockSpec((tm, dw), lambda i: (i, 0)),
        ],
        out_shape=[
            jax.ShapeDtypeStruct((t, qw), BF16),
            jax.ShapeDtypeStruct((t, dw), BF16),
        ],
        scratch_shapes=[pltpu.VMEM((tm + 8, dw), F32)],
        compiler_params=_params("arbitrary"),
        name="odd_in",
    )(x, g, wqk, w_stacked, cw)


def _mem_kv_kernel(mem_ref, g_ref, w_ref, k_ref, v_ref):
    d = mem_ref.shape[2]
    mn = _rmsnorm(mem_ref[0], g_ref[...]).astype(BF16)
    k_ref[0] = _dot(mn, w_ref[:, 0:d].astype(BF16)).astype(BF16)
    v_ref[0] = _dot(mn, w_ref[:, d:2 * d].astype(BF16)).astype(BF16)


def _mem_kv(mem, g, w_stacked, layer):
    b, m, d = mem.shape
    return pl.pallas_call(
        _mem_kv_kernel,
        grid=(b,),
        in_specs=[
            pl.BlockSpec((1, m, d), lambda i: (i, 0, 0)),
            _const_spec((1, d)),
            _layer_spec(w_stacked, layer),
        ],
        out_specs=[pl.BlockSpec((1, m, d), lambda i: (i, 0, 0))] * 2,
        out_shape=[jax.ShapeDtypeStruct((b, m, d), BF16)] * 2,
        compiler_params=_params("arbitrary"),
        name="mem_kv",
    )(mem, g, w_stacked)


def _post_mixer_kernel(x_ref, ya_ref, yb_ref, wout_ref, g_ref, wq_ref, k_ref, v_ref, wo_ref, o_ref):
    wa = ya_ref.shape[1]
    d = x_ref.shape[1]
    hd = d // N_HEADS
    x1 = (x_ref[...] + _dot(ya_ref[...], wout_ref[0:wa, :].astype(BF16))
          + _dot(yb_ref[...], wout_ref[wa:, :].astype(BF16)))
    xn = _rmsnorm(x1, g_ref[...]).astype(BF16)
    q = (_dot(xn, wq_ref[...].astype(BF16)) * (float(hd) ** -0.5)).astype(BF16)
    heads = []
    for h in range(N_HEADS):
        lg = _dot_nt(q[:, h * hd:(h + 1) * hd], k_ref[0, :, h * hd:(h + 1) * hd])
        p = jnp.exp(lg - jnp.max(lg, axis=-1, keepdims=True))
        p = p / jnp.sum(p, axis=-1, keepdims=True)
        heads.append(_dot(p.astype(BF16), v_ref[0, :, h * hd:(h + 1) * hd]).astype(BF16))
    o_ref[...] = x1 + _dot(jnp.concatenate(heads, axis=-1), wo_ref[...].astype(BF16))


def _post_mixer(x, ya, yb, wout_stacked, g, wq_stacked, kmem, vmem, wo_stacked, layer, *, seq):
    t, d = x.shape
    tm = min(TOK_TILE, seq)
    nt = seq // tm
    m = kmem.shape[1]
    wa = ya.shape[1]
    return pl.pallas_call(
        _post_mixer_kernel,
        grid=(t // tm,),
        in_specs=[
            pl.BlockSpec((tm, d), lambda i: (i, 0)),
            pl.BlockSpec((tm, wa), lambda i: (i, 0)),
            pl.BlockSpec((tm, yb.shape[1]), lambda i: (i, 0)),
            _layer_spec(wout_stacked, 0),
            _const_spec((1, d)),
            _layer_spec(wq_stacked, layer),
            pl.BlockSpec((1, m, d), lambda i: (i // nt, 0, 0)),
            pl.BlockSpec((1, m, d), lambda i: (i // nt, 0, 0)),
            _layer_spec(wo_stacked, layer),
        ],
        out_specs=pl.BlockSpec((tm, d), lambda i: (i, 0)),
        out_shape=jax.ShapeDtypeStruct((t, d), F32),
        compiler_params=_params("arbitrary"),
        name="post_mixer",
    )(x, ya, yb, wout_stacked, g, wq_stacked, kmem, vmem, wo_stacked)


def _ffn_kernel(x_ref, g_ref, wg_ref, wu_ref, wd_ref, o_ref):
    x = x_ref[...]
    h = _rmsnorm(x, g_ref[...]).astype(BF16)
    ff = wg_ref.shape[1]
    acc = x
    for c0 in range(0, ff, FFN_CHUNK):
        c1 = min(c0 + FFN_CHUNK, ff)
        a = _dot(h, wg_ref[:, c0:c1].astype(BF16))
        u = _dot(h, wu_ref[:, c0:c1].astype(BF16))
        acc = acc + _dot((jax.nn.silu(a) * u).astype(BF16), wd_ref[c0:c1, :].astype(BF16))
    o_ref[...] = acc


def _ffn(x, g, wg, wu, wd, *, seq):
    t, d = x.shape
    tm = min(TOK_TILE_SMALL, seq)
    return pl.pallas_call(
        _ffn_kernel,
        grid=(t // tm,),
        in_specs=[
            pl.BlockSpec((tm, d), lambda i: (i, 0)),
            _const_spec((1, d)),
            _const_spec(wg.shape),
            _const_spec(wu.shape),
            _const_spec(wd.shape),
        ],
        out_specs=pl.BlockSpec((tm, d), lambda i: (i, 0)),
        out_shape=jax.ShapeDtypeStruct((t, d), F32),
        compiler_params=_params("arbitrary"),
        name="ffn",
    )(x, g, wg, wu, wd)


META_I1, META_I2, META_R1, META_R2, META_G1, META_G2 = range(6)


def _router_kernel(x_ref, g_ref, wr_ref, meta_ref, metat_ref, cnt_ref, carry_ref):
    i = pl.program_id(0)
    tm = x_ref.shape[0]

    @pl.when(i == 0)
    def _():
        carry_ref[...] = jnp.zeros_like(carry_ref)

    h = _rmsnorm(x_ref[...], g_ref[...])
    hp = _split3(h)
    wp = _split3(wr_ref[...])
    logits = _dot(hp[1], wp[0]) + _dot(hp[0], wp[1]) + _dot(hp[0], wp[0])
    lane = lax.broadcasted_iota(I32, logits.shape, 1)
    logits = jnp.where(lane < N_EXPERTS, logits, NEG)
    m1 = jnp.max(logits, axis=-1, keepdims=True)
    i1 = jnp.min(jnp.where(logits == m1, lane, LANES), axis=-1, keepdims=True)
    rest = jnp.where(lane == i1, NEG, logits)
    m2 = jnp.max(rest, axis=-1, keepdims=True)
    i2 = jnp.min(jnp.where(rest == m2, lane, LANES), axis=-1, keepdims=True)
    e = jnp.exp(m2 - m1)
    g1 = 1.0 / (1.0 + e)
    g2 = e / (1.0 + e)

    sel = jnp.where((lane == i1) | (lane == i2), 1.0, 0.0)
    row = lax.broadcasted_iota(I32, (tm, tm), 0)
    col = lax.broadcasted_iota(I32, (tm, tm), 1)
    strict = jnp.where(col < row, 1.0, 0.0).astype(BF16)
    rank =_dot(strict, sel.astype(BF16)) + carry_ref[0:1, :]
    total = rank[tm - 1:tm, :] + sel[tm - 1:tm, :]
    carry_ref[...] = jnp.broadcast_to(total, carry_ref.shape)
    cnt_ref[...] = jnp.broadcast_to(total, cnt_ref.shape)
    r1 = jnp.sum(jnp.where(lane == i1, rank, 0.0), axis=-1, keepdims=True)
    r2 = jnp.sum(jnp.where(lane == i2, rank, 0.0), axis=-1, keepdims=True)
    meta = jnp.zeros_like(logits)
    for slot, val in ((META_I1, i1.astype(F32)), (META_I2, i2.astype(F32)), (META_R1, r1),
                      (META_R2, r2), (META_G1, g1), (META_G2, g2)):
        meta = jnp.where(lane == slot, val, meta)
    meta_ref[...] = meta
    metat_ref[...] = meta.T[0:8, :]


def _router(x, g, wr, *, seq):
    t, d = x.shape
    tm = min(TOK_TILE, seq)
    return pl.pallas_call(
        _router_kernel,
        grid=(t // tm,),
        in_specs=[
            pl.BlockSpec((tm, d), lambda i: (i, 0)),
            _const_spec((1, d)),
            _const_spec(wr.shape),
        ],
        out_specs=[
            pl.BlockSpec((tm, LANES), lambda i: (i, 0)),
            pl.BlockSpec((8, tm), lambda i: (0, i)),
            pl.BlockSpec((8, LANES), lambda i: (0, 0)),
        ],
        out_shape=[
            jax.ShapeDtypeStruct((t, LANES), F32),
            jax.ShapeDtypeStruct((8, t), F32),
            jax.ShapeDtypeStruct((8, LANES), F32),
        ],
        scratch_shapes=[pltpu.VMEM((8, LANES), F32)],
        compiler_params=_params("arbitrary"),
        name="router",
    )(x, g, wr)


def _row_copy(src, src_row, dst, dst_row, sem):
    return pltpu.make_async_copy(src.at[pl.ds(src_row, 1)], dst.at[pl.ds(dst_row, 1)], sem)


def _dispatch_kernel(pos_ref, pad_ref, x_ref, g_ref, xs_ref, h_ref, zrow_ref, sem_ref, zsem_ref):
    c = pl.program_id(0)
    n_chunks = pl.num_programs(0)
    tc = x_ref.shape[0]
    slot = c % 2

    @pl.when(c == 0)
    def _():
        zrow_ref[...] = jnp.zeros_like(zrow_ref)
        for e in range(N_EXPERTS):
            start = pad_ref[2 * e]
            count = pad_ref[2 * e + 1] - start

            def zissue(r, carry, start=start):
                _row_copy(zrow_ref, 0, xs_ref, start + r, zsem_ref).start()
                return carry

            def zwait(r, carry):
                _row_copy(zrow_ref, 0, xs_ref, 0, zsem_ref).wait()
                return carry

            lax.fori_loop(0, count, zissue, 0)
            lax.fori_loop(0, count, zwait, 0)

        tail = pad_ref[2 * N_EXPERTS - 1]
        groups = (xs_ref.shape[0] - tail) // 8

        def tissue(r, carry):
            start = pl.multiple_of(tail + 8 * r, 8)
            pltpu.make_async_copy(zrow_ref, xs_ref.at[pl.ds(start, 8)], zsem_ref).start()
            return carry

        def twait(r, carry):
            pltpu.make_async_copy(zrow_ref, xs_ref.at[pl.ds(0, 8)], zsem_ref).wait()
            return carry

        lax.fori_loop(0, groups, tissue, 0)
        lax.fori_loop(0, groups, twait, 0)

    h_ref[slot] = _rmsnorm(x_ref[...], g_ref[...])

    def issue(r, carry):
        tok = c * tc + r
        _row_copy(h_ref.at[slot], r, xs_ref, pos_ref[tok], sem_ref.at[slot]).start()
        _row_copy(h_ref.at[slot], r, xs_ref, pos_ref[n_chunks * tc + tok], sem_ref.at[slot]).start()
        return carry

    def drain(s):
        for _ in range(2):
            pltpu.make_async_copy(h_ref.at[s], xs_ref.at[pl.ds(0, tc)], sem_ref.at[s]).wait()

    lax.fori_loop(0, tc, issue, 0, unroll=DMA_UNROLL)

    @pl.when(c > 0)
    def _():
        drain(1 - slot)

    @pl.when(c == n_chunks - 1)
    def _():
        drain(slot)


def _dispatch(pos, pad, x, g, n_rows):
    t, d = x.shape
    tc = DISPATCH_CHUNK
    grid_spec = pltpu.PrefetchScalarGridSpec(
        num_scalar_prefetch=2,
        grid=(t // tc,),
        in_specs=[
            pl.BlockSpec((tc, d), lambda c, pos, pad: (c, 0)),
            pl.BlockSpec((1, d), lambda c, pos, pad: (0, 0)),
        ],
        out_specs=pl.BlockSpec(memory_space=pl.ANY),
        scratch_shapes=[
            pltpu.VMEM((2, tc, d), F32),
            pltpu.VMEM((8, d), F32),
            pltpu.SemaphoreType.DMA((2,)),
            pltpu.SemaphoreType.DMA(()),
        ],
    )
    return pl.pallas_call(
        _dispatch_kernel,
        grid_spec=grid_spec,
        out_shape=jax.ShapeDtypeStruct((n_rows, d), F32),
        compiler_params=_params("arbitrary"),
        name="moe_dispatch",
    )(pos, pad, x, g)


def _experts_kernel(te_ref, rows_ref, nused_ref, xs_ref, wg_ref, wu_ref, wd_ref, y_ref, xb_ref):
    p = pl.program_id(0)
    f = pl.program_id(1)
    rows = rows_ref[p]
    half = MOE_TILE // 2

    @pl.when(f == 0)
    def _():
        y_ref[...] = jnp.zeros_like(y_ref)

    @pl.when(rows > 0)
    def _():
        @pl.when(f == 0)
        def _():
            xb_ref[...] = xs_ref[...].astype(BF16)

        tf = wg_ref.shape[2]
        wgu = jnp.concatenate([wg_ref[0].astype(BF16), wu_ref[0].astype(BF16)], axis=1)
        wd = wd_ref[0].astype(BF16)

        def run(r0):
            gu = _dot(xb_ref[r0:r0 + half, :], wgu)
            act = (jax.nn.silu(gu[:, :tf]) * gu[:, tf:]).astype(BF16)
            y_ref[r0:r0 + half, :] += _dot(act, wd)

        run(0)

        @pl.when(rows > half)
        def _():
            run(half)


def _experts(tile_expert, tile_rows, n_used, xs, wg, wu, wd):
    n_rows, d = xs.shape
    ff = wg.shape[2]
    tm = MOE_TILE
    tf = MOE_FF_CHUNK
    n_tiles = n_rows // tm

    def x_map(p, f, te, tr, nu):
        return (jnp.minimum(p, nu[0] - 1), 0)

    grid_spec = pltpu.PrefetchScalarGridSpec(
        num_scalar_prefetch=3,
        grid=(n_tiles, ff // tf),
        in_specs=[
            pl.BlockSpec((tm, d), x_map),
            pl.BlockSpec((1, d, tf), lambda p, f, te, tr, nu: (te[p], 0, f)),
            pl.BlockSpec((1, d, tf), lambda p, f, te, tr, nu: (te[p], 0, f)),
            pl.BlockSpec((1, tf, d), lambda p, f, te, tr, nu: (te[p], f, 0)),
        ],
        out_specs=pl.BlockSpec((tm, d), lambda p, f, te, tr, nu: (p, 0)),
        scratch_shapes=[pltpu.VMEM((tm, d), BF16)],
    )
    return pl.pallas_call(
        _experts_kernel,
        grid_spec=grid_spec,
        out_shape=jax.ShapeDtypeStruct((n_rows, d), F32),
        compiler_params=_params("arbitrary", "arbitrary"),
        name="moe_experts",
    )(tile_expert, tile_rows, n_used, xs, wg, wu, wd)


def _combine_kernel(pos_ref, y_ref, x_ref, meta_ref, g_ref, o_ref, buf_ref, sem_ref):
    i = pl.program_id(0)
    n = pl.num_programs(0)
    tc = x_ref.shape[0]

    def fetch(tile, slot):
        def body(r, carry):
            tok = tile * tc + r
            for k in range(2):
                pltpu.make_async_copy(y_ref.at[pl.ds(pos_ref[k * n * tc + tok], 1)],
                                      buf_ref.at[slot, k, pl.ds(r, 1)], sem_ref.at[slot]).start()
            return carry
        lax.fori_loop(0, tc, body, 0, unroll=DMA_UNROLL)

    @pl.when(i == 0)
    def _():
        fetch(0, 0)

    @pl.when(i + 1 < n)
    def _():
        fetch(i + 1, (i + 1) % 2)

    slot = i % 2

    for k in range(2):
        pltpu.make_async_copy(y_ref.at[pl.ds(0, tc)], buf_ref.at[slot, k], sem_ref.at[slot]).wait()
    meta = meta_ref[...]
    g1 = meta[:, META_G1:META_G1 + 1]
    g2 = meta[:, META_G2:META_G2 + 1]
    x = x_ref[...] + g1 * buf_ref[slot, 0] + g2 * buf_ref[slot, 1]
    o_ref[...] = _rmsnorm(x, g_ref[...])


def _combine(pos, y, x, meta, g):
    t, d = x.shape
    tc = COMBINE_TILE
    grid_spec = pltpu.PrefetchScalarGridSpec(
        num_scalar_prefetch=1,
        grid=(t // tc,),
        in_specs=[
            pl.BlockSpec(memory_space=pl.ANY),
            pl.BlockSpec((tc, d), lambda i, pos: (i, 0)),
            pl.BlockSpec((tc, LANES), lambda i, pos: (i, 0)),
            pl.BlockSpec((1, d), lambda i, pos: (0, 0)),
        ],
        out_specs=pl.BlockSpec((tc, d), lambda i, pos: (i, 0)),
        scratch_shapes=[pltpu.VMEM((2, 2, tc, d), F32), pltpu.SemaphoreType.DMA((2,))],
    )
    return pl.pallas_call(
        _combine_kernel,
        grid_spec=grid_spec,
        out_shape=jax.ShapeDtypeStruct((t, d), F32),
        compiler_params=_params("arbitrary"),
        name="moe_combine",
    )(pos, y, x, meta, g)


def _moe_plan(meta_t, counts, n_tiles):
    cnt = counts[0, :N_EXPERTS].astype(I32)
    tiles = (cnt + MOE_TILE - 1) // MOE_TILE
    tile_end = jnp.cumsum(tiles)
    row_off = (tile_end - tiles) * MOE_TILE
    n_used = tile_end[-1:]
    experts = jnp.arange(N_EXPERTS, dtype=I32)

    def rows(idx_row, rank_row):
        idx = meta_t[idx_row].astype(I32)
        off = jnp.sum(jnp.where(idx[None, :] == experts[:, None], row_off[:, None], 0), axis=0)
        return off + meta_t[rank_row].astype(I32)

    pos = jnp.concatenate([rows(META_I1, META_R1), rows(META_I2, META_R2)])
    tile_ids = jnp.arange(n_tiles, dtype=I32)
    te = jnp.sum((tile_end[None, :] <= tile_ids[:, None]).astype(I32), axis=1)
    mine = te[:, None] == experts[None, :]
    first_row = tile_ids * MOE_TILE - jnp.sum(jnp.where(mine, row_off[None, :], 0), axis=1)
    tile_rows = jnp.clip(jnp.sum(jnp.where(mine, cnt[None, :], 0), axis=1) - first_row, 0, MOE_TILE)
    last = jnp.sum((tile_end <= n_used - 1).astype(I32))
    te = jnp.minimum(te, last).astype(I32)
    pad = jnp.stack([row_off + cnt, row_off + tiles * MOE_TILE], axis=1).reshape(-1).astype(I32)
    return pos.astype(I32), pad, te, tile_rows.astype(I32), n_used.astype(I32)


def _moe_and_final_norm(x, g_ffn, wr, wg, wu, wd, g_final, *, seq):
    t, d = x.shape
    n_tiles = 2 * t // MOE_TILE + N_EXPERTS
    meta, meta_t, counts = _router(x, g_ffn, wr, seq=seq)
    pos, pad, te, tile_rows, n_used = _moe_plan(meta_t, counts, n_tiles)
    xs = _dispatch(pos, pad, x, g_ffn, n_tiles * MOE_TILE)
    y = _experts(te, tile_rows, n_used, xs, wg, wu, wd)
    return _combine(pos, y, x, meta, g_final)


def _row(v):
    return v.reshape(1, -1).astype(F32)


def kernel(x, mem, norm_mix, norm_mem_q, norm_mem_kv, norm_ffn, norm_final, even_w_in, fox_b_f, gmlp_v_gain, gmlp_w_s, gmlp_b_s, even_w_out, odd_w_in, diff_lambda_q1, diff_lambda_k1, diff_lambda_q2, diff_lambda_k2, diff_subln_gain, conv_w, odd_w_out, rel_bias, mem_w_q, mem_w_kv, mem_w_o, ffn_w_gate, ffn_w_up, ffn_w_down, router_w, moe_w_gate, moe_w_up, moe_w_down):
    batch, seq, d = x.shape
    t = batch * seq
    depth = norm_mix.shape[0]
    assert depth == 2 and seq % (2 * ATT_BLOCK) == 0 and seq % TOK_TILE == 0
    xf = x.reshape(t, d)
    hd = d // N_HEADS
    aw = gmlp_v_gain.shape[1]
    n_blk = gmlp_w_s.shape[2]

    gate_cols = even_w_in.shape[2] - N_HEADS
    wf = jnp.pad(even_w_in[0, :, gate_cols:], ((0, 0), (0, LANES - N_HEADS)))
    bf = jnp.pad(fox_b_f[0], (0, LANES - N_HEADS)).reshape(1, LANES)
    bs = jnp.broadcast_to(gmlp_b_s[0][:, :, None], (A_GROUPS, n_blk, aw // A_GROUPS)).astype(F32)
    ya, qkv, aug = _even_in(xf, _row(norm_mix[0]), even_w_in, wf, bf, _row(gmlp_v_gain[0]),
                            gmlp_w_s[0], bs, seq=seq)
    yb = _fox_attention(qkv, aug, batch=batch, seq=seq)

    def mem_attention(layer, x_in, ya_, yb_, w_out):
        km, vm = _mem_kv(mem, _row(norm_mem_kv[layer]), mem_w_kv, layer)
        return _post_mixer(x_in, ya_, yb_, w_out, _row(norm_mem_q[layer]), mem_w_q, km, vm,
                           mem_w_o, layer, seq=seq)

    xf = mem_attention(0, xf, ya, yb, even_w_out)
    xf = _ffn(xf, _row(norm_ffn[0]), ffn_w_gate[0], ffn_w_up[0], ffn_w_down[0], seq=seq)

    qk_dim = LANES // 2
    qk_w = 2 * N_HEADS * qk_dim

    def regroup(cols):
        return cols.reshape(d, 2, N_HEADS, qk_dim).transpose(0, 2, 1, 3).reshape(d, qk_w)

    wqk = jnp.concatenate([regroup(odd_w_in[0, :, :qk_w]) * float(qk_dim) ** -0.5,
                           regroup(odd_w_in[0, :, qk_w:2 * qk_w])], axis=1).astype(BF16)
    qkv, yd = _odd_in(xf, _row(norm_mix[1]), wqk, odd_w_in, conv_w[0].astype(F32), seq=seq,
                      qw=2 * qk_w + N_HEADS * LANES)
    lam_init = 0.8 - 0.6 * math.exp(-0.3 * 1)
    lam = (jnp.exp(jnp.sum(diff_lambda_q1[0] * diff_lambda_k1[0]))
           - jnp.exp(jnp.sum(diff_lambda_q2[0] * diff_lambda_k2[0])) + lam_init)
    bias = _diff_bias_blocks(rel_bias, min(ATT_BLOCK, seq))
    yc = _diff_attention(qkv, lam.reshape(1, 1).astype(F32), bias, _row(diff_subln_gain[0]),
                         batch=batch, seq=seq, out_scale=1.0 - lam_init)
    xf = mem_attention(1, xf, yc, yd, odd_w_out)
    wr = jnp.pad(router_w[0], ((0, 0), (0, LANES - N_EXPERTS))).astype(F32)
    out = _moe_and_final_norm(xf, _row(norm_ffn[1]), wr, moe_w_gate[0], moe_w_up[0], moe_w_down[0],
                              _row(norm_final), seq=seq)
    return out.reshape(batch, seq, d)
```

```python
import functools
import math

import numpy as np
import jax
import jax.numpy as jnp
from jax import lax
from jax.experimental import pallas as pl
from jax.experimental.pallas import tpu as pltpu

F32 = jnp.float32
BF16 = jnp.bfloat16
I32 = jnp.int32

EPS = 1e-6
NEG = -1e30
LANES = 128
VMEM_LIMIT = 56 * 1024 * 1024

CHUNK = 64
A_GROUPS = 4
N_HEADS = 4
N_EXPERTS = 8
REL_BUCKETS = 32
REL_MAX_DIST = 128

TOK_TILE = 1024
TOK_TILE_SMALL = 512
FOX_BLOCK = 1024
ATT_BLOCK = 512
ATT_STRIP = 64
LAG_LIMIT = 40.0
MOE_TILE = 1024
MOE_FF_CHUNK = 896
FFN_CHUNK = 512
DISPATCH_CHUNK = 512
COMBINE_TILE = 512
DMA_UNROLL = 8


def _dot(a, b):
    return jnp.dot(a, b, preferred_element_type=F32)


def _dot_nt(a, b):
    return lax.dot_general(a, b, (((1,), (1,)), ((), ())), preferred_element_type=F32)


def _rmsnorm(x, g):
    return x * lax.rsqrt(jnp.mean(x * x, axis=-1, keepdims=True) + EPS) * g


def _params(*sem):
    return pltpu.CompilerParams(dimension_semantics=sem, vmem_limit_bytes=VMEM_LIMIT)


def _const_spec(shape):
    return pl.BlockSpec(shape, lambda *_: (0,) * len(shape), pipeline_mode=pl.Buffered(1))


def _layer_spec(stacked, layer):
    rest = stacked.shape[1:]
    return pl.BlockSpec((None,) + rest, lambda *_: (layer,) + (0,) * len(rest),
                        pipeline_mode=pl.Buffered(1))


def _split3(v):
    a1 = v.astype(BF16)
    r1 = v - a1.astype(F32)
    a2 = r1.astype(BF16)
    a3 = (r1 - a2.astype(F32)).astype(BF16)
    return a1, a2, a3


def _even_in_kernel(x_ref, g_ref, w_ref, wf_ref, bf_ref, vg_ref, ws_ref, bs_ref,
                    ya_ref, qkv_ref, aug_ref, carry_ref, *, tiles_per_batch):
    i = pl.program_id(0)
    tm = x_ref.shape[0]
    aw = ya_ref.shape[1]
    qw = qkv_ref.shape[1]

    @pl.when(i % tiles_per_batch == 0)
    def _():
        carry_ref[...] = jnp.zeros_like(carry_ref)

    xn = _rmsnorm(x_ref[...], g_ref[...]).astype(BF16)

    hw = N_HEADS * LANES
    q = _dot(xn, w_ref[:, 2 * aw:2 * aw + hw].astype(BF16)) * (float(LANES) ** -0.5)
    qkv_ref[:, 0:hw] = q.astype(BF16)
    qkv_ref[:, hw:qw] = _dot(xn, w_ref[:, 2 * aw + hw:2 * aw + qw].astype(BF16)).astype(BF16)

    fl = _dot(xn, wf_ref[...].astype(BF16)) + bf_ref[...]
    ls = jnp.minimum(fl, 0.0) - jnp.log1p(jnp.exp(-jnp.abs(fl)))
    row = lax.broadcasted_iota(I32, (tm, tm), 0)
    col = lax.broadcasted_iota(I32, (tm, tm), 1)
    tri = jnp.where(col <= row, 1.0, 0.0).astype(BF16)
    a1, a2, a3 = _split3(ls)
    csum = _dot(tri, a1) + _dot(tri, a2) + _dot(tri, a3)
    csum = csum + carry_ref[0:1, :]
    carry_ref[...] = jnp.broadcast_to(csum[tm - 1:tm, :], carry_ref.shape)

    lane = lax.broadcasted_iota(I32, (tm, LANES), 1)
    for h in range(N_HEADS):
        c1, c2, c3 = (piece.astype(F32)
                      for piece in _split3(jnp.broadcast_to(csum[:, h:h + 1], (tm, LANES))))
        aq = jnp.where(lane == 0, c1, jnp.where(lane == 1, c2, jnp.where(lane == 2, c3,
                       jnp.where(lane < 6, 1.0, 0.0))))
        ak = jnp.where(lane == 3, -c1, jnp.where(lane == 4, -c2, jnp.where(lane == 5, -c3,
                       jnp.where(lane < 3, 1.0, 0.0))))
        aug_ref[:, h * LANES:(h + 1) * LANES] = aq.astype(BF16)
        aug_ref[:, (N_HEADS + h) * LANES:(N_HEADS + h + 1) * LANES] = ak.astype(BF16)

    gu = jax.nn.gelu(_dot(xn, w_ref[:, 0:aw].astype(BF16)), approximate=True)
    gv = jax.nn.gelu(_dot(xn, w_ref[:, aw:2 * aw].astype(BF16)), approximate=True)
    blk = ws_ref.shape[1]
    ch = aw // A_GROUPS
    r = lax.broadcasted_iota(I32, (blk, blk), 0)
    c = lax.broadcasted_iota(I32, (blk, blk), 1)
    causal = (c // CHUNK) <= (r // CHUNK)
    for g in range(A_GROUPS):
        vgrp = gv[:, g * ch:(g + 1) * ch]
        vn = _rmsnorm(vgrp, vg_ref[:, g * ch:(g + 1) * ch]).astype(BF16)
        wmix = jnp.where(causal, ws_ref[g], 0.0).astype(BF16)
        for n in range(tm // blk):
            mixed = _dot(wmix, vn[n * blk:(n + 1) * blk, :]) + bs_ref[g]
            ya_ref[n * blk:(n + 1) * blk, g * ch:(g + 1) * ch] = (
                gu[n * blk:(n + 1) * blk, g * ch:(g + 1) * ch] * mixed).astype(BF16)


def _even_in(x, g, w_stacked, wf, bf, vgain, ws, bs, *, seq):
    t, d = x.shape
    tm = min(TOK_TILE_SMALL, seq)
    aw = vgain.shape[1]
    qw = 3 * N_HEADS * LANES
    blk = ws.shape[1]
    return pl.pallas_call(
        functools.partial(_even_in_kernel, tiles_per_batch=seq // tm),
        grid=(t // tm,),
        in_specs=[
            pl.BlockSpec((tm, d), lambda i: (i, 0)),
            _const_spec((1, d)),
            _layer_spec(w_stacked, 0),
            _const_spec(wf.shape),
            _const_spec((1, LANES)),
            _const_spec((1, aw)),
            _const_spec(ws.shape),
            _const_spec(bs.shape),
        ],
        out_specs=[
            pl.BlockSpec((tm, aw), lambda i: (i, 0)),
            pl.BlockSpec((tm, qw), lambda i: (i, 0)),
            pl.BlockSpec((tm, 2 * N_HEADS * LANES), lambda i: (i, 0)),
        ],
        out_shape=[
            jax.ShapeDtypeStruct((t, aw), BF16),
            jax.ShapeDtypeStruct((t, qw), BF16),
            jax.ShapeDtypeStruct((t, 2 * N_HEADS * LANES), BF16),
        ],
        scratch_shapes=[pltpu.VMEM((8, LANES), F32)],
        compiler_params=_params("arbitrary"),
        name="even_in",
    )(x, g, w_stacked, wf, bf, vgain, ws, bs)


def _tri_pairs(nblk):
    qi = np.array([i for i in range(nblk) for _ in range(i + 1)], np.int32)
    kj = np.array([j for i in range(nblk) for j in range(i, -1, -1)], np.int32)
    return jnp.asarray(qi), jnp.asarray(kj)


def _softmax_strips(logits, p_ref, alpha_ref, m_in, m_out, add_bias=None):
    rows, tk = logits.shape
    for r0 in range(0, rows, ATT_STRIP):
        rs = slice(r0, r0 + ATT_STRIP)
        s = logits[rs, :]
        if add_bias is not None:
            s = add_bias(s, r0)
        m_cur = jnp.max(s, axis=-1, keepdims=True)
        if m_in is None:
            m_new = jnp.broadcast_to(m_cur, (ATT_STRIP, LANES))
        else:
            m_prev = m_in[rs, :]
            m_new = jnp.maximum(m_prev, m_cur)
            alpha_ref[rs, :] = jnp.exp(m_prev - m_new)
        m_out[rs, :] = m_new
        p_ref[rs, 0:tk] = jnp.exp(
            (s - jnp.concatenate([m_new] * (tk // LANES), axis=-1)).astype(BF16))


def _softmax_strips_lagged(logits, p_ref, alpha_ref, m_in, m_out, excess):
    rows, tk = logits.shape
    for r0 in range(0, rows, ATT_STRIP):
        rs = slice(r0, r0 + ATT_STRIP)
        s = logits[rs, :]
        m_prev = m_in[rs, :]
        p_ref[rs, 0:tk] = jnp.exp(
            (s - jnp.concatenate([m_prev] * (tk // LANES), axis=-1)).astype(BF16))
        m_cur = jnp.max(s, axis=-1, keepdims=True)
        m_new = jnp.maximum(m_prev, m_cur)
        alpha_ref[rs, :] = jnp.exp(m_prev - m_new)
        m_out[rs, :] = m_new
        excess = jnp.maximum(excess, m_cur - m_prev)
    return excess


def _accumulate(p_ref, alpha_ref, v, ones, acc_in, acc_out, lagged=False):
    pv = _dot(p_ref[:, 0:v.shape[0]], jnp.concatenate([v, ones], axis=-1))
    if acc_in is None:
        acc_out[...] = pv
        return
    alpha = alpha_ref[...]
    alpha2 = jnp.concatenate([alpha, alpha], axis=-1)
    acc_out[...] = alpha2 * (acc_in[...] + pv) if lagged else alpha2 * acc_in[...] + pv


def _fox_kernel(qi_ref, kj_ref, q_ref, aq_ref, k_ref, ak_ref, v_ref, ones_ref, mask_ref, o_ref,
                p_ref, alpha_ref, m_ref, acc_ref):
    step = pl.program_id(1)
    i = qi_ref[step]
    j = kj_ref[step]
    parity = (i - j) % 2

    def logits(h):
        hs = slice(h * LANES, (h + 1) * LANES)
        qa = jnp.concatenate([q_ref[:, hs], aq_ref[:, hs]], axis=-1)
        ka = jnp.concatenate([k_ref[:, hs], ak_ref[:, hs]], axis=-1)
        return _dot_nt(qa, ka)

    def sweep(mode, src, dst):
        excess = jnp.full((ATT_STRIP, LANES), NEG, F32)
        nxt = logits(0)
        for h in range(N_HEADS):
            hs = slice(h * LANES, (h + 1) * LANES)
            sb = h % 2
            cur = nxt
            if h + 1 < N_HEADS:
                nxt = logits(h + 1)
            p_h, alpha_h = p_ref.at[sb], alpha_ref.at[sb]
            m_out, acc_out = m_ref.at[dst, h], acc_ref.at[dst, h]
            if mode != "diagonal":
                m_in, acc_in = m_ref.at[src, h], acc_ref.at[src, h]
            if mode == "diagonal":
                add_mask = lambda s, r0: s + mask_ref[r0:r0 + ATT_STRIP, :]
                _softmax_strips(cur, p_h, alpha_h, None, m_out, add_mask)
                _accumulate(p_h, alpha_h, v_ref[:, hs], ones_ref[...], None, acc_out)
            elif mode == "lagged":
                excess = _softmax_strips_lagged(cur, p_h, alpha_h, m_in, m_out, excess)
                _accumulate(p_h, alpha_h, v_ref[:, hs], ones_ref[...], acc_in, acc_out, lagged=True)
            else:
                _softmax_strips(cur, p_h, alpha_h, m_in, m_out)
                _accumulate(p_h, alpha_h, v_ref[:, hs], ones_ref[...], acc_in, acc_out)
        return jnp.max(excess)

    @pl.when(j == i)
    def _():
        sweep("diagonal", None, 1)

    for src in (0, 1):
        @pl.when((j < i) & (parity == src))
        def _(src=src):
            worst = sweep("lagged", src, 1 - src)

            @pl.when(worst > LAG_LIMIT)
            def _():
                sweep("standard", src, 1 - src)

    @pl.when(j == 0)
    def _():
        for h in range(N_HEADS):
            acc = acc_ref[1 - parity, h]
            o_ref[:, h * LANES:(h + 1) * LANES] = (
                acc[:, :LANES] / acc[:, LANES:LANES + 1]).astype(o_ref.dtype)


def _ones_column(rows):
    ones = np.zeros((rows, LANES), np.float32)
    ones[:, 0] = 1.0
    return jnp.asarray(ones, BF16)


def _fox_attention(qkv, aug, *, batch, seq):
    t = qkv.shape[0]
    w = N_HEADS * LANES
    blk = min(FOX_BLOCK, seq)
    nblk = seq // blk
    qi, kj = _tri_pairs(nblk)
    r = np.arange(blk)
    mask = jnp.asarray(np.where(r[None, :] <= r[:, None], 0.0, NEG), F32)
    grid_spec = pltpu.PrefetchScalarGridSpec(
        num_scalar_prefetch=2,
        grid=(batch, qi.shape[0]),
        in_specs=[
            pl.BlockSpec((blk, w), lambda b, s, qi, kj: (b * nblk + qi[s], 0)),
            pl.BlockSpec((blk, w), lambda b, s, qi, kj: (b * nblk + qi[s], 0)),
            pl.BlockSpec((blk, w), lambda b, s, qi, kj: (b * nblk + kj[s], 1)),
            pl.BlockSpec((blk, w), lambda b, s, qi, kj: (b * nblk + kj[s], 1)),
            pl.BlockSpec((blk, w), lambda b, s, qi, kj: (b * nblk + kj[s], 2)),
            _const_spec((blk, LANES)),
            _const_spec((blk, blk)),
        ],
        out_specs=pl.BlockSpec((blk, w), lambda b, s, qi, kj: (b * nblk + qi[s], 0)),
        scratch_shapes=[
            pltpu.VMEM((2, blk, blk), BF16),
            pltpu.VMEM((2, blk, LANES), F32),
            pltpu.VMEM((2, N_HEADS, blk, LANES), F32),
            pltpu.VMEM((2, N_HEADS, blk, 2 * LANES), F32),
        ],
    )
    return pl.pallas_call(
        _fox_kernel,
        grid_spec=grid_spec,
        out_shape=jax.ShapeDtypeStruct((t, w), BF16),
        compiler_params=_params("arbitrary", "arbitrary"),
        name="fox_attention",
    )(qi, kj, qkv, aug, qkv, aug, qkv, _ones_column(blk), mask)


def _diff_pairs(nq):
    qi = np.array([i for i in range(nq) for _ in range(i // 2 + 1)], np.int32)
    kj = np.array([j for i in range(nq) for j in range(i // 2 + 1)], np.int32)
    return jnp.asarray(qi), jnp.asarray(kj)


def _diff_kernel(qi_ref, kj_ref, lam_ref, q_ref, k_ref, v_ref, ones_ref, bias_ref, gain_ref, o_ref,
                 qs_ref, p_ref, alpha_ref, m_ref, acc_ref, *, out_scale):
    step = pl.program_id(1)
    i = qi_ref[step]
    j = kj_ref[step]
    tq = q_ref.shape[0]
    half = LANES // 2
    jd = i // 2
    odd = (i % 2) == 1

    @pl.when(j == 0)
    def _():
        m_ref[...] = jnp.full_like(m_ref, NEG)
        acc_ref[...] = jnp.zeros_like(acc_ref)
        lane = lax.broadcasted_iota(I32, (tq, LANES), 1)
        zero = jnp.zeros((tq, LANES), BF16)
        for h in range(N_HEADS):
            q = q_ref[:, h * LANES:(h + 1) * LANES]
            qs_ref[h, 0:tq, :] = jnp.where(lane < half, q, zero)
            qs_ref[h, tq:2 * tq, :] = jnp.where(lane < half, zero, q)

    def sweep(tk, add_bias):
        def logits(h):
            return _dot_nt(qs_ref[h], k_ref[0:tk, h * LANES:(h + 1) * LANES])

        nxt = logits(0)
        for h in range(N_HEADS):
            hs = slice(h * LANES, (h + 1) * LANES)
            sb = h % 2
            cur = nxt
            if h + 1 < N_HEADS:
                nxt = logits(h + 1)
            bias_fn = None if add_bias is None else functools.partial(add_bias, h)
            _softmax_strips(cur, p_ref.at[sb], alpha_ref.at[sb], m_ref.at[h], m_ref.at[h], bias_fn)
            _accumulate(p_ref.at[sb], alpha_ref.at[sb], v_ref[0:tk, hs], ones_ref[0:tk, :],
                        acc_ref.at[h], acc_ref.at[h])

    def bias_rows(h, which, r0):
        return bias_ref[h, which, pl.ds(r0 % tq, ATT_STRIP), :]

    def odd_diag(h, s, r0):
        return jnp.concatenate([s[:, :tq] + bias_rows(h, 1, r0), s[:, tq:] + bias_rows(h, 0, r0)],
                               axis=-1)

    def even_prev(h, s, r0):
        return jnp.concatenate([s[:, :tq], s[:, tq:] + bias_rows(h, 1, r0)], axis=-1)

    def even_diag(h, s, r0):
        return s + bias_rows(h, 0, r0)

    @pl.when((j < jd - 1) | ((j == jd - 1) & odd))
    def _():
        sweep(2 * tq, None)

    @pl.when((j == jd - 1) & jnp.logical_not(odd))
    def _():
        sweep(2 * tq, even_prev)

    @pl.when((j == jd) & odd)
    def _():
        sweep(2 * tq, odd_diag)

    @pl.when((j == jd) & jnp.logical_not(odd))
    def _():
        sweep(tq, even_diag)

    @pl.when(j == jd)
    def _():
        for h in range(N_HEADS):
            hs = slice(h * LANES, (h + 1) * LANES)
            acc = acc_ref[h]
            o = acc[:, :LANES] / acc[:, LANES:LANES + 1]
            o = o[0:tq, :] - lam_ref[0, 0] * o[tq:2 * tq, :]
            o_ref[:, hs] = (_rmsnorm(o, gain_ref[:, hs]) * out_scale).astype(o_ref.dtype)


def _diff_attention(qkv, lam, bias, gain, *, batch, seq, out_scale):
    t = qkv.shape[0]
    w = N_HEADS * LANES
    blk = bias.shape[2]
    kblk = 2 * blk
    nq = seq // blk
    nk = seq // kblk
    qi, kj = _diff_pairs(nq)
    grid_spec = pltpu.PrefetchScalarGridSpec(
        num_scalar_prefetch=2,
        grid=(batch, qi.shape[0]),
        in_specs=[
            pl.BlockSpec(memory_space=pltpu.SMEM),
            pl.BlockSpec((blk, w), lambda b, s, qi, kj: (b * nq + qi[s], 0)),
            pl.BlockSpec((kblk, w), lambda b, s, qi, kj: (b * nk + kj[s], 1)),
            pl.BlockSpec((kblk, w), lambda b, s, qi, kj: (b * nk + kj[s], 2)),
            pl.BlockSpec((kblk, LANES), lambda b, s, qi, kj: (0, 0)),
            pl.BlockSpec(bias.shape, lambda b, s, qi, kj: (0, 0, 0, 0)),
            pl.BlockSpec((1, w), lambda b, s, qi, kj: (0, 0)),
        ],
        out_specs=pl.BlockSpec((blk, w), lambda b, s, qi, kj: (b * nq + qi[s], 0)),
        scratch_shapes=[
            pltpu.VMEM((N_HEADS, 2 * blk, LANES), BF16),
            pltpu.VMEM((2, 2 * blk, kblk), BF16),
            pltpu.VMEM((2, 2 * blk, LANES), F32),
            pltpu.VMEM((N_HEADS, 2 * blk, LANES), F32),
            pltpu.VMEM((N_HEADS, 2 * blk, 2 * LANES), F32),
        ],
    )
    return pl.pallas_call(
        functools.partial(_diff_kernel, out_scale=out_scale),
        grid_spec=grid_spec,
        out_shape=jax.ShapeDtypeStruct((t, w), BF16),
        compiler_params=_params("arbitrary", "arbitrary"),
        name="diff_attention",
    )(qi, kj, lam, qkv, qkv, qkv, _ones_column(kblk), bias, gain)


def _rel_bucket(rel):
    n_half = REL_BUCKETS // 2
    max_exact = n_half // 2
    ret = jnp.where(rel > 0, n_half, 0)
    n = jnp.abs(rel)
    nf = jnp.maximum(n, 1).astype(F32)
    large = max_exact + (jnp.log(nf / max_exact) / math.log(REL_MAX_DIST / max_exact)
                         * (n_half - max_exact)).astype(I32)
    large = jnp.minimum(large, n_half - 1)
    return ret + jnp.where(n < max_exact, n, large)


def _rel_bias_kernel(table_ref, idx_ref, o_ref):
    h = pl.program_id(0)
    idx = idx_ref[0]
    far = table_ref[REL_BUCKETS // 2 - 1, h]
    out = jnp.full(idx.shape, NEG, F32)
    for b in range(REL_BUCKETS):
        out = jnp.where(idx == b, table_ref[b, h] - far, out)
    o_ref[0, 0] = out


def _diff_bias_blocks(rel_table, blk):
    assert blk >= REL_MAX_DIST
    r = jnp.arange(blk)[:, None]
    c = jnp.arange(blk)[None, :]
    diag = jnp.where((c // CHUNK) <= (r // CHUNK), _rel_bucket(c - r), -1)
    prev = _rel_bucket(c - r - blk)
    idx = jnp.stack([diag, prev], axis=0).astype(I32)
    n_heads = rel_table.shape[1]
    return pl.pallas_call(
        _rel_bias_kernel,
        grid=(n_heads, 2),
        in_specs=[
            pl.BlockSpec(memory_space=pltpu.SMEM),
            pl.BlockSpec((1, blk, blk), lambda h, s: (s, 0, 0)),
        ],
        out_specs=pl.BlockSpec((1, 1, blk, blk), lambda h, s: (h, s, 0, 0)),
        out_shape=jax.ShapeDtypeStruct((n_heads, 2, blk, blk), F32),
        compiler_params=_params("arbitrary", "arbitrary"),
        name="rel_bias",
    )(rel_table.astype(F32), idx)


def _odd_in_kernel(x_ref, g_ref, wqk_ref, w_ref, cw_ref, qkv_ref, yd_ref, zbuf_ref, *,
                   tiles_per_batch):
    i = pl.program_id(0)
    tm = x_ref.shape[0]
    qw = qkv_ref.shape[1]
    dw = yd_ref.shape[1]
    qkw = wqk_ref.shape[1]

    @pl.when(i % tiles_per_batch == 0)
    def _():
        zbuf_ref[tm:tm + 8, :] = jnp.zeros((8, dw), F32)

    xn = _rmsnorm(x_ref[...], g_ref[...]).astype(BF16)
    qkv_ref[:, 0:qkw] = _dot(xn, wqk_ref[...]).astype(BF16)
    qkv_ref[:, qkw:qw] = _dot(xn, w_ref[:, qkw:qw].astype(BF16)).astype(BF16)
    hh = _dot(xn, w_ref[:, qw:qw + dw].astype(BF16))
    gb = _dot(xn, w_ref[:, qw + dw:qw + 2 * dw].astype(BF16))
    gc = _dot(xn, w_ref[:, qw + 2 * dw:qw + 3 * dw].astype(BF16))
    z = gc * hh

    zbuf_ref[0:8, :] = zbuf_ref[tm:tm + 8, :]
    zbuf_ref[8:tm + 8, :] = z
    y = (cw_ref[0:1, :] * zbuf_ref[6:tm + 6, :] + cw_ref[1:2, :] * zbuf_ref[7:tm + 7, :]
         + cw_ref[2:3, :] * z)
    yd_ref[...] = (gb * y).astype(BF16)


def _odd_in(x, g, wqk, w_stacked, cw, *, seq, qw):
    t, d = x.shape
    tm = min(TOK_TILE, seq)
    dw = cw.shape[1]
    return pl.pallas_call(
        functools.partial(_odd_in_kernel, tiles_per_batch=seq // tm),
        grid=(t // tm,),
        in_specs=[
            pl.BlockSpec((tm, d), lambda i: (i, 0)),
            _const_spec((1, d)),
            _const_spec(wqk.shape),
            _layer_spec(w_stacked, 0),
            _const_spec(cw.shape),
        ],
        out_specs=[
            pl.BlockSpec((tm, qw), lambda i: (i, 0)),
            pl.BlockSpec((tm, dw), lambda i: (i, 0)),
        ],
        out_shape=[
            jax.ShapeDtypeStruct((t, qw), BF16),
            jax.ShapeDtypeStruct((t, dw), BF16),
        ],
        scratch_shapes=[pltpu.VMEM((tm + 8, dw), F32)],
        compiler_params=_params("arbitrary"),
        name="odd_in",
    )(x, g, wqk, w_stacked, cw)


def _mem_kv_kernel(mem_ref, g_ref, w_ref, k_ref, v_ref):
    d = mem_ref.shape[2]
    mn = _rmsnorm(mem_ref[0], g_ref[...]).astype(BF16)
    k_ref[0] = _dot(mn, w_ref[:, 0:d].astype(BF16)).astype(BF16)
    v_ref[0] = _dot(mn, w_ref[:, d:2 * d].astype(BF16)).astype(BF16)


def _mem_kv(mem, g, w_stacked, layer):
    b, m, d = mem.shape
    return pl.pallas_call(
        _mem_kv_kernel,
        grid=(b,),
        in_specs=[
            pl.BlockSpec((1, m, d), lambda i: (i, 0, 0)),
            _const_spec((1, d)),
            _layer_spec(w_stacked, layer),
        ],
        out_specs=[pl.BlockSpec((1, m, d), lambda i: (i, 0, 0))] * 2,
        out_shape=[jax.ShapeDtypeStruct((b, m, d), BF16)] * 2,
        compiler_params=_params("arbitrary"),
        name="mem_kv",
    )(mem, g, w_stacked)


def _post_mixer_kernel(x_ref, ya_ref, yb_ref, wout_ref, g_ref, wq_ref, k_ref, v_ref, wo_ref, o_ref):
    wa = ya_ref.shape[1]
    d = x_ref.shape[1]
    hd = d // N_HEADS
    x1 = (x_ref[...] + _dot(ya_ref[...], wout_ref[0:wa, :].astype(BF16))
          + _dot(yb_ref[...], wout_ref[wa:, :].astype(BF16)))
    xn = _rmsnorm(x1, g_ref[...]).astype(BF16)
    q = (_dot(xn, wq_ref[...].astype(BF16)) * (float(hd) ** -0.5)).astype(BF16)
    heads = []
    for h in range(N_HEADS):
        lg = _dot_nt(q[:, h * hd:(h + 1) * hd], k_ref[0, :, h * hd:(h + 1) * hd])
        p = jnp.exp(lg - jnp.max(lg, axis=-1, keepdims=True))
        p = p / jnp.sum(p, axis=-1, keepdims=True)
        heads.append(_dot(p.astype(BF16), v_ref[0, :, h * hd:(h + 1) * hd]).astype(BF16))
    o_ref[...] = x1 + _dot(jnp.concatenate(heads, axis=-1), wo_ref[...].astype(BF16))


def _post_mixer(x, ya, yb, wout_stacked, g, wq_stacked, kmem, vmem, wo_stacked, layer, *, seq):
    t, d = x.shape
    tm = min(TOK_TILE, seq)
    nt = seq // tm
    m = kmem.shape[1]
    wa = ya.shape[1]
    return pl.pallas_call(
        _post_mixer_kernel,
        grid=(t // tm,),
        in_specs=[
            pl.BlockSpec((tm, d), lambda i: (i, 0)),
            pl.BlockSpec((tm, wa), lambda i: (i, 0)),
            pl.BlockSpec((tm, yb.shape[1]), lambda i: (i, 0)),
            _layer_spec(wout_stacked, 0),
            _const_spec((1, d)),
            _layer_spec(wq_stacked, layer),
            pl.BlockSpec((1, m, d), lambda i: (i // nt, 0, 0)),
            pl.BlockSpec((1, m, d), lambda i: (i // nt, 0, 0)),
            _layer_spec(wo_stacked, layer),
        ],
        out_specs=pl.BlockSpec((tm, d), lambda i: (i, 0)),
        out_shape=jax.ShapeDtypeStruct((t, d), F32),
        compiler_params=_params("arbitrary"),
        name="post_mixer",
    )(x, ya, yb, wout_stacked, g, wq_stacked, kmem, vmem, wo_stacked)


def _ffn_kernel(x_ref, g_ref, wg_ref, wu_ref, wd_ref, o_ref):
    x = x_ref[...]
    h = _rmsnorm(x, g_ref[...]).astype(BF16)
    ff = wg_ref.shape[1]
    acc = x
    for c0 in range(0, ff, FFN_CHUNK):
        c1 = min(c0 + FFN_CHUNK, ff)
        a = _dot(h, wg_ref[:, c0:c1].astype(BF16))
        u = _dot(h, wu_ref[:, c0:c1].astype(BF16))
        acc = acc + _dot((jax.nn.silu(a) * u).astype(BF16), wd_ref[c0:c1, :].astype(BF16))
    o_ref[...] = acc


def _ffn(x, g, wg, wu, wd, *, seq):
    t, d = x.shape
    tm = min(TOK_TILE_SMALL, seq)
    return pl.pallas_call(
        _ffn_kernel,
        grid=(t // tm,),
        in_specs=[
            pl.BlockSpec((tm, d), lambda i: (i, 0)),
            _const_spec((1, d)),
            _const_spec(wg.shape),
            _const_spec(wu.shape),
            _const_spec(wd.shape),
        ],
        out_specs=pl.BlockSpec((tm, d), lambda i: (i, 0)),
        out_shape=jax.ShapeDtypeStruct((t, d), F32),
        compiler_params=_params("arbitrary"),
        name="ffn",
    )(x, g, wg, wu, wd)


META_I1, META_I2, META_R1, META_R2, META_G1, META_G2 = range(6)


def _router_kernel(x_ref, g_ref, wr_ref, meta_ref, metat_ref, cnt_ref, carry_ref):
    i = pl.program_id(0)
    tm = x_ref.shape[0]

    @pl.when(i == 0)
    def _():
        carry_ref[...] = jnp.zeros_like(carry_ref)

    h = _rmsnorm(x_ref[...], g_ref[...])
    hp = _split3(h)
    wp = _split3(wr_ref[...])
    logits = _dot(hp[1], wp[0]) + _dot(hp[0], wp[1]) + _dot(hp[0], wp[0])
    lane = lax.broadcasted_iota(I32, logits.shape, 1)
    logits = jnp.where(lane < N_EXPERTS, logits, NEG)
    m1 = jnp.max(logits, axis=-1, keepdims=True)
    i1 = jnp.min(jnp.where(logits == m1, lane, LANES), axis=-1, keepdims=True)
    rest = jnp.where(lane == i1, NEG, logits)
    m2 = jnp.max(rest, axis=-1, keepdims=True)
    i2 = jnp.min(jnp.where(rest == m2, lane, LANES), axis=-1, keepdims=True)
    e = jnp.exp(m2 - m1)
    g1 = 1.0 / (1.0 + e)
    g2 = e / (1.0 + e)

    sel = jnp.where((lane == i1) | (lane == i2), 1.0, 0.0)
    row = lax.broadcasted_iota(I32, (tm, tm), 0)
    col = lax.broadcasted_iota(I32, (tm, tm), 1)
    strict = jnp.where(col < row, 1.0, 0.0).astype(BF16)
    rank =_dot(strict, sel.astype(BF16)) + carry_ref[0:1, :]
    total = rank[tm - 1:tm, :] + sel[tm - 1:tm, :]
    carry_ref[...] = jnp.broadcast_to(total, carry_ref.shape)
    cnt_ref[...] = jnp.broadcast_to(total, cnt_ref.shape)
    r1 = jnp.sum(jnp.where(lane == i1, rank, 0.0), axis=-1, keepdims=True)
    r2 = jnp.sum(jnp.where(lane == i2, rank, 0.0), axis=-1, keepdims=True)
    meta = jnp.zeros_like(logits)
    for slot, val in ((META_I1, i1.astype(F32)), (META_I2, i2.astype(F32)), (META_R1, r1),
                      (META_R2, r2), (META_G1, g1), (META_G2, g2)):
        meta = jnp.where(lane == slot, val, meta)
    meta_ref[...] = meta
    metat_ref[...] = meta.T[0:8, :]


def _router(x, g, wr, *, seq):
    t, d = x.shape
    tm = min(TOK_TILE, seq)
    return pl.pallas_call(
        _router_kernel,
        grid=(t // tm,),
        in_specs=[
            pl.BlockSpec((tm, d), lambda i: (i, 0)),
            _const_spec((1, d)),
            _const_spec(wr.shape),
        ],
        out_specs=[
            pl.BlockSpec((tm, LANES), lambda i: (i, 0)),
            pl.BlockSpec((8, tm), lambda i: (0, i)),
            pl.BlockSpec((8, LANES), lambda i: (0, 0)),
        ],
        out_shape=[
            jax.ShapeDtypeStruct((t, LANES), F32),
            jax.ShapeDtypeStruct((8, t), F32),
            jax.ShapeDtypeStruct((8, LANES), F32),
        ],
        scratch_shapes=[pltpu.VMEM((8, LANES), F32)],
        compiler_params=_params("arbitrary"),
        name="router",
    )(x, g, wr)


def _row_copy(src, src_row, dst, dst_row, sem):
    return pltpu.make_async_copy(src.at[pl.ds(src_row, 1)], dst.at[pl.ds(dst_row, 1)], sem)


def _dispatch_kernel(pos_ref, pad_ref, x_ref, g_ref, xs_ref, h_ref, zrow_ref, sem_ref, zsem_ref):
    c = pl.program_id(0)
    n_chunks = pl.num_programs(0)
    tc = x_ref.shape[0]
    slot = c % 2

    @pl.when(c == 0)
    def _():
        zrow_ref[...] = jnp.zeros_like(zrow_ref)
        for e in range(N_EXPERTS):
            start = pad_ref[2 * e]
            count = pad_ref[2 * e + 1] - start

            def zissue(r, carry, start=start):
                _row_copy(zrow_ref, 0, xs_ref, start + r, zsem_ref).start()
                return carry

            def zwait(r, carry):
                _row_copy(zrow_ref, 0, xs_ref, 0, zsem_ref).wait()
                return carry

            lax.fori_loop(0, count, zissue, 0)
            lax.fori_loop(0, count, zwait, 0)

        tail = pad_ref[2 * N_EXPERTS - 1]
        groups = (xs_ref.shape[0] - tail) // 8

        def tissue(r, carry):
            start = pl.multiple_of(tail + 8 * r, 8)
            pltpu.make_async_copy(zrow_ref, xs_ref.at[pl.ds(start, 8)], zsem_ref).start()
            return carry

        def twait(r, carry):
            pltpu.make_async_copy(zrow_ref, xs_ref.at[pl.ds(0, 8)], zsem_ref).wait()
            return carry

        lax.fori_loop(0, groups, tissue, 0)
        lax.fori_loop(0, groups, twait, 0)

    h_ref[slot] = _rmsnorm(x_ref[...], g_ref[...])

    def issue(r, carry):
        tok = c * tc + r
        _row_copy(h_ref.at[slot], r, xs_ref, pos_ref[tok], sem_ref.at[slot]).start()
        _row_copy(h_ref.at[slot], r, xs_ref, pos_ref[n_chunks * tc + tok], sem_ref.at[slot]).start()
        return carry

    def drain(s):
        for _ in range(2):
            pltpu.make_async_copy(h_ref.at[s], xs_ref.at[pl.ds(0, tc)], sem_ref.at[s]).wait()

    lax.fori_loop(0, tc, issue, 0, unroll=DMA_UNROLL)

    @pl.when(c > 0)
    def _():
        drain(1 - slot)

    @pl.when(c == n_chunks - 1)
    def _():
        drain(slot)


def _dispatch(pos, pad, x, g, n_rows):
    t, d = x.shape
    tc = DISPATCH_CHUNK
    grid_spec = pltpu.PrefetchScalarGridSpec(
        num_scalar_prefetch=2,
        grid=(t // tc,),
        in_specs=[
            pl.BlockSpec((tc, d), lambda c, pos, pad: (c, 0)),
            pl.BlockSpec((1, d), lambda c, pos, pad: (0, 0)),
        ],
        out_specs=pl.BlockSpec(memory_space=pl.ANY),
        scratch_shapes=[
            pltpu.VMEM((2, tc, d), F32),
            pltpu.VMEM((8, d), F32),
            pltpu.SemaphoreType.DMA((2,)),
            pltpu.SemaphoreType.DMA(()),
        ],
    )
    return pl.pallas_call(
        _dispatch_kernel,
        grid_spec=grid_spec,
        out_shape=jax.ShapeDtypeStruct((n_rows, d), F32),
        compiler_params=_params("arbitrary"),
        name="moe_dispatch",
    )(pos, pad, x, g)


def _experts_kernel(te_ref, rows_ref, nused_ref, xs_ref, wg_ref, wu_ref, wd_ref, y_ref, xb_ref):
    p = pl.program_id(0)
    f = pl.program_id(1)
    rows = rows_ref[p]
    half = MOE_TILE // 2

    @pl.when(f == 0)
    def _():
        y_ref[...] = jnp.zeros_like(y_ref)

    @pl.when(rows > 0)
    def _():
        @pl.when(f == 0)
        def _():
            xb_ref[...] = xs_ref[...].astype(BF16)

        tf = wg_ref.shape[2]
        wgu = jnp.concatenate([wg_ref[0].astype(BF16), wu_ref[0].astype(BF16)], axis=1)
        wd = wd_ref[0].astype(BF16)

        def run(r0):
            gu = _dot(xb_ref[r0:r0 + half, :], wgu)
            act = (jax.nn.silu(gu[:, :tf]) * gu[:, tf:]).astype(BF16)
            y_ref[r0:r0 + half, :] += _dot(act, wd)

        run(0)

        @pl.when(rows > half)
        def _():
            run(half)


def _experts(tile_expert, tile_rows, n_used, xs, wg, wu, wd):
    n_rows, d = xs.shape
    ff = wg.shape[2]
    tm = MOE_TILE
    tf = MOE_FF_CHUNK
    n_tiles = n_rows // tm

    def x_map(p, f, te, tr, nu):
        return (jnp.minimum(p, nu[0] - 1), 0)

    grid_spec = pltpu.PrefetchScalarGridSpec(
        num_scalar_prefetch=3,
        grid=(n_tiles, ff // tf),
        in_specs=[
            pl.BlockSpec((tm, d), x_map),
            pl.BlockSpec((1, d, tf), lambda p, f, te, tr, nu: (te[p], 0, f)),
            pl.BlockSpec((1, d, tf), lambda p, f, te, tr, nu: (te[p], 0, f)),
            pl.BlockSpec((1, tf, d), lambda p, f, te, tr, nu: (te[p], f, 0)),
        ],
        out_specs=pl.BlockSpec((tm, d), lambda p, f, te, tr, nu: (p, 0)),
        scratch_shapes=[pltpu.VMEM((tm, d), BF16)],
    )
    return pl.pallas_call(
        _experts_kernel,
        grid_spec=grid_spec,
        out_shape=jax.ShapeDtypeStruct((n_rows, d), F32),
        compiler_params=_params("arbitrary", "arbitrary"),
        name="moe_experts",
    )(tile_expert, tile_rows, n_used, xs, wg, wu, wd)


def _combine_kernel(pos_ref, y_ref, x_ref, meta_ref, g_ref, o_ref, buf_ref, sem_ref):
    i = pl.program_id(0)
    n = pl.num_programs(0)
    tc = x_ref.shape[0]

    def fetch(tile, slot):
        def body(r, carry):
            tok = tile * tc + r
            for k in range(2):
                pltpu.make_async_copy(y_ref.at[pl.ds(pos_ref[k * n * tc + tok], 1)],
                                      buf_ref.at[slot, k, pl.ds(r, 1)], sem_ref.at[slot]).start()
            return carry
        lax.fori_loop(0, tc, body, 0, unroll=DMA_UNROLL)

    @pl.when(i == 0)
    def _():
        fetch(0, 0)

    @pl.when(i + 1 < n)
    def _():
        fetch(i + 1, (i + 1) % 2)

    slot = i % 2

    for k in range(2):
        pltpu.make_async_copy(y_ref.at[pl.ds(0, tc)], buf_ref.at[slot, k], sem_ref.at[slot]).wait()
    meta = meta_ref[...]
    g1 = meta[:, META_G1:META_G1 + 1]
    g2 = meta[:, META_G2:META_G2 + 1]
    x = x_ref[...] + g1 * buf_ref[slot, 0] + g2 * buf_ref[slot, 1]
    o_ref[...] = _rmsnorm(x, g_ref[...])


def _combine(pos, y, x, meta, g):
    t, d = x.shape
    tc = COMBINE_TILE
    grid_spec = pltpu.PrefetchScalarGridSpec(
        num_scalar_prefetch=1,
        grid=(t // tc,),
        in_specs=[
            pl.BlockSpec(memory_space=pl.ANY),
            pl.BlockSpec((tc, d), lambda i, pos: (i, 0)),
            pl.BlockSpec((tc, LANES), lambda i, pos: (i, 0)),
            pl.BlockSpec((1, d), lambda i, pos: (0, 0)),
        ],
        out_specs=pl.BlockSpec((tc, d), lambda i, pos: (i, 0)),
        scratch_shapes=[pltpu.VMEM((2, 2, tc, d), F32), pltpu.SemaphoreType.DMA((2,))],
    )
    return pl.pallas_call(
        _combine_kernel,
        grid_spec=grid_spec,
        out_shape=jax.ShapeDtypeStruct((t, d), F32),
        compiler_params=_params("arbitrary"),
        name="moe_combine",
    )(pos, y, x, meta, g)


def _moe_plan(meta_t, counts, n_tiles):
    cnt = counts[0, :N_EXPERTS].astype(I32)
    tiles = (cnt + MOE_TILE - 1) // MOE_TILE
    tile_end = jnp.cumsum(tiles)
    row_off = (tile_end - tiles) * MOE_TILE
    n_used = tile_end[-1:]
    experts = jnp.arange(N_EXPERTS, dtype=I32)

    def rows(idx_row, rank_row):
        idx = meta_t[idx_row].astype(I32)
        off = jnp.sum(jnp.where(idx[None, :] == experts[:, None], row_off[:, None], 0), axis=0)
        return off + meta_t[rank_row].astype(I32)

    pos = jnp.concatenate([rows(META_I1, META_R1), rows(META_I2, META_R2)])
    tile_ids = jnp.arange(n_tiles, dtype=I32)
    te = jnp.sum((tile_end[None, :] <= tile_ids[:, None]).astype(I32), axis=1)
    mine = te[:, None] == experts[None, :]
    first_row = tile_ids * MOE_TILE - jnp.sum(jnp.where(mine, row_off[None, :], 0), axis=1)
    tile_rows = jnp.clip(jnp.sum(jnp.where(mine, cnt[None, :], 0), axis=1) - first_row, 0, MOE_TILE)
    last = jnp.sum((tile_end <= n_used - 1).astype(I32))
    te = jnp.minimum(te, last).astype(I32)
    pad = jnp.stack([row_off + cnt, row_off + tiles * MOE_TILE], axis=1).reshape(-1).astype(I32)
    return pos.astype(I32), pad, te, tile_rows.astype(I32), n_used.astype(I32)


def _moe_and_final_norm(x, g_ffn, wr, wg, wu, wd, g_final, *, seq):
    t, d = x.shape
    n_tiles = 2 * t // MOE_TILE + N_EXPERTS
    meta, meta_t, counts = _router(x, g_ffn, wr, seq=seq)
    pos, pad, te, tile_rows, n_used = _moe_plan(meta_t, counts, n_tiles)
    xs = _dispatch(pos, pad, x, g_ffn, n_tiles * MOE_TILE)
    y = _experts(te, tile_rows, n_used, xs, wg, wu, wd)
    return _combine(pos, y, x, meta, g_final)


def _row(v):
    return v.reshape(1, -1).astype(F32)


def kernel(x, mem, norm_mix, norm_mem_q, norm_mem_kv, norm_ffn, norm_final, even_w_in, fox_b_f, gmlp_v_gain, gmlp_w_s, gmlp_b_s, even_w_out, odd_w_in, diff_lambda_q1, diff_lambda_k1, diff_lambda_q2, diff_lambda_k2, diff_subln_gain, conv_w, odd_w_out, rel_bias, mem_w_q, mem_w_kv, mem_w_o, ffn_w_gate, ffn_w_up, ffn_w_down, router_w, moe_w_gate, moe_w_up, moe_w_down):
    batch, seq, d = x.shape
    t = batch * seq
    depth = norm_mix.shape[0]
    assert depth == 2 and seq % (2 * ATT_BLOCK) == 0 and seq % TOK_TILE == 0
    xf = x.reshape(t, d)
    hd = d // N_HEADS
    aw = gmlp_v_gain.shape[1]
    n_blk = gmlp_w_s.shape[2]

    gate_cols = even_w_in.shape[2] - N_HEADS
    wf = jnp.pad(even_w_in[0, :, gate_cols:], ((0, 0), (0, LANES - N_HEADS)))
    bf = jnp.pad(fox_b_f[0], (0, LANES - N_HEADS)).reshape(1, LANES)
    bs = jnp.broadcast_to(gmlp_b_s[0][:, :, None], (A_GROUPS, n_blk, aw // A_GROUPS)).astype(F32)
    ya, qkv, aug = _even_in(xf, _row(norm_mix[0]), even_w_in, wf, bf, _row(gmlp_v_gain[0]),
                            gmlp_w_s[0], bs, seq=seq)
    yb = _fox_attention(qkv, aug, batch=batch, seq=seq)

    def mem_attention(layer, x_in, ya_, yb_, w_out):
        km, vm = _mem_kv(mem, _row(norm_mem_kv[layer]), mem_w_kv, layer)
        return _post_mixer(x_in, ya_, yb_, w_out, _row(norm_mem_q[layer]), mem_w_q, km, vm,
                           mem_w_o, layer, seq=seq)

    xf = mem_attention(0, xf, ya, yb, even_w_out)
    xf = _ffn(xf, _row(norm_ffn[0]), ffn_w_gate[0], ffn_w_up[0], ffn_w_down[0], seq=seq)

    qk_dim = LANES // 2
    qk_w = 2 * N_HEADS * qk_dim

    def regroup(cols):
        return cols.reshape(d, 2, N_HEADS, qk_dim).transpose(0, 2, 1, 3).reshape(d, qk_w)

    wqk = jnp.concatenate([regroup(odd_w_in[0, :, :qk_w]) * float(qk_dim) ** -0.5,
                           regroup(odd_w_in[0, :, qk_w:2 * qk_w])], axis=1).astype(BF16)
    qkv, yd = _odd_in(xf, _row(norm_mix[1]), wqk, odd_w_in, conv_w[0].astype(F32), seq=seq,
                      qw=2 * qk_w + N_HEADS * LANES)
    lam_init = 0.8 - 0.6 * math.exp(-0.3 * 1)
    lam = (jnp.exp(jnp.sum(diff_lambda_q1[0] * diff_lambda_k1[0]))
           - jnp.exp(jnp.sum(diff_lambda_q2[0] * diff_lambda_k2[0])) + lam_init)
    bias = _diff_bias_blocks(rel_bias, min(ATT_BLOCK, seq))
    yc = _diff_attention(qkv, lam.reshape(1, 1).astype(F32), bias, _row(diff_subln_gain[0]),
                         batch=batch, seq=seq, out_scale=1.0 - lam_init)
    xf = mem_attention(1, xf, yc, yd, odd_w_out)
    wr = jnp.pad(router_w[0], ((0, 0), (0, LANES - N_EXPERTS))).astype(F32)
    out = _moe_and_final_norm(xf, _row(norm_ffn[1]), wr, moe_w_gate[0], moe_w_up[0], moe_w_down[0],
                              _row(norm_final), seq=seq)
    return out.reshape(batch, seq, d)
```

```python
import functools
import math

import numpy as np
import jax
import jax.numpy as jnp
from jax import lax
from jax.experimental import pallas as pl
from jax.experimental.pallas import tpu as pltpu

F32 = jnp.float32
BF16 = jnp.bfloat16
I32 = jnp.int32

EPS = 1e-6
NEG = -1e30
LANES = 128
VMEM_LIMIT = 56 * 1024 * 1024

CHUNK = 64
A_GROUPS = 4
N_HEADS = 4
N_EXPERTS = 8
REL_BUCKETS = 32
REL_MAX_DIST = 128

TOK_TILE = 1024
TOK_TILE_SMALL = 512
FOX_BLOCK = 1024
ATT_BLOCK = 512
ATT_STRIP = 64
LAG_LIMIT = 40.0
MOE_TILE = 1024
MOE_FF_CHUNK = 896
FFN_CHUNK = 512
DISPATCH_CHUNK = 256
COMBINE_TILE = 256
DMA_UNROLL = 8


def _dot(a, b):
    return jnp.dot(a, b, preferred_element_type=F32)


def _dot_nt(a, b):
    return lax.dot_general(a, b, (((1,), (1,)), ((), ())), preferred_element_type=F32)


def _rmsnorm(x, g):
    return x * lax.rsqrt(jnp.mean(x * x, axis=-1, keepdims=True) + EPS) * g


def _params(*sem):
    return pltpu.CompilerParams(dimension_semantics=sem, vmem_limit_bytes=VMEM_LIMIT)


def _const_spec(shape):
    return pl.BlockSpec(shape, lambda *_: (0,) * len(shape), pipeline_mode=pl.Buffered(1))


def _layer_spec(stacked, layer):
    rest = stacked.shape[1:]
    return pl.BlockSpec((None,) + rest, lambda *_: (layer,) + (0,) * len(rest),
                        pipeline_mode=pl.Buffered(1))


def _split3(v):
    a1 = v.astype(BF16)
    r1 = v - a1.astype(F32)
    a2 = r1.astype(BF16)
    a3 = (r1 - a2.astype(F32)).astype(BF16)
    return a1, a2, a3


def _even_in_kernel(x_ref, g_ref, w_ref, wf_ref, bf_ref, vg_ref, ws_ref, bs_ref,
                    ya_ref, qkv_ref, aug_ref, carry_ref, *, tiles_per_batch):
    i = pl.program_id(0)
    tm = x_ref.shape[0]
    aw = ya_ref.shape[1]
    qw = qkv_ref.shape[1]

    @pl.when(i % tiles_per_batch == 0)
    def _():
        carry_ref[...] = jnp.zeros_like(carry_ref)

    xn = _rmsnorm(x_ref[...], g_ref[...]).astype(BF16)

    hw = N_HEADS * LANES
    q = _dot(xn, w_ref[:, 2 * aw:2 * aw + hw].astype(BF16)) * (float(LANES) ** -0.5)
    qkv_ref[:, 0:hw] = q.astype(BF16)
    qkv_ref[:, hw:qw] = _dot(xn, w_ref[:, 2 * aw + hw:2 * aw + qw].astype(BF16)).astype(BF16)

    fl = _dot(xn, wf_ref[...].astype(BF16)) + bf_ref[...]
    ls = jnp.minimum(fl, 0.0) - jnp.log1p(jnp.exp(-jnp.abs(fl)))
    row = lax.broadcasted_iota(I32, (tm, tm), 0)
    col = lax.broadcasted_iota(I32, (tm, tm), 1)
    tri = jnp.where(col <= row, 1.0, 0.0).astype(BF16)
    a1, a2, a3 = _split3(ls)
    csum = _dot(tri, a1) + _dot(tri, a2) + _dot(tri, a3)
    csum = csum + carry_ref[0:1, :]
    carry_ref[...] = jnp.broadcast_to(csum[tm - 1:tm, :], carry_ref.shape)

    lane = lax.broadcasted_iota(I32, (tm, LANES), 1)
    for h in range(N_HEADS):
        c1, c2, c3 = (piece.astype(F32)
                      for piece in _split3(jnp.broadcast_to(csum[:, h:h + 1], (tm, LANES))))
        aq = jnp.where(lane == 0, c1, jnp.where(lane == 1, c2, jnp.where(lane == 2, c3,
                       jnp.where(lane < 6, 1.0, 0.0))))
        ak = jnp.where(lane == 3, -c1, jnp.where(lane == 4, -c2, jnp.where(lane == 5, -c3,
                       jnp.where(lane < 3, 1.0, 0.0))))
        aug_ref[:, h * LANES:(h + 1) * LANES] = aq.astype(BF16)
        aug_ref[:, (N_HEADS + h) * LANES:(N_HEADS + h + 1) * LANES] = ak.astype(BF16)

    gu = jax.nn.gelu(_dot(xn, w_ref[:, 0:aw].astype(BF16)), approximate=True)
    gv = jax.nn.gelu(_dot(xn, w_ref[:, aw:2 * aw].astype(BF16)), approximate=True)
    blk = ws_ref.shape[1]
    ch = aw // A_GROUPS
    r = lax.broadcasted_iota(I32, (blk, blk), 0)
    c = lax.broadcasted_iota(I32, (blk, blk), 1)
    causal = (c // CHUNK) <= (r // CHUNK)
    for g in range(A_GROUPS):
        vgrp = gv[:, g * ch:(g + 1) * ch]
        vn = _rmsnorm(vgrp, vg_ref[:, g * ch:(g + 1) * ch]).astype(BF16)
        wmix = jnp.where(causal, ws_ref[g], 0.0).astype(BF16)
        for n in range(tm // blk):
            mixed = _dot(wmix, vn[n * blk:(n + 1) * blk, :]) + bs_ref[g]
            ya_ref[n * blk:(n + 1) * blk, g * ch:(g + 1) * ch] = (
                gu[n * blk:(n + 1) * blk, g * ch:(g + 1) * ch] * mixed).astype(BF16)


def _even_in(x, g, w_stacked, wf, bf, vgain, ws, bs, *, seq):
    t, d = x.shape
    tm = min(TOK_TILE_SMALL, seq)
    aw = vgain.shape[1]
    qw = 3 * N_HEADS * LANES
    blk = ws.shape[1]
    return pl.pallas_call(
        functools.partial(_even_in_kernel, tiles_per_batch=seq // tm),
        grid=(t // tm,),
        in_specs=[
            pl.BlockSpec((tm, d), lambda i: (i, 0)),
            _const_spec((1, d)),
            _layer_spec(w_stacked, 0),
            _const_spec(wf.shape),
            _const_spec((1, LANES)),
            _const_spec((1, aw)),
            _const_spec(ws.shape),
            _const_spec(bs.shape),
        ],
        out_specs=[
            pl.BlockSpec((tm, aw), lambda i: (i, 0)),
            pl.BlockSpec((tm, qw), lambda i: (i, 0)),
            pl.BlockSpec((tm, 2 * N_HEADS * LANES), lambda i: (i, 0)),
        ],
        out_shape=[
            jax.ShapeDtypeStruct((t, aw), BF16),
            jax.ShapeDtypeStruct((t, qw), BF16),
            jax.ShapeDtypeStruct((t, 2 * N_HEADS * LANES), BF16),
        ],
        scratch_shapes=[pltpu.VMEM((8, LANES), F32)],
        compiler_params=_params("arbitrary"),
        name="even_in",
    )(x, g, w_stacked, wf, bf, vgain, ws, bs)


def _tri_pairs(nblk):
    qi = np.array([i for i in range(nblk) for _ in range(i + 1)], np.int32)
    kj = np.array([j for i in range(nblk) for j in range(i, -1, -1)], np.int32)
    return jnp.asarray(qi), jnp.asarray(kj)


def _softmax_strips(logits, p_ref, alpha_ref, m_in, m_out, add_bias=None):
    rows, tk = logits.shape
    for r0 in range(0, rows, ATT_STRIP):
        rs = slice(r0, r0 + ATT_STRIP)
        s = logits[rs, :]
        if add_bias is not None:
            s = add_bias(s, r0)
        m_cur = jnp.max(s, axis=-1, keepdims=True)
        if m_in is None:
            m_new = jnp.broadcast_to(m_cur, (ATT_STRIP, LANES))
        else:
            m_prev = m_in[rs, :]
            m_new = jnp.maximum(m_prev, m_cur)
            alpha_ref[rs, :] = jnp.exp(m_prev - m_new)
        m_out[rs, :] = m_new
        p_ref[rs, 0:tk] = jnp.exp(
            (s - jnp.concatenate([m_new] * (tk // LANES), axis=-1)).astype(BF16))


def _softmax_strips_lagged(logits, p_ref, alpha_ref, m_in, m_out, excess):
    rows, tk = logits.shape
    for r0 in range(0, rows, ATT_STRIP):
        rs = slice(r0, r0 + ATT_STRIP)
        s = logits[rs, :]
        m_prev = m_in[rs, :]
        p_ref[rs, 0:tk] = jnp.exp(
            (s - jnp.concatenate([m_prev] * (tk // LANES), axis=-1)).astype(BF16))
        m_cur = jnp.max(s, axis=-1, keepdims=True)
        m_new = jnp.maximum(m_prev, m_cur)
        alpha_ref[rs, :] = jnp.exp(m_prev - m_new)
        m_out[rs, :] = m_new
        excess = jnp.maximum(excess, m_cur - m_prev)
    return excess


def _accumulate(p_ref, alpha_ref, v, ones, acc_in, acc_out, lagged=False):
    pv = _dot(p_ref[:, 0:v.shape[0]], jnp.concatenate([v, ones], axis=-1))
    if acc_in is None:
        acc_out[...] = pv
        return
    alpha = alpha_ref[...]
    alpha2 = jnp.concatenate([alpha, alpha], axis=-1)
    acc_out[...] = alpha2 * (acc_in[...] + pv) if lagged else alpha2 * acc_in[...] + pv


def _fox_kernel(qi_ref, kj_ref, q_ref, aq_ref, k_ref, ak_ref, v_ref, ones_ref, mask_ref, o_ref,
                p_ref, alpha_ref, m_ref, acc_ref):
    step = pl.program_id(1)
    i = qi_ref[step]
    j = kj_ref[step]
    parity = (i - j) % 2

    def logits(h):
        hs = slice(h * LANES, (h + 1) * LANES)
        qa = jnp.concatenate([q_ref[:, hs], aq_ref[:, hs]], axis=-1)
        ka = jnp.concatenate([k_ref[:, hs], ak_ref[:, hs]], axis=-1)
        return _dot_nt(qa, ka)

    def sweep(mode, src, dst):
        excess = jnp.full((ATT_STRIP, LANES), NEG, F32)
        nxt = logits(0)
        for h in range(N_HEADS):
            hs = slice(h * LANES, (h + 1) * LANES)
            sb = h % 2
            cur = nxt
            if h + 1 < N_HEADS:
                nxt = logits(h + 1)
            p_h, alpha_h = p_ref.at[sb], alpha_ref.at[sb]
            m_out, acc_out = m_ref.at[dst, h], acc_ref.at[dst, h]
            if mode != "diagonal":
                m_in, acc_in = m_ref.at[src, h], acc_ref.at[src, h]
            if mode == "diagonal":
                add_mask = lambda s, r0: s + mask_ref[r0:r0 + ATT_STRIP, :]
                _softmax_strips(cur, p_h, alpha_h, None, m_out, add_mask)
                _accumulate(p_h, alpha_h, v_ref[:, hs], ones_ref[...], None, acc_out)
            elif mode == "lagged":
                excess = _softmax_strips_lagged(cur, p_h, alpha_h, m_in, m_out, excess)
                _accumulate(p_h, alpha_h, v_ref[:, hs], ones_ref[...], acc_in, acc_out, lagged=True)
            else:
                _softmax_strips(cur, p_h, alpha_h, m_in, m_out)
                _accumulate(p_h, alpha_h, v_ref[:, hs], ones_ref[...], acc_in, acc_out)
        return jnp.max(excess)

    @pl.when(j == i)
    def _():
        sweep("diagonal", None, 1)

    for src in (0, 1):
        @pl.when((j < i) & (parity == src))
        def _(src=src):
            worst = sweep("lagged", src, 1 - src)

            @pl.when(worst > LAG_LIMIT)
            def _():
                sweep("standard", src, 1 - src)

    @pl.when(j == 0)
    def _():
        for h in range(N_HEADS):
            acc = acc_ref[1 - parity, h]
            o_ref[:, h * LANES:(h + 1) * LANES] = (
                acc[:, :LANES] / acc[:, LANES:LANES + 1]).astype(o_ref.dtype)


def _ones_column(rows):
    ones = np.zeros((rows, LANES), np.float32)
    ones[:, 0] = 1.0
    return jnp.asarray(ones, BF16)


def _fox_attention(qkv, aug, *, batch, seq):
    t = qkv.shape[0]
    w = N_HEADS * LANES
    blk = min(FOX_BLOCK, seq)
    nblk = seq // blk
    qi, kj = _tri_pairs(nblk)
    r = np.arange(blk)
    mask = jnp.asarray(np.where(r[None, :] <= r[:, None], 0.0, NEG), F32)
    grid_spec = pltpu.PrefetchScalarGridSpec(
        num_scalar_prefetch=2,
        grid=(batch, qi.shape[0]),
        in_specs=[
            pl.BlockSpec((blk, w), lambda b, s, qi, kj: (b * nblk + qi[s], 0)),
            pl.BlockSpec((blk, w), lambda b, s, qi, kj: (b * nblk + qi[s], 0)),
            pl.BlockSpec((blk, w), lambda b, s, qi, kj: (b * nblk + kj[s], 1)),
            pl.BlockSpec((blk, w), lambda b, s, qi, kj: (b * nblk + kj[s], 1)),
            pl.BlockSpec((blk, w), lambda b, s, qi, kj: (b * nblk + kj[s], 2)),
            _const_spec((blk, LANES)),
            _const_spec((blk, blk)),
        ],
        out_specs=pl.BlockSpec((blk, w), lambda b, s, qi, kj: (b * nblk + qi[s], 0)),
        scratch_shapes=[
            pltpu.VMEM((2, blk, blk), BF16),
            pltpu.VMEM((2, blk, LANES), F32),
            pltpu.VMEM((2, N_HEADS, blk, LANES), F32),
            pltpu.VMEM((2, N_HEADS, blk, 2 * LANES), F32),
        ],
    )
    return pl.pallas_call(
        _fox_kernel,
        grid_spec=grid_spec,
        out_shape=jax.ShapeDtypeStruct((t, w), BF16),
        compiler_params=_params("arbitrary", "arbitrary"),
        name="fox_attention",
    )(qi, kj, qkv, aug, qkv, aug, qkv, _ones_column(blk), mask)


def _diff_pairs(nq):
    qi = np.array([i for i in range(nq) for _ in range(i // 2 + 1)], np.int32)
    kj = np.array([j for i in range(nq) for j in range(i // 2, -1, -1)], np.int32)
    return jnp.asarray(qi), jnp.asarray(kj)


def _diff_kernel(qi_ref, kj_ref, lam_ref, q_ref, k_ref, v_ref, ones_ref, bias_ref, gain_ref, o_ref,
                 qs_ref, p_ref, alpha_ref, m_ref, acc_ref, *, out_scale):
    step = pl.program_id(1)
    i = qi_ref[step]
    j = kj_ref[step]
    tq = q_ref.shape[0]
    half = LANES // 2
    jd = i // 2
    odd = (i % 2) == 1
    parity = (jd - j) % 2

    @pl.when(j == jd)
    def _():
        lane = lax.broadcasted_iota(I32, (tq, LANES), 1)
        zero = jnp.zeros((tq, LANES), BF16)
        for h in range(N_HEADS):
            q = q_ref[:, h * LANES:(h + 1) * LANES]
            qs_ref[h, 0:tq, :] = jnp.where(lane < half, q, zero)
            qs_ref[h, tq:2 * tq, :] = jnp.where(lane < half, zero, q)

    def sweep(tk, add_bias, mode, src, dst):
        def logits(h):
            return _dot_nt(qs_ref[h], k_ref[0:tk, h * LANES:(h + 1) * LANES])

        excess = jnp.full((ATT_STRIP, LANES), NEG, F32)
        nxt = logits(0)
        for h in range(N_HEADS):
            hs = slice(h * LANES, (h + 1) * LANES)
            sb = h % 2
            cur = nxt
            if h + 1 < N_HEADS:
                nxt = logits(h + 1)
            bias_fn = None if add_bias is None else functools.partial(add_bias, h)
            p_h, alpha_h = p_ref.at[sb], alpha_ref.at[sb]
            m_out, acc_out = m_ref.at[dst, h], acc_ref.at[dst, h]
            v, ones = v_ref[0:tk, hs], ones_ref[0:tk, :]
            if mode == "first":
                _softmax_strips(cur, p_h, alpha_h, None, m_out, bias_fn)
                _accumulate(p_h, alpha_h, v, ones, None, acc_out)
                continue
            m_in, acc_in = m_ref.at[src, h], acc_ref.at[src, h]
            if mode == "lagged":
                excess = _softmax_strips_lagged(cur, p_h, alpha_h, m_in, m_out, excess)
                _accumulate(p_h, alpha_h, v, ones, acc_in, acc_out, lagged=True)
            else:
                _softmax_strips(cur, p_h, alpha_h, m_in, m_out, bias_fn)
                _accumulate(p_h, alpha_h, v, ones, acc_in, acc_out)
        return jnp.max(excess)

    def bias_rows(h, which, r0):
        return bias_ref[h, which, pl.ds(r0 % tq, ATT_STRIP), :]

    def odd_diag(h, s, r0):
        return jnp.concatenate([s[:, :tq] + bias_rows(h, 1, r0), s[:, tq:] + bias_rows(h, 0, r0)],
                               axis=-1)

    def even_prev(h, s, r0):
        return jnp.concatenate([s[:, :tq], s[:, tq:] + bias_rows(h, 1, r0)], axis=-1)

    def even_diag(h, s, r0):
        return s + bias_rows(h, 0, r0)

    @pl.when((j == jd) & odd)
    def _():
        sweep(2 * tq, odd_diag, "first", None, 1)

    @pl.when((j == jd) & jnp.logical_not(odd))
    def _():
        sweep(tq, even_diag, "first", None, 1)

    @pl.when((j == jd - 1) & jnp.logical_not(odd))
    def _():
        sweep(2 * tq, even_prev, "standard", 1, 0)

    far = (j < jd - 1) | ((j == jd - 1) & odd)
    for src in (0, 1):
        @pl.when(far & (parity == src))
        def _(src=src):
            worst = sweep(2 * tq, None, "lagged", src, 1 - src)

            @pl.when(worst > LAG_LIMIT)
            def _():
                sweep(2 * tq, None, "standard", src, 1 - src)

    @pl.when(j == 0)
    def _():
        for h in range(N_HEADS):
            hs = slice(h * LANES, (h + 1) * LANES)
            acc = acc_ref[1 - parity, h]
            o = acc[:, :LANES] / acc[:, LANES:LANES + 1]
            o = o[0:tq, :] - lam_ref[0, 0] * o[tq:2 * tq, :]
            o_ref[:, hs] = (_rmsnorm(o, gain_ref[:, hs]) * out_scale).astype(o_ref.dtype)


def _diff_attention(qkv, lam, bias, gain, *, batch, seq, out_scale):
    t = qkv.shape[0]
    w = N_HEADS * LANES
    blk = bias.shape[2]
    kblk = 2 * blk
    nq = seq // blk
    nk = seq // kblk
    qi, kj = _diff_pairs(nq)
    grid_spec = pltpu.PrefetchScalarGridSpec(
        num_scalar_prefetch=2,
        grid=(batch, qi.shape[0]),
        in_specs=[
            pl.BlockSpec(memory_space=pltpu.SMEM),
            pl.BlockSpec((blk, w), lambda b, s, qi, kj: (b * nq + qi[s], 0)),
            pl.BlockSpec((kblk, w), lambda b, s, qi, kj: (b * nk + kj[s], 1)),
            pl.BlockSpec((kblk, w), lambda b, s, qi, kj: (b * nk + kj[s], 2)),
            _const_spec((kblk, LANES)),
            _const_spec(bias.shape),
            _const_spec((1, w)),
        ],
        out_specs=pl.BlockSpec((blk, w), lambda b, s, qi, kj: (b * nq + qi[s], 0)),
        scratch_shapes=[
            pltpu.VMEM((N_HEADS, 2 * blk, LANES), BF16),
            pltpu.VMEM((2, 2 * blk, kblk), BF16),
            pltpu.VMEM((2, 2 * blk, LANES), F32),
            pltpu.VMEM((2, N_HEADS, 2 * blk, LANES), F32),
            pltpu.VMEM((2, N_HEADS, 2 * blk, 2 * LANES), F32),
        ],
    )
    return pl.pallas_call(
        functools.partial(_diff_kernel, out_scale=out_scale),
        grid_spec=grid_spec,
        out_shape=jax.ShapeDtypeStruct((t, w), BF16),
        compiler_params=_params("arbitrary", "arbitrary"),
        name="diff_attention",
    )(qi, kj, lam, qkv, qkv, qkv, _ones_column(kblk), bias, gain)


def _rel_bucket(rel):
    n_half = REL_BUCKETS // 2
    max_exact = n_half // 2
    ret = jnp.where(rel > 0, n_half, 0)
    n = jnp.abs(rel)
    nf = jnp.maximum(n, 1).astype(F32)
    large = max_exact + (jnp.log(nf / max_exact) / math.log(REL_MAX_DIST / max_exact)
                         * (n_half - max_exact)).astype(I32)
    large = jnp.minimum(large, n_half - 1)
    return ret + jnp.where(n < max_exact, n, large)


def _rel_bias_kernel(table_ref, idx_ref, o_ref):
    h = pl.program_id(0)
    idx = idx_ref[0]
    far = table_ref[REL_BUCKETS // 2 - 1, h]
    out = jnp.full(idx.shape, NEG, F32)
    for b in range(REL_BUCKETS):
        out = jnp.where(idx == b, table_ref[b, h] - far, out)
    o_ref[0, 0] = out


def _diff_bias_blocks(rel_table, blk):
    assert blk >= REL_MAX_DIST
    r = jnp.arange(blk)[:, None]
    c = jnp.arange(blk)[None, :]
    diag = jnp.where((c // CHUNK) <= (r // CHUNK), _rel_bucket(c - r), -1)
    prev = _rel_bucket(c - r - blk)
    idx = jnp.stack([diag, prev], axis=0).astype(I32)
    n_heads = rel_table.shape[1]
    return pl.pallas_call(
        _rel_bias_kernel,
        grid=(n_heads, 2),
        in_specs=[
            pl.BlockSpec(memory_space=pltpu.SMEM),
            pl.BlockSpec((1, blk, blk), lambda h, s: (s, 0, 0)),
        ],
        out_specs=pl.BlockSpec((1, 1, blk, blk), lambda h, s: (h, s, 0, 0)),
        out_shape=jax.ShapeDtypeStruct((n_heads, 2, blk, blk), F32),
        compiler_params=_params("arbitrary", "arbitrary"),
        name="rel_bias",
    )(rel_table.astype(F32), idx)


def _odd_in_kernel(x_ref, g_ref, wqk_ref, w_ref, cw_ref, qkv_ref, yd_ref, zbuf_ref, *,
                   tiles_per_batch):
    i = pl.program_id(0)
    tm = x_ref.shape[0]
    qw = qkv_ref.shape[1]
    dw = yd_ref.shape[1]
    qkw = wqk_ref.shape[1]

    @pl.when(i % tiles_per_batch == 0)
    def _():
        zbuf_ref[tm:tm + 8, :] = jnp.zeros((8, dw), F32)

    xn = _rmsnorm(x_ref[...], g_ref[...]).astype(BF16)
    qkv_ref[:, 0:qkw] = _dot(xn, wqk_ref[...]).astype(BF16)
    qkv_ref[:, qkw:qw] = _dot(xn, w_ref[:, qkw:qw].astype(BF16)).astype(BF16)
    hh = _dot(xn, w_ref[:, qw:qw + dw].astype(BF16))
    gb = _dot(xn, w_ref[:, qw + dw:qw + 2 * dw].astype(BF16))
    gc = _dot(xn, w_ref[:, qw + 2 * dw:qw + 3 * dw].astype(BF16))
    z = gc * hh

    zbuf_ref[0:8, :] = zbuf_ref[tm:tm + 8, :]
    zbuf_ref[8:tm + 8, :] = z
    y = (cw_ref[0:1, :] * zbuf_ref[6:tm + 6, :] + cw_ref[1:2, :] * zbuf_ref[7:tm + 7, :]
         + cw_ref[2:3, :] * z)
    yd_ref[...] = (gb * y).astype(BF16)


def _odd_in(x, g, wqk, w_stacked, cw, *, seq, qw):
    t, d = x.shape
    tm = min(TOK_TILE, seq)
    dw = cw.shape[1]
    return pl.pallas_call(
        functools.partial(_odd_in_kernel, tiles_per_batch=seq // tm),
        grid=(t // tm,),
        in_specs=[
            pl.BlockSpec((tm, d), lambda i: (i, 0)),
            _const_spec((1, d)),
            _const_spec(wqk.shape),
            _layer_spec(w_stacked, 0),
            _const_spec(cw.shape),
        ],
        out_specs=[
            pl.BlockSpec((tm, qw), lambda i: (i, 0)),
            pl.BlockSpec((tm, dw), lambda i: (i, 0)),
        ],
        out_shape=[
            jax.ShapeDtypeStruct((t, qw), BF16),
            jax.ShapeDtypeStruct((t, dw), BF16),
        ],
        scratch_shapes=[pltpu.VMEM((tm + 8, dw), F32)],
        compiler_params=_params("arbitrary"),
        name="odd_in",
    )(x, g, wqk, w_stacked, cw)


def _mem_kv_kernel(mem_ref, g_ref, w_ref, k_ref, v_ref):
    d = mem_ref.shape[2]
    mn = _rmsnorm(mem_ref[0], g_ref[...]).astype(BF16)
    k_ref[0] = _dot(mn, w_ref[:, 0:d].astype(BF16)).astype(BF16)
    v_ref[0] = _dot(mn, w_ref[:, d:2 * d].astype(BF16)).astype(BF16)


def _mem_kv(mem, g, w_stacked, layer):
    b, m, d = mem.shape
    return pl.pallas_call(
        _mem_kv_kernel,
        grid=(b,),
        in_specs=[
            pl.BlockSpec((1, m, d), lambda i: (i, 0, 0)),
            _const_spec((1, d)),
            _layer_spec(w_stacked, layer),
        ],
        out_specs=[pl.BlockSpec((1, m, d), lambda i: (i, 0, 0))] * 2,
        out_shape=[jax.ShapeDtypeStruct((b, m, d), BF16)] * 2,
        compiler_params=_params("arbitrary"),
        name="mem_kv",
    )(mem, g, w_stacked)


def _post_mixer_kernel(x_ref, ya_ref, yb_ref, wout_ref, g_ref, wq_ref, k_ref, v_ref, wo_ref, o_ref):
    wa = ya_ref.shape[1]
    d = x_ref.shape[1]
    hd = d // N_HEADS
    x1 = (x_ref[...] + _dot(ya_ref[...], wout_ref[0:wa, :].astype(BF16))
          + _dot(yb_ref[...], wout_ref[wa:, :].astype(BF16)))
    xn = _rmsnorm(x1, g_ref[...]).astype(BF16)
    q = (_dot(xn, wq_ref[...].astype(BF16)) * (float(hd) ** -0.5)).astype(BF16)
    heads = []
    for h in range(N_HEADS):
        lg = _dot_nt(q[:, h * hd:(h + 1) * hd], k_ref[0, :, h * hd:(h + 1) * hd])
        p = jnp.exp(lg - jnp.max(lg, axis=-1, keepdims=True))
        p = p / jnp.sum(p, axis=-1, keepdims=True)
        heads.append(_dot(p.astype(BF16), v_ref[0, :, h * hd:(h + 1) * hd]).astype(BF16))
    o_ref[...] = x1 + _dot(jnp.concatenate(heads, axis=-1), wo_ref[...].astype(BF16))


def _post_mixer(x, ya, yb, wout_stacked, g, wq_stacked, kmem, vmem, wo_stacked, layer, *, seq):
    t, d = x.shape
    tm = min(TOK_TILE, seq)
    nt = seq // tm
    m = kmem.shape[1]
    wa = ya.shape[1]
    return pl.pallas_call(
        _post_mixer_kernel,
        grid=(t // tm,),
        in_specs=[
            pl.BlockSpec((tm, d), lambda i: (i, 0)),
            pl.BlockSpec((tm, wa), lambda i: (i, 0)),
            pl.BlockSpec((tm, yb.shape[1]), lambda i: (i, 0)),
            _layer_spec(wout_stacked, 0),
            _const_spec((1, d)),
            _layer_spec(wq_stacked, layer),
            pl.BlockSpec((1, m, d), lambda i: (i // nt, 0, 0)),
            pl.BlockSpec((1, m, d), lambda i: (i // nt, 0, 0)),
            _layer_spec(wo_stacked, layer),
        ],
        out_specs=pl.BlockSpec((tm, d), lambda i: (i, 0)),
        out_shape=jax.ShapeDtypeStruct((t, d), F32),
        compiler_params=_params("arbitrary"),
        name="post_mixer",
    )(x, ya, yb, wout_stacked, g, wq_stacked, kmem, vmem, wo_stacked)


def _ffn_kernel(x_ref, g_ref, wg_ref, wu_ref, wd_ref, o_ref):
    x = x_ref[...]
    h = _rmsnorm(x, g_ref[...]).astype(BF16)
    ff = wg_ref.shape[1]
    acc = x
    for c0 in range(0, ff, FFN_CHUNK):
        c1 = min(c0 + FFN_CHUNK, ff)
        a = _dot(h, wg_ref[:, c0:c1].astype(BF16))
        u = _dot(h, wu_ref[:, c0:c1].astype(BF16))
        acc = acc + _dot((jax.nn.silu(a) * u).astype(BF16), wd_ref[c0:c1, :].astype(BF16))
    o_ref[...] = acc


def _ffn(x, g, wg, wu, wd, *, seq):
    t, d = x.shape
    tm = min(TOK_TILE_SMALL, seq)
    return pl.pallas_call(
        _ffn_kernel,
        grid=(t // tm,),
        in_specs=[
            pl.BlockSpec((tm, d), lambda i: (i, 0)),
            _const_spec((1, d)),
            _const_spec(wg.shape),
            _const_spec(wu.shape),
            _const_spec(wd.shape),
        ],
        out_specs=pl.BlockSpec((tm, d), lambda i: (i, 0)),
        out_shape=jax.ShapeDtypeStruct((t, d), F32),
        compiler_params=_params("arbitrary"),
        name="ffn",
    )(x, g, wg, wu, wd)


META_I1, META_I2, META_R1, META_R2, META_G1, META_G2 = range(6)


def _router_kernel(x_ref, g_ref, wr_ref, meta_ref, metat_ref, cnt_ref, carry_ref):
    i = pl.program_id(0)
    tm = x_ref.shape[0]

    @pl.when(i == 0)
    def _():
        carry_ref[...] = jnp.zeros_like(carry_ref)

    h = _rmsnorm(x_ref[...], g_ref[...])
    hp = _split3(h)
    wp = _split3(wr_ref[...])
    logits = _dot(hp[1], wp[0]) + _dot(hp[0], wp[1]) + _dot(hp[0], wp[0])
    lane = lax.broadcasted_iota(I32, logits.shape, 1)
    logits = jnp.where(lane < N_EXPERTS, logits, NEG)
    m1 = jnp.max(logits, axis=-1, keepdims=True)
    i1 = jnp.min(jnp.where(logits == m1, lane, LANES), axis=-1, keepdims=True)
    rest = jnp.where(lane == i1, NEG, logits)
    m2 = jnp.max(rest, axis=-1, keepdims=True)
    i2 = jnp.min(jnp.where(rest == m2, lane, LANES), axis=-1, keepdims=True)
    e = jnp.exp(m2 - m1)
    g1 = 1.0 / (1.0 + e)
    g2 = e / (1.0 + e)

    sel = jnp.where((lane == i1) | (lane == i2), 1.0, 0.0)
    row = lax.broadcasted_iota(I32, (tm, tm), 0)
    col = lax.broadcasted_iota(I32, (tm, tm), 1)
    strict = jnp.where(col < row, 1.0, 0.0).astype(BF16)
    rank =_dot(strict, sel.astype(BF16)) + carry_ref[0:1, :]
    total = rank[tm - 1:tm, :] + sel[tm - 1:tm, :]
    carry_ref[...] = jnp.broadcast_to(total, carry_ref.shape)
    cnt_ref[...] = jnp.broadcast_to(total, cnt_ref.shape)
    r1 = jnp.sum(jnp.where(lane == i1, rank, 0.0), axis=-1, keepdims=True)
    r2 = jnp.sum(jnp.where(lane == i2, rank, 0.0), axis=-1, keepdims=True)
    meta = jnp.zeros_like(logits)
    for slot, val in ((META_I1, i1.astype(F32)), (META_I2, i2.astype(F32)), (META_R1, r1),
                      (META_R2, r2), (META_G1, g1), (META_G2, g2)):
        meta = jnp.where(lane == slot, val, meta)
    meta_ref[...] = meta
    metat_ref[...] = meta.T[0:8, :]


def _router(x, g, wr, *, seq):
    t, d = x.shape
    tm = min(TOK_TILE, seq)
    return pl.pallas_call(
        _router_kernel,
        grid=(t // tm,),
        in_specs=[
            pl.BlockSpec((tm, d), lambda i: (i, 0)),
            _const_spec((1, d)),
            _const_spec(wr.shape),
        ],
        out_specs=[
            pl.BlockSpec((tm, LANES), lambda i: (i, 0)),
            pl.BlockSpec((8, tm), lambda i: (0, i)),
            pl.BlockSpec((8, LANES), lambda i: (0, 0)),
        ],
        out_shape=[
            jax.ShapeDtypeStruct((t, LANES), F32),
            jax.ShapeDtypeStruct((8, t), F32),
            jax.ShapeDtypeStruct((8, LANES), F32),
        ],
        scratch_shapes=[pltpu.VMEM((8, LANES), F32)],
        compiler_params=_params("arbitrary"),
        name="router",
    )(x, g, wr)


def _row_copy(src, src_row, dst, dst_row, sem):
    return pltpu.make_async_copy(src.at[pl.ds(src_row, 1)], dst.at[pl.ds(dst_row, 1)], sem)


def _dispatch_kernel(pos_ref, pad_ref, x_ref, g_ref, xs_ref, h_ref, zrow_ref, sem_ref, zsem_ref):
    c = pl.program_id(0)
    n_chunks = pl.num_programs(0)
    tc = x_ref.shape[0]
    slot = c % 2

    @pl.when(c == 0)
    def _():
        zrow_ref[...] = jnp.zeros_like(zrow_ref)
        for e in range(N_EXPERTS):
            start = pad_ref[2 * e]
            count = pad_ref[2 * e + 1] - start

            def zissue(r, carry, start=start):
                _row_copy(zrow_ref, 0, xs_ref, start + r, zsem_ref).start()
                return carry

            def zwait(r, carry):
                _row_copy(zrow_ref, 0, xs_ref, 0, zsem_ref).wait()
                return carry

            lax.fori_loop(0, count, zissue, 0)
            lax.fori_loop(0, count, zwait, 0)

        tail = pad_ref[2 * N_EXPERTS - 1]
        groups = (xs_ref.shape[0] - tail) // 8

        def tissue(r, carry):
            start = pl.multiple_of(tail + 8 * r, 8)
            pltpu.make_async_copy(zrow_ref, xs_ref.at[pl.ds(start, 8)], zsem_ref).start()
            return carry

        def twait(r, carry):
            pltpu.make_async_copy(zrow_ref, xs_ref.at[pl.ds(0, 8)], zsem_ref).wait()
            return carry

        lax.fori_loop(0, groups, tissue, 0)
        lax.fori_loop(0, groups, twait, 0)

    h_ref[slot] = _rmsnorm(x_ref[...], g_ref[...])

    def issue(r, carry):
        tok = c * tc + r
        _row_copy(h_ref.at[slot], r, xs_ref, pos_ref[tok], sem_ref.at[slot]).start()
        _row_copy(h_ref.at[slot], r, xs_ref, pos_ref[n_chunks * tc + tok], sem_ref.at[slot]).start()
        return carry

    def drain(s):
        for _ in range(2):
            pltpu.make_async_copy(h_ref.at[s], xs_ref.at[pl.ds(0, tc)], sem_ref.at[s]).wait()

    lax.fori_loop(0, tc, issue, 0, unroll=DMA_UNROLL)

    @pl.when(c > 0)
    def _():
        drain(1 - slot)

    @pl.when(c == n_chunks - 1)
    def _():
        drain(slot)


def _dispatch(pos, pad, x, g, n_rows):
    t, d = x.shape
    tc = DISPATCH_CHUNK
    grid_spec = pltpu.PrefetchScalarGridSpec(
        num_scalar_prefetch=2,
        grid=(t // tc,),
        in_specs=[
            pl.BlockSpec((tc, d), lambda c, pos, pad: (c, 0)),
            pl.BlockSpec((1, d), lambda c, pos, pad: (0, 0)),
        ],
        out_specs=pl.BlockSpec(memory_space=pl.ANY),
        scratch_shapes=[
            pltpu.VMEM((2, tc, d), F32),
            pltpu.VMEM((8, d), F32),
            pltpu.SemaphoreType.DMA((2,)),
            pltpu.SemaphoreType.DMA(()),
        ],
    )
    return pl.pallas_call(
        _dispatch_kernel,
        grid_spec=grid_spec,
        out_shape=jax.ShapeDtypeStruct((n_rows, d), F32),
        compiler_params=_params("arbitrary"),
        name="moe_dispatch",
    )(pos, pad, x, g)


def _experts_kernel(te_ref, rows_ref, nused_ref, xs_ref, wg_ref, wu_ref, wd_ref, y_ref, xb_ref):
    p = pl.program_id(0)
    f = pl.program_id(1)
    rows = rows_ref[p]
    half = MOE_TILE // 2

    @pl.when(f == 0)
    def _():
        y_ref[...] = jnp.zeros_like(y_ref)

    @pl.when(rows > 0)
    def _():
        @pl.when(f == 0)
        def _():
            xb_ref[...] = xs_ref[...].astype(BF16)

        tf = wg_ref.shape[2]
        wgu = jnp.concatenate([wg_ref[0].astype(BF16), wu_ref[0].astype(BF16)], axis=1)
        wd = wd_ref[0].astype(BF16)

        def run(r0):
            gu = _dot(xb_ref[r0:r0 + half, :], wgu)
            act = (jax.nn.silu(gu[:, :tf]) * gu[:, tf:]).astype(BF16)
            y_ref[r0:r0 + half, :] += _dot(act, wd)

        run(0)

        @pl.when(rows > half)
        def _():
            run(half)


def _experts(tile_expert, tile_rows, n_used, xs, wg, wu, wd):
    n_rows, d = xs.shape
    ff = wg.shape[2]
    tm = MOE_TILE
    tf = MOE_FF_CHUNK
    n_tiles = n_rows // tm

    def x_map(p, f, te, tr, nu):
        return (jnp.minimum(p, nu[0] - 1), 0)

    grid_spec = pltpu.PrefetchScalarGridSpec(
        num_scalar_prefetch=3,
        grid=(n_tiles, ff // tf),
        in_specs=[
            pl.BlockSpec((tm, d), x_map),
            pl.BlockSpec((1, d, tf), lambda p, f, te, tr, nu: (te[p], 0, f)),
            pl.BlockSpec((1, d, tf), lambda p, f, te, tr, nu: (te[p], 0, f)),
            pl.BlockSpec((1, tf, d), lambda p, f, te, tr, nu: (te[p], f, 0)),
        ],
        out_specs=pl.BlockSpec((tm, d), lambda p, f, te, tr, nu: (p, 0)),
        scratch_shapes=[pltpu.VMEM((tm, d), BF16)],
    )
    return pl.pallas_call(
        _experts_kernel,
        grid_spec=grid_spec,
        out_shape=jax.ShapeDtypeStruct((n_rows, d), F32),
        compiler_params=_params("arbitrary", "arbitrary"),
        name="moe_experts",
    )(tile_expert, tile_rows, n_used, xs, wg, wu, wd)


def _combine_kernel(pos_ref, y_ref, x_ref, meta_ref, g_ref, o_ref, buf_ref, sem_ref):
    i = pl.program_id(0)
    n = pl.num_programs(0)
    tc = x_ref.shape[0]

    def fetch(tile, slot):
        def body(r, carry):
            tok = tile * tc + r
            for k in range(2):
                pltpu.make_async_copy(y_ref.at[pl.ds(pos_ref[k * n * tc + tok], 1)],
                                      buf_ref.at[slot, k, pl.ds(r, 1)], sem_ref.at[slot]).start()
            return carry
        lax.fori_loop(0, tc, body, 0, unroll=DMA_UNROLL)

    @pl.when(i == 0)
    def _():
        fetch(0, 0)

    @pl.when(i + 1 < n)
    def _():
        fetch(i + 1, (i + 1) % 2)

    slot = i % 2

    for k in range(2):
        pltpu.make_async_copy(y_ref.at[pl.ds(0, tc)], buf_ref.at[slot, k], sem_ref.at[slot]).wait()
    meta = meta_ref[...]
    g1 = meta[:, META_G1:META_G1 + 1]
    g2 = meta[:, META_G2:META_G2 + 1]
    x = x_ref[...] + g1 * buf_ref[slot, 0] + g2 * buf_ref[slot, 1]
    o_ref[...] = _rmsnorm(x, g_ref[...])


def _combine(pos, y, x, meta, g):
    t, d = x.shape
    tc = COMBINE_TILE
    grid_spec = pltpu.PrefetchScalarGridSpec(
        num_scalar_prefetch=1,
        grid=(t // tc,),
        in_specs=[
            pl.BlockSpec(memory_space=pl.ANY),
            pl.BlockSpec((tc, d), lambda i, pos: (i, 0)),
            pl.BlockSpec((tc, LANES), lambda i, pos: (i, 0)),
            pl.BlockSpec((1, d), lambda i, pos: (0, 0)),
        ],
        out_specs=pl.BlockSpec((tc, d), lambda i, pos: (i, 0)),
        scratch_shapes=[pltpu.VMEM((2, 2, tc, d), F32), pltpu.SemaphoreType.DMA((2,))],
    )
    return pl.pallas_call(
        _combine_kernel,
        grid_spec=grid_spec,
        out_shape=jax.ShapeDtypeStruct((t, d), F32),
        compiler_params=_params("arbitrary"),
        name="moe_combine",
    )(pos, y, x, meta, g)


def _moe_plan(meta_t, counts, n_tiles):
    cnt = counts[0, :N_EXPERTS].astype(I32)
    tiles = (cnt + MOE_TILE - 1) // MOE_TILE
    tile_end = jnp.cumsum(tiles)
    row_off = (tile_end - tiles) * MOE_TILE
    n_used = tile_end[-1:]
    experts = jnp.arange(N_EXPERTS, dtype=I32)

    def rows(idx_row, rank_row):
        idx = meta_t[idx_row].astype(I32)
        off = jnp.sum(jnp.where(idx[None, :] == experts[:, None], row_off[:, None], 0), axis=0)
        return off + meta_t[rank_row].astype(I32)

    pos = jnp.concatenate([rows(META_I1, META_R1), rows(META_I2, META_R2)])
    tile_ids = jnp.arange(n_tiles, dtype=I32)
    te = jnp.sum((tile_end[None, :] <= tile_ids[:, None]).astype(I32), axis=1)
    mine = te[:, None] == experts[None, :]
    first_row = tile_ids * MOE_TILE - jnp.sum(jnp.where(mine, row_off[None, :], 0), axis=1)
    tile_rows = jnp.clip(jnp.sum(jnp.where(mine, cnt[None, :], 0), axis=1) - first_row, 0, MOE_TILE)
    last = jnp.sum((tile_end <= n_used - 1).astype(I32))
    te = jnp.minimum(te, last).astype(I32)
    pad = jnp.stack([row_off + cnt, row_off + tiles * MOE_TILE], axis=1).reshape(-1).astype(I32)
    return pos.astype(I32), pad, te, tile_rows.astype(I32), n_used.astype(I32)


def _moe_and_final_norm(x, g_ffn, wr, wg, wu, wd, g_final, *, seq):
    t, d = x.shape
    n_tiles = 2 * t // MOE_TILE + N_EXPERTS
    meta, meta_t, counts = _router(x, g_ffn, wr, seq=seq)
    pos, pad, te, tile_rows, n_used = _moe_plan(meta_t, counts, n_tiles)
    xs = _dispatch(pos, pad, x, g_ffn, n_tiles * MOE_TILE)
    y = _experts(te, tile_rows, n_used, xs, wg, wu, wd)
    return _combine(pos, y, x, meta, g_final)


def _row(v):
    return v.reshape(1, -1).astype(F32)


def kernel(x, mem, norm_mix, norm_mem_q, norm_mem_kv, norm_ffn, norm_final, even_w_in, fox_b_f, gmlp_v_gain, gmlp_w_s, gmlp_b_s, even_w_out, odd_w_in, diff_lambda_q1, diff_lambda_k1, diff_lambda_q2, diff_lambda_k2, diff_subln_gain, conv_w, odd_w_out, rel_bias, mem_w_q, mem_w_kv, mem_w_o, ffn_w_gate, ffn_w_up, ffn_w_down, router_w, moe_w_gate, moe_w_up, moe_w_down):
    batch, seq, d = x.shape
    t = batch * seq
    depth = norm_mix.shape[0]
    assert depth == 2 and seq % (2 * ATT_BLOCK) == 0 and seq % TOK_TILE == 0
    xf = x.reshape(t, d)
    hd = d // N_HEADS
    aw = gmlp_v_gain.shape[1]
    n_blk = gmlp_w_s.shape[2]

    gate_cols = even_w_in.shape[2] - N_HEADS
    wf = jnp.pad(even_w_in[0, :, gate_cols:], ((0, 0), (0, LANES - N_HEADS)))
    bf = jnp.pad(fox_b_f[0], (0, LANES - N_HEADS)).reshape(1, LANES)
    bs = jnp.broadcast_to(gmlp_b_s[0][:, :, None], (A_GROUPS, n_blk, aw // A_GROUPS)).astype(F32)
    ya, qkv, aug = _even_in(xf, _row(norm_mix[0]), even_w_in, wf, bf, _row(gmlp_v_gain[0]),
                            gmlp_w_s[0], bs, seq=seq)
    yb = _fox_attention(qkv, aug, batch=batch, seq=seq)

    def mem_attention(layer, x_in, ya_, yb_, w_out):
        km, vm = _mem_kv(mem, _row(norm_mem_kv[layer]), mem_w_kv, layer)
        return _post_mixer(x_in, ya_, yb_, w_out, _row(norm_mem_q[layer]), mem_w_q, km, vm,
                           mem_w_o, layer, seq=seq)

    xf = mem_attention(0, xf, ya, yb, even_w_out)
    xf = _ffn(xf, _row(norm_ffn[0]), ffn_w_gate[0], ffn_w_up[0], ffn_w_down[0], seq=seq)

    qk_dim = LANES // 2
    qk_w = 2 * N_HEADS * qk_dim

    def regroup(cols):
        return cols.reshape(d, 2, N_HEADS, qk_dim).transpose(0, 2, 1, 3).reshape(d, qk_w)

    wqk = jnp.concatenate([regroup(odd_w_in[0, :, :qk_w]) * float(qk_dim) ** -0.5,
                           regroup(odd_w_in[0, :, qk_w:2 * qk_w])], axis=1).astype(BF16)
    qkv, yd = _odd_in(xf, _row(norm_mix[1]), wqk, odd_w_in, conv_w[0].astype(F32), seq=seq,
                      qw=2 * qk_w + N_HEADS * LANES)
    lam_init = 0.8 - 0.6 * math.exp(-0.3 * 1)
    lam = (jnp.exp(jnp.sum(diff_lambda_q1[0] * diff_lambda_k1[0]))
           - jnp.exp(jnp.sum(diff_lambda_q2[0] * diff_lambda_k2[0])) + lam_init)
    bias = _diff_bias_blocks(rel_bias, min(ATT_BLOCK, seq))
    yc = _diff_attention(qkv, lam.reshape(1, 1).astype(F32), bias, _row(diff_subln_gain[0]),
                         batch=batch, seq=seq, out_scale=1.0 - lam_init)
    xf = mem_attention(1, xf, yc, yd, odd_w_out)
    wr = jnp.pad(router_w[0], ((0, 0), (0, LANES - N_EXPERTS))).astype(F32)
    out = _moe_and_final_norm(xf, _row(norm_ffn[1]), wr, moe_w_gate[0], moe_w_up[0], moe_w_down[0],
                              _row(norm_final), seq=seq)
    return out.reshape(batch, seq, d)
```

```python
import functools
import math

import numpy as np
import jax
import jax.numpy as jnp
from jax import lax
from jax.experimental import pallas as pl
from jax.experimental.pallas import tpu as pltpu

F32 = jnp.float32
BF16 = jnp.bfloat16
I32 = jnp.int32

EPS = 1e-6
NEG = -1e30
LANES = 128
VMEM_LIMIT = 56 * 1024 * 1024

CHUNK = 64
A_GROUPS = 4
N_HEADS = 4
N_EXPERTS = 8
REL_BUCKETS = 32
REL_MAX_DIST = 128

TOK_TILE = 1024
TOK_TILE_SMALL = 512
FOX_BLOCK = 1024
ATT_BLOCK = 512
ATT_STRIP = 64
LAG_LIMIT = 40.0
MOE_TILE = 1024
MOE_FF_CHUNK = 896
FFN_CHUNK = 512
DISPATCH_CHUNK = 256
COMBINE_TILE = 256
DMA_UNROLL = 8


def _dot(a, b):
    return jnp.dot(a, b, preferred_element_type=F32)


def _dot_nt(a, b):
    return lax.dot_general(a, b, (((1,), (1,)), ((), ())), preferred_element_type=F32)


def _rmsnorm(x, g):
    return x * lax.rsqrt(jnp.mean(x * x, axis=-1, keepdims=True) + EPS) * g


def _params(*sem):
    return pltpu.CompilerParams(dimension_semantics=sem, vmem_limit_bytes=VMEM_LIMIT)


def _const_spec(shape):
    return pl.BlockSpec(shape, lambda *_: (0,) * len(shape), pipeline_mode=pl.Buffered(1))


def _layer_spec(stacked, layer):
    rest = stacked.shape[1:]
    return pl.BlockSpec((None,) + rest, lambda *_: (layer,) + (0,) * len(rest),
                        pipeline_mode=pl.Buffered(1))


def _split3(v):
    a1 = v.astype(BF16)
    r1 = v - a1.astype(F32)
    a2 = r1.astype(BF16)
    a3 = (r1 - a2.astype(F32)).astype(BF16)
    return a1, a2, a3


def _even_in_kernel(x_ref, g_ref, w_ref, wf_ref, bf_ref, vg_ref, ws_ref, bs_ref,
                    ya_ref, qkv_ref, aug_ref, carry_ref, *, tiles_per_batch):
    i = pl.program_id(0)
    tm = x_ref.shape[0]
    aw = ya_ref.shape[1]
    qw = qkv_ref.shape[1]

    @pl.when(i % tiles_per_batch == 0)
    def _():
        carry_ref[...] = jnp.zeros_like(carry_ref)

    xn = _rmsnorm(x_ref[...], g_ref[...]).astype(BF16)

    hw = N_HEADS * LANES
    q = _dot(xn, w_ref[:, 2 * aw:2 * aw + hw].astype(BF16)) * (float(LANES) ** -0.5)
    qkv_ref[:, 0:hw] = q.astype(BF16)
    qkv_ref[:, hw:qw] = _dot(xn, w_ref[:, 2 * aw + hw:2 * aw + qw].astype(BF16)).astype(BF16)

    fl = _dot(xn, wf_ref[...].astype(BF16)) + bf_ref[...]
    ls = jnp.minimum(fl, 0.0) - jnp.log1p(jnp.exp(-jnp.abs(fl)))
    row = lax.broadcasted_iota(I32, (tm, tm), 0)
    col = lax.broadcasted_iota(I32, (tm, tm), 1)
    tri = jnp.where(col <= row, 1.0, 0.0).astype(BF16)
    a1, a2, a3 = _split3(ls)
    csum = _dot(tri, a1) + _dot(tri, a2) + _dot(tri, a3)
    csum = csum + carry_ref[0:1, :]
    carry_ref[...] = jnp.broadcast_to(csum[tm - 1:tm, :], carry_ref.shape)

    lane = lax.broadcasted_iota(I32, (tm, LANES), 1)
    for h in range(N_HEADS):
        c1, c2, c3 = (piece.astype(F32)
                      for piece in _split3(jnp.broadcast_to(csum[:, h:h + 1], (tm, LANES))))
        aq = jnp.where(lane == 0, c1, jnp.where(lane == 1, c2, jnp.where(lane == 2, c3,
                       jnp.where(lane < 6, 1.0, 0.0))))
        ak = jnp.where(lane == 3, -c1, jnp.where(lane == 4, -c2, jnp.where(lane == 5, -c3,
                       jnp.where(lane < 3, 1.0, 0.0))))
        aug_ref[:, h * LANES:(h + 1) * LANES] = aq.astype(BF16)
        aug_ref[:, (N_HEADS + h) * LANES:(N_HEADS + h + 1) * LANES] = ak.astype(BF16)

    gu = jax.nn.gelu(_dot(xn, w_ref[:, 0:aw].astype(BF16)), approximate=True)
    gv = jax.nn.gelu(_dot(xn, w_ref[:, aw:2 * aw].astype(BF16)), approximate=True)
    blk = ws_ref.shape[1]
    ch = aw // A_GROUPS
    r = lax.broadcasted_iota(I32, (blk, blk), 0)
    c = lax.broadcasted_iota(I32, (blk, blk), 1)
    causal = (c // CHUNK) <= (r // CHUNK)
    for g in range(A_GROUPS):
        vgrp = gv[:, g * ch:(g + 1) * ch]
        vn = _rmsnorm(vgrp, vg_ref[:, g * ch:(g + 1) * ch]).astype(BF16)
        wmix = jnp.where(causal, ws_ref[g], 0.0).astype(BF16)
        for n in range(tm // blk):
            mixed = _dot(wmix, vn[n * blk:(n + 1) * blk, :]) + bs_ref[g]
            ya_ref[n * blk:(n + 1) * blk, g * ch:(g + 1) * ch] = (
                gu[n * blk:(n + 1) * blk, g * ch:(g + 1) * ch] * mixed).astype(BF16)


def _even_in(x, g, w_stacked, wf, bf, vgain, ws, bs, *, seq):
    t, d = x.shape
    tm = min(TOK_TILE_SMALL, seq)
    aw = vgain.shape[1]
    qw = 3 * N_HEADS * LANES
    blk = ws.shape[1]
    return pl.pallas_call(
        functools.partial(_even_in_kernel, tiles_per_batch=seq // tm),
        grid=(t // tm,),
        in_specs=[
            pl.BlockSpec((tm, d), lambda i: (i, 0)),
            _const_spec((1, d)),
            _layer_spec(w_stacked, 0),
            _const_spec(wf.shape),
            _const_spec((1, LANES)),
            _const_spec((1, aw)),
            _const_spec(ws.shape),
            _const_spec(bs.shape),
        ],
        out_specs=[
            pl.BlockSpec((tm, aw), lambda i: (i, 0)),
            pl.BlockSpec((tm, qw), lambda i: (i, 0)),
            pl.BlockSpec((tm, 2 * N_HEADS * LANES), lambda i: (i, 0)),
        ],
        out_shape=[
            jax.ShapeDtypeStruct((t, aw), BF16),
            jax.ShapeDtypeStruct((t, qw), BF16),
            jax.ShapeDtypeStruct((t, 2 * N_HEADS * LANES), BF16),
        ],
        scratch_shapes=[pltpu.VMEM((8, LANES), F32)],
        compiler_params=_params("arbitrary"),
        name="even_in",
    )(x, g, w_stacked, wf, bf, vgain, ws, bs)


def _tri_pairs(nblk):
    qi = np.array([i for i in range(nblk) for _ in range(i + 1)], np.int32)
    kj = np.array([j for i in range(nblk) for j in range(i, -1, -1)], np.int32)
    return jnp.asarray(qi), jnp.asarray(kj)


def _softmax_strips(logits, p_ref, alpha_ref, m_in, m_out, add_bias=None):
    rows, tk = logits.shape
    for r0 in range(0, rows, ATT_STRIP):
        rs = slice(r0, r0 + ATT_STRIP)
        s = logits[rs, :]
        if add_bias is not None:
            s = add_bias(s, r0)
        m_cur = jnp.max(s, axis=-1, keepdims=True)
        if m_in is None:
            m_new = jnp.broadcast_to(m_cur, (ATT_STRIP, LANES))
        else:
            m_prev = m_in[rs, :]
            m_new = jnp.maximum(m_prev, m_cur)
            alpha_ref[rs, :] = jnp.exp(m_prev - m_new)
        m_out[rs, :] = m_new
        p_ref[rs, 0:tk] = jnp.exp(
            (s - jnp.concatenate([m_new] * (tk // LANES), axis=-1)).astype(BF16))


def _softmax_strips_lagged(logits, p_ref, alpha_ref, m_in, m_out, excess):
    rows, tk = logits.shape
    for r0 in range(0, rows, ATT_STRIP):
        rs = slice(r0, r0 + ATT_STRIP)
        s = logits[rs, :]
        m_prev = m_in[rs, :]
        p_ref[rs, 0:tk] = jnp.exp(
            (s - jnp.concatenate([m_prev] * (tk // LANES), axis=-1)).astype(BF16))
        m_cur = jnp.max(s, axis=-1, keepdims=True)
        m_new = jnp.maximum(m_prev, m_cur)
        alpha_ref[rs, :] = jnp.exp(m_prev - m_new)
        m_out[rs, :] = m_new
        excess = jnp.maximum(excess, m_cur - m_prev)
    return excess


def _accumulate(p_ref, alpha_ref, v, ones, acc_in, acc_out, lagged=False):
    pv = _dot(p_ref[:, 0:v.shape[0]], jnp.concatenate([v, ones], axis=-1))
    if acc_in is None:
        acc_out[...] = pv
        return
    alpha = alpha_ref[...]
    alpha2 = jnp.concatenate([alpha, alpha], axis=-1)
    acc_out[...] = alpha2 * (acc_in[...] + pv) if lagged else alpha2 * acc_in[...] + pv


def _fox_kernel(qi_ref, kj_ref, q_ref, aq_ref, k_ref, ak_ref, v_ref, ones_ref, mask_ref, o_ref,
                p_ref, alpha_ref, m_ref, acc_ref, redo_ref):
    step = pl.program_id(1)
    i = qi_ref[step]
    j = kj_ref[step]
    parity = (i - j) % 2

    def logits(h):
        hs = slice(h * LANES, (h + 1) * LANES)
        qa = jnp.concatenate([q_ref[:, hs], aq_ref[:, hs]], axis=-1)
        ka = jnp.concatenate([k_ref[:, hs], ak_ref[:, hs]], axis=-1)
        return _dot_nt(qa, ka)

    def sweep(mode, src, dst):
        excess = jnp.full((ATT_STRIP, LANES), NEG, F32)
        nxt = logits(0)
        for h in range(N_HEADS):
            hs = slice(h * LANES, (h + 1) * LANES)
            sb = h % 2
            cur = nxt
            if h + 1 < N_HEADS:
                nxt = logits(h + 1)
            p_h, alpha_h = p_ref.at[sb], alpha_ref.at[sb]
            m_out, acc_out = m_ref.at[dst, h], acc_ref.at[dst, h]
            if mode != "diagonal":
                m_in, acc_in = m_ref.at[src, h], acc_ref.at[src, h]
            if mode == "diagonal":
                add_mask = lambda s, r0: s + mask_ref[r0:r0 + ATT_STRIP, :]
                _softmax_strips(cur, p_h, alpha_h, None, m_out, add_mask)
                _accumulate(p_h, alpha_h, v_ref[:, hs], ones_ref[...], None, acc_out)
            elif mode == "lagged":
                excess = _softmax_strips_lagged(cur, p_h, alpha_h, m_in, m_out, excess)
                _accumulate(p_h, alpha_h, v_ref[:, hs], ones_ref[...], acc_in, acc_out, lagged=True)
            else:
                _softmax_strips(cur, p_h, alpha_h, m_in, m_out)
                _accumulate(p_h, alpha_h, v_ref[:, hs], ones_ref[...], acc_in, acc_out)
        return jnp.max(excess)

    @pl.when(j == i)
    def _():
        sweep("diagonal", None, 1)
        redo_ref[0] = NEG

    for src in (0, 1):
        @pl.when((j < i) & (parity == src))
        def _(src=src):
            redo_ref[0] = sweep("lagged", src, 1 - src)

    @pl.when((j < i) & (redo_ref[0] > LAG_LIMIT))
    def _():
        sweep("standard", parity, 1 - parity)

    @pl.when(j == 0)
    def _():
        for h in range(N_HEADS):
            acc = acc_ref[1 - parity, h]
            o_ref[:, h * LANES:(h + 1) * LANES] = (
                acc[:, :LANES] / acc[:, LANES:LANES + 1]).astype(o_ref.dtype)


def _ones_column(rows):
    ones = np.zeros((rows, LANES), np.float32)
    ones[:, 0] = 1.0
    return jnp.asarray(ones, BF16)


def _fox_attention(qkv, aug, *, batch, seq):
    t = qkv.shape[0]
    w = N_HEADS * LANES
    blk = min(FOX_BLOCK, seq)
    nblk = seq // blk
    qi, kj = _tri_pairs(nblk)
    r = np.arange(blk)
    mask = jnp.asarray(np.where(r[None, :] <= r[:, None], 0.0, NEG), F32)
    grid_spec = pltpu.PrefetchScalarGridSpec(
        num_scalar_prefetch=2,
        grid=(batch, qi.shape[0]),
        in_specs=[
            pl.BlockSpec((blk, w), lambda b, s, qi, kj: (b * nblk + qi[s], 0)),
            pl.BlockSpec((blk, w), lambda b, s, qi, kj: (b * nblk + qi[s], 0)),
            pl.BlockSpec((blk, w), lambda b, s, qi, kj: (b * nblk + kj[s], 1)),
            pl.BlockSpec((blk, w), lambda b, s, qi, kj: (b * nblk + kj[s], 1)),
            pl.BlockSpec((blk, w), lambda b, s, qi, kj: (b * nblk + kj[s], 2)),
            _const_spec((blk, LANES)),
            _const_spec((blk, blk)),
        ],
        out_specs=pl.BlockSpec((blk, w), lambda b, s, qi, kj: (b * nblk + qi[s], 0)),
        scratch_shapes=[
            pltpu.VMEM((2, blk, blk), BF16),
            pltpu.VMEM((2, blk, LANES), F32),
            pltpu.VMEM((2, N_HEADS, blk, LANES), F32),
            pltpu.VMEM((2, N_HEADS, blk, 2 * LANES), F32),
            pltpu.SMEM((1,), F32),
        ],
    )
    return pl.pallas_call(
        _fox_kernel,
        grid_spec=grid_spec,
        out_shape=jax.ShapeDtypeStruct((t, w), BF16),
        compiler_params=_params("arbitrary", "arbitrary"),
        name="fox_attention",
    )(qi, kj, qkv, aug, qkv, aug, qkv, _ones_column(blk), mask)


def _diff_pairs(nq):
    qi = np.array([i for i in range(nq) for _ in range(i // 2 + 1)], np.int32)
    kj = np.array([j for i in range(nq) for j in range(i // 2 + 1)], np.int32)
    return jnp.asarray(qi), jnp.asarray(kj)


def _diff_kernel(qi_ref, kj_ref, lam_ref, q_ref, k_ref, v_ref, ones_ref, bias_ref, gain_ref, o_ref,
                 qs_ref, p_ref, alpha_ref, m_ref, acc_ref, *, out_scale):
    step = pl.program_id(1)
    i = qi_ref[step]
    j = kj_ref[step]
    tq = q_ref.shape[0]
    half = LANES // 2
    jd = i // 2
    odd = (i % 2) == 1

    @pl.when(j == 0)
    def _():
        m_ref[...] = jnp.full_like(m_ref, NEG)
        acc_ref[...] = jnp.zeros_like(acc_ref)
        lane = lax.broadcasted_iota(I32, (tq, LANES), 1)
        zero = jnp.zeros((tq, LANES), BF16)
        for h in range(N_HEADS):
            q = q_ref[:, h * LANES:(h + 1) * LANES]
            qs_ref[h, 0:tq, :] = jnp.where(lane < half, q, zero)
            qs_ref[h, tq:2 * tq, :] = jnp.where(lane < half, zero, q)

    def sweep(tk, add_bias):
        def logits(h):
            return _dot_nt(qs_ref[h], k_ref[0:tk, h * LANES:(h + 1) * LANES])

        nxt = logits(0)
        for h in range(N_HEADS):
            hs = slice(h * LANES, (h + 1) * LANES)
            sb = h % 2
            cur = nxt
            if h + 1 < N_HEADS:
                nxt = logits(h + 1)
            bias_fn = None if add_bias is None else functools.partial(add_bias, h)
            _softmax_strips(cur, p_ref.at[sb], alpha_ref.at[sb], m_ref.at[h], m_ref.at[h], bias_fn)
            _accumulate(p_ref.at[sb], alpha_ref.at[sb], v_ref[0:tk, hs], ones_ref[0:tk, :],
                        acc_ref.at[h], acc_ref.at[h])

    def bias_rows(h, which, r0):
        return bias_ref[h, which, pl.ds(r0 % tq, ATT_STRIP), :]

    def odd_diag(h, s, r0):
        return jnp.concatenate([s[:, :tq] + bias_rows(h, 1, r0), s[:, tq:] + bias_rows(h, 0, r0)],
                               axis=-1)

    def even_prev(h, s, r0):
        return jnp.concatenate([s[:, :tq], s[:, tq:] + bias_rows(h, 1, r0)], axis=-1)

    def even_diag(h, s, r0):
        return s + bias_rows(h, 0, r0)

    @pl.when((j < jd - 1) | ((j == jd - 1) & odd))
    def _():
        sweep(2 * tq, None)

    @pl.when((j == jd - 1) & jnp.logical_not(odd))
    def _():
        sweep(2 * tq, even_prev)

    @pl.when((j == jd) & odd)
    def _():
        sweep(2 * tq, odd_diag)

    @pl.when((j == jd) & jnp.logical_not(odd))
    def _():
        sweep(tq, even_diag)

    @pl.when(j == jd)
    def _():
        for h in range(N_HEADS):
            hs = slice(h * LANES, (h + 1) * LANES)
            acc = acc_ref[h]
            o = acc[:, :LANES] / acc[:, LANES:LANES + 1]
            o = o[0:tq, :] - lam_ref[0, 0] * o[tq:2 * tq, :]
            o_ref[:, hs] = (_rmsnorm(o, gain_ref[:, hs]) * out_scale).astype(o_ref.dtype)


def _diff_attention(qkv, lam, bias, gain, *, batch, seq, out_scale):
    t = qkv.shape[0]
    w = N_HEADS * LANES
    blk = bias.shape[2]
    kblk = 2 * blk
    nq = seq // blk
    nk = seq // kblk
    qi, kj = _diff_pairs(nq)
    grid_spec = pltpu.PrefetchScalarGridSpec(
        num_scalar_prefetch=2,
        grid=(batch, qi.shape[0]),
        in_specs=[
            pl.BlockSpec(memory_space=pltpu.SMEM),
            pl.BlockSpec((blk, w), lambda b, s, qi, kj: (b * nq + qi[s], 0)),
            pl.BlockSpec((kblk, w), lambda b, s, qi, kj: (b * nk + kj[s], 1)),
            pl.BlockSpec((kblk, w), lambda b, s, qi, kj: (b * nk + kj[s], 2)),
            pl.BlockSpec((kblk, LANES), lambda b, s, qi, kj: (0, 0)),
            pl.BlockSpec(bias.shape, lambda b, s, qi, kj: (0, 0, 0, 0)),
            pl.BlockSpec((1, w), lambda b, s, qi, kj: (0, 0)),
        ],
        out_specs=pl.BlockSpec((blk, w), lambda b, s, qi, kj: (b * nq + qi[s], 0)),
        scratch_shapes=[
            pltpu.VMEM((N_HEADS, 2 * blk, LANES), BF16),
            pltpu.VMEM((2, 2 * blk, kblk), BF16),
            pltpu.VMEM((2, 2 * blk, LANES), F32),
            pltpu.VMEM((N_HEADS, 2 * blk, LANES), F32),
            pltpu.VMEM((N_HEADS, 2 * blk, 2 * LANES), F32),
        ],
    )
    return pl.pallas_call(
        functools.partial(_diff_kernel, out_scale=out_scale),
        grid_spec=grid_spec,
        out_shape=jax.ShapeDtypeStruct((t, w), BF16),
        compiler_params=_params("arbitrary", "arbitrary"),
        name="diff_attention",
    )(qi, kj, lam, qkv, qkv, qkv, _ones_column(kblk), bias, gain)


def _rel_bucket(rel):
    n_half = REL_BUCKETS // 2
    max_exact = n_half // 2
    ret = jnp.where(rel > 0, n_half, 0)
    n = jnp.abs(rel)
    nf = jnp.maximum(n, 1).astype(F32)
    large = max_exact + (jnp.log(nf / max_exact) / math.log(REL_MAX_DIST / max_exact)
                         * (n_half - max_exact)).astype(I32)
    large = jnp.minimum(large, n_half - 1)
    return ret + jnp.where(n < max_exact, n, large)


def _rel_bias_kernel(table_ref, idx_ref, o_ref):
    h = pl.program_id(0)
    idx = idx_ref[0]
    far = table_ref[REL_BUCKETS // 2 - 1, h]
    out = jnp.full(idx.shape, NEG, F32)
    for b in range(REL_BUCKETS):
        out = jnp.where(idx == b, table_ref[b, h] - far, out)
    o_ref[0, 0] = out


def _diff_bias_blocks(rel_table, blk):
    assert blk >= REL_MAX_DIST
    r = jnp.arange(blk)[:, None]
    c = jnp.arange(blk)[None, :]
    diag = jnp.where((c // CHUNK) <= (r // CHUNK), _rel_bucket(c - r), -1)
    prev = _rel_bucket(c - r - blk)
    idx = jnp.stack([diag, prev], axis=0).astype(I32)
    n_heads = rel_table.shape[1]
    return pl.pallas_call(
        _rel_bias_kernel,
        grid=(n_heads, 2),
        in_specs=[
            pl.BlockSpec(memory_space=pltpu.SMEM),
            pl.BlockSpec((1, blk, blk), lambda h, s: (s, 0, 0)),
        ],
        out_specs=pl.BlockSpec((1, 1, blk, blk), lambda h, s: (h, s, 0, 0)),
        out_shape=jax.ShapeDtypeStruct((n_heads, 2, blk, blk), F32),
        compiler_params=_params("arbitrary", "arbitrary"),
        name="rel_bias",
    )(rel_table.astype(F32), idx)


def _odd_in_kernel(x_ref, g_ref, wqk_ref, w_ref, cw_ref, qkv_ref, yd_ref, zbuf_ref, *,
                   tiles_per_batch):
    i = pl.program_id(0)
    tm = x_ref.shape[0]
    qw = qkv_ref.shape[1]
    dw = yd_ref.shape[1]
    qkw = wqk_ref.shape[1]

    @pl.when(i % tiles_per_batch == 0)
    def _():
        zbuf_ref[tm:tm + 8, :] = jnp.zeros((8, dw), F32)

    xn = _rmsnorm(x_ref[...], g_ref[...]).astype(BF16)
    qkv_ref[:, 0:qkw] = _dot(xn, wqk_ref[...]).astype(BF16)
    qkv_ref[:, qkw:qw] = _dot(xn, w_ref[:, qkw:qw].astype(BF16)).astype(BF16)
    hh = _dot(xn, w_ref[:, qw:qw + dw].astype(BF16))
    gb = _dot(xn, w_ref[:, qw + dw:qw + 2 * dw].astype(BF16))
    gc = _dot(xn, w_ref[:, qw + 2 * dw:qw + 3 * dw].astype(BF16))
    z = gc * hh

    zbuf_ref[0:8, :] = zbuf_ref[tm:tm + 8, :]
    zbuf_ref[8:tm + 8, :] = z
    y = (cw_ref[0:1, :] * zbuf_ref[6:tm + 6, :] + cw_ref[1:2, :] * zbuf_ref[7:tm + 7, :]
         + cw_ref[2:3, :] * z)
    yd_ref[...] = (gb * y).astype(BF16)


def _odd_in(x, g, wqk, w_stacked, cw, *, seq, qw):
    t, d = x.shape
    tm = min(TOK_TILE, seq)
    dw = cw.shape[1]
    return pl.pallas_call(
        functools.partial(_odd_in_kernel, tiles_per_batch=seq // tm),
        grid=(t // tm,),
        in_specs=[
            pl.BlockSpec((tm, d), lambda i: (i, 0)),
            _const_spec((1, d)),
            _const_spec(wqk.shape),
            _layer_spec(w_stacked, 0),
            _const_spec(cw.shape),
        ],
        out_specs=[
            pl.BlockSpec((tm, qw), lambda i: (i, 0)),
            pl.BlockSpec((tm, dw), lambda i: (i, 0)),
        ],
        out_shape=[
            jax.ShapeDtypeStruct((t, qw), BF16),
            jax.ShapeDtypeStruct((t, dw), BF16),
        ],
        scratch_shapes=[pltpu.VMEM((tm + 8, dw), F32)],
        compiler_params=_params("arbitrary"),
        name="odd_in",
    )(x, g, wqk, w_stacked, cw)


def _mem_kv_kernel(mem_ref, g_ref, w_ref, k_ref, v_ref):
    d = mem_ref.shape[2]
    mn = _rmsnorm(mem_ref[0], g_ref[...]).astype(BF16)
    k_ref[0] = _dot(mn, w_ref[:, 0:d].astype(BF16)).astype(BF16)
    v_ref[0] = _dot(mn, w_ref[:, d:2 * d].astype(BF16)).astype(BF16)


def _mem_kv(mem, g, w_stacked, layer):
    b, m, d = mem.shape
    return pl.pallas_call(
        _mem_kv_kernel,
        grid=(b,),
        in_specs=[
            pl.BlockSpec((1, m, d), lambda i: (i, 0, 0)),
            _const_spec((1, d)),
            _layer_spec(w_stacked, layer),
        ],
        out_specs=[pl.BlockSpec((1, m, d), lambda i: (i, 0, 0))] * 2,
        out_shape=[jax.ShapeDtypeStruct((b, m, d), BF16)] * 2,
        compiler_params=_params("arbitrary"),
        name="mem_kv",
    )(mem, g, w_stacked)


def _post_mixer_kernel(x_ref, ya_ref, yb_ref, wout_ref, g_ref, wq_ref, k_ref, v_ref, wo_ref, o_ref):
    wa = ya_ref.shape[1]
    d = x_ref.shape[1]
    hd = d // N_HEADS
    x1 = (x_ref[...] + _dot(ya_ref[...], wout_ref[0:wa, :].astype(BF16))
          + _dot(yb_ref[...], wout_ref[wa:, :].astype(BF16)))
    xn = _rmsnorm(x1, g_ref[...]).astype(BF16)
    q = (_dot(xn, wq_ref[...].astype(BF16)) * (float(hd) ** -0.5)).astype(BF16)
    heads = []
    for h in range(N_HEADS):
        lg = _dot_nt(q[:, h * hd:(h + 1) * hd], k_ref[0, :, h * hd:(h + 1) * hd])
        p = jnp.exp(lg - jnp.max(lg, axis=-1, keepdims=True))
        p = p / jnp.sum(p, axis=-1, keepdims=True)
        heads.append(_dot(p.astype(BF16), v_ref[0, :, h * hd:(h + 1) * hd]).astype(BF16))
    o_ref[...] = x1 + _dot(jnp.concatenate(heads, axis=-1), wo_ref[...].astype(BF16))


def _post_mixer(x, ya, yb, wout_stacked, g, wq_stacked, kmem, vmem, wo_stacked, layer, *, seq):
    t, d = x.shape
    tm = min(TOK_TILE, seq)
    nt = seq // tm
    m = kmem.shape[1]
    wa = ya.shape[1]
    return pl.pallas_call(
        _post_mixer_kernel,
        grid=(t // tm,),
        in_specs=[
            pl.BlockSpec((tm, d), lambda i: (i, 0)),
            pl.BlockSpec((tm, wa), lambda i: (i, 0)),
            pl.BlockSpec((tm, yb.shape[1]), lambda i: (i, 0)),
            _layer_spec(wout_stacked, 0),
            _const_spec((1, d)),
            _layer_spec(wq_stacked, layer),
            pl.BlockSpec((1, m, d), lambda i: (i // nt, 0, 0)),
            pl.BlockSpec((1, m, d), lambda i: (i // nt, 0, 0)),
            _layer_spec(wo_stacked, layer),
        ],
        out_specs=pl.BlockSpec((tm, d), lambda i: (i, 0)),
        out_shape=jax.ShapeDtypeStruct((t, d), F32),
        compiler_params=_params("arbitrary"),
        name="post_mixer",
    )(x, ya, yb, wout_stacked, g, wq_stacked, kmem, vmem, wo_stacked)


def _ffn_kernel(x_ref, g_ref, wg_ref, wu_ref, wd_ref, o_ref):
    x = x_ref[...]
    h = _rmsnorm(x, g_ref[...]).astype(BF16)
    ff = wg_ref.shape[1]
    acc = x
    for c0 in range(0, ff, FFN_CHUNK):
        c1 = min(c0 + FFN_CHUNK, ff)
        a = _dot(h, wg_ref[:, c0:c1].astype(BF16))
        u = _dot(h, wu_ref[:, c0:c1].astype(BF16))
        acc = acc + _dot((jax.nn.silu(a) * u).astype(BF16), wd_ref[c0:c1, :].astype(BF16))
    o_ref[...] = acc


def _ffn(x, g, wg, wu, wd, *, seq):
    t, d = x.shape
    tm = min(TOK_TILE_SMALL, seq)
    return pl.pallas_call(
        _ffn_kernel,
        grid=(t // tm,),
        in_specs=[
            pl.BlockSpec((tm, d), lambda i: (i, 0)),
            _const_spec((1, d)),
            _const_spec(wg.shape),
            _const_spec(wu.shape),
            _const_spec(wd.shape),
        ],
        out_specs=pl.BlockSpec((tm, d), lambda i: (i, 0)),
        out_shape=jax.ShapeDtypeStruct((t, d), F32),
        compiler_params=_params("arbitrary"),
        name="ffn",
    )(x, g, wg, wu, wd)


META_I1, META_I2, META_R1, META_R2, META_G1, META_G2 = range(6)


def _router_kernel(x_ref, g_ref, wr_ref, meta_ref, metat_ref, cnt_ref, carry_ref):
    i = pl.program_id(0)
    tm = x_ref.shape[0]

    @pl.when(i == 0)
    def _():
        carry_ref[...] = jnp.zeros_like(carry_ref)

    h = _rmsnorm(x_ref[...], g_ref[...])
    hp = _split3(h)
    wp = _split3(wr_ref[...])
    logits = _dot(hp[1], wp[0]) + _dot(hp[0], wp[1]) + _dot(hp[0], wp[0])
    lane = lax.broadcasted_iota(I32, logits.shape, 1)
    logits = jnp.where(lane < N_EXPERTS, logits, NEG)
    m1 = jnp.max(logits, axis=-1, keepdims=True)
    i1 = jnp.min(jnp.where(logits == m1, lane, LANES), axis=-1, keepdims=True)
    rest = jnp.where(lane == i1, NEG, logits)
    m2 = jnp.max(rest, axis=-1, keepdims=True)
    i2 = jnp.min(jnp.where(rest == m2, lane, LANES), axis=-1, keepdims=True)
    e = jnp.exp(m2 - m1)
    g1 = 1.0 / (1.0 + e)
    g2 = e / (1.0 + e)

    sel = jnp.where((lane == i1) | (lane == i2), 1.0, 0.0)
    row = lax.broadcasted_iota(I32, (tm, tm), 0)
    col = lax.broadcasted_iota(I32, (tm, tm), 1)
    strict = jnp.where(col < row, 1.0, 0.0).astype(BF16)
    rank =_dot(strict, sel.astype(BF16)) + carry_ref[0:1, :]
    total = rank[tm - 1:tm, :] + sel[tm - 1:tm, :]
    carry_ref[...] = jnp.broadcast_to(total, carry_ref.shape)
    cnt_ref[...] = jnp.broadcast_to(total, cnt_ref.shape)
    r1 = jnp.sum(jnp.where(lane == i1, rank, 0.0), axis=-1, keepdims=True)
    r2 = jnp.sum(jnp.where(lane == i2, rank, 0.0), axis=-1, keepdims=True)
    meta = jnp.zeros_like(logits)
    for slot, val in ((META_I1, i1.astype(F32)), (META_I2, i2.astype(F32)), (META_R1, r1),
                      (META_R2, r2), (META_G1, g1), (META_G2, g2)):
        meta = jnp.where(lane == slot, val, meta)
    meta_ref[...] = meta
    metat_ref[...] = meta.T[0:8, :]


def _router(x, g, wr, *, seq):
    t, d = x.shape
    tm = min(TOK_TILE, seq)
    return pl.pallas_call(
        _router_kernel,
        grid=(t // tm,),
        in_specs=[
            pl.BlockSpec((tm, d), lambda i: (i, 0)),
            _const_spec((1, d)),
            _const_spec(wr.shape),
        ],
        out_specs=[
            pl.BlockSpec((tm, LANES), lambda i: (i, 0)),
            pl.BlockSpec((8, tm), lambda i: (0, i)),
            pl.BlockSpec((8, LANES), lambda i: (0, 0)),
        ],
        out_shape=[
            jax.ShapeDtypeStruct((t, LANES), F32),
            jax.ShapeDtypeStruct((8, t), F32),
            jax.ShapeDtypeStruct((8, LANES), F32),
        ],
        scratch_shapes=[pltpu.VMEM((8, LANES), F32)],
        compiler_params=_params("arbitrary"),
        name="router",
    )(x, g, wr)


def _row_copy(src, src_row, dst, dst_row, sem):
    return pltpu.make_async_copy(src.at[pl.ds(src_row, 1)], dst.at[pl.ds(dst_row, 1)], sem)


def _dispatch_kernel(pos_ref, pad_ref, x_ref, g_ref, xs_ref, h_ref, zrow_ref, sem_ref, zsem_ref):
    c = pl.program_id(0)
    n_chunks = pl.num_programs(0)
    tc = x_ref.shape[0]
    slot = c % 2

    @pl.when(c == 0)
    def _():
        zrow_ref[...] = jnp.zeros_like(zrow_ref)
        for e in range(N_EXPERTS):
            start = pad_ref[2 * e]
            count = pad_ref[2 * e + 1] - start

            def zissue(r, carry, start=start):
                _row_copy(zrow_ref, 0, xs_ref, start + r, zsem_ref).start()
                return carry

            def zwait(r, carry):
                _row_copy(zrow_ref, 0, xs_ref, 0, zsem_ref).wait()
                return carry

            lax.fori_loop(0, count, zissue, 0)
            lax.fori_loop(0, count, zwait, 0)

        tail = pad_ref[2 * N_EXPERTS - 1]
        groups = (xs_ref.shape[0] - tail) // 8

        def tissue(r, carry):
            start = pl.multiple_of(tail + 8 * r, 8)
            pltpu.make_async_copy(zrow_ref, xs_ref.at[pl.ds(start, 8)], zsem_ref).start()
            return carry

        def twait(r, carry):
            pltpu.make_async_copy(zrow_ref, xs_ref.at[pl.ds(0, 8)], zsem_ref).wait()
            return carry

        lax.fori_loop(0, groups, tissue, 0)
        lax.fori_loop(0, groups, twait, 0)

    h_ref[slot] = _rmsnorm(x_ref[...], g_ref[...])

    def issue(r, carry):
        tok = c * tc + r
        _row_copy(h_ref.at[slot], r, xs_ref, pos_ref[tok], sem_ref.at[slot]).start()
        _row_copy(h_ref.at[slot], r, xs_ref, pos_ref[n_chunks * tc + tok], sem_ref.at[slot]).start()
        return carry

    def drain(s):
        for _ in range(2):
            pltpu.make_async_copy(h_ref.at[s], xs_ref.at[pl.ds(0, tc)], sem_ref.at[s]).wait()

    lax.fori_loop(0, tc, issue, 0, unroll=DMA_UNROLL)

    @pl.when(c > 0)
    def _():
        drain(1 - slot)

    @pl.when(c == n_chunks - 1)
    def _():
        drain(slot)


def _dispatch(pos, pad, x, g, n_rows):
    t, d = x.shape
    tc = DISPATCH_CHUNK
    grid_spec = pltpu.PrefetchScalarGridSpec(
        num_scalar_prefetch=2,
        grid=(t // tc,),
        in_specs=[
            pl.BlockSpec((tc, d), lambda c, pos, pad: (c, 0)),
            pl.BlockSpec((1, d), lambda c, pos, pad: (0, 0)),
        ],
        out_specs=pl.BlockSpec(memory_space=pl.ANY),
        scratch_shapes=[
            pltpu.VMEM((2, tc, d), F32),
            pltpu.VMEM((8, d), F32),
            pltpu.SemaphoreType.DMA((2,)),
            pltpu.SemaphoreType.DMA(()),
        ],
    )
    return pl.pallas_call(
        _dispatch_kernel,
        grid_spec=grid_spec,
        out_shape=jax.ShapeDtypeStruct((n_rows, d), F32),
        compiler_params=_params("arbitrary"),
        name="moe_dispatch",
    )(pos, pad, x, g)


def _experts_kernel(te_ref, rows_ref, nused_ref, xs_ref, wg_ref, wu_ref, wd_ref, y_ref, xb_ref):
    p = pl.program_id(0)
    f = pl.program_id(1)
    rows = rows_ref[p]
    half = MOE_TILE // 2

    @pl.when(f == 0)
    def _():
        y_ref[...] = jnp.zeros_like(y_ref)

    @pl.when(rows > 0)
    def _():
        @pl.when(f == 0)
        def _():
            xb_ref[...] = xs_ref[...].astype(BF16)

        tf = wg_ref.shape[2]
        wgu = jnp.concatenate([wg_ref[0].astype(BF16), wu_ref[0].astype(BF16)], axis=1)
        wd = wd_ref[0].astype(BF16)

        def run(r0):
            gu = _dot(xb_ref[r0:r0 + half, :], wgu)
            act = (jax.nn.silu(gu[:, :tf]) * gu[:, tf:]).astype(BF16)
            y_ref[r0:r0 + half, :] += _dot(act, wd)

        run(0)

        @pl.when(rows > half)
        def _():
            run(half)


def _experts(tile_expert, tile_rows, n_used, xs, wg, wu, wd):
    n_rows, d = xs.shape
    ff = wg.shape[2]
    tm = MOE_TILE
    tf = MOE_FF_CHUNK
    n_tiles = n_rows // tm

    def x_map(p, f, te, tr, nu):
        return (jnp.minimum(p, nu[0] - 1), 0)

    grid_spec = pltpu.PrefetchScalarGridSpec(
        num_scalar_prefetch=3,
        grid=(n_tiles, ff // tf),
        in_specs=[
            pl.BlockSpec((tm, d), x_map),
            pl.BlockSpec((1, d, tf), lambda p, f, te, tr, nu: (te[p], 0, f)),
            pl.BlockSpec((1, d, tf), lambda p, f, te, tr, nu: (te[p], 0, f)),
            pl.BlockSpec((1, tf, d), lambda p, f, te, tr, nu: (te[p], f, 0)),
        ],
        out_specs=pl.BlockSpec((tm, d), lambda p, f, te, tr, nu: (p, 0)),
        scratch_shapes=[pltpu.VMEM((tm, d), BF16)],
    )
    return pl.pallas_call(
        _experts_kernel,
        grid_spec=grid_spec,
        out_shape=jax.ShapeDtypeStruct((n_rows, d), F32),
        compiler_params=_params("arbitrary", "arbitrary"),
        name="moe_experts",
    )(tile_expert, tile_rows, n_used, xs, wg, wu, wd)


def _combine_kernel(pos_ref, y_ref, x_ref, meta_ref, g_ref, o_ref, buf_ref, sem_ref):
    i = pl.program_id(0)
    n = pl.num_programs(0)
    tc = x_ref.shape[0]

    def fetch(tile, slot):
        def body(r, carry):
            tok = tile * tc + r
            for k in range(2):
                pltpu.make_async_copy(y_ref.at[pl.ds(pos_ref[k * n * tc + tok], 1)],
                                      buf_ref.at[slot, k, pl.ds(r, 1)], sem_ref.at[slot]).start()
            return carry
        lax.fori_loop(0, tc, body, 0, unroll=DMA_UNROLL)

    @pl.when(i == 0)
    def _():
        fetch(0, 0)

    @pl.when(i + 1 < n)
    def _():
        fetch(i + 1, (i + 1) % 2)

    slot = i % 2

    for k in range(2):
        pltpu.make_async_copy(y_ref.at[pl.ds(0, tc)], buf_ref.at[slot, k], sem_ref.at[slot]).wait()
    meta = meta_ref[...]
    g1 = meta[:, META_G1:META_G1 + 1]
    g2 = meta[:, META_G2:META_G2 + 1]
    x = x_ref[...] + g1 * buf_ref[slot, 0] + g2 * buf_ref[slot, 1]
    o_ref[...] = _rmsnorm(x, g_ref[...])


def _combine(pos, y, x, meta, g):
    t, d = x.shape
    tc = COMBINE_TILE
    grid_spec = pltpu.PrefetchScalarGridSpec(
        num_scalar_prefetch=1,
        grid=(t // tc,),
        in_specs=[
            pl.BlockSpec(memory_space=pl.ANY),
            pl.BlockSpec((tc, d), lambda i, pos: (i, 0)),
            pl.BlockSpec((tc, LANES), lambda i, pos: (i, 0)),
            pl.BlockSpec((1, d), lambda i, pos: (0, 0)),
        ],
        out_specs=pl.BlockSpec((tc, d), lambda i, pos: (i, 0)),
        scratch_shapes=[pltpu.VMEM((2, 2, tc, d), F32), pltpu.SemaphoreType.DMA((2,))],
    )
    return pl.pallas_call(
        _combine_kernel,
        grid_spec=grid_spec,
        out_shape=jax.ShapeDtypeStruct((t, d), F32),
        compiler_params=_params("arbitrary"),
        name="moe_combine",
    )(pos, y, x, meta, g)


def _moe_plan(meta_t, counts, n_tiles):
    cnt = counts[0, :N_EXPERTS].astype(I32)
    tiles = (cnt + MOE_TILE - 1) // MOE_TILE
    tile_end = jnp.cumsum(tiles)
    row_off = (tile_end - tiles) * MOE_TILE
    n_used = tile_end[-1:]
    experts = jnp.arange(N_EXPERTS, dtype=I32)

    def rows(idx_row, rank_row):
        idx = meta_t[idx_row].astype(I32)
        off = jnp.sum(jnp.where(idx[None, :] == experts[:, None], row_off[:, None], 0), axis=0)
        return off + meta_t[rank_row].astype(I32)

    pos = jnp.concatenate([rows(META_I1, META_R1), rows(META_I2, META_R2)])
    tile_ids = jnp.arange(n_tiles, dtype=I32)
    te = jnp.sum((tile_end[None, :] <= tile_ids[:, None]).astype(I32), axis=1)
    mine = te[:, None] == experts[None, :]
    first_row = tile_ids * MOE_TILE - jnp.sum(jnp.where(mine, row_off[None, :], 0), axis=1)
    tile_rows = jnp.clip(jnp.sum(jnp.where(mine, cnt[None, :], 0), axis=1) - first_row, 0, MOE_TILE)
    last = jnp.sum((tile_end <= n_used - 1).astype(I32))
    te = jnp.minimum(te, last).astype(I32)
    pad = jnp.stack([row_off + cnt, row_off + tiles * MOE_TILE], axis=1).reshape(-1).astype(I32)
    return pos.astype(I32), pad, te, tile_rows.astype(I32), n_used.astype(I32)


def _moe_and_final_norm(x, g_ffn, wr, wg, wu, wd, g_final, *, seq):
    t, d = x.shape
    n_tiles = 2 * t // MOE_TILE + N_EXPERTS
    meta, meta_t, counts = _router(x, g_ffn, wr, seq=seq)
    pos, pad, te, tile_rows, n_used = _moe_plan(meta_t, counts, n_tiles)
    xs = _dispatch(pos, pad, x, g_ffn, n_tiles * MOE_TILE)
    y = _experts(te, tile_rows, n_used, xs, wg, wu, wd)
    return _combine(pos, y, x, meta, g_final)


def _row(v):
    return v.reshape(1, -1).astype(F32)


def kernel(x, mem, norm_mix, norm_mem_q, norm_mem_kv, norm_ffn, norm_final, even_w_in, fox_b_f, gmlp_v_gain, gmlp_w_s, gmlp_b_s, even_w_out, odd_w_in, diff_lambda_q1, diff_lambda_k1, diff_lambda_q2, diff_lambda_k2, diff_subln_gain, conv_w, odd_w_out, rel_bias, mem_w_q, mem_w_kv, mem_w_o, ffn_w_gate, ffn_w_up, ffn_w_down, router_w, moe_w_gate, moe_w_up, moe_w_down):
    batch, seq, d = x.shape
    t = batch * seq
    depth = norm_mix.shape[0]
    assert depth == 2 and seq % (2 * ATT_BLOCK) == 0 and seq % TOK_TILE == 0
    xf = x.reshape(t, d)
    hd = d // N_HEADS
    aw = gmlp_v_gain.shape[1]
    n_blk = gmlp_w_s.shape[2]

    gate_cols = even_w_in.shape[2] - N_HEADS
    wf = jnp.pad(even_w_in[0, :, gate_cols:], ((0, 0), (0, LANES - N_HEADS)))
    bf = jnp.pad(fox_b_f[0], (0, LANES - N_HEADS)).reshape(1, LANES)
    bs = jnp.broadcast_to(gmlp_b_s[0][:, :, None], (A_GROUPS, n_blk, aw // A_GROUPS)).astype(F32)
    ya, qkv, aug = _even_in(xf, _row(norm_mix[0]), even_w_in, wf, bf, _row(gmlp_v_gain[0]),
                            gmlp_w_s[0], bs, seq=seq)
    yb = _fox_attention(qkv, aug, batch=batch, seq=seq)

    def mem_attention(layer, x_in, ya_, yb_, w_out):
        km, vm = _mem_kv(mem, _row(norm_mem_kv[layer]), mem_w_kv, layer)
        return _post_mixer(x_in, ya_, yb_, w_out, _row(norm_mem_q[layer]), mem_w_q, km, vm,
                           mem_w_o, layer, seq=seq)

    xf = mem_attention(0, xf, ya, yb, even_w_out)
    xf = _ffn(xf, _row(norm_ffn[0]), ffn_w_gate[0], ffn_w_up[0], ffn_w_down[0], seq=seq)

    qk_dim = LANES // 2
    qk_w = 2 * N_HEADS * qk_dim

    def regroup(cols):
        return cols.reshape(d, 2, N_HEADS, qk_dim).transpose(0, 2, 1, 3).reshape(d, qk_w)

    wqk = jnp.concatenate([regroup(odd_w_in[0, :, :qk_w]) * float(qk_dim) ** -0.5,
                           regroup(odd_w_in[0, :, qk_w:2 * qk_w])], axis=1).astype(BF16)
    qkv, yd = _odd_in(xf, _row(norm_mix[1]), wqk, odd_w_in, conv_w[0].astype(F32), seq=seq,
                      qw=2 * qk_w + N_HEADS * LANES)
    lam_init = 0.8 - 0.6 * math.exp(-0.3 * 1)
    lam = (jnp.exp(jnp.sum(diff_lambda_q1[0] * diff_lambda_k1[0]))
           - jnp.exp(jnp.sum(diff_lambda_q2[0] * diff_lambda_k2[0])) + lam_init)
    bias = _diff_bias_blocks(rel_bias, min(ATT_BLOCK, seq))
    yc = _diff_attention(qkv, lam.reshape(1, 1).astype(F32), bias, _row(diff_subln_gain[0]),
                         batch=batch, seq=seq, out_scale=1.0 - lam_init)
    xf = mem_attention(1, xf, yc, yd, odd_w_out)
    wr = jnp.pad(router_w[0], ((0, 0), (0, LANES - N_EXPERTS))).astype(F32)
    out = _moe_and_final_norm(xf, _row(norm_ffn[1]), wr, moe_w_gate[0], moe_w_up[0], moe_w_down[0],
                              _row(norm_final), seq=seq)
    return out.reshape(batch, seq, d)
```

```python
import functools
import math

import numpy as np
import jax
import jax.numpy as jnp
from jax import lax
from jax.experimental import pallas as pl
from jax.experimental.pallas import tpu as pltpu

F32 = jnp.float32
BF16 = jnp.bfloat16
I32 = jnp.int32

EPS = 1e-6
NEG = -1e30
LANES = 128
VMEM_LIMIT = 56 * 1024 * 1024

CHUNK = 64
A_GROUPS = 4
N_HEADS = 4
N_EXPERTS = 8
REL_BUCKETS = 32
REL_MAX_DIST = 128

TOK_TILE = 1024
TOK_TILE_SMALL = 512
FOX_BLOCK = 1024
ATT_BLOCK = 512
ATT_STRIP = 64
LAG_LIMIT = 40.0
MOE_TILE = 1024
MOE_FF_CHUNK = 512
MOE_WEIGHT_BUFFERS = 3
FFN_CHUNK = 512
DISPATCH_CHUNK = 256
COMBINE_TILE = 256
DMA_UNROLL = 8


def _dot(a, b):
    return jnp.dot(a, b, preferred_element_type=F32)


def _dot_nt(a, b):
    return lax.dot_general(a, b, (((1,), (1,)), ((), ())), preferred_element_type=F32)


def _rmsnorm(x, g):
    return x * lax.rsqrt(jnp.mean(x * x, axis=-1, keepdims=True) + EPS) * g


def _params(*sem):
    return pltpu.CompilerParams(dimension_semantics=sem, vmem_limit_bytes=VMEM_LIMIT)


def _const_spec(shape):
    return pl.BlockSpec(shape, lambda *_: (0,) * len(shape), pipeline_mode=pl.Buffered(1))


def _layer_spec(stacked, layer):
    rest = stacked.shape[1:]
    return pl.BlockSpec((None,) + rest, lambda *_: (layer,) + (0,) * len(rest),
                        pipeline_mode=pl.Buffered(1))


def _split3(v):
    a1 = v.astype(BF16)
    r1 = v - a1.astype(F32)
    a2 = r1.astype(BF16)
    a3 = (r1 - a2.astype(F32)).astype(BF16)
    return a1, a2, a3


def _even_in_kernel(x_ref, g_ref, w_ref, wf_ref, bf_ref, vg_ref, ws_ref, bs_ref,
                    ya_ref, qkv_ref, aug_ref, carry_ref, *, tiles_per_batch):
    i = pl.program_id(0)
    tm = x_ref.shape[0]
    aw = ya_ref.shape[1]
    qw = qkv_ref.shape[1]

    @pl.when(i % tiles_per_batch == 0)
    def _():
        carry_ref[...] = jnp.zeros_like(carry_ref)

    xn = _rmsnorm(x_ref[...], g_ref[...]).astype(BF16)

    hw = N_HEADS * LANES
    q = _dot(xn, w_ref[:, 2 * aw:2 * aw + hw].astype(BF16)) * (float(LANES) ** -0.5)
    qkv_ref[:, 0:hw] = q.astype(BF16)
    qkv_ref[:, hw:qw] = _dot(xn, w_ref[:, 2 * aw + hw:2 * aw + qw].astype(BF16)).astype(BF16)

    fl = _dot(xn, wf_ref[...].astype(BF16)) + bf_ref[...]
    ls = jnp.minimum(fl, 0.0) - jnp.log1p(jnp.exp(-jnp.abs(fl)))
    row = lax.broadcasted_iota(I32, (tm, tm), 0)
    col = lax.broadcasted_iota(I32, (tm, tm), 1)
    tri = jnp.where(col <= row, 1.0, 0.0).astype(BF16)
    a1, a2, a3 = _split3(ls)
    csum = _dot(tri, a1) + _dot(tri, a2) + _dot(tri, a3)
    csum = csum + carry_ref[0:1, :]
    carry_ref[...] = jnp.broadcast_to(csum[tm - 1:tm, :], carry_ref.shape)

    lane = lax.broadcasted_iota(I32, (tm, LANES), 1)
    for h in range(N_HEADS):
        c1, c2, c3 = (piece.astype(F32)
                      for piece in _split3(jnp.broadcast_to(csum[:, h:h + 1], (tm, LANES))))
        aq = jnp.where(lane == 0, c1, jnp.where(lane == 1, c2, jnp.where(lane == 2, c3,
                       jnp.where(lane < 6, 1.0, 0.0))))
        ak = jnp.where(lane == 3, -c1, jnp.where(lane == 4, -c2, jnp.where(lane == 5, -c3,
                       jnp.where(lane < 3, 1.0, 0.0))))
        aug_ref[:, h * LANES:(h + 1) * LANES] = aq.astype(BF16)
        aug_ref[:, (N_HEADS + h) * LANES:(N_HEADS + h + 1) * LANES] = ak.astype(BF16)

    gu = jax.nn.gelu(_dot(xn, w_ref[:, 0:aw].astype(BF16)), approximate=True)
    gv = jax.nn.gelu(_dot(xn, w_ref[:, aw:2 * aw].astype(BF16)), approximate=True)
    blk = ws_ref.shape[1]
    ch = aw // A_GROUPS
    r = lax.broadcasted_iota(I32, (blk, blk), 0)
    c = lax.broadcasted_iota(I32, (blk, blk), 1)
    causal = (c // CHUNK) <= (r // CHUNK)
    for g in range(A_GROUPS):
        vgrp = gv[:, g * ch:(g + 1) * ch]
        vn = _rmsnorm(vgrp, vg_ref[:, g * ch:(g + 1) * ch]).astype(BF16)
        wmix = jnp.where(causal, ws_ref[g], 0.0).astype(BF16)
        for n in range(tm // blk):
            mixed = _dot(wmix, vn[n * blk:(n + 1) * blk, :]) + bs_ref[g]
            ya_ref[n * blk:(n + 1) * blk, g * ch:(g + 1) * ch] = (
                gu[n * blk:(n + 1) * blk, g * ch:(g + 1) * ch] * mixed).astype(BF16)


def _even_in(x, g, w_stacked, wf, bf, vgain, ws, bs, *, seq):
    t, d = x.shape
    tm = min(TOK_TILE_SMALL, seq)
    aw = vgain.shape[1]
    qw = 3 * N_HEADS * LANES
    blk = ws.shape[1]
    return pl.pallas_call(
        functools.partial(_even_in_kernel, tiles_per_batch=seq // tm),
        grid=(t // tm,),
        in_specs=[
            pl.BlockSpec((tm, d), lambda i: (i, 0)),
            _const_spec((1, d)),
            _layer_spec(w_stacked, 0),
            _const_spec(wf.shape),
            _const_spec((1, LANES)),
            _const_spec((1, aw)),
            _const_spec(ws.shape),
            _const_spec(bs.shape),
        ],
        out_specs=[
            pl.BlockSpec((tm, aw), lambda i: (i, 0)),
            pl.BlockSpec((tm, qw), lambda i: (i, 0)),
            pl.BlockSpec((tm, 2 * N_HEADS * LANES), lambda i: (i, 0)),
        ],
        out_shape=[
            jax.ShapeDtypeStruct((t, aw), BF16),
            jax.ShapeDtypeStruct((t, qw), BF16),
            jax.ShapeDtypeStruct((t, 2 * N_HEADS * LANES), BF16),
        ],
        scratch_shapes=[pltpu.VMEM((8, LANES), F32)],
        compiler_params=_params("arbitrary"),
        name="even_in",
    )(x, g, w_stacked, wf, bf, vgain, ws, bs)


def _tri_pairs(nblk):
    qi = np.array([i for i in range(nblk) for _ in range(i + 1)], np.int32)
    kj = np.array([j for i in range(nblk) for j in range(i, -1, -1)], np.int32)
    return jnp.asarray(qi), jnp.asarray(kj)


def _softmax_strips(logits, p_ref, alpha_ref, m_in, m_out, add_bias=None):
    rows, tk = logits.shape
    for r0 in range(0, rows, ATT_STRIP):
        rs = slice(r0, r0 + ATT_STRIP)
        s = logits[rs, :]
        if add_bias is not None:
            s = add_bias(s, r0)
        m_cur = jnp.max(s, axis=-1, keepdims=True)
        if m_in is None:
            m_new = jnp.broadcast_to(m_cur, (ATT_STRIP, LANES))
        else:
            m_prev = m_in[rs, :]
            m_new = jnp.maximum(m_prev, m_cur)
            alpha_ref[rs, :] = jnp.exp(m_prev - m_new)
        m_out[rs, :] = m_new
        p_ref[rs, 0:tk] = jnp.exp(
            (s - jnp.concatenate([m_new] * (tk // LANES), axis=-1)).astype(BF16))


def _softmax_strips_lagged(logits, p_ref, alpha_ref, m_in, m_out, excess):
    rows, tk = logits.shape
    for r0 in range(0, rows, ATT_STRIP):
        rs = slice(r0, r0 + ATT_STRIP)
        s = logits[rs, :]
        m_prev = m_in[rs, :]
        p_ref[rs, 0:tk] = jnp.exp(
            (s - jnp.concatenate([m_prev] * (tk // LANES), axis=-1)).astype(BF16))
        m_cur = jnp.max(s, axis=-1, keepdims=True)
        m_new = jnp.maximum(m_prev, m_cur)
        alpha_ref[rs, :] = jnp.exp(m_prev - m_new)
        m_out[rs, :] = m_new
        excess = jnp.maximum(excess, m_cur - m_prev)
    return excess


def _accumulate(p_ref, alpha_ref, v, ones, acc_in, acc_out, lagged=False):
    pv = _dot(p_ref[:, 0:v.shape[0]], jnp.concatenate([v, ones], axis=-1))
    if acc_in is None:
        acc_out[...] = pv
        return
    alpha = alpha_ref[...]
    alpha2 = jnp.concatenate([alpha, alpha], axis=-1)
    acc_out[...] = alpha2 * (acc_in[...] + pv) if lagged else alpha2 * acc_in[...] + pv


def _fox_kernel(qi_ref, kj_ref, q_ref, aq_ref, k_ref, ak_ref, v_ref, ones_ref, mask_ref, o_ref,
                p_ref, alpha_ref, m_ref, acc_ref):
    step = pl.program_id(1)
    i = qi_ref[step]
    j = kj_ref[step]
    parity = (i - j) % 2

    def logits(h):
        hs = slice(h * LANES, (h + 1) * LANES)
        qa = jnp.concatenate([q_ref[:, hs], aq_ref[:, hs]], axis=-1)
        ka = jnp.concatenate([k_ref[:, hs], ak_ref[:, hs]], axis=-1)
        return _dot_nt(qa, ka)

    def sweep(mode, src, dst):
        excess = jnp.full((ATT_STRIP, LANES), NEG, F32)
        nxt = logits(0)
        for h in range(N_HEADS):
            hs = slice(h * LANES, (h + 1) * LANES)
            sb = h % 2
            cur = nxt
            if h + 1 < N_HEADS:
                nxt = logits(h + 1)
            p_h, alpha_h = p_ref.at[sb], alpha_ref.at[sb]
            m_out, acc_out = m_ref.at[dst, h], acc_ref.at[dst, h]
            if mode != "diagonal":
                m_in, acc_in = m_ref.at[src, h], acc_ref.at[src, h]
            if mode == "diagonal":
                add_mask = lambda s, r0: s + mask_ref[r0:r0 + ATT_STRIP, :]
                _softmax_strips(cur, p_h, alpha_h, None, m_out, add_mask)
                _accumulate(p_h, alpha_h, v_ref[:, hs], ones_ref[...], None, acc_out)
            elif mode == "lagged":
                excess = _softmax_strips_lagged(cur, p_h, alpha_h, m_in, m_out, excess)
                _accumulate(p_h, alpha_h, v_ref[:, hs], ones_ref[...], acc_in, acc_out, lagged=True)
            else:
                _softmax_strips(cur, p_h, alpha_h, m_in, m_out)
                _accumulate(p_h, alpha_h, v_ref[:, hs], ones_ref[...], acc_in, acc_out)
        return jnp.max(excess)

    @pl.when(j == i)
    def _():
        sweep("diagonal", None, 1)

    for src in (0, 1):
        @pl.when((j < i) & (parity == src))
        def _(src=src):
            worst = sweep("lagged", src, 1 - src)

            @pl.when(worst > LAG_LIMIT)
            def _():
                sweep("standard", src, 1 - src)

    @pl.when(j == 0)
    def _():
        for h in range(N_HEADS):
            acc = acc_ref[1 - parity, h]
            o_ref[:, h * LANES:(h + 1) * LANES] = (
                acc[:, :LANES] / acc[:, LANES:LANES + 1]).astype(o_ref.dtype)


def _ones_column(rows):
    ones = np.zeros((rows, LANES), np.float32)
    ones[:, 0] = 1.0
    return jnp.asarray(ones, BF16)


def _fox_attention(qkv, aug, *, batch, seq):
    t = qkv.shape[0]
    w = N_HEADS * LANES
    blk = min(FOX_BLOCK, seq)
    nblk = seq // blk
    qi, kj = _tri_pairs(nblk)
    r = np.arange(blk)
    mask = jnp.asarray(np.where(r[None, :] <= r[:, None], 0.0, NEG), F32)
    grid_spec = pltpu.PrefetchScalarGridSpec(
        num_scalar_prefetch=2,
        grid=(batch, qi.shape[0]),
        in_specs=[
            pl.BlockSpec((blk, w), lambda b, s, qi, kj: (b * nblk + qi[s], 0)),
            pl.BlockSpec((blk, w), lambda b, s, qi, kj: (b * nblk + qi[s], 0)),
            pl.BlockSpec((blk, w), lambda b, s, qi, kj: (b * nblk + kj[s], 1)),
            pl.BlockSpec((blk, w), lambda b, s, qi, kj: (b * nblk + kj[s], 1)),
            pl.BlockSpec((blk, w), lambda b, s, qi, kj: (b * nblk + kj[s], 2)),
            _const_spec((blk, LANES)),
            _const_spec((blk, blk)),
        ],
        out_specs=pl.BlockSpec((blk, w), lambda b, s, qi, kj: (b * nblk + qi[s], 0)),
        scratch_shapes=[
            pltpu.VMEM((2, blk, blk), BF16),
            pltpu.VMEM((2, blk, LANES), F32),
            pltpu.VMEM((2, N_HEADS, blk, LANES), F32),
            pltpu.VMEM((2, N_HEADS, blk, 2 * LANES), F32),
        ],
    )
    return pl.pallas_call(
        _fox_kernel,
        grid_spec=grid_spec,
        out_shape=jax.ShapeDtypeStruct((t, w), BF16),
        compiler_params=_params("arbitrary", "arbitrary"),
        name="fox_attention",
    )(qi, kj, qkv, aug, qkv, aug, qkv, _ones_column(blk), mask)


def _diff_pairs(nq):
    qi = np.array([i for i in range(nq) for _ in range(i // 2 + 1)], np.int32)
    kj = np.array([j for i in range(nq) for j in range(i // 2 + 1)], np.int32)
    return jnp.asarray(qi), jnp.asarray(kj)


def _diff_kernel(qi_ref, kj_ref, lam_ref, q_ref, k_ref, v_ref, ones_ref, bias_ref, gain_ref, o_ref,
                 qs_ref, p_ref, alpha_ref, m_ref, acc_ref, *, out_scale):
    step = pl.program_id(1)
    i = qi_ref[step]
    j = kj_ref[step]
    tq = q_ref.shape[0]
    half = LANES // 2
    jd = i // 2
    odd = (i % 2) == 1

    @pl.when(j == 0)
    def _():
        m_ref[...] = jnp.full_like(m_ref, NEG)
        acc_ref[...] = jnp.zeros_like(acc_ref)
        lane = lax.broadcasted_iota(I32, (tq, LANES), 1)
        zero = jnp.zeros((tq, LANES), BF16)
        for h in range(N_HEADS):
            q = q_ref[:, h * LANES:(h + 1) * LANES]
            qs_ref[h, 0:tq, :] = jnp.where(lane < half, q, zero)
            qs_ref[h, tq:2 * tq, :] = jnp.where(lane < half, zero, q)

    def sweep(tk, add_bias):
        def logits(h):
            return _dot_nt(qs_ref[h], k_ref[0:tk, h * LANES:(h + 1) * LANES])

        nxt = logits(0)
        for h in range(N_HEADS):
            hs = slice(h * LANES, (h + 1) * LANES)
            sb = h % 2
            cur = nxt
            if h + 1 < N_HEADS:
                nxt = logits(h + 1)
            bias_fn = None if add_bias is None else functools.partial(add_bias, h)
            _softmax_strips(cur, p_ref.at[sb], alpha_ref.at[sb], m_ref.at[h], m_ref.at[h], bias_fn)
            _accumulate(p_ref.at[sb], alpha_ref.at[sb], v_ref[0:tk, hs], ones_ref[0:tk, :],
                        acc_ref.at[h], acc_ref.at[h])

    def bias_rows(h, which, r0):
        return bias_ref[h, which, pl.ds(r0 % tq, ATT_STRIP), :]

    def odd_diag(h, s, r0):
        return jnp.concatenate([s[:, :tq] + bias_rows(h, 1, r0), s[:, tq:] + bias_rows(h, 0, r0)],
                               axis=-1)

    def even_prev(h, s, r0):
        return jnp.concatenate([s[:, :tq], s[:, tq:] + bias_rows(h, 1, r0)], axis=-1)

    def even_diag(h, s, r0):
        return s + bias_rows(h, 0, r0)

    @pl.when((j < jd - 1) | ((j == jd - 1) & odd))
    def _():
        sweep(2 * tq, None)

    @pl.when((j == jd - 1) & jnp.logical_not(odd))
    def _():
        sweep(2 * tq, even_prev)

    @pl.when((j == jd) & odd)
    def _():
        sweep(2 * tq, odd_diag)

    @pl.when((j == jd) & jnp.logical_not(odd))
    def _():
        sweep(tq, even_diag)

    @pl.when(j == jd)
    def _():
        for h in range(N_HEADS):
            hs = slice(h * LANES, (h + 1) * LANES)
            acc = acc_ref[h]
            o = acc[:, :LANES] / acc[:, LANES:LANES + 1]
            o = o[0:tq, :] - lam_ref[0, 0] * o[tq:2 * tq, :]
            o_ref[:, hs] = (_rmsnorm(o, gain_ref[:, hs]) * out_scale).astype(o_ref.dtype)


def _diff_attention(qkv, lam, bias, gain, *, batch, seq, out_scale):
    t = qkv.shape[0]
    w = N_HEADS * LANES
    blk = bias.shape[2]
    kblk = 2 * blk
    nq = seq // blk
    nk = seq // kblk
    qi, kj = _diff_pairs(nq)
    grid_spec = pltpu.PrefetchScalarGridSpec(
        num_scalar_prefetch=2,
        grid=(batch, qi.shape[0]),
        in_specs=[
            pl.BlockSpec(memory_space=pltpu.SMEM),
            pl.BlockSpec((blk, w), lambda b, s, qi, kj: (b * nq + qi[s], 0)),
            pl.BlockSpec((kblk, w), lambda b, s, qi, kj: (b * nk + kj[s], 1)),
            pl.BlockSpec((kblk, w), lambda b, s, qi, kj: (b * nk + kj[s], 2)),
            pl.BlockSpec((kblk, LANES), lambda b, s, qi, kj: (0, 0)),
            pl.BlockSpec(bias.shape, lambda b, s, qi, kj: (0, 0, 0, 0)),
            pl.BlockSpec((1, w), lambda b, s, qi, kj: (0, 0)),
        ],
        out_specs=pl.BlockSpec((blk, w), lambda b, s, qi, kj: (b * nq + qi[s], 0)),
        scratch_shapes=[
            pltpu.VMEM((N_HEADS, 2 * blk, LANES), BF16),
            pltpu.VMEM((2, 2 * blk, kblk), BF16),
            pltpu.VMEM((2, 2 * blk, LANES), F32),
            pltpu.VMEM((N_HEADS, 2 * blk, LANES), F32),
            pltpu.VMEM((N_HEADS, 2 * blk, 2 * LANES), F32),
        ],
    )
    return pl.pallas_call(
        functools.partial(_diff_kernel, out_scale=out_scale),
        grid_spec=grid_spec,
        out_shape=jax.ShapeDtypeStruct((t, w), BF16),
        compiler_params=_params("arbitrary", "arbitrary"),
        name="diff_attention",
    )(qi, kj, lam, qkv, qkv, qkv, _ones_column(kblk), bias, gain)


def _rel_bucket(rel):
    n_half = REL_BUCKETS // 2
    max_exact = n_half // 2
    ret = jnp.where(rel > 0, n_half, 0)
    n = jnp.abs(rel)
    nf = jnp.maximum(n, 1).astype(F32)
    large = max_exact + (jnp.log(nf / max_exact) / math.log(REL_MAX_DIST / max_exact)
                         * (n_half - max_exact)).astype(I32)
    large = jnp.minimum(large, n_half - 1)
    return ret + jnp.where(n < max_exact, n, large)


def _rel_bias_kernel(table_ref, idx_ref, o_ref):
    h = pl.program_id(0)
    idx = idx_ref[0]
    far = table_ref[REL_BUCKETS // 2 - 1, h]
    out = jnp.full(idx.shape, NEG, F32)
    for b in range(REL_BUCKETS):
        out = jnp.where(idx == b, table_ref[b, h] - far, out)
    o_ref[0, 0] = out


def _diff_bias_blocks(rel_table, blk):
    assert blk >= REL_MAX_DIST
    r = jnp.arange(blk)[:, None]
    c = jnp.arange(blk)[None, :]
    diag = jnp.where((c // CHUNK) <= (r // CHUNK), _rel_bucket(c - r), -1)
    prev = _rel_bucket(c - r - blk)
    idx = jnp.stack([diag, prev], axis=0).astype(I32)
    n_heads = rel_table.shape[1]
    return pl.pallas_call(
        _rel_bias_kernel,
        grid=(n_heads, 2),
        in_specs=[
            pl.BlockSpec(memory_space=pltpu.SMEM),
            pl.BlockSpec((1, blk, blk), lambda h, s: (s, 0, 0)),
        ],
        out_specs=pl.BlockSpec((1, 1, blk, blk), lambda h, s: (h, s, 0, 0)),
        out_shape=jax.ShapeDtypeStruct((n_heads, 2, blk, blk), F32),
        compiler_params=_params("arbitrary", "arbitrary"),
        name="rel_bias",
    )(rel_table.astype(F32), idx)


def _odd_in_kernel(x_ref, g_ref, wqk_ref, w_ref, cw_ref, qkv_ref, yd_ref, zbuf_ref, *,
                   tiles_per_batch):
    i = pl.program_id(0)
    tm = x_ref.shape[0]
    qw = qkv_ref.shape[1]
    dw = yd_ref.shape[1]
    qkw = wqk_ref.shape[1]

    @pl.when(i % tiles_per_batch == 0)
    def _():
        zbuf_ref[tm:tm + 8, :] = jnp.zeros((8, dw), F32)

    xn = _rmsnorm(x_ref[...], g_ref[...]).astype(BF16)
    qkv_ref[:, 0:qkw] = _dot(xn, wqk_ref[...]).astype(BF16)
    qkv_ref[:, qkw:qw] = _dot(xn, w_ref[:, qkw:qw].astype(BF16)).astype(BF16)
    hh = _dot(xn, w_ref[:, qw:qw + dw].astype(BF16))
    gb = _dot(xn, w_ref[:, qw + dw:qw + 2 * dw].astype(BF16))
    gc = _dot(xn, w_ref[:, qw + 2 * dw:qw + 3 * dw].astype(BF16))
    z = gc * hh

    zbuf_ref[0:8, :] = zbuf_ref[tm:tm + 8, :]
    zbuf_ref[8:tm + 8, :] = z
    y = (cw_ref[0:1, :] * zbuf_ref[6:tm + 6, :] + cw_ref[1:2, :] * zbuf_ref[7:tm + 7, :]
         + cw_ref[2:3, :] * z)
    yd_ref[...] = (gb * y).astype(BF16)


def _odd_in(x, g, wqk, w_stacked, cw, *, seq, qw):
    t, d = x.shape
    tm = min(TOK_TILE, seq)
    dw = cw.shape[1]
    return pl.pallas_call(
        functools.partial(_odd_in_kernel, tiles_per_batch=seq // tm),
        grid=(t // tm,),
        in_specs=[
            pl.BlockSpec((tm, d), lambda i: (i, 0)),
            _const_spec((1, d)),
            _const_spec(wqk.shape),
            _layer_spec(w_stacked, 0),
            _const_spec(cw.shape),
        ],
        out_specs=[
            pl.BlockSpec((tm, qw), lambda i: (i, 0)),
            pl.BlockSpec((tm, dw), lambda i: (i, 0)),
        ],
        out_shape=[
            jax.ShapeDtypeStruct((t, qw), BF16),
            jax.ShapeDtypeStruct((t, dw), BF16),
        ],
        scratch_shapes=[pltpu.VMEM((tm + 8, dw), F32)],
        compiler_params=_params("arbitrary"),
        name="odd_in",
    )(x, g, wqk, w_stacked, cw)


def _mem_kv_kernel(mem_ref, g_ref, w_ref, k_ref, v_ref):
    d = mem_ref.shape[2]
    mn = _rmsnorm(mem_ref[0], g_ref[...]).astype(BF16)
    k_ref[0] = _dot(mn, w_ref[:, 0:d].astype(BF16)).astype(BF16)
    v_ref[0] = _dot(mn, w_ref[:, d:2 * d].astype(BF16)).astype(BF16)


def _mem_kv(mem, g, w_stacked, layer):
    b, m, d = mem.shape
    return pl.pallas_call(
        _mem_kv_kernel,
        grid=(b,),
        in_specs=[
            pl.BlockSpec((1, m, d), lambda i: (i, 0, 0)),
            _const_spec((1, d)),
            _layer_spec(w_stacked, layer),
        ],
        out_specs=[pl.BlockSpec((1, m, d), lambda i: (i, 0, 0))] * 2,
        out_shape=[jax.ShapeDtypeStruct((b, m, d), BF16)] * 2,
        compiler_params=_params("arbitrary"),
        name="mem_kv",
    )(mem, g, w_stacked)


def _post_mixer_kernel(x_ref, ya_ref, yb_ref, wout_ref, g_ref, wq_ref, k_ref, v_ref, wo_ref, o_ref):
    wa = ya_ref.shape[1]
    d = x_ref.shape[1]
    hd = d // N_HEADS
    x1 = (x_ref[...] + _dot(ya_ref[...], wout_ref[0:wa, :].astype(BF16))
          + _dot(yb_ref[...], wout_ref[wa:, :].astype(BF16)))
    xn = _rmsnorm(x1, g_ref[...]).astype(BF16)
    q = (_dot(xn, wq_ref[...].astype(BF16)) * (float(hd) ** -0.5)).astype(BF16)
    heads = []
    for h in range(N_HEADS):
        lg = _dot_nt(q[:, h * hd:(h + 1) * hd], k_ref[0, :, h * hd:(h + 1) * hd])
        p = jnp.exp(lg - jnp.max(lg, axis=-1, keepdims=True))
        p = p / jnp.sum(p, axis=-1, keepdims=True)
        heads.append(_dot(p.astype(BF16), v_ref[0, :, h * hd:(h + 1) * hd]).astype(BF16))
    o_ref[...] = x1 + _dot(jnp.concatenate(heads, axis=-1), wo_ref[...].astype(BF16))


def _post_mixer(x, ya, yb, wout_stacked, g, wq_stacked, kmem, vmem, wo_stacked, layer, *, seq):
    t, d = x.shape
    tm = min(TOK_TILE, seq)
    nt = seq // tm
    m = kmem.shape[1]
    wa = ya.shape[1]
    return pl.pallas_call(
        _post_mixer_kernel,
        grid=(t // tm,),
        in_specs=[
            pl.BlockSpec((tm, d), lambda i: (i, 0)),
            pl.BlockSpec((tm, wa), lambda i: (i, 0)),
            pl.BlockSpec((tm, yb.shape[1]), lambda i: (i, 0)),
            _layer_spec(wout_stacked, 0),
            _const_spec((1, d)),
            _layer_spec(wq_stacked, layer),
            pl.BlockSpec((1, m, d), lambda i: (i // nt, 0, 0)),
            pl.BlockSpec((1, m, d), lambda i: (i // nt, 0, 0)),
            _layer_spec(wo_stacked, layer),
        ],
        out_specs=pl.BlockSpec((tm, d), lambda i: (i, 0)),
        out_shape=jax.ShapeDtypeStruct((t, d), F32),
        compiler_params=_params("arbitrary"),
        name="post_mixer",
    )(x, ya, yb, wout_stacked, g, wq_stacked, kmem, vmem, wo_stacked)


def _ffn_kernel(x_ref, g_ref, wg_ref, wu_ref, wd_ref, o_ref):
    x = x_ref[...]
    h = _rmsnorm(x, g_ref[...]).astype(BF16)
    ff = wg_ref.shape[1]
    acc = x
    for c0 in range(0, ff, FFN_CHUNK):
        c1 = min(c0 + FFN_CHUNK, ff)
        a = _dot(h, wg_ref[:, c0:c1].astype(BF16))
        u = _dot(h, wu_ref[:, c0:c1].astype(BF16))
        acc = acc + _dot((jax.nn.silu(a) * u).astype(BF16), wd_ref[c0:c1, :].astype(BF16))
    o_ref[...] = acc


def _ffn(x, g, wg, wu, wd, *, seq):
    t, d = x.shape
    tm = min(TOK_TILE_SMALL, seq)
    return pl.pallas_call(
        _ffn_kernel,
        grid=(t // tm,),
        in_specs=[
            pl.BlockSpec((tm, d), lambda i: (i, 0)),
            _const_spec((1, d)),
            _const_spec(wg.shape),
            _const_spec(wu.shape),
            _const_spec(wd.shape),
        ],
        out_specs=pl.BlockSpec((tm, d), lambda i: (i, 0)),
        out_shape=jax.ShapeDtypeStruct((t, d), F32),
        compiler_params=_params("arbitrary"),
        name="ffn",
    )(x, g, wg, wu, wd)


META_I1, META_I2, META_R1, META_R2, META_G1, META_G2 = range(6)


def _router_kernel(x_ref, g_ref, wr_ref, meta_ref, metat_ref, cnt_ref, carry_ref):
    i = pl.program_id(0)
    tm = x_ref.shape[0]

    @pl.when(i == 0)
    def _():
        carry_ref[...] = jnp.zeros_like(carry_ref)

    h = _rmsnorm(x_ref[...], g_ref[...])
    hp = _split3(h)
    wp = _split3(wr_ref[...])
    logits = _dot(hp[1], wp[0]) + _dot(hp[0], wp[1]) + _dot(hp[0], wp[0])
    lane = lax.broadcasted_iota(I32, logits.shape, 1)
    logits = jnp.where(lane < N_EXPERTS, logits, NEG)
    m1 = jnp.max(logits, axis=-1, keepdims=True)
    i1 = jnp.min(jnp.where(logits == m1, lane, LANES), axis=-1, keepdims=True)
    rest = jnp.where(lane == i1, NEG, logits)
    m2 = jnp.max(rest, axis=-1, keepdims=True)
    i2 = jnp.min(jnp.where(rest == m2, lane, LANES), axis=-1, keepdims=True)
    e = jnp.exp(m2 - m1)
    g1 = 1.0 / (1.0 + e)
    g2 = e / (1.0 + e)

    sel = jnp.where((lane == i1) | (lane == i2), 1.0, 0.0)
    row = lax.broadcasted_iota(I32, (tm, tm), 0)
    col = lax.broadcasted_iota(I32, (tm, tm), 1)
    strict = jnp.where(col < row, 1.0, 0.0).astype(BF16)
    rank =_dot(strict, sel.astype(BF16)) + carry_ref[0:1, :]
    total = rank[tm - 1:tm, :] + sel[tm - 1:tm, :]
    carry_ref[...] = jnp.broadcast_to(total, carry_ref.shape)
    cnt_ref[...] = jnp.broadcast_to(total, cnt_ref.shape)
    r1 = jnp.sum(jnp.where(lane == i1, rank, 0.0), axis=-1, keepdims=True)
    r2 = jnp.sum(jnp.where(lane == i2, rank, 0.0), axis=-1, keepdims=True)
    meta = jnp.zeros_like(logits)
    for slot, val in ((META_I1, i1.astype(F32)), (META_I2, i2.astype(F32)), (META_R1, r1),
                      (META_R2, r2), (META_G1, g1), (META_G2, g2)):
        meta = jnp.where(lane == slot, val, meta)
    meta_ref[...] = meta
    metat_ref[...] = meta.T[0:8, :]


def _router(x, g, wr, *, seq):
    t, d = x.shape
    tm = min(TOK_TILE, seq)
    return pl.pallas_call(
        _router_kernel,
        grid=(t // tm,),
        in_specs=[
            pl.BlockSpec((tm, d), lambda i: (i, 0)),
            _const_spec((1, d)),
            _const_spec(wr.shape),
        ],
        out_specs=[
            pl.BlockSpec((tm, LANES), lambda i: (i, 0)),
            pl.BlockSpec((8, tm), lambda i: (0, i)),
            pl.BlockSpec((8, LANES), lambda i: (0, 0)),
        ],
        out_shape=[
            jax.ShapeDtypeStruct((t, LANES), F32),
            jax.ShapeDtypeStruct((8, t), F32),
            jax.ShapeDtypeStruct((8, LANES), F32),
        ],
        scratch_shapes=[pltpu.VMEM((8, LANES), F32)],
        compiler_params=_params("arbitrary"),
        name="router",
    )(x, g, wr)


def _row_copy(src, src_row, dst, dst_row, sem):
    return pltpu.make_async_copy(src.at[pl.ds(src_row, 1)], dst.at[pl.ds(dst_row, 1)], sem)


def _dispatch_kernel(pos_ref, pad_ref, x_ref, g_ref, xs_ref, h_ref, zrow_ref, sem_ref, zsem_ref):
    c = pl.program_id(0)
    n_chunks = pl.num_programs(0)
    tc = x_ref.shape[0]
    slot = c % 2

    @pl.when(c == 0)
    def _():
        zrow_ref[...] = jnp.zeros_like(zrow_ref)
        for e in range(N_EXPERTS):
            start = pad_ref[2 * e]
            count = pad_ref[2 * e + 1] - start

            def zissue(r, carry, start=start):
                _row_copy(zrow_ref, 0, xs_ref, start + r, zsem_ref).start()
                return carry

            def zwait(r, carry):
                _row_copy(zrow_ref, 0, xs_ref, 0, zsem_ref).wait()
                return carry

            lax.fori_loop(0, count, zissue, 0)
            lax.fori_loop(0, count, zwait, 0)

        tail = pad_ref[2 * N_EXPERTS - 1]
        groups = (xs_ref.shape[0] - tail) // 8

        def tissue(r, carry):
            start = pl.multiple_of(tail + 8 * r, 8)
            pltpu.make_async_copy(zrow_ref, xs_ref.at[pl.ds(start, 8)], zsem_ref).start()
            return carry

        def twait(r, carry):
            pltpu.make_async_copy(zrow_ref, xs_ref.at[pl.ds(0, 8)], zsem_ref).wait()
            return carry

        lax.fori_loop(0, groups, tissue, 0)
        lax.fori_loop(0, groups, twait, 0)

    h_ref[slot] = _rmsnorm(x_ref[...], g_ref[...])

    def issue(r, carry):
        tok = c * tc + r
        _row_copy(h_ref.at[slot], r, xs_ref, pos_ref[tok], sem_ref.at[slot]).start()
        _row_copy(h_ref.at[slot], r, xs_ref, pos_ref[n_chunks * tc + tok], sem_ref.at[slot]).start()
        return carry

    def drain(s):
        for _ in range(2):
            pltpu.make_async_copy(h_ref.at[s], xs_ref.at[pl.ds(0, tc)], sem_ref.at[s]).wait()

    lax.fori_loop(0, tc, issue, 0, unroll=DMA_UNROLL)

    @pl.when(c > 0)
    def _():
        drain(1 - slot)

    @pl.when(c == n_chunks - 1)
    def _():
        drain(slot)


def _dispatch(pos, pad, x, g, n_rows):
    t, d = x.shape
    tc = DISPATCH_CHUNK
    grid_spec = pltpu.PrefetchScalarGridSpec(
        num_scalar_prefetch=2,
        grid=(t // tc,),
        in_specs=[
            pl.BlockSpec((tc, d), lambda c, pos, pad: (c, 0)),
            pl.BlockSpec((1, d), lambda c, pos, pad: (0, 0)),
        ],
        out_specs=pl.BlockSpec(memory_space=pl.ANY),
        scratch_shapes=[
            pltpu.VMEM((2, tc, d), F32),
            pltpu.VMEM((8, d), F32),
            pltpu.SemaphoreType.DMA((2,)),
            pltpu.SemaphoreType.DMA(()),
        ],
    )
    return pl.pallas_call(
        _dispatch_kernel,
        grid_spec=grid_spec,
        out_shape=jax.ShapeDtypeStruct((n_rows, d), F32),
        compiler_params=_params("arbitrary"),
        name="moe_dispatch",
    )(pos, pad, x, g)


def _experts_kernel(te_ref, rows_ref, nused_ref, xs_ref, wg_hbm, wu_hbm, wd_hbm, y_ref,
                    xb_ref, wg_buf, wu_buf, wd_buf, wsem_ref):
    p = pl.program_id(0)
    f = pl.program_id(1)
    n_f = pl.num_programs(1)
    rows = rows_ref[p]
    half = MOE_TILE // 2
    tf = wg_buf.shape[2]
    step = p * n_f + f
    total = pl.num_programs(0) * n_f

    def weight_copies(s, slot):
        pp = lax.div(s, n_f)
        col = pl.multiple_of(lax.rem(s, n_f) * tf, LANES)
        e = te_ref[pp]
        return (pltpu.make_async_copy(wg_hbm.at[e, :, pl.ds(col, tf)], wg_buf.at[slot], wsem_ref.at[slot]),
                pltpu.make_async_copy(wu_hbm.at[e, :, pl.ds(col, tf)], wu_buf.at[slot], wsem_ref.at[slot]),
                pltpu.make_async_copy(wd_hbm.at[e, pl.ds(col, tf), :], wd_buf.at[slot], wsem_ref.at[slot]))

    @pl.when(step == 0)
    def _():
        for s in range(MOE_WEIGHT_BUFFERS - 1):
            for c in weight_copies(s, s):
                c.start()

    ahead = step + MOE_WEIGHT_BUFFERS - 1

    @pl.when(ahead < total)
    def _():
        for c in weight_copies(ahead, lax.rem(ahead, MOE_WEIGHT_BUFFERS)):
            c.start()

    slot = lax.rem(step, MOE_WEIGHT_BUFFERS)
    for c in weight_copies(step, slot):
        c.wait()

    @pl.when(f == 0)
    def _():
        y_ref[...] = jnp.zeros_like(y_ref)

    @pl.when(rows > 0)
    def _():
        @pl.when(f == 0)
        def _():
            xb_ref[...] = xs_ref[...].astype(BF16)

        wgu = jnp.concatenate([wg_buf[slot].astype(BF16), wu_buf[slot].astype(BF16)], axis=1)
        wd = wd_buf[slot].astype(BF16)

        def run(r0):
            gu = _dot(xb_ref[r0:r0 + half, :], wgu)
            act = (jax.nn.silu(gu[:, :tf]) * gu[:, tf:]).astype(BF16)
            y_ref[r0:r0 + half, :] += _dot(act, wd)

        run(0)

        @pl.when(rows > half)
        def _():
            run(half)


def _experts(tile_expert, tile_rows, n_used, xs, wg, wu, wd):
    n_rows, d = xs.shape
    ff = wg.shape[2]
    tm = MOE_TILE
    tf = MOE_FF_CHUNK
    n_tiles = n_rows // tm

    def x_map(p, f, te, tr, nu):
        return (jnp.minimum(p, nu[0] - 1), 0)

    grid_spec = pltpu.PrefetchScalarGridSpec(
        num_scalar_prefetch=3,
        grid=(n_tiles, ff // tf),
        in_specs=[
            pl.BlockSpec((tm, d), x_map),
            pl.BlockSpec(memory_space=pl.ANY),
            pl.BlockSpec(memory_space=pl.ANY),
            pl.BlockSpec(memory_space=pl.ANY),
        ],
        out_specs=pl.BlockSpec((tm, d), lambda p, f, te, tr, nu: (p, 0)),
        scratch_shapes=[
            pltpu.VMEM((tm, d), BF16),
            pltpu.VMEM((MOE_WEIGHT_BUFFERS, d, tf), F32),
            pltpu.VMEM((MOE_WEIGHT_BUFFERS, d, tf), F32),
            pltpu.VMEM((MOE_WEIGHT_BUFFERS, tf, d), F32),
            pltpu.SemaphoreType.DMA((MOE_WEIGHT_BUFFERS,)),
        ],
    )
    return pl.pallas_call(
        _experts_kernel,
        grid_spec=grid_spec,
        out_shape=jax.ShapeDtypeStruct((n_rows, d), F32),
        compiler_params=_params("arbitrary", "arbitrary"),
        name="moe_experts",
    )(tile_expert, tile_rows, n_used, xs, wg, wu, wd)


def _combine_kernel(pos_ref, y_ref, x_ref, meta_ref, g_ref, o_ref, buf_ref, sem_ref):
    i = pl.program_id(0)
    n = pl.num_programs(0)
    tc = x_ref.shape[0]

    def fetch(tile, slot):
        def body(r, carry):
            tok = tile * tc + r
            for k in range(2):
                pltpu.make_async_copy(y_ref.at[pl.ds(pos_ref[k * n * tc + tok], 1)],
                                      buf_ref.at[slot, k, pl.ds(r, 1)], sem_ref.at[slot]).start()
            return carry
        lax.fori_loop(0, tc, body, 0, unroll=DMA_UNROLL)

    @pl.when(i == 0)
    def _():
        fetch(0, 0)

    @pl.when(i + 1 < n)
    def _():
        fetch(i + 1, (i + 1) % 2)

    slot = i % 2

    for k in range(2):
        pltpu.make_async_copy(y_ref.at[pl.ds(0, tc)], buf_ref.at[slot, k], sem_ref.at[slot]).wait()
    meta = meta_ref[...]
    g1 = meta[:, META_G1:META_G1 + 1]
    g2 = meta[:, META_G2:META_G2 + 1]
    x = x_ref[...] + g1 * buf_ref[slot, 0] + g2 * buf_ref[slot, 1]
    o_ref[...] = _rmsnorm(x, g_ref[...])


def _combine(pos, y, x, meta, g):
    t, d = x.shape
    tc = COMBINE_TILE
    grid_spec = pltpu.PrefetchScalarGridSpec(
        num_scalar_prefetch=1,
        grid=(t // tc,),
        in_specs=[
            pl.BlockSpec(memory_space=pl.ANY),
            pl.BlockSpec((tc, d), lambda i, pos: (i, 0)),
            pl.BlockSpec((tc, LANES), lambda i, pos: (i, 0)),
            pl.BlockSpec((1, d), lambda i, pos: (0, 0)),
        ],
        out_specs=pl.BlockSpec((tc, d), lambda i, pos: (i, 0)),
        scratch_shapes=[pltpu.VMEM((2, 2, tc, d), F32), pltpu.SemaphoreType.DMA((2,))],
    )
    return pl.pallas_call(
        _combine_kernel,
        grid_spec=grid_spec,
        out_shape=jax.ShapeDtypeStruct((t, d), F32),
        compiler_params=_params("arbitrary"),
        name="moe_combine",
    )(pos, y, x, meta, g)


def _moe_plan(meta_t, counts, n_tiles):
    cnt = counts[0, :N_EXPERTS].astype(I32)
    tiles = (cnt + MOE_TILE - 1) // MOE_TILE
    tile_end = jnp.cumsum(tiles)
    row_off = (tile_end - tiles) * MOE_TILE
    n_used = tile_end[-1:]
    experts = jnp.arange(N_EXPERTS, dtype=I32)

    def rows(idx_row, rank_row):
        idx = meta_t[idx_row].astype(I32)
        off = jnp.sum(jnp.where(idx[None, :] == experts[:, None], row_off[:, None], 0), axis=0)
        return off + meta_t[rank_row].astype(I32)

    pos = jnp.concatenate([rows(META_I1, META_R1), rows(META_I2, META_R2)])
    tile_ids = jnp.arange(n_tiles, dtype=I32)
    te = jnp.sum((tile_end[None, :] <= tile_ids[:, None]).astype(I32), axis=1)
    mine = te[:, None] == experts[None, :]
    first_row = tile_ids * MOE_TILE - jnp.sum(jnp.where(mine, row_off[None, :], 0), axis=1)
    tile_rows = jnp.clip(jnp.sum(jnp.where(mine, cnt[None, :], 0), axis=1) - first_row, 0, MOE_TILE)
    last = jnp.sum((tile_end <= n_used - 1).astype(I32))
    te = jnp.minimum(te, last).astype(I32)
    pad = jnp.stack([row_off + cnt, row_off + tiles * MOE_TILE], axis=1).reshape(-1).astype(I32)
    return pos.astype(I32), pad, te, tile_rows.astype(I32), n_used.astype(I32)


def _moe_and_final_norm(x, g_ffn, wr, wg, wu, wd, g_final, *, seq):
    t, d = x.shape
    n_tiles = 2 * t // MOE_TILE + N_EXPERTS
    meta, meta_t, counts = _router(x, g_ffn, wr, seq=seq)
    pos, pad, te, tile_rows, n_used = _moe_plan(meta_t, counts, n_tiles)
    xs = _dispatch(pos, pad, x, g_ffn, n_tiles * MOE_TILE)
    y = _experts(te, tile_rows, n_used, xs, wg, wu, wd)
    return _combine(pos, y, x, meta, g_final)


def _row(v):
    return v.reshape(1, -1).astype(F32)


def kernel(x, mem, norm_mix, norm_mem_q, norm_mem_kv, norm_ffn, norm_final, even_w_in, fox_b_f, gmlp_v_gain, gmlp_w_s, gmlp_b_s, even_w_out, odd_w_in, diff_lambda_q1, diff_lambda_k1, diff_lambda_q2, diff_lambda_k2, diff_subln_gain, conv_w, odd_w_out, rel_bias, mem_w_q, mem_w_kv, mem_w_o, ffn_w_gate, ffn_w_up, ffn_w_down, router_w, moe_w_gate, moe_w_up, moe_w_down):
    batch, seq, d = x.shape
    t = batch * seq
    depth = norm_mix.shape[0]
    assert depth == 2 and seq % (2 * ATT_BLOCK) == 0 and seq % TOK_TILE == 0
    xf = x.reshape(t, d)
    hd = d // N_HEADS
    aw = gmlp_v_gain.shape[1]
    n_blk = gmlp_w_s.shape[2]

    gate_cols = even_w_in.shape[2] - N_HEADS
    wf = jnp.pad(even_w_in[0, :, gate_cols:], ((0, 0), (0, LANES - N_HEADS)))
    bf = jnp.pad(fox_b_f[0], (0, LANES - N_HEADS)).reshape(1, LANES)
    bs = jnp.broadcast_to(gmlp_b_s[0][:, :, None], (A_GROUPS, n_blk, aw // A_GROUPS)).astype(F32)
    ya, qkv, aug = _even_in(xf, _row(norm_mix[0]), even_w_in, wf, bf, _row(gmlp_v_gain[0]),
                            gmlp_w_s[0], bs, seq=seq)
    yb = _fox_attention(qkv, aug, batch=batch, seq=seq)

    def mem_attention(layer, x_in, ya_, yb_, w_out):
        km, vm = _mem_kv(mem, _row(norm_mem_kv[layer]), mem_w_kv, layer)
        return _post_mixer(x_in, ya_, yb_, w_out, _row(norm_mem_q[layer]), mem_w_q, km, vm,
                           mem_w_o, layer, seq=seq)

    xf = mem_attention(0, xf, ya, yb, even_w_out)
    xf = _ffn(xf, _row(norm_ffn[0]), ffn_w_gate[0], ffn_w_up[0], ffn_w_down[0], seq=seq)

    qk_dim = LANES // 2
    qk_w = 2 * N_HEADS * qk_dim

    def regroup(cols):
        return cols.reshape(d, 2, N_HEADS, qk_dim).transpose(0, 2, 1, 3).reshape(d, qk_w)

    wqk = jnp.concatenate([regroup(odd_w_in[0, :, :qk_w]) * float(qk_dim) ** -0.5,
                           regroup(odd_w_in[0, :, qk_w:2 * qk_w])], axis=1).astype(BF16)
    qkv, yd = _odd_in(xf, _row(norm_mix[1]), wqk, odd_w_in, conv_w[0].astype(F32), seq=seq,
                      qw=2 * qk_w + N_HEADS * LANES)
    lam_init = 0.8 - 0.6 * math.exp(-0.3 * 1)
    lam = (jnp.exp(jnp.sum(diff_lambda_q1[0] * diff_lambda_k1[0]))
           - jnp.exp(jnp.sum(diff_lambda_q2[0] * diff_lambda_k2[0])) + lam_init)
    bias = _diff_bias_blocks(rel_bias, min(ATT_BLOCK, seq))
    yc = _diff_attention(qkv, lam.reshape(1, 1).astype(F32), bias, _row(diff_subln_gain[0]),
                         batch=batch, seq=seq, out_scale=1.0 - lam_init)
    xf = mem_attention(1, xf, yc, yd, odd_w_out)
    wr = jnp.pad(router_w[0], ((0, 0), (0, LANES - N_EXPERTS))).astype(F32)
    out = _moe_and_final_norm(xf, _row(norm_ffn[1]), wr, moe_w_gate[0], moe_w_up[0], moe_w_down[0],
                              _row(norm_final), seq=seq)
    return out.reshape(batch, seq, d)
```
